```python
import jax
import jax.numpy as jnp
from jax import lax
import numpy as np

D_MODEL = 1024
BATCH = 8
SEQ = 2048
DEPTH = 1
DEC_BATCH = 128
DEC_SEQ = 8
PAST_LEN = 16384
PAGE_SIZE = 128

N_META = 16
HA_HEADS = 4
HA_DK = 128
HA_DV = 128
HA_QK = HA_HEADS * HA_DK
HA_WIDTH = HA_HEADS * HA_DV
HA_IN = 2 * HA_QK + 2 * HA_WIDTH
HA_CHUNK = 64
HB_HEADS = 8
HB_HEAD = 64
HB_WIDTH = HB_HEADS * HB_HEAD
LORA_W = 64
LORA_A = 64
LORA_G = 128
RWKV_IN = 3 * HB_WIDTH + LORA_W + LORA_A + LORA_G
D_IN = HA_IN + RWKV_IN
MIX_WIDTH = HA_WIDTH + HB_WIDTH
PEER_HEADS = 8
N_KEYS = 128
N_EXPERTS = N_KEYS * N_KEYS
D_KEY = 256
PEER_TOPK = 16
PEER_BLOCK = 256
EPS = 1e-6
GN_EPS = 64e-5

kernel_name = 'hymba_hgrn2_rwkv7_peer_step'


def rmsnorm(x, g):
    xf = x.astype(jnp.float32)
    y = xf * lax.rsqrt(jnp.mean(xf * xf, axis=-1, keepdims=True) + EPS)
    return (y * g.astype(jnp.float32)).astype(x.dtype)


def hgrn2_chunked(q, k, v, logf, s0, chunk):
    b_, l_, h_, _ = q.shape
    n = l_ // chunk

    def to_chunks(t):
        return t.reshape(b_, n, chunk, h_, t.shape[-1]).transpose(1, 0, 3, 2, 4)

    causal = jnp.tril(jnp.ones((chunk, chunk), dtype=bool))[:, :, None]

    def step(s, inp):
        qc, kc, vc, gc = inp
        cum = jnp.cumsum(gc, axis=2)
        rel = jnp.where(causal, cum[:, :, :, None, :] - cum[:, :, None, :, :], -jnp.inf)
        scores = jnp.einsum('bhtd,bhsd,bhtsd->bhts', qc, kc, jnp.exp(rel))
        o = (jnp.einsum('bhts,bhse->bhte', scores, vc)
             + jnp.einsum('bhtd,bhde->bhte', qc * jnp.exp(cum), s))
        last = cum[:, :, -1:, :]
        s_new = (jnp.exp(last[:, :, 0, :])[..., None] * s
                 + jnp.einsum('bhsd,bhse->bhde', kc * jnp.exp(last - cum), vc))
        return s_new, o

    s_fin, o = lax.scan(step, s0, (to_chunks(q), to_chunks(k), to_chunks(v), to_chunks(logf)))
    return o.transpose(1, 0, 3, 2, 4).reshape(b_, l_, h_, v.shape[-1]), s_fin


def rwkv7_scan(r, w, k, v, kk, a, s0):
    def step(s, inp):
        rt, wt, kt, vt, kkt, at = inp
        sa = jnp.einsum('bhvk,bhk->bhv', s, -kkt)
        s = (s * wt[:, :, None, :] + sa[..., None] * (kkt * at)[:, :, None, :]
             + vt[..., None] * kt[:, :, None, :])
        return s, jnp.einsum('bhvk,bhk->bhv', s, rt)

    xs = tuple(t.transpose(1, 0, 2, 3) for t in (r, w, k, v, kk, a))
    s_fin, y = lax.scan(step, s0, xs)
    return y.transpose(1, 0, 2, 3), s_fin


def time_mix(h, shift_prev, s_hgrn, s_rwkv, segments, lb, w_in, hgrn_norm, mu_shift,
             w0, w_w2, a0, a_w2, g_w2, k_k, k_a, r_k, ln_w, ln_b, w_out):
    f32 = jnp.float32
    bsz, seq, _ = h.shape
    z = h @ w_in
    za, zb = z[..., :HA_IN], z[..., HA_IN:]

    zq, zf, zi, zo = jnp.split(za, [HA_QK, 2 * HA_QK, 2 * HA_QK + HA_WIDTH], axis=-1)
    zf = zf.astype(f32)
    logf = jnp.log(lb + (1.0 - lb) * jax.nn.sigmoid(zf))
    kin = (1.0 - lb) * jax.nn.sigmoid(-zf)
    qa = jax.nn.silu(zq.astype(f32)).reshape(bsz, seq, HA_HEADS, HA_DK)
    ia = zi.astype(f32).reshape(bsz, seq, HA_HEADS, HA_DV)
    kin = kin.reshape(bsz, seq, HA_HEADS, HA_DK)
    logf = logf.reshape(bsz, seq, HA_HEADS, HA_DK)
    outs = []
    s_a = s_hgrn
    start = 0
    for length, chunk in segments:
        sl = slice(start, start + length)
        o_seg, s_a = hgrn2_chunked(qa[:, sl], kin[:, sl], ia[:, sl], logf[:, sl], s_a, chunk)
        outs.append(o_seg)
        start += length
    oa = jnp.concatenate(outs, axis=1)
    oa = oa * lax.rsqrt(jnp.mean(oa * oa, axis=-1, keepdims=True) + EPS)
    oa = (oa * hgrn_norm.astype(f32).reshape(HA_HEADS, HA_DV)).reshape(bsz, seq, HA_WIDTH)
    oa = oa * jax.nn.silu(zo.astype(f32))

    prev = jnp.concatenate([shift_prev[:, None].astype(zb.dtype), zb[:, :-1]], axis=1)
    zm = zb + (prev - zb) * mu_shift
    new_shift = zb[:, -1]
    r, k, v, dw, da, dg = (t.astype(f32) for t in jnp.split(
        zm, [HB_WIDTH, 2 * HB_WIDTH, 3 * HB_WIDTH, 3 * HB_WIDTH + LORA_W,
             3 * HB_WIDTH + LORA_W + LORA_A], axis=-1))
    w_raw = w0 + jnp.tanh(dw) @ w_w2
    decay = jnp.exp(-jnp.exp(-jax.nn.softplus(-w_raw) - 0.5))
    a = jax.nn.sigmoid(a0 + da @ a_w2)
    g = jax.nn.sigmoid(dg) @ g_w2
    hd = lambda t: t.reshape(bsz, seq, HB_HEADS, HB_HEAD)
    kk = hd(k * k_k)
    kk = kk / jnp.maximum(jnp.sqrt(jnp.sum(kk * kk, axis=-1, keepdims=True)), 1e-12)
    k = k * (1.0 + (a - 1.0) * k_a)
    rh, kh, vh = hd(r), hd(k), hd(v)
    y, s_b = rwkv7_scan(rh, hd(decay), kh, vh, kk, hd(a), s_rwkv)
    mu = jnp.mean(y, axis=-1, keepdims=True)
    var = jnp.mean(jnp.square(y - mu), axis=-1, keepdims=True)
    yn = ((y - mu) * lax.rsqrt(var + GN_EPS) * ln_w.astype(f32).reshape(HB_HEADS, HB_HEAD)
          + ln_b.astype(f32).reshape(HB_HEADS, HB_HEAD))
    bonus = jnp.sum(rh * kh * r_k.astype(f32).reshape(HB_HEADS, HB_HEAD), axis=-1, keepdims=True) * vh
    ob = (yn + bonus).reshape(bsz, seq, HB_WIDTH) * g

    out = jnp.concatenate([oa, ob], axis=-1).astype(h.dtype) @ w_out
    return out, s_a, s_b, new_shift


def peer(h, w_q, sub_keys, u_tab, v_tab):
    bsz, seq, d = h.shape
    n_tok = bsz * seq
    n_blk = -(-n_tok // PEER_BLOCK)
    t = jnp.pad(h.reshape(n_tok, d), ((0, n_blk * PEER_BLOCK - n_tok), (0, 0)))
    t = t.reshape(n_blk, PEER_BLOCK, d)

    def block(xb):
        q = (xb @ w_q).reshape(PEER_BLOCK, PEER_HEADS, 2, D_KEY // 2)
        s = jnp.einsum('thpc,pnc->thpn', q, sub_keys).astype(jnp.float32)
        sv, si = lax.top_k(s, PEER_TOPK)
        cand = (sv[:, :, 0, :, None] + sv[:, :, 1, None, :]).reshape(
            PEER_BLOCK, PEER_HEADS, PEER_TOPK * PEER_TOPK)
        best, flat = lax.top_k(cand, PEER_TOPK)
        i1 = jnp.take_along_axis(si[:, :, 0], flat // PEER_TOPK, axis=-1)
        i2 = jnp.take_along_axis(si[:, :, 1], flat % PEER_TOPK, axis=-1)
        idx = i1 * N_KEYS + i2
        gate = jax.nn.softmax(best, axis=-1)
        act = jax.nn.gelu(jnp.einsum('thkd,td->thk', u_tab[idx], xb).astype(jnp.float32),
                          approximate=False)
        coef = (gate * act).astype(xb.dtype)
        return jnp.einsum('thk,thkd->td', coef, v_tab[idx])

    out = lax.map(block, t).reshape(n_blk * PEER_BLOCK, d)[:n_tok]
    return out.reshape(bsz, seq, d)


def trunk(x, st_hgrn, st_rwkv, st_shift, segments, norm1, w_in, lb_theta, hgrn_norm, mu_shift,
          w0, w_w2, a0, a_w2, g_w2, k_k, k_a, r_k, ln_w, ln_b, w_out, norm2,
          peer_wq, peer_keys, peer_u, peer_v, norm_f):
    lb_all = jnp.cumsum(jax.nn.softmax(lb_theta.astype(jnp.float32), axis=0), axis=0)
    new_h, new_r, new_s = [], [], []
    for l in range(DEPTH):
        mix, sh, sr, ss = time_mix(
            rmsnorm(x, norm1[l]), st_shift[l], st_hgrn[l].astype(jnp.float32),
            st_rwkv[l].astype(jnp.float32), segments, lb_all[l], w_in[l], hgrn_norm[l],
            mu_shift[l], w0[l], w_w2[l], a0[l], a_w2[l], g_w2[l], k_k[l], k_a[l], r_k[l],
            ln_w[l], ln_b[l], w_out[l])
        x = x + mix
        x = x + peer(rmsnorm(x, norm2[l]), peer_wq[l], peer_keys[l], peer_u[l], peer_v[l])
        new_h.append(sh)
        new_r.append(sr)
        new_s.append(ss)
    return rmsnorm(x, norm_f), jnp.stack(new_h), jnp.stack(new_r), jnp.stack(new_s)


def setup_inputs(seed: int = 0) -> dict:
    key = jax.random.key(seed)
    ks = iter(jax.random.split(key, 40))
    nrm = lambda shape, scale: scale * jax.random.normal(next(ks), shape, jnp.float32)
    return {
        'x_prompt': nrm((BATCH, SEQ, D_MODEL), 1.0),
        'x_sample': nrm((DEC_BATCH, DEC_SEQ, D_MODEL), 1.0),
        'state_hgrn': nrm((DEPTH, DEC_BATCH, HA_HEADS, HA_DK, HA_DV), 0.5),
        'state_rwkv': nrm((DEPTH, DEC_BATCH, HB_HEADS, HB_HEAD, HB_HEAD), 0.3),
        'state_shift': nrm((DEPTH, DEC_BATCH, RWKV_IN), 1.0),
        'meta_tokens': nrm((N_META, D_MODEL), 1.0),
        'norm1': 1.0 + nrm((DEPTH, D_MODEL), 0.02),
        'w_in': nrm((DEPTH, D_MODEL, D_IN), D_MODEL ** -0.5),
        'lb_theta': nrm((DEPTH + 1, HA_QK), 0.3),
        'hgrn_norm': 1.0 + nrm((DEPTH, HA_WIDTH), 0.02),
        'mu_shift': jax.random.uniform(next(ks), (DEPTH, RWKV_IN), jnp.float32),
        'w0': jax.random.uniform(next(ks), (DEPTH, HB_WIDTH), jnp.float32, -5.0, 0.5),
        'w_w2': nrm((DEPTH, LORA_W, HB_WIDTH), 0.05),
        'a0': nrm((DEPTH, HB_WIDTH), 0.1),
        'a_w2': nrm((DEPTH, LORA_A, HB_WIDTH), 0.05),
        'g_w2': nrm((DEPTH, LORA_G, HB_WIDTH), LORA_G ** -0.5),
        'k_k': 0.85 + nrm((DEPTH, HB_WIDTH), 0.05),
        'k_a': 1.0 + nrm((DEPTH, HB_WIDTH), 0.05),
        'r_k': nrm((DEPTH, HB_WIDTH), 0.1),
        'ln_w': 1.0 + nrm((DEPTH, HB_WIDTH), 0.02),
        'ln_b': nrm((DEPTH, HB_WIDTH), 0.01),
        'w_out': nrm((DEPTH, MIX_WIDTH, D_MODEL), 0.5 * MIX_WIDTH ** -0.5),
        'norm2': 1.0 + nrm((DEPTH, D_MODEL), 0.02),
        'peer_wq': nrm((DEPTH, D_MODEL, PEER_HEADS * D_KEY), D_MODEL ** -0.5),
        'peer_keys': nrm((DEPTH, 2, N_KEYS, D_KEY // 2), (D_KEY // 2) ** -0.5),
        'peer_u': nrm((DEPTH, N_EXPERTS, D_MODEL), D_MODEL ** -0.5),
        'peer_v': nrm((DEPTH, N_EXPERTS, D_MODEL), 0.5),
        'norm_f': 1.0 + nrm((D_MODEL,), 0.02),
    }


def reference(x_prompt, x_sample, state_hgrn, state_rwkv, state_shift, meta_tokens, norm1, w_in,
              lb_theta, hgrn_norm, mu_shift, w0, w_w2, a0, a_w2, g_w2, k_k, k_a, r_k, ln_w, ln_b,
              w_out, norm2, peer_wq, peer_keys, peer_u, peer_v, norm_f):
    f32 = jnp.float32
    bp, sp, d = x_prompt.shape
    bs, ss = x_sample.shape[0], x_sample.shape[1]

    xp = jnp.concatenate([jnp.broadcast_to(meta_tokens[None].astype(x_prompt.dtype), (bp, N_META, d)),
                          x_prompt], axis=1)
    yp, hgrn_p, rwkv_p, shift_p = trunk(
        xp, jnp.zeros((DEPTH, bp, HA_HEADS, HA_DK, HA_DV), f32),
        jnp.zeros((DEPTH, bp, HB_HEADS, HB_HEAD, HB_HEAD), f32),
        jnp.zeros((DEPTH, bp, RWKV_IN), x_prompt.dtype),
        ((N_META, N_META), (sp, min(HA_CHUNK, sp))),
        norm1, w_in, lb_theta, hgrn_norm, mu_shift, w0, w_w2, a0, a_w2, g_w2, k_k, k_a, r_k,
        ln_w, ln_b, w_out, norm2, peer_wq, peer_keys, peer_u, peer_v, norm_f)
    y_prompt = yp[:, N_META:]

    y_sample, hgrn_s, rwkv_s, shift_s = trunk(
        x_sample, state_hgrn, state_rwkv, state_shift, ((ss, ss),),
        norm1, w_in, lb_theta, hgrn_norm, mu_shift, w0, w_w2, a0, a_w2, g_w2, k_k, k_a, r_k,
        ln_w, ln_b, w_out, norm2, peer_wq, peer_keys, peer_u, peer_v, norm_f)

    return (y_prompt, y_sample,
            hgrn_p.astype(state_hgrn.dtype), rwkv_p.astype(state_rwkv.dtype),
            shift_p.astype(state_shift.dtype),
            hgrn_s.astype(state_hgrn.dtype), rwkv_s.astype(state_rwkv.dtype),
            shift_s.astype(state_shift.dtype))
```

```python
import functools

import numpy as np
import jax
import jax.numpy as jnp
from jax import lax
from jax.experimental import pallas as pl
from jax.experimental.pallas import tpu as pltpu

F32 = jnp.float32
BF16 = jnp.bfloat16

EPS = 1e-6
GN_EPS = 64e-5
HGRN_CHUNK = 64
PEER_TOPK = 16
LANES = 128
SUBLANES = 8
VMEM_LIMIT_BYTES = 52 * 1024 * 1024

_NT = (((1,), (1,)), ((), ()))
_TN = (((0,), (0,)), ((), ()))


def _dot(a, b):
    return jnp.dot(a, b, preferred_element_type=F32)


def _split3(a):
    a1 = a.astype(BF16)
    r1 = a - a1.astype(F32)
    a2 = r1.astype(BF16)
    a3 = (r1 - a2.astype(F32)).astype(BF16)
    return a1, a2, a3


def _dot_exact_rhs(a, b_exact):
    a1, a2, a3 = _split3(a)
    return _dot(a1, b_exact) + _dot(a2, b_exact) + _dot(a3, b_exact)


def _dot_exact_lhs(a_exact, b):
    b1, b2, b3 = _split3(b)
    return _dot(a_exact, b1) + _dot(a_exact, b2) + _dot(a_exact, b3)


def _dot_hi(a, b):
    ah = a.astype(BF16)
    al = (a - ah.astype(F32)).astype(BF16)
    bh = b.astype(BF16)
    bl = (b - bh.astype(F32)).astype(BF16)
    return _dot(ah, bh) + _dot(ah, bl) + _dot(al, bh)


def _sigmoid(x):
    return 1.0 / (1.0 + jnp.exp(-x))


def _rms(x, g):
    return x * lax.rsqrt(jnp.mean(x * x, axis=-1, keepdims=True) + EPS) * g


def _params(*sem):
    return pltpu.CompilerParams(dimension_semantics=sem, vmem_limit_bytes=VMEM_LIMIT_BYTES)


def _const_spec(shape):
    nd = len(shape)
    return pl.BlockSpec(shape, lambda *_: (0,) * nd)


def _in_proj_body(x_ref, g_ref, wa_ref, wb_ref, za_ref, zb_ref):
    hb = _rms(x_ref[...], g_ref[...]).astype(BF16)
    za_ref[...] = _dot(hb, wa_ref[...])
    zb_ref[...] = _dot(hb, wb_ref[...])


def _in_proj(x, g, wa, wb, tb):
    n, d = x.shape
    na, nb = wa.shape[1], wb.shape[1]
    return pl.pallas_call(
        _in_proj_body,
        grid=(n // tb,),
        in_specs=[pl.BlockSpec((tb, d), lambda i: (i, 0)), _const_spec((1, d)),
                  _const_spec(wa.shape), _const_spec(wb.shape)],
        out_specs=[pl.BlockSpec((tb, na), lambda i: (i, 0)), pl.BlockSpec((tb, nb), lambda i: (i, 0))],
        out_shape=[jax.ShapeDtypeStruct((n, na), F32), jax.ShapeDtypeStruct((n, nb), F32)],
        compiler_params=_params("parallel"),
        name="in_proj",
    )(x, g, wa, wb)


def _hgrn_tables(c):
    nl = int(np.log2(c)) + 1
    mats = np.zeros((nl, c, c), np.float32)
    masks = np.zeros((nl, c, c), np.float32)
    idx = np.arange(c)
    mats[0] = (idx[None, :] <= idx[:, None])
    masks[0] = np.eye(c)
    for l in range(1, nl):
        blk = 1 << l
        half = blk >> 1
        for t in range(c):
            start = (t // blk) * blk
            ref = start + half - 1
            if t - start >= half:
                mats[l, t, ref + 1:t + 1] = 1.0
                masks[l, t, start:start + half] = 1.0
            else:
                mats[l, t, t + 1:ref + 1] = 1.0
    return mats.reshape(nl * c, c), masks


def _hgrn_body(za_ref, lbt_ref, gn_ref, s0_ref, mats_ref, masks_ref, oa_ref, sout_ref, st_scr,
               *, rows, chunk, heads, dk, dv):
    c_id = pl.program_id(1)
    qk = heads * dk
    wd = heads * dv
    nl = masks_ref.shape[0]

    @pl.when(c_id == 0)
    def _():
        for h in range(heads):
            st_scr[h] = s0_ref[0, h].T

    th = lbt_ref[...]
    ex = jnp.exp(th - jnp.max(th, axis=0, keepdims=True))
    lb = ex[0:1] / jnp.sum(ex, axis=0, keepdims=True)

    za = za_ref[...]
    zq = za[:, :qk]
    zf = za[:, qk:2 * qk]
    zi = za[:, 2 * qk:2 * qk + wd]
    zo = za[:, 2 * qk + wd:]
    logf = jnp.log(lb + (1.0 - lb) * _sigmoid(zf))
    kin = (1.0 - lb) * _sigmoid(-zf)
    qa = zq * _sigmoid(zq)
    if rows < chunk:
        pad = jnp.zeros((chunk - rows, qk), F32)
        logf = jnp.concatenate([logf, pad], axis=0)
        kin = jnp.concatenate([kin, pad], axis=0)
        qa = jnp.concatenate([qa, pad], axis=0)
        zi = jnp.concatenate([zi, jnp.zeros((chunk - rows, wd), F32)], axis=0)

    dec = _dot_exact_lhs(mats_ref[...], logf)
    gn = gn_ref[...]
    for h in range(heads):
        ks = slice(h * dk, (h + 1) * dk)
        vs = slice(h * dv, (h + 1) * dv)
        q = qa[:, ks]
        k = kin[:, ks]
        v = zi[:, vs].astype(BF16)
        cum = dec[0:chunk, ks]
        sc = lax.dot_general(q.astype(BF16), k.astype(BF16), _NT, preferred_element_type=F32)
        scores = jnp.where(masks_ref[0] != 0.0, sc, 0.0)
        for l in range(1, nl):
            e = jnp.exp(dec[l * chunk:(l + 1) * chunk, ks])
            sc = lax.dot_general((q * e).astype(BF16), (k * e).astype(BF16), _NT, preferred_element_type=F32)
            scores = scores + jnp.where(masks_ref[l] != 0.0, sc, 0.0)
        st = st_scr[h]
        o = _dot(scores.astype(BF16), v) + lax.dot_general(
            (q * jnp.exp(cum)).astype(BF16), st.astype(BF16), _NT, preferred_element_type=F32)
        last = cum[chunk - 1:chunk, :]
        kd = (k * jnp.exp(last - cum)).astype(BF16)
        st_scr[h] = jnp.exp(last) * st + lax.dot_general(v, kd, _TN, preferred_element_type=F32)

        o = o[:rows]
        o = o * lax.rsqrt(jnp.mean(o * o, axis=-1, keepdims=True) + EPS) * gn[:, vs]
        zoh = zo[:, vs]
        oa_ref[:, vs] = o * (zoh * _sigmoid(zoh))

    @pl.when(c_id == pl.num_programs(1) - 1)
    def _():
        for h in range(heads):
            sout_ref[0, h] = st_scr[h].T


def _hgrn(za, row_block0, nseq, nchunk, rows, lb_theta, gn, s0, heads, dk, dv):
    chunk = HGRN_CHUNK
    mats, masks = _hgrn_tables(chunk)
    mats = jnp.asarray(mats, BF16)
    masks = jnp.asarray(masks, F32)
    n = nseq * nchunk * rows
    wd = heads * dv
    s0_map = (lambda s, c: (0, 0, 0, 0)) if s0.shape[0] == 1 else (lambda s, c: (s, 0, 0, 0))
    body = functools.partial(_hgrn_body, rows=rows, chunk=chunk, heads=heads, dk=dk, dv=dv)
    return pl.pallas_call(
        body,
        grid=(nseq, nchunk),
        in_specs=[pl.BlockSpec((rows, za.shape[1]), lambda s, c: (row_block0 + s * nchunk + c, 0)),
                  _const_spec(lb_theta.shape), _const_spec(gn.shape),
                  pl.BlockSpec((1, heads, dk, dv), s0_map),
                  _const_spec(mats.shape), _const_spec(masks.shape)],
        out_specs=[pl.BlockSpec((rows, wd), lambda s, c: (s * nchunk + c, 0)),
                   pl.BlockSpec((1, heads, dk, dv), lambda s, c: (s, 0, 0, 0))],
        out_shape=[jax.ShapeDtypeStruct((n, wd), F32), jax.ShapeDtypeStruct((nseq, heads, dk, dv), F32)],
        scratch_shapes=[pltpu.VMEM((heads, dv, dk), F32)],
        compiler_params=_params("parallel", "arbitrary"),
        name="hgrn",
    )(za, lb_theta, gn, s0, mats, masks)


def _rwkv_body(zb_ref, sh0_ref, s0_ref, mu_ref, w0_ref, ww_ref, a0_ref, aw_ref, gw_ref, kk_ref, ka_ref,
               rk_ref, lnw_ref, lnb_ref, ones_ref, seg_ref,
               ob_ref, sout_ref, shout_ref,
               s_scr, prev_scr, al_scr, w_scr, be_scr, k_scr, r_scr, v1_scr, v2_scr, v3_scr, g_scr, y_scr,
               *, nb, tc, hw, hd, lora):
    t_id = pl.program_id(1)
    pairs = hw // LANES
    rin = zb_ref.shape[-1]

    @pl.when(t_id == 0)
    def _():
        s_scr[...] = s0_ref[...]
        prev_scr[...] = sh0_ref[...]

    row = lax.broadcasted_iota(jnp.int32, (tc, rin), 0)
    mu = mu_ref[...]
    zms = []
    for b in range(nb):
        zb = zb_ref[b]
        prev = jnp.where(row == 0, prev_scr[b:b + 1, :], pltpu.roll(zb, 1, axis=0))
        prev_scr[b:b + 1, :] = zb[tc - 1:tc, :]
        zms.append(zb + (prev - zb) * mu)
    zm = jnp.concatenate(zms, axis=0)
    r = zm[:, :hw]
    k = zm[:, hw:2 * hw]
    v = zm[:, 2 * hw:3 * hw]
    dwa = zm[:, 3 * hw:3 * hw + lora]
    dg = zm[:, 3 * hw + lora:]
    w_raw = w0_ref[...] + _dot_hi(jnp.tanh(dwa), ww_ref[...])
    decay = jnp.exp(-np.float32(np.exp(-0.5)) * _sigmoid(w_raw))
    a = _sigmoid(a0_ref[...] + _dot_hi(dwa, aw_ref[...]))
    g = _dot_hi(_sigmoid(dg), gw_ref[...])
    kk = k * kk_ref[...]
    ss = _dot_exact_rhs(kk * kk, seg_ref[...])
    kk = kk / jnp.maximum(jnp.sqrt(ss), 1e-12)
    k2 = k * (1.0 + (a - 1.0) * ka_ref[...])
    v1, v2, v3 = _split3(v)
    shp = (nb, tc, hw)
    al_scr[...] = (-kk).reshape(shp)
    w_scr[...] = decay.reshape(shp)
    be_scr[...] = (kk * a).reshape(shp)
    k_scr[...] = k2.reshape(shp)
    r_scr[...] = r.reshape(shp)
    v1_scr[...] = v1.astype(F32).reshape(shp)
    v2_scr[...] = v2.astype(F32).reshape(shp)
    v3_scr[...] = v3.astype(F32).reshape(shp)
    g_scr[...] = g.reshape(shp)

    rows = pairs * hd
    ri = lax.broadcasted_iota(jnp.int32, (rows, LANES), 0)
    li = lax.broadcasted_iota(jnp.int32, (rows, LANES), 1)
    diag = (ri % hd) == (li % hd)
    ones = ones_ref[...]

    def bcast(ref, b, t):
        rowv = ref[b, pl.ds(t, 1), :]
        return jnp.concatenate(
            [jnp.broadcast_to(rowv[:, p * LANES:(p + 1) * LANES], (hd, LANES)) for p in range(pairs)], axis=0)

    def step(t, carry):
        for b in range(nb):
            s = s_scr[b].reshape(rows, LANES)
            p1 = s * bcast(al_scr, b, t)
            p1h = p1.astype(BF16)
            p1l = (p1 - p1h.astype(F32)).astype(BF16)
            sa = _dot(p1h, ones) + _dot(p1l, ones)
            vcol = (_dot(jnp.where(diag, bcast(v1_scr, b, t), 0.0).astype(BF16), ones)
                    + _dot(jnp.where(diag, bcast(v2_scr, b, t), 0.0).astype(BF16), ones)
                    + _dot(jnp.where(diag, bcast(v3_scr, b, t), 0.0).astype(BF16), ones))
            s = s * bcast(w_scr, b, t) + sa * bcast(be_scr, b, t) + vcol * bcast(k_scr, b, t)
            s_scr[b] = s.reshape(pairs, hd, LANES)
            yrep = _dot((s * bcast(r_scr, b, t)).astype(BF16), ones)
            ysel = jnp.where(diag, yrep, 0.0).reshape(pairs, hd, LANES)
            for p in range(pairs):
                y_scr[b, p, pl.ds(t, 1), :] = jnp.sum(ysel[p], axis=0, keepdims=True)
        return carry

    lax.fori_loop(0, tc, step, 0)

    y = jnp.concatenate([y_scr[:, p].reshape(nb * tc, LANES) for p in range(pairs)], axis=1)
    seg = seg_ref[...]
    inv = np.float32(1.0 / hd)
    mean = _dot_exact_rhs(y, seg) * inv
    yc = y - mean
    var = _dot_exact_rhs(yc * yc, seg) * inv
    yn = yc * lax.rsqrt(var + GN_EPS) * lnw_ref[...] + lnb_ref[...]
    r = r_scr[...].reshape(nb * tc, hw)
    k2 = k_scr[...].reshape(nb * tc, hw)
    v = (v1_scr[...] + v2_scr[...] + v3_scr[...]).reshape(nb * tc, hw)
    bonus = _dot_exact_rhs(r * k2 * rk_ref[...], seg) * v
    ob_ref[...] = ((yn + bonus) * g_scr[...].reshape(nb * tc, hw)).reshape(shp)

    @pl.when(t_id == pl.num_programs(1) - 1)
    def _():
        sout_ref[...] = s_scr[...]
        shout_ref[...] = prev_scr[...]


def _rwkv(zb, sh0, s0, wts, nb, tc, hd):
    nseq, length, rin = zb.shape
    hw = wts["w0"].shape[1]
    pairs = hw // LANES
    lora = wts["ww"].shape[0]
    body = functools.partial(_rwkv_body, nb=nb, tc=tc, hw=hw, hd=hd, lora=lora)
    names = ("mu", "w0", "ww", "a0", "aw", "gw", "kk", "ka", "rk", "lnw", "lnb", "ones", "seg")
    consts = [wts[n] for n in names]
    tok = pltpu.VMEM((nb, tc, hw), F32)
    return pl.pallas_call(
        body,
        grid=(nseq // nb, length // tc),
        in_specs=[pl.BlockSpec((nb, tc, rin), lambda s, t: (s, t, 0)),
                  pl.BlockSpec((nb, rin), lambda s, t: (s, 0)),
                  pl.BlockSpec((nb, pairs, hd, LANES), lambda s, t: (s, 0, 0, 0))]
                 + [_const_spec(c.shape) for c in consts],
        out_specs=[pl.BlockSpec((nb, tc, hw), lambda s, t: (s, t, 0)),
                   pl.BlockSpec((nb, pairs, hd, LANES), lambda s, t: (s, 0, 0, 0)),
                   pl.BlockSpec((nb, rin), lambda s, t: (s, 0))],
        out_shape=[jax.ShapeDtypeStruct((nseq, length, hw), F32),
                   jax.ShapeDtypeStruct((nseq, pairs, hd, LANES), F32),
                   jax.ShapeDtypeStruct((nseq, rin), F32)],
        scratch_shapes=[pltpu.VMEM((nb, pairs, hd, LANES), F32), pltpu.VMEM((nb, rin), F32)]
                       + [tok] * 9 + [pltpu.VMEM((nb, pairs, tc, LANES), F32)],
        compiler_params=_params("parallel", "arbitrary"),
        name="rwkv",
    )(zb, sh0, s0, *consts)


def _pack_pairs(s):
    n, heads, hd, _ = s.shape
    return s.reshape(n, heads // 2, 2, hd, hd).transpose(0, 1, 3, 2, 4).reshape(n, heads // 2, hd, 2 * hd)


def _unpack_pairs(s):
    n, pairs, hd, _ = s.shape
    return s.reshape(n, pairs, hd, 2, hd).transpose(0, 1, 3, 2, 4).reshape(n, 2 * pairs, hd, hd)


def _out_proj_body(oa_ref, ob_ref, x_ref, woa_ref, wob_ref, g2_ref, wq_ref, keys_ref,
                   x1_ref, h2_ref, st_ref, *, nk, dh):
    mix = _dot(oa_ref[...].astype(BF16), woa_ref[...]) + _dot(ob_ref[...].astype(BF16), wob_ref[...])
    x1 = x_ref[...] + mix
    x1_ref[...] = x1
    h2 = _rms(x1, g2_ref[...]).astype(BF16)
    h2_ref[...] = h2
    q = _dot(h2, wq_ref[...]).astype(BF16)
    for hp in range(q.shape[1] // dh):
        st_ref[hp * nk:(hp + 1) * nk, :] = lax.dot_general(
            keys_ref[hp % 2], q[:, hp * dh:(hp + 1) * dh], _NT, preferred_element_type=F32)


def _out_proj(oa, ob, x, woa, wob, g2, wq, keys, tb):
    n, d = x.shape
    nk, dh = keys.shape[1], keys.shape[2]
    nrow = (wq.shape[1] // dh) * nk
    body = functools.partial(_out_proj_body, nk=nk, dh=dh)
    return pl.pallas_call(
        body,
        grid=(n // tb,),
        in_specs=[pl.BlockSpec((tb, oa.shape[1]), lambda i: (i, 0)),
                  pl.BlockSpec((tb, ob.shape[1]), lambda i: (i, 0)),
                  pl.BlockSpec((tb, d), lambda i: (i, 0)),
                  _const_spec(woa.shape), _const_spec(wob.shape), _const_spec(g2.shape),
                  _const_spec(wq.shape), _const_spec(keys.shape)],
        out_specs=[pl.BlockSpec((tb, d), lambda i: (i, 0)), pl.BlockSpec((tb, d), lambda i: (i, 0)),
                   pl.BlockSpec((nrow, tb), lambda i: (0, i))],
        out_shape=[jax.ShapeDtypeStruct((n, d), F32), jax.ShapeDtypeStruct((n, d), BF16),
                   jax.ShapeDtypeStruct((nrow, n), F32)],
        compiler_params=_params("parallel"),
        name="out_proj",
    )(oa, ob, x, woa, wob, g2, wq, keys)


def _route_body(st_ref, thr_ref, e1_ref, e2_ref, top_scr, cand_scr, *, nk, topk):
    neg = -jnp.inf

    def extract(work, n, on_value):
        for i in range(n):
            m = jnp.max(work, axis=0, keepdims=True)
            on_value(i, m)
            work = jnp.where(work == m, neg, work)

    def keep(half):
        def on_value(i, m):
            top_scr[half, i:i + 1, :] = m
        return on_value

    s1 = st_ref[0:nk, :]
    s2 = st_ref[nk:2 * nk, :]
    extract(s1, topk, keep(0))
    extract(s2, topk, keep(1))
    sv1 = top_scr[0]
    sv2 = top_scr[1]
    half = topk // 2
    cand_scr[0:topk, :] = sv1[0:1] + sv2
    for a in range(1, half):
        cand_scr[topk + (a - 1) * half:topk + a * half, :] = sv1[a:a + 1] + sv2[0:half]
    cand_scr[topk + (half - 1) * half:, :] = sv1[half:topk] + sv2[0:1]

    best0 = sv1[0:1] + sv2[0:1]
    acc = {"z": jnp.zeros_like(best0), "m": best0}

    def on_best(i, m):
        acc["z"] = acc["z"] + jnp.exp(m - best0)
        acc["m"] = m

    extract(cand_scr[...], topk, on_best)
    thr_ref[0] = acc["m"]
    e1_ref[0] = jnp.exp(s1 - sv1[0:1]) / acc["z"]
    e2_ref[0] = jnp.exp(s2 - sv2[0:1])


def _route(st, heads, nk, tb):
    n = st.shape[1]
    topk = PEER_TOPK
    half = topk // 2
    ncand = topk + (half - 1) * half + half
    body = functools.partial(_route_body, nk=nk, topk=topk)
    return pl.pallas_call(
        body,
        grid=(n // tb, heads),
        in_specs=[pl.BlockSpec((2 * nk, tb), lambda i, h: (h, i))],
        out_specs=[pl.BlockSpec((1, 1, tb), lambda i, h: (h, 0, i)),
                   pl.BlockSpec((1, nk, tb), lambda i, h: (h, 0, i)),
                   pl.BlockSpec((1, nk, tb), lambda i, h: (h, 0, i))],
        out_shape=[jax.ShapeDtypeStruct((heads, 1, n), F32), jax.ShapeDtypeStruct((heads, nk, n), F32),
                   jax.ShapeDtypeStruct((heads, nk, n), F32)],
        scratch_shapes=[pltpu.VMEM((2, topk, tb), F32), pltpu.VMEM((ncand, tb), F32)],
        compiler_params=_params("parallel", "parallel"),
        name="route",
    )(st)


def _peer_body(h2_ref, st_ref, e1_ref, e2_ref, thr_ref, u_ref, vt_ref, x1_ref, gf_ref, y_ref,
               acc_scr, coef_scr, *, heads, nk, tb):
    eb = pl.program_id(1)
    i1_per_block = u_ref.shape[0] // nk

    @pl.when(eb == 0)
    def _():
        acc_scr[...] = jnp.zeros_like(acc_scr)

    act = lax.dot_general(u_ref[...], h2_ref[...], _NT, preferred_element_type=F32)
    i1_0 = pl.multiple_of(eb * i1_per_block, SUBLANES)
    for lg in range(tb // LANES):
        ls = slice(lg * LANES, (lg + 1) * LANES)
        s1b = [st_ref[pl.ds(2 * h * nk + i1_0, i1_per_block), ls] for h in range(heads)]
        e1b = [e1_ref[h, pl.ds(i1_0, i1_per_block), ls] for h in range(heads)]
        for j in range(i1_per_block):
            gate = jnp.zeros((nk, LANES), F32)
            for h in range(heads):
                s2 = st_ref[(2 * h + 1) * nk:(2 * h + 2) * nk, ls]
                csum = s1b[h][j:j + 1] + s2
                gate = gate + jnp.where(csum >= thr_ref[h, :, ls], e1b[h][j:j + 1] * e2_ref[h, :, ls], 0.0)
            a = act[j * nk:(j + 1) * nk, ls]
            gelu = 0.5 * a * (1.0 + lax.erf(a * np.float32(np.sqrt(0.5))))
            coef_scr[j * nk:(j + 1) * nk, ls] = (gate * gelu).astype(BF16)
    acc_scr[...] += _dot(vt_ref[...], coef_scr[...])

    @pl.when(eb == pl.num_programs(1) - 1)
    def _():
        y_ref[...] = _rms(x1_ref[...] + acc_scr[...].T, gf_ref[...])


def _peer(h2, st, e1, e2, thr, u, vt, x1, gf, heads, nk, tb):
    n, d = x1.shape
    ne = u.shape[0]
    eblk = SUBLANES * nk
    body = functools.partial(_peer_body, heads=heads, nk=nk, tb=tb)
    return pl.pallas_call(
        body,
        grid=(n // tb, ne // eblk),
        in_specs=[pl.BlockSpec((tb, d), lambda i, e: (i, 0)),
                  pl.BlockSpec((st.shape[0], tb), lambda i, e: (0, i)),
                  pl.BlockSpec((heads, nk, tb), lambda i, e: (0, 0, i)),
                  pl.BlockSpec((heads, nk, tb), lambda i, e: (0, 0, i)),
                  pl.BlockSpec((heads, 1, tb), lambda i, e: (0, 0, i)),
                  pl.BlockSpec((eblk, d), lambda i, e: (e, 0)),
                  pl.BlockSpec((d, eblk), lambda i, e: (0, e)),
                  pl.BlockSpec((tb, d), lambda i, e: (i, 0)),
                  _const_spec(gf.shape)],
        out_specs=pl.BlockSpec((tb, d), lambda i, e: (i, 0)),
        out_shape=jax.ShapeDtypeStruct((n, d), F32),
        scratch_shapes=[pltpu.VMEM((d, tb), F32), pltpu.VMEM((eblk, tb), BF16)],
        compiler_params=_params("parallel", "arbitrary"),
        name="peer",
    )(h2, st, e1, e2, thr, u, vt, x1, gf)


def _block(n, pref):
    b = pref
    while n % b:
        b //= 2
    return b


def kernel(x_prompt, x_sample, state_hgrn, state_rwkv, state_shift, meta_tokens, norm1, w_in, lb_theta, hgrn_norm, mu_shift, w0, w_w2, a0, a_w2, g_w2, k_k, k_a, r_k, ln_w, ln_b, w_out, norm2, peer_wq, peer_keys, peer_u, peer_v, norm_f):
    assert w_in.shape[0] == 1, "single-layer trunk"
    bp, sp, d = x_prompt.shape
    bs, ss, _ = x_sample.shape
    n_meta = meta_tokens.shape[0]
    _, _, ha_heads, dk, dv = state_hgrn.shape
    _, _, hb_heads, hd, _ = state_rwkv.shape
    ha_in = 2 * ha_heads * dk + 2 * ha_heads * dv
    hw = hb_heads * hd
    lw, la = w_w2.shape[1], a_w2.shape[1]
    nk, dh = peer_keys.shape[2], peer_keys.shape[3]
    p_heads = peer_wq.shape[2] // (2 * dh)
    assert sp % HGRN_CHUNK == 0 and ss <= HGRN_CHUNK and n_meta <= HGRN_CHUNK
    assert hb_heads % 2 == 0 and 2 * hd == LANES and lw + la == LANES and nk == LANES

    row = lambda t: t.reshape(1, -1).astype(F32)
    w_a = w_in[0, :, :ha_in].astype(BF16)
    w_b = w_in[0, :, ha_in:].astype(BF16)
    g1 = row(norm1[0])
    lbt = lb_theta.astype(F32)
    gn = row(hgrn_norm[0])

    head_of_lane = np.arange(LANES) // hd
    head_of_col = np.arange(hw) // hd
    rw = {
        "mu": row(mu_shift[0]), "w0": row(w0[0]), "a0": row(a0[0]), "kk": row(k_k[0]), "ka": row(k_a[0]),
        "rk": row(r_k[0]), "lnw": row(ln_w[0]), "lnb": row(ln_b[0]), "gw": g_w2[0].astype(F32),
        "ww": jnp.concatenate([w_w2[0], jnp.zeros((la, hw), F32)], axis=0),
        "aw": jnp.concatenate([jnp.zeros((lw, hw), F32), a_w2[0]], axis=0),
        "ones": jnp.asarray(head_of_lane[:, None] == head_of_lane[None, :], BF16),
        "seg": jnp.asarray(head_of_col[:, None] == head_of_col[None, :], BF16),
    }
    woa = w_out[0, :ha_heads * dv].astype(BF16)
    wob = w_out[0, ha_heads * dv:].astype(BF16)
    g2 = row(norm2[0])
    wq = peer_wq[0].astype(BF16)
    keys = peer_keys[0].astype(BF16)
    u = peer_u[0].astype(BF16)
    vt = peer_v[0].astype(BF16).T
    gf = row(norm_f)

    za_m, zb_m = _in_proj(meta_tokens.astype(F32), g1, w_a, w_b, n_meta)
    _, hs_m = _hgrn(za_m, 0, 1, 1, n_meta, lbt, gn, jnp.zeros((1, ha_heads, dk, dv), F32), ha_heads, dk, dv)
    _, rs_m, sh_m = _rwkv(zb_m.reshape(1, n_meta, -1), jnp.zeros((1, zb_m.shape[1]), F32),
                          jnp.zeros((1, hb_heads // 2, hd, LANES), F32), rw, 1, n_meta, hd)

    def group(x, nseq, length, hs0, rs0, sh0, chunk_rows, tc):
        xf = x.reshape(nseq * length, d)
        n = xf.shape[0]
        tb = _block(n, 256)
        assert tb % LANES == 0
        za, zb = _in_proj(xf, g1, w_a, w_b, tb)
        oa, hs = _hgrn(za, 0, nseq, length // chunk_rows, chunk_rows, lbt, gn, hs0, ha_heads, dk, dv)
        ob, rs, sh = _rwkv(zb.reshape(nseq, length, -1), sh0, rs0, rw, 8, tc, hd)
        x1, h2, st = _out_proj(oa, ob.reshape(n, hw), xf, woa, wob, g2, wq, keys, tb)
        thr, e1, e2 = _route(st, p_heads, nk, tb)
        y = _peer(h2, st, e1, e2, thr, u, vt, x1, gf, p_heads, nk, tb)
        return y.reshape(nseq, length, d), hs, _unpack_pairs(rs), sh

    y_p, hs_p, rs_p, sh_p = group(
        x_prompt, bp, sp, hs_m, jnp.broadcast_to(rs_m, (bp,) + rs_m.shape[1:]),
        jnp.broadcast_to(sh_m, (bp, sh_m.shape[1])), HGRN_CHUNK, 64)
    y_s, hs_s, rs_s, sh_s = group(
        x_sample, bs, ss, state_hgrn[0].astype(F32), _pack_pairs(state_rwkv[0].astype(F32)),
        state_shift[0].astype(F32), ss, ss)

    return (y_p, y_s, hs_p[None], rs_p[None], sh_p[None], hs_s[None], rs_s[None], sh_s[None])
```

```python
import functools

import numpy as np
import jax
import jax.numpy as jnp
from jax import lax
from jax.experimental import pallas as pl
from jax.experimental.pallas import tpu as pltpu

F32 = jnp.float32
BF16 = jnp.bfloat16

EPS = 1e-6
GN_EPS = 64e-5
HGRN_CHUNK = 64
PEER_TOPK = 16
RWKV_SLOT = 32
LANES = 128
SUBLANES = 8
VMEM_LIMIT_BYTES = 52 * 1024 * 1024

_NT = (((1,), (1,)), ((), ()))
_TN = (((0,), (0,)), ((), ()))


def _dot(a, b):
    return jnp.dot(a, b, preferred_element_type=F32)


def _split3(a):
    a1 = a.astype(BF16)
    r1 = a - a1.astype(F32)
    a2 = r1.astype(BF16)
    a3 = (r1 - a2.astype(F32)).astype(BF16)
    return a1, a2, a3


def _dot_exact_rhs(a, b_exact):
    a1, a2, a3 = _split3(a)
    return _dot(a1, b_exact) + _dot(a2, b_exact) + _dot(a3, b_exact)


def _dot_exact_lhs(a_exact, b):
    b1, b2, b3 = _split3(b)
    return _dot(a_exact, b1) + _dot(a_exact, b2) + _dot(a_exact, b3)


def _dot_hi(a, b):
    ah = a.astype(BF16)
    al = (a - ah.astype(F32)).astype(BF16)
    bh = b.astype(BF16)
    bl = (b - bh.astype(F32)).astype(BF16)
    return _dot(ah, bh) + _dot(ah, bl) + _dot(al, bh)


def _sigmoid(x):
    return 1.0 / (1.0 + jnp.exp(-x))


def _rms(x, g):
    return x * lax.rsqrt(jnp.mean(x * x, axis=-1, keepdims=True) + EPS) * g


def _params(*sem):
    return pltpu.CompilerParams(dimension_semantics=sem, vmem_limit_bytes=VMEM_LIMIT_BYTES)


def _const_spec(shape):
    nd = len(shape)
    return pl.BlockSpec(shape, lambda *_: (0,) * nd)


def _in_proj_body(x_ref, g_ref, wa_ref, wb_ref, za_ref, zb_ref):
    hb = _rms(x_ref[...], g_ref[...]).astype(BF16)
    za_ref[...] = _dot(hb, wa_ref[...])
    zb_ref[...] = _dot(hb, wb_ref[...])


def _in_proj(x, g, wa, wb, tb):
    n, d = x.shape
    na, nb = wa.shape[1], wb.shape[1]
    return pl.pallas_call(
        _in_proj_body,
        grid=(n // tb,),
        in_specs=[pl.BlockSpec((tb, d), lambda i: (i, 0)), _const_spec((1, d)),
                  _const_spec(wa.shape), _const_spec(wb.shape)],
        out_specs=[pl.BlockSpec((tb, na), lambda i: (i, 0)), pl.BlockSpec((tb, nb), lambda i: (i, 0))],
        out_shape=[jax.ShapeDtypeStruct((n, na), F32), jax.ShapeDtypeStruct((n, nb), F32)],
        compiler_params=_params("parallel"),
        name="in_proj",
    )(x, g, wa, wb)


def _hgrn_tables(c):
    nl = int(np.log2(c)) + 1
    mats = np.zeros((nl, c, c), np.float32)
    masks = np.zeros((nl, c, c), np.float32)
    idx = np.arange(c)
    mats[0] = (idx[None, :] <= idx[:, None])
    masks[0] = np.eye(c)
    for l in range(1, nl):
        blk = 1 << l
        half = blk >> 1
        for t in range(c):
            start = (t // blk) * blk
            ref = start + half - 1
            if t - start >= half:
                mats[l, t, ref + 1:t + 1] = 1.0
                masks[l, t, start:start + half] = 1.0
            else:
                mats[l, t, t + 1:ref + 1] = 1.0
    return mats.reshape(nl * c, c), masks


def _hgrn_body(za_ref, lbt_ref, gn_ref, s0_ref, mats_ref, masks_ref, oa_ref, sout_ref, st_scr,
               *, rows, chunk, heads, dk, dv):
    c_id = pl.program_id(1)
    qk = heads * dk
    wd = heads * dv
    nl = masks_ref.shape[0]

    @pl.when(c_id == 0)
    def _():
        for h in range(heads):
            st_scr[h] = s0_ref[0, h].T

    th = lbt_ref[...]
    ex = jnp.exp(th - jnp.max(th, axis=0, keepdims=True))
    lb = ex[0:1] / jnp.sum(ex, axis=0, keepdims=True)

    za = za_ref[...]
    zq = za[:, :qk]
    zf = za[:, qk:2 * qk]
    zi = za[:, 2 * qk:2 * qk + wd]
    zo = za[:, 2 * qk + wd:]
    logf = jnp.log(lb + (1.0 - lb) * _sigmoid(zf))
    kin = (1.0 - lb) * _sigmoid(-zf)
    qa = zq * _sigmoid(zq)
    if rows < chunk:
        pad = jnp.zeros((chunk - rows, qk), F32)
        logf = jnp.concatenate([logf, pad], axis=0)
        kin = jnp.concatenate([kin, pad], axis=0)
        qa = jnp.concatenate([qa, pad], axis=0)
        zi = jnp.concatenate([zi, jnp.zeros((chunk - rows, wd), F32)], axis=0)

    dec = _dot_exact_lhs(mats_ref[...], logf)
    gn = gn_ref[...]
    for h in range(heads):
        ks = slice(h * dk, (h + 1) * dk)
        vs = slice(h * dv, (h + 1) * dv)
        q = qa[:, ks]
        k = kin[:, ks]
        v = zi[:, vs].astype(BF16)
        cum = dec[0:chunk, ks]
        sc = lax.dot_general(q.astype(BF16), k.astype(BF16), _NT, preferred_element_type=F32)
        scores = jnp.where(masks_ref[0] != 0.0, sc, 0.0)
        for l in range(1, nl):
            e = jnp.exp(dec[l * chunk:(l + 1) * chunk, ks])
            sc = lax.dot_general((q * e).astype(BF16), (k * e).astype(BF16), _NT, preferred_element_type=F32)
            scores = scores + jnp.where(masks_ref[l] != 0.0, sc, 0.0)
        st = st_scr[h]
        o = _dot(scores.astype(BF16), v) + lax.dot_general(
            (q * jnp.exp(cum)).astype(BF16), st.astype(BF16), _NT, preferred_element_type=F32)
        last = cum[chunk - 1:chunk, :]
        kd = (k * jnp.exp(last - cum)).astype(BF16)
        st_scr[h] = jnp.exp(last) * st + lax.dot_general(v, kd, _TN, preferred_element_type=F32)

        o = o[:rows]
        o = o * lax.rsqrt(jnp.mean(o * o, axis=-1, keepdims=True) + EPS) * gn[:, vs]
        zoh = zo[:, vs]
        oa_ref[:, vs] = o * (zoh * _sigmoid(zoh))

    @pl.when(c_id == pl.num_programs(1) - 1)
    def _():
        for h in range(heads):
            sout_ref[0, h] = st_scr[h].T


def _hgrn(za, row_block0, nseq, nchunk, rows, lb_theta, gn, s0, heads, dk, dv):
    chunk = HGRN_CHUNK
    mats, masks = _hgrn_tables(chunk)
    mats = jnp.asarray(mats, BF16)
    masks = jnp.asarray(masks, F32)
    n = nseq * nchunk * rows
    wd = heads * dv
    s0_map = (lambda s, c: (0, 0, 0, 0)) if s0.shape[0] == 1 else (lambda s, c: (s, 0, 0, 0))
    body = functools.partial(_hgrn_body, rows=rows, chunk=chunk, heads=heads, dk=dk, dv=dv)
    return pl.pallas_call(
        body,
        grid=(nseq, nchunk),
        in_specs=[pl.BlockSpec((rows, za.shape[1]), lambda s, c: (row_block0 + s * nchunk + c, 0)),
                  _const_spec(lb_theta.shape), _const_spec(gn.shape),
                  pl.BlockSpec((1, heads, dk, dv), s0_map),
                  _const_spec(mats.shape), _const_spec(masks.shape)],
        out_specs=[pl.BlockSpec((rows, wd), lambda s, c: (s * nchunk + c, 0)),
                   pl.BlockSpec((1, heads, dk, dv), lambda s, c: (s, 0, 0, 0))],
        out_shape=[jax.ShapeDtypeStruct((n, wd), F32), jax.ShapeDtypeStruct((nseq, heads, dk, dv), F32)],
        scratch_shapes=[pltpu.VMEM((heads, dv, dk), F32)],
        compiler_params=_params("parallel", "arbitrary"),
        name="hgrn",
    )(za, lb_theta, gn, s0, mats, masks)


def _rwkv_body(zb_ref, sh0_ref, s0_ref, mu_ref, w0_ref, ww_ref, a0_ref, aw_ref, gw_ref, kk_ref, ka_ref,
               rk_ref, lnw_ref, lnb_ref, ones_ref, seg_ref,
               ob_ref, sout_ref, shout_ref,
               s_scr, prev_scr, al_scr, w_scr, be_scr, k_scr, r_scr, v_scr, g_scr, y_scr, vt_scr,
               *, nb, tc, hw, hd, lora):
    t_id = pl.program_id(1)
    pairs = hw // LANES
    rin = zb_ref.shape[-1]

    @pl.when(t_id == 0)
    def _():
        s_scr[...] = s0_ref[...]
        prev_scr[...] = sh0_ref[...]

    row = lax.broadcasted_iota(jnp.int32, (tc, rin), 0)
    mu = mu_ref[...]
    zms = []
    for b in range(nb):
        zb = zb_ref[b]
        prev = jnp.where(row == 0, prev_scr[b:b + 1, :], pltpu.roll(zb, 1, axis=0))
        prev_scr[b:b + 1, :] = zb[tc - 1:tc, :]
        zms.append(zb + (prev - zb) * mu)
    zm = jnp.concatenate(zms, axis=0)
    r = zm[:, :hw]
    k = zm[:, hw:2 * hw]
    v = zm[:, 2 * hw:3 * hw]
    dwa = zm[:, 3 * hw:3 * hw + lora]
    dg = zm[:, 3 * hw + lora:]
    w_raw = w0_ref[...] + _dot_hi(jnp.tanh(dwa), ww_ref[...])
    decay = jnp.exp(-np.float32(np.exp(-0.5)) * _sigmoid(w_raw))
    a = _sigmoid(a0_ref[...] + _dot_hi(dwa, aw_ref[...]))
    g = _dot_hi(_sigmoid(dg), gw_ref[...])
    kk = k * kk_ref[...]
    ss = _dot_exact_rhs(kk * kk, seg_ref[...])
    kk = kk / jnp.maximum(jnp.sqrt(ss), 1e-12)
    k2 = k * (1.0 + (a - 1.0) * ka_ref[...])
    shp = (nb, tc, hw)
    al_scr[...] = (-kk).reshape(shp)
    w_scr[...] = decay.reshape(shp)
    be_scr[...] = (kk * a).reshape(shp)
    k_scr[...] = k2.reshape(shp)
    r_scr[...] = r.reshape(shp)
    v_scr[...] = v.reshape(shp)
    g_scr[...] = g.reshape(shp)

    slot = RWKV_SLOT
    li2 = lax.broadcasted_iota(jnp.int32, (slot, LANES), 1)
    vi = lax.broadcasted_iota(jnp.int32, (hd, LANES), 0)
    sel = jnp.where(vi == lax.broadcasted_iota(jnp.int32, (hd, LANES), 1) % hd, 1.0, 0.0).astype(BF16)
    for b in range(nb):
        for p in range(pairs):
            for g0 in range(0, tc, slot):
                nt = min(slot, tc - g0)
                vb = v[b * tc + g0:b * tc + g0 + nt, p * LANES:(p + 1) * LANES]
                if nt < slot:
                    vb = jnp.concatenate([vb, jnp.zeros((slot - nt, LANES), F32)], axis=0)
                vh = vb.astype(BF16)
                vl = (vb - vh.astype(F32)).astype(BF16)
                zero = jnp.zeros_like(vh)
                stack = jnp.concatenate([jnp.where(li2 < hd, vh, zero), jnp.where(li2 >= hd, vh, zero),
                                         jnp.where(li2 < hd, vl, zero), jnp.where(li2 >= hd, vl, zero)], axis=0)
                vt_scr[g0 // slot, b, p] = lax.dot_general(
                    sel, stack, _NT, preferred_element_type=F32).astype(BF16)

    rows = pairs * hd
    ri = lax.broadcasted_iota(jnp.int32, (rows, LANES), 0)
    li = lax.broadcasted_iota(jnp.int32, (rows, LANES), 1)
    diag = (ri % hd) == (li % hd)
    ones2 = ones_ref[...]
    ones1 = ones_ref[0:LANES, :]
    qi = lax.broadcasted_iota(jnp.int32, (4 * slot, LANES), 0)
    ql = lax.broadcasted_iota(jnp.int32, (4 * slot, LANES), 1)
    tok_of_row = jnp.where((qi // slot) % 2 == ql // hd, qi % slot, -1)

    def bcast(ref, b, t):
        rowv = ref[b, pl.ds(t, 1), :]
        return jnp.concatenate(
            [jnp.broadcast_to(rowv[:, p * LANES:(p + 1) * LANES], (hd, LANES)) for p in range(pairs)], axis=0)

    def hilo(x):
        xh = x.astype(BF16)
        return jnp.concatenate([xh, (x - xh.astype(F32)).astype(BF16)], axis=1)

    def step(t, carry):
        pick = jnp.where(tok_of_row == t % slot, 1.0, 0.0).astype(BF16)
        seqs = range(nb)
        vcols = [_dot(vt_scr[t // slot, b].reshape(rows, 4 * slot), pick) for b in seqs]
        olds = [s_scr[b].reshape(rows, LANES) for b in seqs]
        sas = [_dot(hilo(olds[b] * bcast(al_scr, b, t)), ones2) for b in seqs]
        news = [olds[b] * bcast(w_scr, b, t) + sas[b] * bcast(be_scr, b, t) + vcols[b] * bcast(k_scr, b, t)
                for b in seqs]
        for b in seqs:
            s_scr[b] = news[b].reshape(pairs, hd, LANES)
        yreps = [_dot((news[b] * bcast(r_scr, b, t)).astype(BF16), ones1) for b in seqs]
        for b in seqs:
            ysel = jnp.where(diag, yreps[b], 0.0).reshape(pairs, hd, LANES)
            for p in range(pairs):
                y_scr[b, p, pl.ds(t, 1), :] = jnp.sum(ysel[p], axis=0, keepdims=True)
        return carry

    lax.fori_loop(0, tc, step, 0, unroll=2)

    y = jnp.concatenate([y_scr[:, p].reshape(nb * tc, LANES) for p in range(pairs)], axis=1)
    seg = seg_ref[...]
    inv = np.float32(1.0 / hd)
    mean = _dot_exact_rhs(y, seg) * inv
    yc = y - mean
    var = _dot_exact_rhs(yc * yc, seg) * inv
    yn = yc * lax.rsqrt(var + GN_EPS) * lnw_ref[...] + lnb_ref[...]
    r = r_scr[...].reshape(nb * tc, hw)
    k2 = k_scr[...].reshape(nb * tc, hw)
    v = v_scr[...].reshape(nb * tc, hw)
    bonus = _dot_exact_rhs(r * k2 * rk_ref[...], seg) * v
    ob_ref[...] = ((yn + bonus) * g_scr[...].reshape(nb * tc, hw)).reshape(shp)

    @pl.when(t_id == pl.num_programs(1) - 1)
    def _():
        sout_ref[...] = s_scr[...]
        shout_ref[...] = prev_scr[...]


def _rwkv(zb, sh0, s0, wts, nb, tc, hd):
    nseq, length, rin = zb.shape
    hw = wts["w0"].shape[1]
    pairs = hw // LANES
    lora = wts["ww"].shape[0]
    body = functools.partial(_rwkv_body, nb=nb, tc=tc, hw=hw, hd=hd, lora=lora)
    names = ("mu", "w0", "ww", "a0", "aw", "gw", "kk", "ka", "rk", "lnw", "lnb", "ones", "seg")
    consts = [wts[n] for n in names]
    tok = pltpu.VMEM((nb, tc, hw), F32)
    return pl.pallas_call(
        body,
        grid=(nseq // nb, length // tc),
        in_specs=[pl.BlockSpec((nb, tc, rin), lambda s, t: (s, t, 0)),
                  pl.BlockSpec((nb, rin), lambda s, t: (s, 0)),
                  pl.BlockSpec((nb, pairs, hd, LANES), lambda s, t: (s, 0, 0, 0))]
                 + [_const_spec(c.shape) for c in consts],
        out_specs=[pl.BlockSpec((nb, tc, hw), lambda s, t: (s, t, 0)),
                   pl.BlockSpec((nb, pairs, hd, LANES), lambda s, t: (s, 0, 0, 0)),
                   pl.BlockSpec((nb, rin), lambda s, t: (s, 0))],
        out_shape=[jax.ShapeDtypeStruct((nseq, length, hw), F32),
                   jax.ShapeDtypeStruct((nseq, pairs, hd, LANES), F32),
                   jax.ShapeDtypeStruct((nseq, rin), F32)],
        scratch_shapes=[pltpu.VMEM((nb, pairs, hd, LANES), F32), pltpu.VMEM((nb, rin), F32)]
                       + [tok] * 7 + [pltpu.VMEM((nb, pairs, tc, LANES), F32),
                                      pltpu.VMEM((-(-tc // RWKV_SLOT), nb, pairs, hd, 4 * RWKV_SLOT), BF16)],
        compiler_params=_params("parallel", "arbitrary"),
        name="rwkv",
    )(zb, sh0, s0, *consts)


def _pack_pairs(s):
    n, heads, hd, _ = s.shape
    return s.reshape(n, heads // 2, 2, hd, hd).transpose(0, 1, 3, 2, 4).reshape(n, heads // 2, hd, 2 * hd)


def _unpack_pairs(s):
    n, pairs, hd, _ = s.shape
    return s.reshape(n, pairs, hd, 2, hd).transpose(0, 1, 3, 2, 4).reshape(n, 2 * pairs, hd, hd)


def _out_proj_body(oa_ref, ob_ref, x_ref, woa_ref, wob_ref, g2_ref, wq_ref, keys_ref,
                   x1_ref, h2t_ref, st_ref, *, nk, dh):
    mix = _dot(oa_ref[...].astype(BF16), woa_ref[...]) + _dot(ob_ref[...].astype(BF16), wob_ref[...])
    x1 = x_ref[...] + mix
    x1_ref[...] = x1
    h2f = _rms(x1, g2_ref[...])
    h2t_ref[...] = h2f.T.astype(BF16)
    q = _dot(h2f.astype(BF16), wq_ref[...]).astype(BF16)
    for hp in range(q.shape[1] // dh):
        st_ref[hp * nk:(hp + 1) * nk, :] = lax.dot_general(
            keys_ref[hp % 2], q[:, hp * dh:(hp + 1) * dh], _NT, preferred_element_type=F32)


def _out_proj(oa, ob, x, woa, wob, g2, wq, keys, tb):
    n, d = x.shape
    nk, dh = keys.shape[1], keys.shape[2]
    nrow = (wq.shape[1] // dh) * nk
    body = functools.partial(_out_proj_body, nk=nk, dh=dh)
    return pl.pallas_call(
        body,
        grid=(n // tb,),
        in_specs=[pl.BlockSpec((tb, oa.shape[1]), lambda i: (i, 0)),
                  pl.BlockSpec((tb, ob.shape[1]), lambda i: (i, 0)),
                  pl.BlockSpec((tb, d), lambda i: (i, 0)),
                  _const_spec(woa.shape), _const_spec(wob.shape), _const_spec(g2.shape),
                  _const_spec(wq.shape), _const_spec(keys.shape)],
        out_specs=[pl.BlockSpec((tb, d), lambda i: (i, 0)), pl.BlockSpec((d, tb), lambda i: (0, i)),
                   pl.BlockSpec((nrow, tb), lambda i: (0, i))],
        out_shape=[jax.ShapeDtypeStruct((n, d), F32), jax.ShapeDtypeStruct((d, n), BF16),
                   jax.ShapeDtypeStruct((nrow, n), F32)],
        compiler_params=_params("parallel"),
        name="out_proj",
    )(oa, ob, x, woa, wob, g2, wq, keys)


def _route_body(st_ref, t1_ref, e1_ref, e2_ref, top_scr, cand_scr, *, nk, topk):
    neg = -jnp.inf
    nx = topk + 1
    half = topk // 2

    def extract(work, n, on_value):
        for i in range(n):
            m = jnp.max(work, axis=0, keepdims=True)
            on_value(i, m)
            work = jnp.where(work == m, neg, work)

    def keep(h):
        def on_value(i, m):
            top_scr[h, i:i + 1, :] = m
        return on_value

    s1 = st_ref[0:nk, :]
    s2 = st_ref[nk:2 * nk, :]
    extract(s1, nx, keep(0))
    extract(s2, nx, keep(1))
    cand_scr[...] = jnp.full(cand_scr.shape, neg, F32)
    sv2 = top_scr[1, 0:nx, :]
    cand_scr[0:nx, :] = top_scr[0, 0:1, :] + sv2
    base = 3 * SUBLANES
    for a in range(1, half):
        cand_scr[base + (a - 1) * half:base + a * half, :] = top_scr[0, a:a + 1, :] + sv2[0:half]
    base += (half - 1) * half
    cand_scr[base:base + nx - half, :] = top_scr[0, half:nx, :] + sv2[0:1]

    best0 = top_scr[0, 0:1, :] + sv2[0:1]
    acc = {"z": jnp.zeros_like(best0), "lo": best0, "hi": best0}

    def on_best(i, m):
        if i < topk:
            acc["z"] = acc["z"] + jnp.exp(m - best0)
            acc["hi"] = m
        acc["lo"] = m

    extract(cand_scr[...], nx, on_best)
    thr = jnp.where(acc["lo"] == neg, acc["hi"], 0.5 * (acc["hi"] + acc["lo"]))
    t1_ref[0] = thr - s1
    e1_ref[0] = jnp.exp(s1 - top_scr[0, 0:1, :]) / acc["z"]
    e2_ref[0] = jnp.exp(s2 - sv2[0:1])


def _route(st, heads, nk, tb):
    n = st.shape[1]
    topk = PEER_TOPK
    half = topk // 2
    ncand = 3 * SUBLANES + (half - 1) * half + 2 * SUBLANES
    assert topk + 1 <= 3 * SUBLANES and topk + 1 - half <= 2 * SUBLANES and half == SUBLANES
    body = functools.partial(_route_body, nk=nk, topk=topk)
    blk = pl.BlockSpec((1, nk, tb), lambda i, h: (h, 0, i))
    return pl.pallas_call(
        body,
        grid=(n // tb, heads),
        in_specs=[pl.BlockSpec((2 * nk, tb), lambda i, h: (h, i))],
        out_specs=[blk, blk, blk],
        out_shape=[jax.ShapeDtypeStruct((heads, nk, n), F32)] * 3,
        scratch_shapes=[pltpu.VMEM((2, 3 * SUBLANES, tb), F32), pltpu.VMEM((ncand, tb), F32)],
        compiler_params=_params("parallel", "parallel"),
        name="route",
    )(st)


PEER_SUB = 2 * LANES
PEER_ROW_SPLIT = 2


def _peer_body(h2t_ref, s2_ref, t1_ref, e1_ref, e2_ref, u_ref, vt_ref, x1_ref, gf_ref, y_ref,
               acc_scr, coef_scr, *, heads, nk, tb, n_eb):
    e = pl.program_id(1)
    eblk = u_ref.shape[0]
    i1_per_block = eblk // nk
    i1_per_sub = PEER_SUB // nk
    cur = e % 2

    @pl.when(e == 0)
    def _():
        acc_scr[...] = jnp.zeros_like(acc_scr)
        coef_scr[1] = jnp.zeros(coef_scr.shape[1:], BF16)

    i1_0 = pl.multiple_of(jnp.minimum(e, n_eb - 1) * i1_per_block, SUBLANES)
    n_sub = eblk // PEER_SUB
    h2t = h2t_ref[...]

    def matmuls(k):
        ks = slice(k * PEER_SUB, (k + 1) * PEER_SUB)
        return _dot(u_ref[ks, :], h2t), _dot(vt_ref[:, ks], coef_scr[1 - cur, ks, :])

    act, out = matmuls(0)
    for k in range(n_sub):
        if k + 1 < n_sub:
            act_next, part = matmuls(k + 1)
            out = out + part
        for lg in range(tb // LANES):
            ls = slice(lg * LANES, (lg + 1) * LANES)
            t1b = [t1_ref[h, pl.ds(i1_0, i1_per_block), ls] for h in range(heads)]
            e1b = [e1_ref[h, pl.ds(i1_0, i1_per_block), ls] for h in range(heads)]
            for rs in range(PEER_ROW_SPLIT):
                rsl = slice(rs * nk // PEER_ROW_SPLIT, (rs + 1) * nk // PEER_ROW_SPLIT)
                gates = [None] * i1_per_sub
                for h in range(heads):
                    s2 = s2_ref[h, rsl, ls]
                    e2 = e2_ref[h, rsl, ls]
                    for jj in range(i1_per_sub):
                        j = k * i1_per_sub + jj
                        term = jnp.where(s2 >= t1b[h][j:j + 1], e1b[h][j:j + 1] * e2, 0.0)
                        gates[jj] = term if gates[jj] is None else gates[jj] + term
                for jj in range(i1_per_sub):
                    a = act[jj * nk + rsl.start:jj * nk + rsl.stop, ls]
                    gelu = 0.5 * a * (1.0 + lax.erf(a * np.float32(np.sqrt(0.5))))
                    r0 = (k * i1_per_sub + jj) * nk
                    coef_scr[cur, r0 + rsl.start:r0 + rsl.stop, ls] = (gates[jj] * gelu).astype(BF16)
        if k + 1 < n_sub:
            act = act_next
    acc_scr[...] += out

    @pl.when(e == n_eb)
    def _():
        y_ref[...] = _rms(x1_ref[...] + acc_scr[...].T, gf_ref[...])


def _peer(h2t, st, t1, e1, e2, u, vt, x1, gf, heads, nk, tb):
    n, d = x1.shape
    ne = u.shape[0]
    eblk = SUBLANES * nk
    n_eb = ne // eblk
    body = functools.partial(_peer_body, heads=heads, nk=nk, tb=tb, n_eb=n_eb)
    tok = pl.BlockSpec((heads, nk, tb), lambda i, e: (0, 0, i))
    return pl.pallas_call(
        body,
        grid=(n // tb, n_eb + 1),
        in_specs=[pl.BlockSpec((d, tb), lambda i, e: (0, i)),
                  pl.BlockSpec((heads, None, nk, tb), lambda i, e: (0, 1, 0, i)),
                  tok, tok, tok,
                  pl.BlockSpec((eblk, d), lambda i, e: (jnp.minimum(e, n_eb - 1), 0)),
                  pl.BlockSpec((d, eblk), lambda i, e: (0, jnp.maximum(e - 1, 0))),
                  pl.BlockSpec((tb, d), lambda i, e: (i, 0)),
                  _const_spec(gf.shape)],
        out_specs=pl.BlockSpec((tb, d), lambda i, e: (i, 0)),
        out_shape=jax.ShapeDtypeStruct((n, d), F32),
        scratch_shapes=[pltpu.VMEM((d, tb), F32), pltpu.VMEM((2, eblk, tb), BF16)],
        compiler_params=_params("parallel", "arbitrary"),
        name="peer",
    )(h2t, st.reshape(heads, 2, nk, n), t1, e1, e2, u, vt, x1, gf)


def _block(n, pref):
    b = pref
    while n % b:
        b //= 2
    return b


def kernel(x_prompt, x_sample, state_hgrn, state_rwkv, state_shift, meta_tokens, norm1, w_in, lb_theta, hgrn_norm, mu_shift, w0, w_w2, a0, a_w2, g_w2, k_k, k_a, r_k, ln_w, ln_b, w_out, norm2, peer_wq, peer_keys, peer_u, peer_v, norm_f):
    assert w_in.shape[0] == 1, "single-layer trunk"
    bp, sp, d = x_prompt.shape
    bs, ss, _ = x_sample.shape
    n_meta = meta_tokens.shape[0]
    _, _, ha_heads, dk, dv = state_hgrn.shape
    _, _, hb_heads, hd, _ = state_rwkv.shape
    ha_in = 2 * ha_heads * dk + 2 * ha_heads * dv
    hw = hb_heads * hd
    lw, la = w_w2.shape[1], a_w2.shape[1]
    nk, dh = peer_keys.shape[2], peer_keys.shape[3]
    p_heads = peer_wq.shape[2] // (2 * dh)
    assert sp % HGRN_CHUNK == 0 and ss <= HGRN_CHUNK and n_meta <= HGRN_CHUNK
    assert hb_heads % 2 == 0 and 2 * hd == LANES and lw + la == LANES and nk == LANES

    row = lambda t: t.reshape(1, -1).astype(F32)
    w_a = w_in[0, :, :ha_in].astype(BF16)
    w_b = w_in[0, :, ha_in:].astype(BF16)
    g1 = row(norm1[0])
    lbt = lb_theta.astype(F32)
    gn = row(hgrn_norm[0])

    head_of_lane = np.arange(LANES) // hd
    head_of_col = np.arange(hw) // hd
    rw = {
        "mu": row(mu_shift[0]), "w0": row(w0[0]), "a0": row(a0[0]), "kk": row(k_k[0]), "ka": row(k_a[0]),
        "rk": row(r_k[0]), "lnw": row(ln_w[0]), "lnb": row(ln_b[0]), "gw": g_w2[0].astype(F32),
        "ww": jnp.concatenate([w_w2[0], jnp.zeros((la, hw), F32)], axis=0),
        "aw": jnp.concatenate([jnp.zeros((lw, hw), F32), a_w2[0]], axis=0),
        "ones": jnp.asarray(np.tile(head_of_lane[:, None] == head_of_lane[None, :], (2, 1)), BF16),
        "seg": jnp.asarray(head_of_col[:, None] == head_of_col[None, :], BF16),
    }
    woa = w_out[0, :ha_heads * dv].astype(BF16)
    wob = w_out[0, ha_heads * dv:].astype(BF16)
    g2 = row(norm2[0])
    wq = peer_wq[0].astype(BF16)
    keys = peer_keys[0].astype(BF16)
    u = peer_u[0].astype(BF16)
    vt = peer_v[0].astype(BF16).T
    gf = row(norm_f)

    za_m, zb_m = _in_proj(meta_tokens.astype(F32), g1, w_a, w_b, n_meta)
    _, hs_m = _hgrn(za_m, 0, 1, 1, n_meta, lbt, gn, jnp.zeros((1, ha_heads, dk, dv), F32), ha_heads, dk, dv)
    _, rs_m, sh_m = _rwkv(zb_m.reshape(1, n_meta, -1), jnp.zeros((1, zb_m.shape[1]), F32),
                          jnp.zeros((1, hb_heads // 2, hd, LANES), F32), rw, 1, n_meta, hd)

    def group(x, nseq, length, hs0, rs0, sh0, chunk_rows, tc):
        xf = x.reshape(nseq * length, d)
        n = xf.shape[0]
        tb = _block(n, 256)
        assert tb % LANES == 0
        za, zb = _in_proj(xf, g1, w_a, w_b, tb)
        oa, hs = _hgrn(za, 0, nseq, length // chunk_rows, chunk_rows, lbt, gn, hs0, ha_heads, dk, dv)
        ob, rs, sh = _rwkv(zb.reshape(nseq, length, -1), sh0, rs0, rw, 8, tc, hd)
        x1, h2t, st = _out_proj(oa, ob.reshape(n, hw), xf, woa, wob, g2, wq, keys, tb)
        t1, e1, e2 = _route(st, p_heads, nk, tb)
        y = _peer(h2t, st, t1, e1, e2, u, vt, x1, gf, p_heads, nk, tb)
        return y.reshape(nseq, length, d), hs, _unpack_pairs(rs), sh

    y_p, hs_p, rs_p, sh_p = group(
        x_prompt, bp, sp, hs_m, jnp.broadcast_to(rs_m, (bp,) + rs_m.shape[1:]),
        jnp.broadcast_to(sh_m, (bp, sh_m.shape[1])), HGRN_CHUNK, 64)
    y_s, hs_s, rs_s, sh_s = group(
        x_sample, bs, ss, state_hgrn[0].astype(F32), _pack_pairs(state_rwkv[0].astype(F32)),
        state_shift[0].astype(F32), ss, ss)

    return (y_p, y_s, hs_p[None], rs_p[None], sh_p[None], hs_s[None], rs_s[None], sh_s[None])
```

```python
import functools

import numpy as np
import jax
import jax.numpy as jnp
from jax import lax
from jax.experimental import pallas as pl
from jax.experimental.pallas import tpu as pltpu

F32 = jnp.float32
BF16 = jnp.bfloat16

EPS = 1e-6
GN_EPS = 64e-5
HGRN_CHUNK = 64
HGRN_SHORT_CHUNK = 16
PEER_TOPK = 16
RWKV_SLOT = 32
LANES = 128
SUBLANES = 8
VMEM_LIMIT_BYTES = 52 * 1024 * 1024

_NT = (((1,), (1,)), ((), ()))
_TN = (((0,), (0,)), ((), ()))


def _dot(a, b):
    return jnp.dot(a, b, preferred_element_type=F32)


def _split3(a):
    a1 = a.astype(BF16)
    r1 = a - a1.astype(F32)
    a2 = r1.astype(BF16)
    a3 = (r1 - a2.astype(F32)).astype(BF16)
    return a1, a2, a3


def _dot_exact_rhs(a, b_exact):
    a1, a2, a3 = _split3(a)
    return _dot(a1, b_exact) + _dot(a2, b_exact) + _dot(a3, b_exact)


def _dot_exact_lhs(a_exact, b):
    b1, b2, b3 = _split3(b)
    return _dot(a_exact, b1) + _dot(a_exact, b2) + _dot(a_exact, b3)


def _dot_hi(a, b):
    ah = a.astype(BF16)
    al = (a - ah.astype(F32)).astype(BF16)
    bh = b.astype(BF16)
    bl = (b - bh.astype(F32)).astype(BF16)
    return _dot(ah, bh) + _dot(ah, bl) + _dot(al, bh)


def _sigmoid(x):
    return 1.0 / (1.0 + jnp.exp(-x))


def _rms(x, g):
    return x * lax.rsqrt(jnp.mean(x * x, axis=-1, keepdims=True) + EPS) * g


def _params(*sem):
    return pltpu.CompilerParams(dimension_semantics=sem, vmem_limit_bytes=VMEM_LIMIT_BYTES)


def _const_spec(shape):
    nd = len(shape)
    return pl.BlockSpec(shape, lambda *_: (0,) * nd)


def _in_proj_body(x_ref, g_ref, wa_ref, wb_ref, za_ref, zb_ref):
    hb = _rms(x_ref[...], g_ref[...]).astype(BF16)
    za_ref[...] = _dot(hb, wa_ref[...])
    zb_ref[...] = _dot(hb, wb_ref[...])


def _in_proj(x, g, wa, wb, tb):
    n, d = x.shape
    na, nb = wa.shape[1], wb.shape[1]
    return pl.pallas_call(
        _in_proj_body,
        grid=(n // tb,),
        in_specs=[pl.BlockSpec((tb, d), lambda i: (i, 0)), _const_spec((1, d)),
                  _const_spec(wa.shape), _const_spec(wb.shape)],
        out_specs=[pl.BlockSpec((tb, na), lambda i: (i, 0)), pl.BlockSpec((tb, nb), lambda i: (i, 0))],
        out_shape=[jax.ShapeDtypeStruct((n, na), F32), jax.ShapeDtypeStruct((n, nb), F32)],
        compiler_params=_params("parallel"),
        name="in_proj",
    )(x, g, wa, wb)


def _hgrn_tables(c):
    nl = int(np.log2(c)) + 1
    mats = np.zeros((nl, c, c), np.float32)
    masks = np.zeros((nl, c, c), np.float32)
    idx = np.arange(c)
    mats[0] = (idx[None, :] <= idx[:, None])
    masks[0] = np.eye(c)
    for l in range(1, nl):
        blk = 1 << l
        half = blk >> 1
        for t in range(c):
            start = (t // blk) * blk
            ref = start + half - 1
            if t - start >= half:
                mats[l, t, ref + 1:t + 1] = 1.0
                masks[l, t, start:start + half] = 1.0
            else:
                mats[l, t, t + 1:ref + 1] = 1.0
    return mats.reshape(nl * c, c), masks


def _hgrn_body(za_ref, lbt_ref, gn_ref, s0_ref, mats_ref, masks_ref, oa_ref, sout_ref, st_scr,
               *, rows, chunk, heads, dk, dv):
    c_id = pl.program_id(1)
    qk = heads * dk
    wd = heads * dv
    nl = masks_ref.shape[0]

    @pl.when(c_id == 0)
    def _():
        for h in range(heads):
            st_scr[h] = s0_ref[0, h].T

    th = lbt_ref[...]
    ex = jnp.exp(th - jnp.max(th, axis=0, keepdims=True))
    lb = ex[0:1] / jnp.sum(ex, axis=0, keepdims=True)

    za = za_ref[...]
    zq = za[:, :qk]
    zf = za[:, qk:2 * qk]
    zi = za[:, 2 * qk:2 * qk + wd]
    zo = za[:, 2 * qk + wd:]
    logf = jnp.log(lb + (1.0 - lb) * _sigmoid(zf))
    kin = (1.0 - lb) * _sigmoid(-zf)
    qa = zq * _sigmoid(zq)
    if rows < chunk:
        pad = jnp.zeros((chunk - rows, qk), F32)
        logf = jnp.concatenate([logf, pad], axis=0)
        kin = jnp.concatenate([kin, pad], axis=0)
        qa = jnp.concatenate([qa, pad], axis=0)
        zi = jnp.concatenate([zi, jnp.zeros((chunk - rows, wd), F32)], axis=0)

    dec = _dot_exact_lhs(mats_ref[...], logf)
    gn = gn_ref[...]
    for h in range(heads):
        ks = slice(h * dk, (h + 1) * dk)
        vs = slice(h * dv, (h + 1) * dv)
        q = qa[:, ks]
        k = kin[:, ks]
        v = zi[:, vs].astype(BF16)
        cum = dec[0:chunk, ks]
        sc = lax.dot_general(q.astype(BF16), k.astype(BF16), _NT, preferred_element_type=F32)
        scores = jnp.where(masks_ref[0] != 0.0, sc, 0.0)
        for l in range(1, nl):
            e = jnp.exp(dec[l * chunk:(l + 1) * chunk, ks])
            sc = lax.dot_general((q * e).astype(BF16), (k * e).astype(BF16), _NT, preferred_element_type=F32)
            scores = scores + jnp.where(masks_ref[l] != 0.0, sc, 0.0)
        st = st_scr[h]
        o = _dot(scores.astype(BF16), v) + lax.dot_general(
            (q * jnp.exp(cum)).astype(BF16), st.astype(BF16), _NT, preferred_element_type=F32)
        last = cum[chunk - 1:chunk, :]
        kd = (k * jnp.exp(last - cum)).astype(BF16)
        st_scr[h] = jnp.exp(last) * st + lax.dot_general(v, kd, _TN, preferred_element_type=F32)

        o = o[:rows]
        o = o * lax.rsqrt(jnp.mean(o * o, axis=-1, keepdims=True) + EPS) * gn[:, vs]
        zoh = zo[:, vs]
        oa_ref[:, vs] = o * (zoh * _sigmoid(zoh))

    @pl.when(c_id == pl.num_programs(1) - 1)
    def _():
        for h in range(heads):
            sout_ref[0, h] = st_scr[h].T


def _hgrn(za, row_block0, nseq, nchunk, rows, lb_theta, gn, s0, heads, dk, dv):
    chunk = HGRN_CHUNK if rows > HGRN_SHORT_CHUNK else HGRN_SHORT_CHUNK
    mats, masks = _hgrn_tables(chunk)
    mats = jnp.asarray(mats, BF16)
    masks = jnp.asarray(masks, F32)
    n = nseq * nchunk * rows
    wd = heads * dv
    s0_map = (lambda s, c: (0, 0, 0, 0)) if s0.shape[0] == 1 else (lambda s, c: (s, 0, 0, 0))
    body = functools.partial(_hgrn_body, rows=rows, chunk=chunk, heads=heads, dk=dk, dv=dv)
    return pl.pallas_call(
        body,
        grid=(nseq, nchunk),
        in_specs=[pl.BlockSpec((rows, za.shape[1]), lambda s, c: (row_block0 + s * nchunk + c, 0)),
                  _const_spec(lb_theta.shape), _const_spec(gn.shape),
                  pl.BlockSpec((1, heads, dk, dv), s0_map),
                  _const_spec(mats.shape), _const_spec(masks.shape)],
        out_specs=[pl.BlockSpec((rows, wd), lambda s, c: (s * nchunk + c, 0)),
                   pl.BlockSpec((1, heads, dk, dv), lambda s, c: (s, 0, 0, 0))],
        out_shape=[jax.ShapeDtypeStruct((n, wd), F32), jax.ShapeDtypeStruct((nseq, heads, dk, dv), F32)],
        scratch_shapes=[pltpu.VMEM((heads, dv, dk), F32)],
        compiler_params=_params("parallel", "arbitrary"),
        name="hgrn",
    )(za, lb_theta, gn, s0, mats, masks)


def _rwkv_body(zb_ref, sh0_ref, s0_ref, mu_ref, w0_ref, ww_ref, a0_ref, aw_ref, gw_ref, kk_ref, ka_ref,
               rk_ref, lnw_ref, lnb_ref, ones_ref, seg_ref,
               ob_ref, sout_ref, shout_ref,
               s_scr, prev_scr, al_scr, w_scr, be_scr, k_scr, r_scr, v_scr, g_scr, y_scr, vt_scr,
               *, nb, tc, hw, hd, lora):
    t_id = pl.program_id(1)
    pairs = hw // LANES
    rin = zb_ref.shape[-1]

    @pl.when(t_id == 0)
    def _():
        s_scr[...] = s0_ref[...]
        prev_scr[...] = sh0_ref[...]

    row = lax.broadcasted_iota(jnp.int32, (tc, rin), 0)
    mu = mu_ref[...]
    zms = []
    for b in range(nb):
        zb = zb_ref[b]
        prev = jnp.where(row == 0, prev_scr[b:b + 1, :], pltpu.roll(zb, 1, axis=0))
        prev_scr[b:b + 1, :] = zb[tc - 1:tc, :]
        zms.append(zb + (prev - zb) * mu)
    zm = jnp.concatenate(zms, axis=0)
    r = zm[:, :hw]
    k = zm[:, hw:2 * hw]
    v = zm[:, 2 * hw:3 * hw]
    dwa = zm[:, 3 * hw:3 * hw + lora]
    dg = zm[:, 3 * hw + lora:]
    w_raw = w0_ref[...] + _dot_hi(jnp.tanh(dwa), ww_ref[...])
    decay = jnp.exp(-np.float32(np.exp(-0.5)) * _sigmoid(w_raw))
    a = _sigmoid(a0_ref[...] + _dot_hi(dwa, aw_ref[...]))
    g = _dot_hi(_sigmoid(dg), gw_ref[...])
    kk = k * kk_ref[...]
    ss = _dot_exact_rhs(kk * kk, seg_ref[...])
    kk = kk / jnp.maximum(jnp.sqrt(ss), 1e-12)
    k2 = k * (1.0 + (a - 1.0) * ka_ref[...])
    shp = (nb, tc, hw)
    al_scr[...] = (-kk).reshape(shp)
    w_scr[...] = decay.reshape(shp)
    be_scr[...] = (kk * a).reshape(shp)
    k_scr[...] = k2.reshape(shp)
    r_scr[...] = r.reshape(shp)
    v_scr[...] = v.reshape(shp)
    g_scr[...] = g.reshape(shp)

    slot = RWKV_SLOT
    li2 = lax.broadcasted_iota(jnp.int32, (slot, LANES), 1)
    vi = lax.broadcasted_iota(jnp.int32, (hd, LANES), 0)
    sel = jnp.where(vi == lax.broadcasted_iota(jnp.int32, (hd, LANES), 1) % hd, 1.0, 0.0).astype(BF16)
    for b in range(nb):
        for p in range(pairs):
            for g0 in range(0, tc, slot):
                nt = min(slot, tc - g0)
                vb = v[b * tc + g0:b * tc + g0 + nt, p * LANES:(p + 1) * LANES]
                if nt < slot:
                    vb = jnp.concatenate([vb, jnp.zeros((slot - nt, LANES), F32)], axis=0)
                vh = vb.astype(BF16)
                vl = (vb - vh.astype(F32)).astype(BF16)
                zero = jnp.zeros_like(vh)
                stack = jnp.concatenate([jnp.where(li2 < hd, vh, zero), jnp.where(li2 >= hd, vh, zero),
                                         jnp.where(li2 < hd, vl, zero), jnp.where(li2 >= hd, vl, zero)], axis=0)
                vt_scr[g0 // slot, b, p] = lax.dot_general(
                    sel, stack, _NT, preferred_element_type=F32).astype(BF16)

    rows = pairs * hd
    ri = lax.broadcasted_iota(jnp.int32, (rows, LANES), 0)
    li = lax.broadcasted_iota(jnp.int32, (rows, LANES), 1)
    diag = (ri % hd) == (li % hd)
    ones2 = ones_ref[...]
    ones1 = ones_ref[0:LANES, :]
    qi = lax.broadcasted_iota(jnp.int32, (4 * slot, LANES), 0)
    ql = lax.broadcasted_iota(jnp.int32, (4 * slot, LANES), 1)
    tok_of_row = jnp.where((qi // slot) % 2 == ql // hd, qi % slot, -1)

    def bcast(ref, b, t):
        rowv = ref[b, pl.ds(t, 1), :]
        return jnp.concatenate(
            [jnp.broadcast_to(rowv[:, p * LANES:(p + 1) * LANES], (hd, LANES)) for p in range(pairs)], axis=0)

    def hilo(x):
        xh = x.astype(BF16)
        return jnp.concatenate([xh, (x - xh.astype(F32)).astype(BF16)], axis=1)

    def step(t, carry):
        pick = jnp.where(tok_of_row == t % slot, 1.0, 0.0).astype(BF16)
        seqs = range(nb)
        vcols = [_dot(vt_scr[t // slot, b].reshape(rows, 4 * slot), pick) for b in seqs]
        olds = [s_scr[b].reshape(rows, LANES) for b in seqs]
        sas = [_dot(hilo(olds[b] * bcast(al_scr, b, t)), ones2) for b in seqs]
        news = [olds[b] * bcast(w_scr, b, t) + sas[b] * bcast(be_scr, b, t) + vcols[b] * bcast(k_scr, b, t)
                for b in seqs]
        for b in seqs:
            s_scr[b] = news[b].reshape(pairs, hd, LANES)
        yreps = [_dot((news[b] * bcast(r_scr, b, t)).astype(BF16), ones1) for b in seqs]
        for b in seqs:
            ysel = jnp.where(diag, yreps[b], 0.0).reshape(pairs, hd, LANES)
            for p in range(pairs):
                y_scr[b, p, pl.ds(t, 1), :] = jnp.sum(ysel[p], axis=0, keepdims=True)
        return carry

    lax.fori_loop(0, tc, step, 0, unroll=2)

    y = jnp.concatenate([y_scr[:, p].reshape(nb * tc, LANES) for p in range(pairs)], axis=1)
    seg = seg_ref[...]
    inv = np.float32(1.0 / hd)
    mean = _dot_exact_rhs(y, seg) * inv
    yc = y - mean
    var = _dot_exact_rhs(yc * yc, seg) * inv
    yn = yc * lax.rsqrt(var + GN_EPS) * lnw_ref[...] + lnb_ref[...]
    r = r_scr[...].reshape(nb * tc, hw)
    k2 = k_scr[...].reshape(nb * tc, hw)
    v = v_scr[...].reshape(nb * tc, hw)
    bonus = _dot_exact_rhs(r * k2 * rk_ref[...], seg) * v
    ob_ref[...] = ((yn + bonus) * g_scr[...].reshape(nb * tc, hw)).reshape(shp)

    @pl.when(t_id == pl.num_programs(1) - 1)
    def _():
        sout_ref[...] = s_scr[...]
        shout_ref[...] = prev_scr[...]


def _rwkv(zb, sh0, s0, wts, nb, tc, hd):
    nseq, length, rin = zb.shape
    hw = wts["w0"].shape[1]
    pairs = hw // LANES
    lora = wts["ww"].shape[0]
    body = functools.partial(_rwkv_body, nb=nb, tc=tc, hw=hw, hd=hd, lora=lora)
    names = ("mu", "w0", "ww", "a0", "aw", "gw", "kk", "ka", "rk", "lnw", "lnb", "ones", "seg")
    consts = [wts[n] for n in names]
    tok = pltpu.VMEM((nb, tc, hw), F32)
    return pl.pallas_call(
        body,
        grid=(nseq // nb, length // tc),
        in_specs=[pl.BlockSpec((nb, tc, rin), lambda s, t: (s, t, 0)),
                  pl.BlockSpec((nb, rin), lambda s, t: (s, 0)),
                  pl.BlockSpec((nb, pairs, hd, LANES), lambda s, t: (s, 0, 0, 0))]
                 + [_const_spec(c.shape) for c in consts],
        out_specs=[pl.BlockSpec((nb, tc, hw), lambda s, t: (s, t, 0)),
                   pl.BlockSpec((nb, pairs, hd, LANES), lambda s, t: (s, 0, 0, 0)),
                   pl.BlockSpec((nb, rin), lambda s, t: (s, 0))],
        out_shape=[jax.ShapeDtypeStruct((nseq, length, hw), F32),
                   jax.ShapeDtypeStruct((nseq, pairs, hd, LANES), F32),
                   jax.ShapeDtypeStruct((nseq, rin), F32)],
        scratch_shapes=[pltpu.VMEM((nb, pairs, hd, LANES), F32), pltpu.VMEM((nb, rin), F32)]
                       + [tok] * 7 + [pltpu.VMEM((nb, pairs, tc, LANES), F32),
                                      pltpu.VMEM((-(-tc // RWKV_SLOT), nb, pairs, hd, 4 * RWKV_SLOT), BF16)],
        compiler_params=_params("parallel", "arbitrary"),
        name="rwkv",
    )(zb, sh0, s0, *consts)


def _pack_pairs(s):
    n, heads, hd, _ = s.shape
    return s.reshape(n, heads // 2, 2, hd, hd).transpose(0, 1, 3, 2, 4).reshape(n, heads // 2, hd, 2 * hd)


def _unpack_pairs(s):
    n, pairs, hd, _ = s.shape
    return s.reshape(n, pairs, hd, 2, hd).transpose(0, 1, 3, 2, 4).reshape(n, 2 * pairs, hd, hd)


def _out_proj_body(oa_ref, ob_ref, x_ref, woa_ref, wob_ref, g2_ref, wq_ref, keys_ref,
                   x1_ref, h2t_ref, st_ref, *, nk, dh):
    mix = _dot(oa_ref[...].astype(BF16), woa_ref[...]) + _dot(ob_ref[...].astype(BF16), wob_ref[...])
    x1 = x_ref[...] + mix
    x1_ref[...] = x1
    h2f = _rms(x1, g2_ref[...])
    h2t_ref[...] = h2f.T.astype(BF16)
    q = _dot(h2f.astype(BF16), wq_ref[...]).astype(BF16)
    for hp in range(q.shape[1] // dh):
        st_ref[hp * nk:(hp + 1) * nk, :] = lax.dot_general(
            keys_ref[hp % 2], q[:, hp * dh:(hp + 1) * dh], _NT, preferred_element_type=F32)


def _out_proj(oa, ob, x, woa, wob, g2, wq, keys, tb):
    n, d = x.shape
    nk, dh = keys.shape[1], keys.shape[2]
    nrow = (wq.shape[1] // dh) * nk
    body = functools.partial(_out_proj_body, nk=nk, dh=dh)
    return pl.pallas_call(
        body,
        grid=(n // tb,),
        in_specs=[pl.BlockSpec((tb, oa.shape[1]), lambda i: (i, 0)),
                  pl.BlockSpec((tb, ob.shape[1]), lambda i: (i, 0)),
                  pl.BlockSpec((tb, d), lambda i: (i, 0)),
                  _const_spec(woa.shape), _const_spec(wob.shape), _const_spec(g2.shape),
                  _const_spec(wq.shape), _const_spec(keys.shape)],
        out_specs=[pl.BlockSpec((tb, d), lambda i: (i, 0)), pl.BlockSpec((d, tb), lambda i: (0, i)),
                   pl.BlockSpec((nrow, tb), lambda i: (0, i))],
        out_shape=[jax.ShapeDtypeStruct((n, d), F32), jax.ShapeDtypeStruct((d, n), BF16),
                   jax.ShapeDtypeStruct((nrow, n), F32)],
        compiler_params=_params("parallel"),
        name="out_proj",
    )(oa, ob, x, woa, wob, g2, wq, keys)


def _route_body(st_ref, t1_ref, e1_ref, e2_ref, top_scr, cand_scr, *, nk, topk):
    neg = -jnp.inf
    nx = topk + 1
    half = topk // 2

    def extract(work, n, on_value):
        for i in range(n):
            m = jnp.max(work, axis=0, keepdims=True)
            on_value(i, m)
            work = jnp.where(work == m, neg, work)

    def keep(h):
        def on_value(i, m):
            top_scr[h, i:i + 1, :] = m
        return on_value

    s1 = st_ref[0:nk, :]
    s2 = st_ref[nk:2 * nk, :]
    extract(s1, nx, keep(0))
    extract(s2, nx, keep(1))
    cand_scr[...] = jnp.full(cand_scr.shape, neg, F32)
    sv2 = top_scr[1, 0:nx, :]
    cand_scr[0:nx, :] = top_scr[0, 0:1, :] + sv2
    base = 3 * SUBLANES
    for a in range(1, half):
        cand_scr[base + (a - 1) * half:base + a * half, :] = top_scr[0, a:a + 1, :] + sv2[0:half]
    base += (half - 1) * half
    cand_scr[base:base + nx - half, :] = top_scr[0, half:nx, :] + sv2[0:1]

    best0 = top_scr[0, 0:1, :] + sv2[0:1]
    acc = {"z": jnp.zeros_like(best0), "lo": best0, "hi": best0}

    def on_best(i, m):
        if i < topk:
            acc["z"] = acc["z"] + jnp.exp(m - best0)
            acc["hi"] = m
        acc["lo"] = m

    extract(cand_scr[...], nx, on_best)
    thr = jnp.where(acc["lo"] == neg, acc["hi"], 0.5 * (acc["hi"] + acc["lo"]))
    t1_ref[0] = thr - s1
    e1_ref[0] = jnp.exp(s1 - top_scr[0, 0:1, :]) / acc["z"]
    e2_ref[0] = jnp.exp(s2 - sv2[0:1])


def _route(st, heads, nk, tb):
    n = st.shape[1]
    topk = PEER_TOPK
    half = topk // 2
    ncand = 3 * SUBLANES + (half - 1) * half + 2 * SUBLANES
    assert topk + 1 <= 3 * SUBLANES and topk + 1 - half <= 2 * SUBLANES and half == SUBLANES
    body = functools.partial(_route_body, nk=nk, topk=topk)
    blk = pl.BlockSpec((1, nk, tb), lambda i, h: (h, 0, i))
    return pl.pallas_call(
        body,
        grid=(n // tb, heads),
        in_specs=[pl.BlockSpec((2 * nk, tb), lambda i, h: (h, i))],
        out_specs=[blk, blk, blk],
        out_shape=[jax.ShapeDtypeStruct((heads, nk, n), F32)] * 3,
        scratch_shapes=[pltpu.VMEM((2, 3 * SUBLANES, tb), F32), pltpu.VMEM((ncand, tb), F32)],
        compiler_params=_params("parallel", "parallel"),
        name="route",
    )(st)


PEER_SUB = 2 * LANES
PEER_ROW_SPLIT = 2


def _peer_body(h2t_ref, s2_ref, t1_ref, e1_ref, e2_ref, u_ref, vt_ref, x1_ref, gf_ref, y_ref,
               acc_scr, coef_scr, *, heads, nk, tb, n_eb):
    e = pl.program_id(1)
    eblk = u_ref.shape[0]
    i1_per_block = eblk // nk
    i1_per_sub = PEER_SUB // nk
    cur = e % 2

    @pl.when(e == 0)
    def _():
        acc_scr[...] = jnp.zeros_like(acc_scr)
        coef_scr[1] = jnp.zeros(coef_scr.shape[1:], BF16)

    i1_0 = pl.multiple_of(jnp.minimum(e, n_eb - 1) * i1_per_block, SUBLANES)
    n_sub = eblk // PEER_SUB
    h2t = h2t_ref[...]

    def matmuls(k):
        ks = slice(k * PEER_SUB, (k + 1) * PEER_SUB)
        return _dot(u_ref[ks, :], h2t), _dot(vt_ref[:, ks], coef_scr[1 - cur, ks, :])

    act, out = matmuls(0)
    for k in range(n_sub):
        if k + 1 < n_sub:
            act_next, part = matmuls(k + 1)
            out = out + part
        for lg in range(tb // LANES):
            ls = slice(lg * LANES, (lg + 1) * LANES)
            t1b = [t1_ref[h, pl.ds(i1_0, i1_per_block), ls] for h in range(heads)]
            e1b = [e1_ref[h, pl.ds(i1_0, i1_per_block), ls] for h in range(heads)]
            for rs in range(PEER_ROW_SPLIT):
                rsl = slice(rs * nk // PEER_ROW_SPLIT, (rs + 1) * nk // PEER_ROW_SPLIT)
                gates = [None] * i1_per_sub
                for h in range(heads):
                    s2 = s2_ref[h, rsl, ls]
                    e2 = e2_ref[h, rsl, ls]
                    for jj in range(i1_per_sub):
                        j = k * i1_per_sub + jj
                        term = jnp.where(s2 >= t1b[h][j:j + 1], e1b[h][j:j + 1] * e2, 0.0)
                        gates[jj] = term if gates[jj] is None else gates[jj] + term
                for jj in range(i1_per_sub):
                    a = act[jj * nk + rsl.start:jj * nk + rsl.stop, ls]
                    gelu = 0.5 * a * (1.0 + lax.erf(a * np.float32(np.sqrt(0.5))))
                    r0 = (k * i1_per_sub + jj) * nk
                    coef_scr[cur, r0 + rsl.start:r0 + rsl.stop, ls] = (gates[jj] * gelu).astype(BF16)
        if k + 1 < n_sub:
            act = act_next
    acc_scr[...] += out

    @pl.when(e == n_eb)
    def _():
        y_ref[...] = _rms(x1_ref[...] + acc_scr[...].T, gf_ref[...])


def _peer(h2t, st, t1, e1, e2, u, vt, x1, gf, heads, nk, tb):
    n, d = x1.shape
    ne = u.shape[0]
    eblk = SUBLANES * nk
    n_eb = ne // eblk
    body = functools.partial(_peer_body, heads=heads, nk=nk, tb=tb, n_eb=n_eb)
    tok = pl.BlockSpec((heads, nk, tb), lambda i, e: (0, 0, i))
    return pl.pallas_call(
        body,
        grid=(n // tb, n_eb + 1),
        in_specs=[pl.BlockSpec((d, tb), lambda i, e: (0, i)),
                  pl.BlockSpec((heads, None, nk, tb), lambda i, e: (0, 1, 0, i)),
                  tok, tok, tok,
                  pl.BlockSpec((eblk, d), lambda i, e: (jnp.minimum(e, n_eb - 1), 0)),
                  pl.BlockSpec((d, eblk), lambda i, e: (0, jnp.maximum(e - 1, 0))),
                  pl.BlockSpec((tb, d), lambda i, e: (i, 0)),
                  _const_spec(gf.shape)],
        out_specs=pl.BlockSpec((tb, d), lambda i, e: (i, 0)),
        out_shape=jax.ShapeDtypeStruct((n, d), F32),
        scratch_shapes=[pltpu.VMEM((d, tb), F32), pltpu.VMEM((2, eblk, tb), BF16)],
        compiler_params=_params("parallel", "arbitrary"),
        name="peer",
    )(h2t, st.reshape(heads, 2, nk, n), t1, e1, e2, u, vt, x1, gf)


def _block(n, pref):
    b = pref
    while n % b:
        b //= 2
    return b


def kernel(x_prompt, x_sample, state_hgrn, state_rwkv, state_shift, meta_tokens, norm1, w_in, lb_theta, hgrn_norm, mu_shift, w0, w_w2, a0, a_w2, g_w2, k_k, k_a, r_k, ln_w, ln_b, w_out, norm2, peer_wq, peer_keys, peer_u, peer_v, norm_f):
    assert w_in.shape[0] == 1, "single-layer trunk"
    bp, sp, d = x_prompt.shape
    bs, ss, _ = x_sample.shape
    n_meta = meta_tokens.shape[0]
    _, _, ha_heads, dk, dv = state_hgrn.shape
    _, _, hb_heads, hd, _ = state_rwkv.shape
    ha_in = 2 * ha_heads * dk + 2 * ha_heads * dv
    hw = hb_heads * hd
    lw, la = w_w2.shape[1], a_w2.shape[1]
    nk, dh = peer_keys.shape[2], peer_keys.shape[3]
    p_heads = peer_wq.shape[2] // (2 * dh)
    assert sp % HGRN_CHUNK == 0 and ss <= HGRN_CHUNK and n_meta <= HGRN_CHUNK
    assert hb_heads % 2 == 0 and 2 * hd == LANES and lw + la == LANES and nk == LANES

    row = lambda t: t.reshape(1, -1).astype(F32)
    w_a = w_in[0, :, :ha_in].astype(BF16)
    w_b = w_in[0, :, ha_in:].astype(BF16)
    g1 = row(norm1[0])
    lbt = lb_theta.astype(F32)
    gn = row(hgrn_norm[0])

    head_of_lane = np.arange(LANES) // hd
    head_of_col = np.arange(hw) // hd
    rw = {
        "mu": row(mu_shift[0]), "w0": row(w0[0]), "a0": row(a0[0]), "kk": row(k_k[0]), "ka": row(k_a[0]),
        "rk": row(r_k[0]), "lnw": row(ln_w[0]), "lnb": row(ln_b[0]), "gw": g_w2[0].astype(F32),
        "ww": jnp.concatenate([w_w2[0], jnp.zeros((la, hw), F32)], axis=0),
        "aw": jnp.concatenate([jnp.zeros((lw, hw), F32), a_w2[0]], axis=0),
        "ones": jnp.asarray(np.tile(head_of_lane[:, None] == head_of_lane[None, :], (2, 1)), BF16),
        "seg": jnp.asarray(head_of_col[:, None] == head_of_col[None, :], BF16),
    }
    woa = w_out[0, :ha_heads * dv].astype(BF16)
    wob = w_out[0, ha_heads * dv:].astype(BF16)
    g2 = row(norm2[0])
    wq = peer_wq[0].astype(BF16)
    keys = peer_keys[0].astype(BF16)
    u = peer_u[0].astype(BF16)
    vt = peer_v[0].astype(BF16).T
    gf = row(norm_f)

    za_m, zb_m = _in_proj(meta_tokens.astype(F32), g1, w_a, w_b, n_meta)
    _, hs_m = _hgrn(za_m, 0, 1, 1, n_meta, lbt, gn, jnp.zeros((1, ha_heads, dk, dv), F32), ha_heads, dk, dv)
    _, rs_m, sh_m = _rwkv(zb_m.reshape(1, n_meta, -1), jnp.zeros((1, zb_m.shape[1]), F32),
                          jnp.zeros((1, hb_heads // 2, hd, LANES), F32), rw, 1, n_meta, hd)

    def group(x, nseq, length, hs0, rs0, sh0, chunk_rows, tc):
        xf = x.reshape(nseq * length, d)
        n = xf.shape[0]
        tb = _block(n, 256)
        assert tb % LANES == 0
        za, zb = _in_proj(xf, g1, w_a, w_b, tb)
        oa, hs = _hgrn(za, 0, nseq, length // chunk_rows, chunk_rows, lbt, gn, hs0, ha_heads, dk, dv)
        ob, rs, sh = _rwkv(zb.reshape(nseq, length, -1), sh0, rs0, rw, 8, tc, hd)
        x1, h2t, st = _out_proj(oa, ob.reshape(n, hw), xf, woa, wob, g2, wq, keys, tb)
        t1, e1, e2 = _route(st, p_heads, nk, tb)
        y = _peer(h2t, st, t1, e1, e2, u, vt, x1, gf, p_heads, nk, _block(n, 512))
        return y.reshape(nseq, length, d), hs, _unpack_pairs(rs), sh

    y_p, hs_p, rs_p, sh_p = group(
        x_prompt, bp, sp, hs_m, jnp.broadcast_to(rs_m, (bp,) + rs_m.shape[1:]),
        jnp.broadcast_to(sh_m, (bp, sh_m.shape[1])), HGRN_CHUNK, 64)
    y_s, hs_s, rs_s, sh_s = group(
        x_sample, bs, ss, state_hgrn[0].astype(F32), _pack_pairs(state_rwkv[0].astype(F32)),
        state_shift[0].astype(F32), ss, ss)

    return (y_p, y_s, hs_p[None], rs_p[None], sh_p[None], hs_s[None], rs_s[None], sh_s[None])
```

```python
import functools

import numpy as np
import jax
import jax.numpy as jnp
from jax import lax
from jax.experimental import pallas as pl
from jax.experimental.pallas import tpu as pltpu

F32 = jnp.float32
BF16 = jnp.bfloat16

EPS = 1e-6
GN_EPS = 64e-5
HGRN_CHUNK = 64
HGRN_SHORT_CHUNK = 16
PEER_TOPK = 16
RWKV_SLOT = 32
LANES = 128
SUBLANES = 8
VMEM_LIMIT_BYTES = 52 * 1024 * 1024

_NT = (((1,), (1,)), ((), ()))
_TN = (((0,), (0,)), ((), ()))


def _dot(a, b):
    return jnp.dot(a, b, preferred_element_type=F32)


def _split3(a):
    a1 = a.astype(BF16)
    r1 = a - a1.astype(F32)
    a2 = r1.astype(BF16)
    a3 = (r1 - a2.astype(F32)).astype(BF16)
    return a1, a2, a3


def _dot_exact_rhs(a, b_exact):
    a1, a2, a3 = _split3(a)
    return _dot(a1, b_exact) + _dot(a2, b_exact) + _dot(a3, b_exact)


def _dot_exact_lhs(a_exact, b):
    b1, b2, b3 = _split3(b)
    return _dot(a_exact, b1) + _dot(a_exact, b2) + _dot(a_exact, b3)


def _dot_hi(a, b):
    ah = a.astype(BF16)
    al = (a - ah.astype(F32)).astype(BF16)
    bh = b.astype(BF16)
    bl = (b - bh.astype(F32)).astype(BF16)
    return _dot(ah, bh) + _dot(ah, bl) + _dot(al, bh)


def _sigmoid(x):
    return 1.0 / (1.0 + jnp.exp(-x))


def _rms(x, g):
    return x * lax.rsqrt(jnp.mean(x * x, axis=-1, keepdims=True) + EPS) * g


def _params(*sem):
    return pltpu.CompilerParams(dimension_semantics=sem, vmem_limit_bytes=VMEM_LIMIT_BYTES)


def _const_spec(shape):
    nd = len(shape)
    return pl.BlockSpec(shape, lambda *_: (0,) * nd)


def _in_proj_body(x_ref, g_ref, wa_ref, wb_ref, za_ref, zb_ref):
    hb = _rms(x_ref[...], g_ref[...]).astype(BF16)
    za_ref[...] = _dot(hb, wa_ref[...])
    zb_ref[...] = _dot(hb, wb_ref[...])


def _in_proj(x, g, wa, wb, tb):
    n, d = x.shape
    na, nb = wa.shape[1], wb.shape[1]
    return pl.pallas_call(
        _in_proj_body,
        grid=(n // tb,),
        in_specs=[pl.BlockSpec((tb, d), lambda i: (i, 0)), _const_spec((1, d)),
                  _const_spec(wa.shape), _const_spec(wb.shape)],
        out_specs=[pl.BlockSpec((tb, na), lambda i: (i, 0)), pl.BlockSpec((tb, nb), lambda i: (i, 0))],
        out_shape=[jax.ShapeDtypeStruct((n, na), F32), jax.ShapeDtypeStruct((n, nb), F32)],
        compiler_params=_params("parallel"),
        name="in_proj",
    )(x, g, wa, wb)


def _hgrn_tables(c):
    nl = int(np.log2(c)) + 1
    mats = np.zeros((nl, c, c), np.float32)
    masks = np.zeros((nl, c, c), np.float32)
    idx = np.arange(c)
    mats[0] = (idx[None, :] <= idx[:, None])
    masks[0] = np.eye(c)
    for l in range(1, nl):
        blk = 1 << l
        half = blk >> 1
        for t in range(c):
            start = (t // blk) * blk
            ref = start + half - 1
            if t - start >= half:
                mats[l, t, ref + 1:t + 1] = 1.0
                masks[l, t, start:start + half] = 1.0
            else:
                mats[l, t, t + 1:ref + 1] = 1.0
    return mats.reshape(nl * c, c), masks


def _hgrn_body(za_ref, lbt_ref, gn_ref, s0_ref, mats_ref, masks_ref, oa_ref, sout_ref, st_scr,
               *, nb, rows, chunk, heads, dk, dv):
    c_id = pl.program_id(1)
    qk = heads * dk
    wd = heads * dv
    nl = masks_ref.shape[0]

    @pl.when(c_id == 0)
    def _():
        for s in range(nb):
            for h in range(heads):
                st_scr[s, h] = s0_ref[min(s, s0_ref.shape[0] - 1), h].T

    th = lbt_ref[...]
    ex = jnp.exp(th - jnp.max(th, axis=0, keepdims=True))
    lb = ex[0:1] / jnp.sum(ex, axis=0, keepdims=True)
    gn = gn_ref[...]

    for s in range(nb):
        za = za_ref[s]
        zq = za[:, :qk]
        zf = za[:, qk:2 * qk]
        zi = za[:, 2 * qk:2 * qk + wd]
        zo = za[:, 2 * qk + wd:]
        logf = jnp.log(lb + (1.0 - lb) * _sigmoid(zf))
        kin = (1.0 - lb) * _sigmoid(-zf)
        qa = zq * _sigmoid(zq)
        if rows < chunk:
            pad = jnp.zeros((chunk - rows, qk), F32)
            logf = jnp.concatenate([logf, pad], axis=0)
            kin = jnp.concatenate([kin, pad], axis=0)
            qa = jnp.concatenate([qa, pad], axis=0)
            zi = jnp.concatenate([zi, jnp.zeros((chunk - rows, wd), F32)], axis=0)

        dec = _dot_exact_lhs(mats_ref[...], logf)
        for h in range(heads):
            ks = slice(h * dk, (h + 1) * dk)
            vs = slice(h * dv, (h + 1) * dv)
            q = qa[:, ks]
            k = kin[:, ks]
            v = zi[:, vs].astype(BF16)
            cum = dec[0:chunk, ks]
            sc = lax.dot_general(q.astype(BF16), k.astype(BF16), _NT, preferred_element_type=F32)
            scores = jnp.where(masks_ref[0] != 0.0, sc, 0.0)
            for l in range(1, nl):
                e = jnp.exp(dec[l * chunk:(l + 1) * chunk, ks])
                sc = lax.dot_general((q * e).astype(BF16), (k * e).astype(BF16), _NT, preferred_element_type=F32)
                scores = scores + jnp.where(masks_ref[l] != 0.0, sc, 0.0)
            st = st_scr[s, h]
            o = _dot(scores.astype(BF16), v) + lax.dot_general(
                (q * jnp.exp(cum)).astype(BF16), st.astype(BF16), _NT, preferred_element_type=F32)
            last = cum[chunk - 1:chunk, :]
            kd = (k * jnp.exp(last - cum)).astype(BF16)
            st_scr[s, h] = jnp.exp(last) * st + lax.dot_general(v, kd, _TN, preferred_element_type=F32)

            o = o[:rows]
            o = o * lax.rsqrt(jnp.mean(o * o, axis=-1, keepdims=True) + EPS) * gn[:, vs]
            zoh = zo[:, vs]
            oa_ref[s, :, vs] = o * (zoh * _sigmoid(zoh))

    @pl.when(c_id == pl.num_programs(1) - 1)
    def _():
        for s in range(nb):
            for h in range(heads):
                sout_ref[s, h] = st_scr[s, h].T


def _hgrn(za, nb, rows, lb_theta, gn, s0, heads, dk, dv):
    nseq, length, cols = za.shape
    chunk = HGRN_CHUNK if rows > HGRN_SHORT_CHUNK else HGRN_SHORT_CHUNK
    mats, masks = _hgrn_tables(chunk)
    mats = jnp.asarray(mats, BF16)
    masks = jnp.asarray(masks, F32)
    wd = heads * dv
    shared = s0.shape[0] == 1
    s0_spec = (pl.BlockSpec((1, heads, dk, dv), lambda s, c: (0, 0, 0, 0)) if shared
               else pl.BlockSpec((nb, heads, dk, dv), lambda s, c: (s, 0, 0, 0)))
    body = functools.partial(_hgrn_body, nb=nb, rows=rows, chunk=chunk, heads=heads, dk=dk, dv=dv)
    return pl.pallas_call(
        body,
        grid=(nseq // nb, length // rows),
        in_specs=[pl.BlockSpec((nb, rows, cols), lambda s, c: (s, c, 0)),
                  _const_spec(lb_theta.shape), _const_spec(gn.shape), s0_spec,
                  _const_spec(mats.shape), _const_spec(masks.shape)],
        out_specs=[pl.BlockSpec((nb, rows, wd), lambda s, c: (s, c, 0)),
                   pl.BlockSpec((nb, heads, dk, dv), lambda s, c: (s, 0, 0, 0))],
        out_shape=[jax.ShapeDtypeStruct((nseq, length, wd), F32),
                   jax.ShapeDtypeStruct((nseq, heads, dk, dv), F32)],
        scratch_shapes=[pltpu.VMEM((nb, heads, dv, dk), F32)],
        compiler_params=_params("parallel", "arbitrary"),
        name="hgrn",
    )(za, lb_theta, gn, s0, mats, masks)


def _rwkv_body(zb_ref, sh0_ref, s0_ref, mu_ref, w0_ref, ww_ref, a0_ref, aw_ref, gw_ref, kk_ref, ka_ref,
               rk_ref, lnw_ref, lnb_ref, ones_ref, seg_ref,
               ob_ref, sout_ref, shout_ref,
               s_scr, prev_scr, al_scr, w_scr, be_scr, k_scr, r_scr, v_scr, g_scr, y_scr, vt_scr,
               *, nb, tc, hw, hd, lora):
    t_id = pl.program_id(1)
    pairs = hw // LANES
    rin = zb_ref.shape[-1]

    @pl.when(t_id == 0)
    def _():
        s_scr[...] = s0_ref[...]
        prev_scr[...] = sh0_ref[...]

    row = lax.broadcasted_iota(jnp.int32, (tc, rin), 0)
    mu = mu_ref[...]
    zms = []
    for b in range(nb):
        zb = zb_ref[b]
        prev = jnp.where(row == 0, prev_scr[b:b + 1, :], pltpu.roll(zb, 1, axis=0))
        prev_scr[b:b + 1, :] = zb[tc - 1:tc, :]
        zms.append(zb + (prev - zb) * mu)
    zm = jnp.concatenate(zms, axis=0)
    r = zm[:, :hw]
    k = zm[:, hw:2 * hw]
    v = zm[:, 2 * hw:3 * hw]
    dwa = zm[:, 3 * hw:3 * hw + lora]
    dg = zm[:, 3 * hw + lora:]
    w_raw = w0_ref[...] + _dot_hi(jnp.tanh(dwa), ww_ref[...])
    decay = jnp.exp(-np.float32(np.exp(-0.5)) * _sigmoid(w_raw))
    a = _sigmoid(a0_ref[...] + _dot_hi(dwa, aw_ref[...]))
    g = _dot_hi(_sigmoid(dg), gw_ref[...])
    kk = k * kk_ref[...]
    ss = _dot_exact_rhs(kk * kk, seg_ref[...])
    kk = kk / jnp.maximum(jnp.sqrt(ss), 1e-12)
    k2 = k * (1.0 + (a - 1.0) * ka_ref[...])
    shp = (nb, tc, hw)
    al_scr[...] = (-kk).reshape(shp)
    w_scr[...] = decay.reshape(shp)
    be_scr[...] = (kk * a).reshape(shp)
    k_scr[...] = k2.reshape(shp)
    r_scr[...] = r.reshape(shp)
    v_scr[...] = v.reshape(shp)
    g_scr[...] = g.reshape(shp)

    slot = RWKV_SLOT
    li2 = lax.broadcasted_iota(jnp.int32, (slot, LANES), 1)
    vi = lax.broadcasted_iota(jnp.int32, (hd, LANES), 0)
    sel = jnp.where(vi == lax.broadcasted_iota(jnp.int32, (hd, LANES), 1) % hd, 1.0, 0.0).astype(BF16)
    for b in range(nb):
        for p in range(pairs):
            for g0 in range(0, tc, slot):
                nt = min(slot, tc - g0)
                vb = v[b * tc + g0:b * tc + g0 + nt, p * LANES:(p + 1) * LANES]
                if nt < slot:
                    vb = jnp.concatenate([vb, jnp.zeros((slot - nt, LANES), F32)], axis=0)
                vh = vb.astype(BF16)
                vl = (vb - vh.astype(F32)).astype(BF16)
                zero = jnp.zeros_like(vh)
                stack = jnp.concatenate([jnp.where(li2 < hd, vh, zero), jnp.where(li2 >= hd, vh, zero),
                                         jnp.where(li2 < hd, vl, zero), jnp.where(li2 >= hd, vl, zero)], axis=0)
                vt_scr[g0 // slot, b, p] = lax.dot_general(
                    sel, stack, _NT, preferred_element_type=F32).astype(BF16)

    rows = pairs * hd
    ri = lax.broadcasted_iota(jnp.int32, (rows, LANES), 0)
    li = lax.broadcasted_iota(jnp.int32, (rows, LANES), 1)
    diag = (ri % hd) == (li % hd)
    ones2 = ones_ref[...]
    ones1 = ones_ref[0:LANES, :]
    qi = lax.broadcasted_iota(jnp.int32, (4 * slot, LANES), 0)
    ql = lax.broadcasted_iota(jnp.int32, (4 * slot, LANES), 1)
    tok_of_row = jnp.where((qi // slot) % 2 == ql // hd, qi % slot, -1)

    def bcast(ref, b, t):
        rowv = ref[b, pl.ds(t, 1), :]
        return jnp.concatenate(
            [jnp.broadcast_to(rowv[:, p * LANES:(p + 1) * LANES], (hd, LANES)) for p in range(pairs)], axis=0)

    def hilo(x):
        xh = x.astype(BF16)
        return jnp.concatenate([xh, (x - xh.astype(F32)).astype(BF16)], axis=1)

    def step(t, carry):
        pick = jnp.where(tok_of_row == t % slot, 1.0, 0.0).astype(BF16)
        seqs = range(nb)
        vcols = [_dot(vt_scr[t // slot, b].reshape(rows, 4 * slot), pick) for b in seqs]
        olds = [s_scr[b].reshape(rows, LANES) for b in seqs]
        sas = [_dot(hilo(olds[b] * bcast(al_scr, b, t)), ones2) for b in seqs]
        news = [olds[b] * bcast(w_scr, b, t) + sas[b] * bcast(be_scr, b, t) + vcols[b] * bcast(k_scr, b, t)
                for b in seqs]
        for b in seqs:
            s_scr[b] = news[b].reshape(pairs, hd, LANES)
        yreps = [_dot((news[b] * bcast(r_scr, b, t)).astype(BF16), ones1) for b in seqs]
        for b in seqs:
            ysel = jnp.where(diag, yreps[b], 0.0).reshape(pairs, hd, LANES)
            for p in range(pairs):
                y_scr[b, p, pl.ds(t, 1), :] = jnp.sum(ysel[p], axis=0, keepdims=True)
        return carry

    lax.fori_loop(0, tc, step, 0, unroll=2)

    y = jnp.concatenate([y_scr[:, p].reshape(nb * tc, LANES) for p in range(pairs)], axis=1)
    seg = seg_ref[...]
    inv = np.float32(1.0 / hd)
    mean = _dot_exact_rhs(y, seg) * inv
    yc = y - mean
    var = _dot_exact_rhs(yc * yc, seg) * inv
    yn = yc * lax.rsqrt(var + GN_EPS) * lnw_ref[...] + lnb_ref[...]
    r = r_scr[...].reshape(nb * tc, hw)
    k2 = k_scr[...].reshape(nb * tc, hw)
    v = v_scr[...].reshape(nb * tc, hw)
    bonus = _dot_exact_rhs(r * k2 * rk_ref[...], seg) * v
    ob_ref[...] = ((yn + bonus) * g_scr[...].reshape(nb * tc, hw)).reshape(shp)

    @pl.when(t_id == pl.num_programs(1) - 1)
    def _():
        sout_ref[...] = s_scr[...]
        shout_ref[...] = prev_scr[...]


def _rwkv(zb, sh0, s0, wts, nb, tc, hd):
    nseq, length, rin = zb.shape
    hw = wts["w0"].shape[1]
    pairs = hw // LANES
    lora = wts["ww"].shape[0]
    body = functools.partial(_rwkv_body, nb=nb, tc=tc, hw=hw, hd=hd, lora=lora)
    names = ("mu", "w0", "ww", "a0", "aw", "gw", "kk", "ka", "rk", "lnw", "lnb", "ones", "seg")
    consts = [wts[n] for n in names]
    tok = pltpu.VMEM((nb, tc, hw), F32)
    return pl.pallas_call(
        body,
        grid=(nseq // nb, length // tc),
        in_specs=[pl.BlockSpec((nb, tc, rin), lambda s, t: (s, t, 0)),
                  pl.BlockSpec((nb, rin), lambda s, t: (s, 0)),
                  pl.BlockSpec((nb, pairs, hd, LANES), lambda s, t: (s, 0, 0, 0))]
                 + [_const_spec(c.shape) for c in consts],
        out_specs=[pl.BlockSpec((nb, tc, hw), lambda s, t: (s, t, 0)),
                   pl.BlockSpec((nb, pairs, hd, LANES), lambda s, t: (s, 0, 0, 0)),
                   pl.BlockSpec((nb, rin), lambda s, t: (s, 0))],
        out_shape=[jax.ShapeDtypeStruct((nseq, length, hw), F32),
                   jax.ShapeDtypeStruct((nseq, pairs, hd, LANES), F32),
                   jax.ShapeDtypeStruct((nseq, rin), F32)],
        scratch_shapes=[pltpu.VMEM((nb, pairs, hd, LANES), F32), pltpu.VMEM((nb, rin), F32)]
                       + [tok] * 7 + [pltpu.VMEM((nb, pairs, tc, LANES), F32),
                                      pltpu.VMEM((-(-tc // RWKV_SLOT), nb, pairs, hd, 4 * RWKV_SLOT), BF16)],
        compiler_params=_params("parallel", "arbitrary"),
        name="rwkv",
    )(zb, sh0, s0, *consts)


def _pack_pairs(s):
    n, heads, hd, _ = s.shape
    return s.reshape(n, heads // 2, 2, hd, hd).transpose(0, 1, 3, 2, 4).reshape(n, heads // 2, hd, 2 * hd)


def _unpack_pairs(s):
    n, pairs, hd, _ = s.shape
    return s.reshape(n, pairs, hd, 2, hd).transpose(0, 1, 3, 2, 4).reshape(n, 2 * pairs, hd, hd)


def _out_proj_body(oa_ref, ob_ref, x_ref, woa_ref, wob_ref, g2_ref, wq_ref, keys_ref,
                   x1_ref, h2t_ref, st_ref, *, nk, dh):
    mix = _dot(oa_ref[...].astype(BF16), woa_ref[...]) + _dot(ob_ref[...].astype(BF16), wob_ref[...])
    x1 = x_ref[...] + mix
    x1_ref[...] = x1
    h2f = _rms(x1, g2_ref[...])
    h2t_ref[...] = h2f.T.astype(BF16)
    q = _dot(h2f.astype(BF16), wq_ref[...]).astype(BF16)
    for hp in range(q.shape[1] // dh):
        st_ref[hp * nk:(hp + 1) * nk, :] = lax.dot_general(
            keys_ref[hp % 2], q[:, hp * dh:(hp + 1) * dh], _NT, preferred_element_type=F32)


def _out_proj(oa, ob, x, woa, wob, g2, wq, keys, tb):
    n, d = x.shape
    nk, dh = keys.shape[1], keys.shape[2]
    nrow = (wq.shape[1] // dh) * nk
    body = functools.partial(_out_proj_body, nk=nk, dh=dh)
    return pl.pallas_call(
        body,
        grid=(n // tb,),
        in_specs=[pl.BlockSpec((tb, oa.shape[1]), lambda i: (i, 0)),
                  pl.BlockSpec((tb, ob.shape[1]), lambda i: (i, 0)),
                  pl.BlockSpec((tb, d), lambda i: (i, 0)),
                  _const_spec(woa.shape), _const_spec(wob.shape), _const_spec(g2.shape),
                  _const_spec(wq.shape), _const_spec(keys.shape)],
        out_specs=[pl.BlockSpec((tb, d), lambda i: (i, 0)), pl.BlockSpec((d, tb), lambda i: (0, i)),
                   pl.BlockSpec((nrow, tb), lambda i: (0, i))],
        out_shape=[jax.ShapeDtypeStruct((n, d), F32), jax.ShapeDtypeStruct((d, n), BF16),
                   jax.ShapeDtypeStruct((nrow, n), F32)],
        compiler_params=_params("parallel"),
        name="out_proj",
    )(oa, ob, x, woa, wob, g2, wq, keys)


def _route_body(st_ref, t1_ref, e1_ref, e2_ref, top_scr, cand_scr, *, nk, topk):
    neg = -jnp.inf
    nx = topk + 1
    half = topk // 2

    def extract(work, n, on_value):
        for i in range(n):
            m = jnp.max(work, axis=0, keepdims=True)
            on_value(i, m)
            work = jnp.where(work == m, neg, work)

    def keep(h):
        def on_value(i, m):
            top_scr[h, i:i + 1, :] = m
        return on_value

    s1 = st_ref[0:nk, :]
    s2 = st_ref[nk:2 * nk, :]
    extract(s1, nx, keep(0))
    extract(s2, nx, keep(1))
    cand_scr[...] = jnp.full(cand_scr.shape, neg, F32)
    sv2 = top_scr[1, 0:nx, :]
    cand_scr[0:nx, :] = top_scr[0, 0:1, :] + sv2
    base = 3 * SUBLANES
    for a in range(1, half):
        cand_scr[base + (a - 1) * half:base + a * half, :] = top_scr[0, a:a + 1, :] + sv2[0:half]
    base += (half - 1) * half
    cand_scr[base:base + nx - half, :] = top_scr[0, half:nx, :] + sv2[0:1]

    best0 = top_scr[0, 0:1, :] + sv2[0:1]
    acc = {"z": jnp.zeros_like(best0), "lo": best0, "hi": best0}

    def on_best(i, m):
        if i < topk:
            acc["z"] = acc["z"] + jnp.exp(m - best0)
            acc["hi"] = m
        acc["lo"] = m

    extract(cand_scr[...], nx, on_best)
    thr = jnp.where(acc["lo"] == neg, acc["hi"], 0.5 * (acc["hi"] + acc["lo"]))
    t1_ref[0] = thr - s1
    e1_ref[0] = jnp.exp(s1 - top_scr[0, 0:1, :]) / acc["z"]
    e2_ref[0] = jnp.exp(s2 - sv2[0:1])


def _route(st, heads, nk, tb):
    n = st.shape[1]
    topk = PEER_TOPK
    half = topk // 2
    ncand = 3 * SUBLANES + (half - 1) * half + 2 * SUBLANES
    assert topk + 1 <= 3 * SUBLANES and topk + 1 - half <= 2 * SUBLANES and half == SUBLANES
    body = functools.partial(_route_body, nk=nk, topk=topk)
    blk = pl.BlockSpec((1, nk, tb), lambda i, h: (h, 0, i))
    return pl.pallas_call(
        body,
        grid=(n // tb, heads),
        in_specs=[pl.BlockSpec((2 * nk, tb), lambda i, h: (h, i))],
        out_specs=[blk, blk, blk],
        out_shape=[jax.ShapeDtypeStruct((heads, nk, n), F32)] * 3,
        scratch_shapes=[pltpu.VMEM((2, 3 * SUBLANES, tb), F32), pltpu.VMEM((ncand, tb), F32)],
        compiler_params=_params("parallel", "parallel"),
        name="route",
    )(st)


PEER_SUB = 2 * LANES
PEER_ROW_SPLIT = 2


def _peer_body(h2t_ref, s2_ref, t1_ref, e1_ref, e2_ref, u_ref, vt_ref, x1_ref, gf_ref, y_ref,
               acc_scr, coef_scr, *, heads, nk, tb, n_eb):
    e = pl.program_id(1)
    eblk = u_ref.shape[0]
    i1_per_block = eblk // nk
    i1_per_sub = PEER_SUB // nk
    cur = e % 2

    @pl.when(e == 0)
    def _():
        acc_scr[...] = jnp.zeros_like(acc_scr)
        coef_scr[1] = jnp.zeros(coef_scr.shape[1:], BF16)

    i1_0 = pl.multiple_of(jnp.minimum(e, n_eb - 1) * i1_per_block, SUBLANES)
    n_sub = eblk // PEER_SUB
    h2t = h2t_ref[...]

    def matmuls(k):
        ks = slice(k * PEER_SUB, (k + 1) * PEER_SUB)
        return _dot(u_ref[ks, :], h2t), _dot(vt_ref[:, ks], coef_scr[1 - cur, ks, :])

    act, out = matmuls(0)
    for k in range(n_sub):
        if k + 1 < n_sub:
            act_next, part = matmuls(k + 1)
            out = out + part
        for lg in range(tb // LANES):
            ls = slice(lg * LANES, (lg + 1) * LANES)
            t1b = [t1_ref[h, pl.ds(i1_0, i1_per_block), ls] for h in range(heads)]
            e1b = [e1_ref[h, pl.ds(i1_0, i1_per_block), ls] for h in range(heads)]
            for rs in range(PEER_ROW_SPLIT):
                rsl = slice(rs * nk // PEER_ROW_SPLIT, (rs + 1) * nk // PEER_ROW_SPLIT)
                gates = [None] * i1_per_sub
                for h in range(heads):
                    s2 = s2_ref[h, rsl, ls]
                    e2 = e2_ref[h, rsl, ls]
                    for jj in range(i1_per_sub):
                        j = k * i1_per_sub + jj
                        term = jnp.where(s2 >= t1b[h][j:j + 1], e1b[h][j:j + 1] * e2, 0.0)
                        gates[jj] = term if gates[jj] is None else gates[jj] + term
                for jj in range(i1_per_sub):
                    a = act[jj * nk + rsl.start:jj * nk + rsl.stop, ls]
                    gelu = 0.5 * a * (1.0 + lax.erf(a * np.float32(np.sqrt(0.5))))
                    r0 = (k * i1_per_sub + jj) * nk
                    coef_scr[cur, r0 + rsl.start:r0 + rsl.stop, ls] = (gates[jj] * gelu).astype(BF16)
        if k + 1 < n_sub:
            act = act_next
    acc_scr[...] += out

    @pl.when(e == n_eb)
    def _():
        y_ref[...] = _rms(x1_ref[...] + acc_scr[...].T, gf_ref[...])


def _peer(h2t, st, t1, e1, e2, u, vt, x1, gf, heads, nk, tb):
    n, d = x1.shape
    ne = u.shape[0]
    eblk = SUBLANES * nk
    n_eb = ne // eblk
    body = functools.partial(_peer_body, heads=heads, nk=nk, tb=tb, n_eb=n_eb)
    tok = pl.BlockSpec((heads, nk, tb), lambda i, e: (0, 0, i))
    return pl.pallas_call(
        body,
        grid=(n // tb, n_eb + 1),
        in_specs=[pl.BlockSpec((d, tb), lambda i, e: (0, i)),
                  pl.BlockSpec((heads, None, nk, tb), lambda i, e: (0, 1, 0, i)),
                  tok, tok, tok,
                  pl.BlockSpec((eblk, d), lambda i, e: (jnp.minimum(e, n_eb - 1), 0)),
                  pl.BlockSpec((d, eblk), lambda i, e: (0, jnp.maximum(e - 1, 0))),
                  pl.BlockSpec((tb, d), lambda i, e: (i, 0)),
                  _const_spec(gf.shape)],
        out_specs=pl.BlockSpec((tb, d), lambda i, e: (i, 0)),
        out_shape=jax.ShapeDtypeStruct((n, d), F32),
        scratch_shapes=[pltpu.VMEM((d, tb), F32), pltpu.VMEM((2, eblk, tb), BF16)],
        compiler_params=_params("parallel", "arbitrary"),
        name="peer",
    )(h2t, st.reshape(heads, 2, nk, n), t1, e1, e2, u, vt, x1, gf)


def _block(n, pref):
    b = pref
    while n % b:
        b //= 2
    return b


def kernel(x_prompt, x_sample, state_hgrn, state_rwkv, state_shift, meta_tokens, norm1, w_in, lb_theta, hgrn_norm, mu_shift, w0, w_w2, a0, a_w2, g_w2, k_k, k_a, r_k, ln_w, ln_b, w_out, norm2, peer_wq, peer_keys, peer_u, peer_v, norm_f):
    assert w_in.shape[0] == 1, "single-layer trunk"
    bp, sp, d = x_prompt.shape
    bs, ss, _ = x_sample.shape
    n_meta = meta_tokens.shape[0]
    _, _, ha_heads, dk, dv = state_hgrn.shape
    _, _, hb_heads, hd, _ = state_rwkv.shape
    ha_in = 2 * ha_heads * dk + 2 * ha_heads * dv
    hw = hb_heads * hd
    lw, la = w_w2.shape[1], a_w2.shape[1]
    nk, dh = peer_keys.shape[2], peer_keys.shape[3]
    p_heads = peer_wq.shape[2] // (2 * dh)
    assert sp % HGRN_CHUNK == 0 and ss <= HGRN_CHUNK and n_meta <= HGRN_CHUNK
    assert hb_heads % 2 == 0 and 2 * hd == LANES and lw + la == LANES and nk == LANES

    row = lambda t: t.reshape(1, -1).astype(F32)
    w_a = w_in[0, :, :ha_in].astype(BF16)
    w_b = w_in[0, :, ha_in:].astype(BF16)
    g1 = row(norm1[0])
    lbt = lb_theta.astype(F32)
    gn = row(hgrn_norm[0])

    head_of_lane = np.arange(LANES) // hd
    head_of_col = np.arange(hw) // hd
    rw = {
        "mu": row(mu_shift[0]), "w0": row(w0[0]), "a0": row(a0[0]), "kk": row(k_k[0]), "ka": row(k_a[0]),
        "rk": row(r_k[0]), "lnw": row(ln_w[0]), "lnb": row(ln_b[0]), "gw": g_w2[0].astype(F32),
        "ww": jnp.concatenate([w_w2[0], jnp.zeros((la, hw), F32)], axis=0),
        "aw": jnp.concatenate([jnp.zeros((lw, hw), F32), a_w2[0]], axis=0),
        "ones": jnp.asarray(np.tile(head_of_lane[:, None] == head_of_lane[None, :], (2, 1)), BF16),
        "seg": jnp.asarray(head_of_col[:, None] == head_of_col[None, :], BF16),
    }
    woa = w_out[0, :ha_heads * dv].astype(BF16)
    wob = w_out[0, ha_heads * dv:].astype(BF16)
    g2 = row(norm2[0])
    wq = peer_wq[0].astype(BF16)
    keys = peer_keys[0].astype(BF16)
    u = peer_u[0].astype(BF16)
    vt = peer_v[0].astype(BF16).T
    gf = row(norm_f)

    za_m, zb_m = _in_proj(meta_tokens.astype(F32), g1, w_a, w_b, n_meta)
    _, hs_m = _hgrn(za_m[None], 1, n_meta, lbt, gn, jnp.zeros((1, ha_heads, dk, dv), F32), ha_heads, dk, dv)
    _, rs_m, sh_m = _rwkv(zb_m.reshape(1, n_meta, -1), jnp.zeros((1, zb_m.shape[1]), F32),
                          jnp.zeros((1, hb_heads // 2, hd, LANES), F32), rw, 1, n_meta, hd)

    def group(x, nseq, length, hs0, rs0, sh0, chunk_rows, hgrn_nb, tc):
        xf = x.reshape(nseq * length, d)
        n = xf.shape[0]
        tb = _block(n, 256)
        assert tb % LANES == 0
        za, zb = _in_proj(xf, g1, w_a, w_b, tb)
        oa, hs = _hgrn(za.reshape(nseq, length, -1), hgrn_nb, chunk_rows, lbt, gn, hs0, ha_heads, dk, dv)
        ob, rs, sh = _rwkv(zb.reshape(nseq, length, -1), sh0, rs0, rw, 8, tc, hd)
        x1, h2t, st = _out_proj(oa.reshape(n, -1), ob.reshape(n, hw), xf, woa, wob, g2, wq, keys, tb)
        t1, e1, e2 = _route(st, p_heads, nk, tb)
        y = _peer(h2t, st, t1, e1, e2, u, vt, x1, gf, p_heads, nk, tb)
        return y.reshape(nseq, length, d), hs, _unpack_pairs(rs), sh

    y_p, hs_p, rs_p, sh_p = group(
        x_prompt, bp, sp, hs_m, jnp.broadcast_to(rs_m, (bp,) + rs_m.shape[1:]),
        jnp.broadcast_to(sh_m, (bp, sh_m.shape[1])), HGRN_CHUNK, 4, 64)
    y_s, hs_s, rs_s, sh_s = group(
        x_sample, bs, ss, state_hgrn[0].astype(F32), _pack_pairs(state_rwkv[0].astype(F32)),
        state_shift[0].astype(F32), ss, 8, ss)

    return (y_p, y_s, hs_p[None], rs_p[None], sh_p[None], hs_s[None], rs_s[None], sh_s[None])
```

```python
import functools

import numpy as np
import jax
import jax.numpy as jnp
from jax import lax
from jax.experimental import pallas as pl
from jax.experimental.pallas import tpu as pltpu

F32 = jnp.float32
BF16 = jnp.bfloat16

EPS = 1e-6
GN_EPS = 64e-5
HGRN_CHUNK = 64
HGRN_SHORT_CHUNK = 16
PEER_TOPK = 16
RWKV_SLOT = 32
LANES = 128
SUBLANES = 8
VMEM_LIMIT_BYTES = 52 * 1024 * 1024

_NT = (((1,), (1,)), ((), ()))
_TN = (((0,), (0,)), ((), ()))


def _dot(a, b):
    return jnp.dot(a, b, preferred_element_type=F32)


def _split3(a):
    a1 = a.astype(BF16)
    r1 = a - a1.astype(F32)
    a2 = r1.astype(BF16)
    a3 = (r1 - a2.astype(F32)).astype(BF16)
    return a1, a2, a3


def _dot_exact_rhs(a, b_exact):
    a1, a2, a3 = _split3(a)
    return _dot(a1, b_exact) + _dot(a2, b_exact) + _dot(a3, b_exact)


def _dot_exact_lhs(a_exact, b):
    b1, b2, b3 = _split3(b)
    return _dot(a_exact, b1) + _dot(a_exact, b2) + _dot(a_exact, b3)


def _dot_hi(a, b):
    ah = a.astype(BF16)
    al = (a - ah.astype(F32)).astype(BF16)
    bh = b.astype(BF16)
    bl = (b - bh.astype(F32)).astype(BF16)
    return _dot(ah, bh) + _dot(ah, bl) + _dot(al, bh)


def _sigmoid(x):
    return 1.0 / (1.0 + jnp.exp(-x))


def _rms(x, g):
    return x * lax.rsqrt(jnp.mean(x * x, axis=-1, keepdims=True) + EPS) * g


def _params(*sem):
    return pltpu.CompilerParams(dimension_semantics=sem, vmem_limit_bytes=VMEM_LIMIT_BYTES)


def _const_spec(shape):
    nd = len(shape)
    return pl.BlockSpec(shape, lambda *_: (0,) * nd)


def _in_proj_body(x_ref, g_ref, wa_ref, wb_ref, za_ref, zb_ref):
    hb = _rms(x_ref[...], g_ref[...]).astype(BF16)
    za_ref[...] = _dot(hb, wa_ref[...])
    zb_ref[...] = _dot(hb, wb_ref[...])


def _in_proj(x, g, wa, wb, tb):
    n, d = x.shape
    na, nb = wa.shape[1], wb.shape[1]
    return pl.pallas_call(
        _in_proj_body,
        grid=(n // tb,),
        in_specs=[pl.BlockSpec((tb, d), lambda i: (i, 0)), _const_spec((1, d)),
                  _const_spec(wa.shape), _const_spec(wb.shape)],
        out_specs=[pl.BlockSpec((tb, na), lambda i: (i, 0)), pl.BlockSpec((tb, nb), lambda i: (i, 0))],
        out_shape=[jax.ShapeDtypeStruct((n, na), F32), jax.ShapeDtypeStruct((n, nb), F32)],
        compiler_params=_params("parallel"),
        name="in_proj",
    )(x, g, wa, wb)


def _hgrn_tables(c):
    nl = int(np.log2(c)) + 1
    mats = np.zeros((nl, c, c), np.float32)
    masks = np.zeros((nl, c, c), np.float32)
    idx = np.arange(c)
    mats[0] = (idx[None, :] <= idx[:, None])
    masks[0] = np.eye(c)
    for l in range(1, nl):
        blk = 1 << l
        half = blk >> 1
        for t in range(c):
            start = (t // blk) * blk
            ref = start + half - 1
            if t - start >= half:
                mats[l, t, ref + 1:t + 1] = 1.0
                masks[l, t, start:start + half] = 1.0
            else:
                mats[l, t, t + 1:ref + 1] = 1.0
    return mats.reshape(nl * c, c), masks


def _hgrn_body(za_ref, lbt_ref, gn_ref, s0_ref, mats_ref, masks_ref, oa_ref, sout_ref, st_scr,
               *, nb, rows, chunk, heads, dk, dv):
    c_id = pl.program_id(1)
    qk = heads * dk
    wd = heads * dv
    nl = masks_ref.shape[0]

    @pl.when(c_id == 0)
    def _():
        for s in range(nb):
            for h in range(heads):
                st_scr[s, h] = s0_ref[min(s, s0_ref.shape[0] - 1), h].T

    th = lbt_ref[...]
    ex = jnp.exp(th - jnp.max(th, axis=0, keepdims=True))
    lb = ex[0:1] / jnp.sum(ex, axis=0, keepdims=True)
    gn = gn_ref[...]

    for s in range(nb):
        za = za_ref[s]
        zq = za[:, :qk]
        zf = za[:, qk:2 * qk]
        zi = za[:, 2 * qk:2 * qk + wd]
        zo = za[:, 2 * qk + wd:]
        logf = jnp.log(lb + (1.0 - lb) * _sigmoid(zf))
        kin = (1.0 - lb) * _sigmoid(-zf)
        qa = zq * _sigmoid(zq)
        if rows < chunk:
            pad = jnp.zeros((chunk - rows, qk), F32)
            logf = jnp.concatenate([logf, pad], axis=0)
            kin = jnp.concatenate([kin, pad], axis=0)
            qa = jnp.concatenate([qa, pad], axis=0)
            zi = jnp.concatenate([zi, jnp.zeros((chunk - rows, wd), F32)], axis=0)

        dec = _dot_exact_lhs(mats_ref[...], logf)
        for h in range(heads):
            ks = slice(h * dk, (h + 1) * dk)
            vs = slice(h * dv, (h + 1) * dv)
            q = qa[:, ks]
            k = kin[:, ks]
            v = zi[:, vs].astype(BF16)
            cum = dec[0:chunk, ks]
            sc = lax.dot_general(q.astype(BF16), k.astype(BF16), _NT, preferred_element_type=F32)
            scores = jnp.where(masks_ref[0] != 0.0, sc, 0.0)
            for l in range(1, nl):
                e = jnp.exp(dec[l * chunk:(l + 1) * chunk, ks])
                sc = lax.dot_general((q * e).astype(BF16), (k * e).astype(BF16), _NT, preferred_element_type=F32)
                scores = scores + jnp.where(masks_ref[l] != 0.0, sc, 0.0)
            st = st_scr[s, h]
            o = _dot(scores.astype(BF16), v) + lax.dot_general(
                (q * jnp.exp(cum)).astype(BF16), st.astype(BF16), _NT, preferred_element_type=F32)
            last = cum[chunk - 1:chunk, :]
            kd = (k * jnp.exp(last - cum)).astype(BF16)
            st_scr[s, h] = jnp.exp(last) * st + lax.dot_general(v, kd, _TN, preferred_element_type=F32)

            o = o[:rows]
            o = o * lax.rsqrt(jnp.mean(o * o, axis=-1, keepdims=True) + EPS) * gn[:, vs]
            zoh = zo[:, vs]
            oa_ref[s, :, vs] = o * (zoh * _sigmoid(zoh))

    @pl.when(c_id == pl.num_programs(1) - 1)
    def _():
        for s in range(nb):
            for h in range(heads):
                sout_ref[s, h] = st_scr[s, h].T


def _hgrn(za, nb, rows, lb_theta, gn, s0, heads, dk, dv):
    nseq, length, cols = za.shape
    chunk = HGRN_CHUNK if rows > HGRN_SHORT_CHUNK else HGRN_SHORT_CHUNK
    mats, masks = _hgrn_tables(chunk)
    mats = jnp.asarray(mats, BF16)
    masks = jnp.asarray(masks, F32)
    wd = heads * dv
    shared = s0.shape[0] == 1
    s0_spec = (pl.BlockSpec((1, heads, dk, dv), lambda s, c: (0, 0, 0, 0)) if shared
               else pl.BlockSpec((nb, heads, dk, dv), lambda s, c: (s, 0, 0, 0)))
    body = functools.partial(_hgrn_body, nb=nb, rows=rows, chunk=chunk, heads=heads, dk=dk, dv=dv)
    return pl.pallas_call(
        body,
        grid=(nseq // nb, length // rows),
        in_specs=[pl.BlockSpec((nb, rows, cols), lambda s, c: (s, c, 0)),
                  _const_spec(lb_theta.shape), _const_spec(gn.shape), s0_spec,
                  _const_spec(mats.shape), _const_spec(masks.shape)],
        out_specs=[pl.BlockSpec((nb, rows, wd), lambda s, c: (s, c, 0)),
                   pl.BlockSpec((nb, heads, dk, dv), lambda s, c: (s, 0, 0, 0))],
        out_shape=[jax.ShapeDtypeStruct((nseq, length, wd), F32),
                   jax.ShapeDtypeStruct((nseq, heads, dk, dv), F32)],
        scratch_shapes=[pltpu.VMEM((nb, heads, dv, dk), F32)],
        compiler_params=_params("parallel", "arbitrary"),
        name="hgrn",
    )(za, lb_theta, gn, s0, mats, masks)


def _rwkv_body(zb_ref, sh0_ref, s0_ref, mu_ref, w0_ref, ww_ref, a0_ref, aw_ref, gw_ref, kk_ref, ka_ref,
               rk_ref, lnw_ref, lnb_ref, ones_ref, seg_ref,
               ob_ref, sout_ref, shout_ref,
               s_scr, prev_scr, al_scr, w_scr, be_scr, k_scr, r_scr, v_scr, g_scr, y_scr, vt_scr,
               *, nb, tc, hw, hd, lora):
    t_id = pl.program_id(1)
    pairs = hw // LANES
    rin = zb_ref.shape[-1]

    @pl.when(t_id == 0)
    def _():
        s_scr[...] = s0_ref[...]
        prev_scr[...] = sh0_ref[...]

    row = lax.broadcasted_iota(jnp.int32, (tc, rin), 0)
    mu = mu_ref[...]
    zms = []
    for b in range(nb):
        zb = zb_ref[b]
        prev = jnp.where(row == 0, prev_scr[b:b + 1, :], pltpu.roll(zb, 1, axis=0))
        prev_scr[b:b + 1, :] = zb[tc - 1:tc, :]
        zms.append(zb + (prev - zb) * mu)
    zm = jnp.concatenate(zms, axis=0)
    r = zm[:, :hw]
    k = zm[:, hw:2 * hw]
    v = zm[:, 2 * hw:3 * hw]
    dwa = zm[:, 3 * hw:3 * hw + lora]
    dg = zm[:, 3 * hw + lora:]
    w_raw = w0_ref[...] + _dot_hi(jnp.tanh(dwa), ww_ref[...])
    decay = jnp.exp(-np.float32(np.exp(-0.5)) * _sigmoid(w_raw))
    a = _sigmoid(a0_ref[...] + _dot_hi(dwa, aw_ref[...]))
    g = _dot_hi(_sigmoid(dg), gw_ref[...])
    kk = k * kk_ref[...]
    ss = _dot_exact_rhs(kk * kk, seg_ref[...])
    kk = kk / jnp.maximum(jnp.sqrt(ss), 1e-12)
    k2 = k * (1.0 + (a - 1.0) * ka_ref[...])
    shp = (nb, tc, hw)
    al_scr[...] = (-kk).reshape(shp)
    w_scr[...] = decay.reshape(shp)
    be_scr[...] = (kk * a).reshape(shp)
    k_scr[...] = k2.reshape(shp)
    r_scr[...] = r.reshape(shp)
    v_scr[...] = v.reshape(shp)
    g_scr[...] = g.reshape(shp)

    slot = RWKV_SLOT
    li2 = lax.broadcasted_iota(jnp.int32, (slot, LANES), 1)
    vi = lax.broadcasted_iota(jnp.int32, (hd, LANES), 0)
    sel = jnp.where(vi == lax.broadcasted_iota(jnp.int32, (hd, LANES), 1) % hd, 1.0, 0.0).astype(BF16)
    for b in range(nb):
        for p in range(pairs):
            for g0 in range(0, tc, slot):
                nt = min(slot, tc - g0)
                vb = v[b * tc + g0:b * tc + g0 + nt, p * LANES:(p + 1) * LANES]
                if nt < slot:
                    vb = jnp.concatenate([vb, jnp.zeros((slot - nt, LANES), F32)], axis=0)
                vh = vb.astype(BF16)
                vl = (vb - vh.astype(F32)).astype(BF16)
                zero = jnp.zeros_like(vh)
                stack = jnp.concatenate([jnp.where(li2 < hd, vh, zero), jnp.where(li2 >= hd, vh, zero),
                                         jnp.where(li2 < hd, vl, zero), jnp.where(li2 >= hd, vl, zero)], axis=0)
                vt_scr[g0 // slot, b, p] = lax.dot_general(
                    sel, stack, _NT, preferred_element_type=F32).astype(BF16)

    rows = pairs * hd
    ri = lax.broadcasted_iota(jnp.int32, (rows, LANES), 0)
    li = lax.broadcasted_iota(jnp.int32, (rows, LANES), 1)
    diag = (ri % hd) == (li % hd)
    ones1 = ones_ref[...]
    qi = lax.broadcasted_iota(jnp.int32, (4 * slot, LANES), 0)
    ql = lax.broadcasted_iota(jnp.int32, (4 * slot, LANES), 1)
    tok_of_row = jnp.where((qi // slot) % 2 == ql // hd, qi % slot, -1)

    def bcast(ref, b, t):
        rowv = ref[b, pl.ds(t, 1), :]
        return jnp.concatenate(
            [jnp.broadcast_to(rowv[:, p * LANES:(p + 1) * LANES], (hd, LANES)) for p in range(pairs)], axis=0)

    def step(t, carry):
        pick = jnp.where(tok_of_row == t % slot, 1.0, 0.0).astype(BF16)
        seqs = range(nb)
        vcols = [_dot(vt_scr[t // slot, b].reshape(rows, 4 * slot), pick) for b in seqs]
        for b in seqs:
            old = s_scr[b].reshape(rows, LANES)
            sa = _dot((old * bcast(al_scr, b, t)).astype(BF16), ones1)
            new = old * bcast(w_scr, b, t) + sa * bcast(be_scr, b, t) + vcols[b] * bcast(k_scr, b, t)
            s_scr[b] = new.reshape(pairs, hd, LANES)
            yrep = _dot((new * bcast(r_scr, b, t)).astype(BF16), ones1)
            ysel = jnp.where(diag, yrep, 0.0).reshape(pairs, hd, LANES)
            for p in range(pairs):
                y_scr[b, p, pl.ds(t, 1), :] = jnp.sum(ysel[p], axis=0, keepdims=True)
        return carry

    lax.fori_loop(0, tc, step, 0, unroll=4)

    y = jnp.concatenate([y_scr[:, p].reshape(nb * tc, LANES) for p in range(pairs)], axis=1)
    seg = seg_ref[...]
    inv = np.float32(1.0 / hd)
    mean = _dot_exact_rhs(y, seg) * inv
    yc = y - mean
    var = _dot_exact_rhs(yc * yc, seg) * inv
    yn = yc * lax.rsqrt(var + GN_EPS) * lnw_ref[...] + lnb_ref[...]
    r = r_scr[...].reshape(nb * tc, hw)
    k2 = k_scr[...].reshape(nb * tc, hw)
    v = v_scr[...].reshape(nb * tc, hw)
    bonus = _dot_exact_rhs(r * k2 * rk_ref[...], seg) * v
    ob_ref[...] = ((yn + bonus) * g_scr[...].reshape(nb * tc, hw)).reshape(shp)

    @pl.when(t_id == pl.num_programs(1) - 1)
    def _():
        sout_ref[...] = s_scr[...]
        shout_ref[...] = prev_scr[...]


def _rwkv(zb, sh0, s0, wts, nb, tc, hd):
    nseq, length, rin = zb.shape
    hw = wts["w0"].shape[1]
    pairs = hw // LANES
    lora = wts["ww"].shape[0]
    body = functools.partial(_rwkv_body, nb=nb, tc=tc, hw=hw, hd=hd, lora=lora)
    names = ("mu", "w0", "ww", "a0", "aw", "gw", "kk", "ka", "rk", "lnw", "lnb", "ones", "seg")
    consts = [wts[n] for n in names]
    tok = pltpu.VMEM((nb, tc, hw), F32)
    return pl.pallas_call(
        body,
        grid=(nseq // nb, length // tc),
        in_specs=[pl.BlockSpec((nb, tc, rin), lambda s, t: (s, t, 0)),
                  pl.BlockSpec((nb, rin), lambda s, t: (s, 0)),
                  pl.BlockSpec((nb, pairs, hd, LANES), lambda s, t: (s, 0, 0, 0))]
                 + [_const_spec(c.shape) for c in consts],
        out_specs=[pl.BlockSpec((nb, tc, hw), lambda s, t: (s, t, 0)),
                   pl.BlockSpec((nb, pairs, hd, LANES), lambda s, t: (s, 0, 0, 0)),
                   pl.BlockSpec((nb, rin), lambda s, t: (s, 0))],
        out_shape=[jax.ShapeDtypeStruct((nseq, length, hw), F32),
                   jax.ShapeDtypeStruct((nseq, pairs, hd, LANES), F32),
                   jax.ShapeDtypeStruct((nseq, rin), F32)],
        scratch_shapes=[pltpu.VMEM((nb, pairs, hd, LANES), F32), pltpu.VMEM((nb, rin), F32)]
                       + [tok] * 7 + [pltpu.VMEM((nb, pairs, tc, LANES), F32),
                                      pltpu.VMEM((-(-tc // RWKV_SLOT), nb, pairs, hd, 4 * RWKV_SLOT), BF16)],
        compiler_params=_params("parallel", "arbitrary"),
        name="rwkv",
    )(zb, sh0, s0, *consts)


def _pack_pairs(s):
    n, heads, hd, _ = s.shape
    return s.reshape(n, heads // 2, 2, hd, hd).transpose(0, 1, 3, 2, 4).reshape(n, heads // 2, hd, 2 * hd)


def _unpack_pairs(s):
    n, pairs, hd, _ = s.shape
    return s.reshape(n, pairs, hd, 2, hd).transpose(0, 1, 3, 2, 4).reshape(n, 2 * pairs, hd, hd)


def _out_proj_body(oa_ref, ob_ref, x_ref, woa_ref, wob_ref, g2_ref, wq_ref, keys_ref,
                   x1_ref, h2t_ref, st_ref, *, nk, dh):
    mix = _dot(oa_ref[...].astype(BF16), woa_ref[...]) + _dot(ob_ref[...].astype(BF16), wob_ref[...])
    x1 = x_ref[...] + mix
    x1_ref[...] = x1
    h2f = _rms(x1, g2_ref[...])
    h2t_ref[...] = h2f.T.astype(BF16)
    q = _dot(h2f.astype(BF16), wq_ref[...]).astype(BF16)
    for hp in range(q.shape[1] // dh):
        st_ref[hp * nk:(hp + 1) * nk, :] = lax.dot_general(
            keys_ref[hp % 2], q[:, hp * dh:(hp + 1) * dh], _NT, preferred_element_type=F32)


def _out_proj(oa, ob, x, woa, wob, g2, wq, keys, tb):
    n, d = x.shape
    nk, dh = keys.shape[1], keys.shape[2]
    nrow = (wq.shape[1] // dh) * nk
    body = functools.partial(_out_proj_body, nk=nk, dh=dh)
    return pl.pallas_call(
        body,
        grid=(n // tb,),
        in_specs=[pl.BlockSpec((tb, oa.shape[1]), lambda i: (i, 0)),
                  pl.BlockSpec((tb, ob.shape[1]), lambda i: (i, 0)),
                  pl.BlockSpec((tb, d), lambda i: (i, 0)),
                  _const_spec(woa.shape), _const_spec(wob.shape), _const_spec(g2.shape),
                  _const_spec(wq.shape), _const_spec(keys.shape)],
        out_specs=[pl.BlockSpec((tb, d), lambda i: (i, 0)), pl.BlockSpec((d, tb), lambda i: (0, i)),
                   pl.BlockSpec((nrow, tb), lambda i: (0, i))],
        out_shape=[jax.ShapeDtypeStruct((n, d), F32), jax.ShapeDtypeStruct((d, n), BF16),
                   jax.ShapeDtypeStruct((nrow, n), F32)],
        compiler_params=_params("parallel"),
        name="out_proj",
    )(oa, ob, x, woa, wob, g2, wq, keys)


def _route_body(st_ref, t1_ref, e1_ref, e2_ref, top_scr, cand_scr, *, nk, topk):
    neg = -jnp.inf
    nx = topk + 1
    half = topk // 2

    def extract(work, n, on_value):
        for i in range(n):
            m = jnp.max(work, axis=0, keepdims=True)
            on_value(i, m)
            work = jnp.where(work == m, neg, work)

    def keep(h):
        def on_value(i, m):
            top_scr[h, i:i + 1, :] = m
        return on_value

    s1 = st_ref[0:nk, :]
    s2 = st_ref[nk:2 * nk, :]
    extract(s1, nx, keep(0))
    extract(s2, nx, keep(1))
    cand_scr[...] = jnp.full(cand_scr.shape, neg, F32)
    sv2 = top_scr[1, 0:nx, :]
    cand_scr[0:nx, :] = top_scr[0, 0:1, :] + sv2
    base = 3 * SUBLANES
    for a in range(1, half):
        cand_scr[base + (a - 1) * half:base + a * half, :] = top_scr[0, a:a + 1, :] + sv2[0:half]
    base += (half - 1) * half
    cand_scr[base:base + nx - half, :] = top_scr[0, half:nx, :] + sv2[0:1]

    best0 = top_scr[0, 0:1, :] + sv2[0:1]
    acc = {"z": jnp.zeros_like(best0), "lo": best0, "hi": best0}

    def on_best(i, m):
        if i < topk:
            acc["z"] = acc["z"] + jnp.exp(m - best0)
            acc["hi"] = m
        acc["lo"] = m

    extract(cand_scr[...], nx, on_best)
    thr = jnp.where(acc["lo"] == neg, acc["hi"], 0.5 * (acc["hi"] + acc["lo"]))
    t1_ref[0] = thr - s1
    e1_ref[0] = jnp.exp(s1 - top_scr[0, 0:1, :]) / acc["z"]
    e2_ref[0] = jnp.exp(s2 - sv2[0:1])


def _route(st, heads, nk, tb):
    n = st.shape[1]
    topk = PEER_TOPK
    half = topk // 2
    ncand = 3 * SUBLANES + (half - 1) * half + 2 * SUBLANES
    assert topk + 1 <= 3 * SUBLANES and topk + 1 - half <= 2 * SUBLANES and half == SUBLANES
    body = functools.partial(_route_body, nk=nk, topk=topk)
    blk = pl.BlockSpec((1, nk, tb), lambda i, h: (h, 0, i))
    return pl.pallas_call(
        body,
        grid=(n // tb, heads),
        in_specs=[pl.BlockSpec((2 * nk, tb), lambda i, h: (h, i))],
        out_specs=[blk, blk, blk],
        out_shape=[jax.ShapeDtypeStruct((heads, nk, n), F32)] * 3,
        scratch_shapes=[pltpu.VMEM((2, 3 * SUBLANES, tb), F32), pltpu.VMEM((ncand, tb), F32)],
        compiler_params=_params("parallel", "parallel"),
        name="route",
    )(st)


PEER_SUB = 2 * LANES
PEER_ROW_SPLIT = 2


def _peer_body(h2t_ref, s2_ref, t1_ref, e1_ref, e2_ref, u_ref, vt_ref, x1_ref, gf_ref, y_ref,
               acc_scr, coef_scr, *, heads, nk, tb, n_eb):
    e = pl.program_id(1)
    eblk = u_ref.shape[0]
    i1_per_block = eblk // nk
    i1_per_sub = PEER_SUB // nk
    cur = e % 2

    @pl.when(e == 0)
    def _():
        acc_scr[...] = jnp.zeros_like(acc_scr)
        coef_scr[1] = jnp.zeros(coef_scr.shape[1:], BF16)

    i1_0 = pl.multiple_of(jnp.minimum(e, n_eb - 1) * i1_per_block, SUBLANES)
    n_sub = eblk // PEER_SUB
    h2t = h2t_ref[...]

    def matmuls(k):
        ks = slice(k * PEER_SUB, (k + 1) * PEER_SUB)
        return _dot(u_ref[ks, :], h2t), _dot(vt_ref[:, ks], coef_scr[1 - cur, ks, :])

    act, out = matmuls(0)
    for k in range(n_sub):
        if k + 1 < n_sub:
            act_next, part = matmuls(k + 1)
            out = out + part
        for lg in range(tb // LANES):
            ls = slice(lg * LANES, (lg + 1) * LANES)
            t1b = [t1_ref[h, pl.ds(i1_0, i1_per_block), ls] for h in range(heads)]
            e1b = [e1_ref[h, pl.ds(i1_0, i1_per_block), ls] for h in range(heads)]
            for rs in range(PEER_ROW_SPLIT):
                rsl = slice(rs * nk // PEER_ROW_SPLIT, (rs + 1) * nk // PEER_ROW_SPLIT)
                gates = [None] * i1_per_sub
                for h in range(heads):
                    s2 = s2_ref[h, rsl, ls]
                    e2 = e2_ref[h, rsl, ls]
                    for jj in range(i1_per_sub):
                        j = k * i1_per_sub + jj
                        term = jnp.where(s2 >= t1b[h][j:j + 1], e1b[h][j:j + 1] * e2, 0.0)
                        gates[jj] = term if gates[jj] is None else gates[jj] + term
                for jj in range(i1_per_sub):
                    a = act[jj * nk + rsl.start:jj * nk + rsl.stop, ls]
                    gelu = 0.5 * a * (1.0 + lax.erf(a * np.float32(np.sqrt(0.5))))
                    r0 = (k * i1_per_sub + jj) * nk
                    coef_scr[cur, r0 + rsl.start:r0 + rsl.stop, ls] = (gates[jj] * gelu).astype(BF16)
        if k + 1 < n_sub:
            act = act_next
    acc_scr[...] += out

    @pl.when(e == n_eb)
    def _():
        y_ref[...] = _rms(x1_ref[...] + acc_scr[...].T, gf_ref[...])


def _peer(h2t, st, t1, e1, e2, u, vt, x1, gf, heads, nk, tb):
    n, d = x1.shape
    ne = u.shape[0]
    eblk = SUBLANES * nk
    n_eb = ne // eblk
    body = functools.partial(_peer_body, heads=heads, nk=nk, tb=tb, n_eb=n_eb)
    tok = pl.BlockSpec((heads, nk, tb), lambda i, e: (0, 0, i))
    return pl.pallas_call(
        body,
        grid=(n // tb, n_eb + 1),
        in_specs=[pl.BlockSpec((d, tb), lambda i, e: (0, i)),
                  pl.BlockSpec((heads, None, nk, tb), lambda i, e: (0, 1, 0, i)),
                  tok, tok, tok,
                  pl.BlockSpec((eblk, d), lambda i, e: (jnp.minimum(e, n_eb - 1), 0)),
                  pl.BlockSpec((d, eblk), lambda i, e: (0, jnp.maximum(e - 1, 0))),
                  pl.BlockSpec((tb, d), lambda i, e: (i, 0)),
                  _const_spec(gf.shape)],
        out_specs=pl.BlockSpec((tb, d), lambda i, e: (i, 0)),
        out_shape=jax.ShapeDtypeStruct((n, d), F32),
        scratch_shapes=[pltpu.VMEM((d, tb), F32), pltpu.VMEM((2, eblk, tb), BF16)],
        compiler_params=_params("parallel", "arbitrary"),
        name="peer",
    )(h2t, st.reshape(heads, 2, nk, n), t1, e1, e2, u, vt, x1, gf)


def _block(n, pref):
    b = pref
    while n % b:
        b //= 2
    return b


def kernel(x_prompt, x_sample, state_hgrn, state_rwkv, state_shift, meta_tokens, norm1, w_in, lb_theta, hgrn_norm, mu_shift, w0, w_w2, a0, a_w2, g_w2, k_k, k_a, r_k, ln_w, ln_b, w_out, norm2, peer_wq, peer_keys, peer_u, peer_v, norm_f):
    assert w_in.shape[0] == 1, "single-layer trunk"
    bp, sp, d = x_prompt.shape
    bs, ss, _ = x_sample.shape
    n_meta = meta_tokens.shape[0]
    _, _, ha_heads, dk, dv = state_hgrn.shape
    _, _, hb_heads, hd, _ = state_rwkv.shape
    ha_in = 2 * ha_heads * dk + 2 * ha_heads * dv
    hw = hb_heads * hd
    lw, la = w_w2.shape[1], a_w2.shape[1]
    nk, dh = peer_keys.shape[2], peer_keys.shape[3]
    p_heads = peer_wq.shape[2] // (2 * dh)
    assert sp % HGRN_CHUNK == 0 and ss <= HGRN_CHUNK and n_meta <= HGRN_CHUNK
    assert hb_heads % 2 == 0 and 2 * hd == LANES and lw + la == LANES and nk == LANES

    row = lambda t: t.reshape(1, -1).astype(F32)
    w_a = w_in[0, :, :ha_in].astype(BF16)
    w_b = w_in[0, :, ha_in:].astype(BF16)
    g1 = row(norm1[0])
    lbt = lb_theta.astype(F32)
    gn = row(hgrn_norm[0])

    head_of_lane = np.arange(LANES) // hd
    head_of_col = np.arange(hw) // hd
    rw = {
        "mu": row(mu_shift[0]), "w0": row(w0[0]), "a0": row(a0[0]), "kk": row(k_k[0]), "ka": row(k_a[0]),
        "rk": row(r_k[0]), "lnw": row(ln_w[0]), "lnb": row(ln_b[0]), "gw": g_w2[0].astype(F32),
        "ww": jnp.concatenate([w_w2[0], jnp.zeros((la, hw), F32)], axis=0),
        "aw": jnp.concatenate([jnp.zeros((lw, hw), F32), a_w2[0]], axis=0),
        "ones": jnp.asarray(head_of_lane[:, None] == head_of_lane[None, :], BF16),
        "seg": jnp.asarray(head_of_col[:, None] == head_of_col[None, :], BF16),
    }
    woa = w_out[0, :ha_heads * dv].astype(BF16)
    wob = w_out[0, ha_heads * dv:].astype(BF16)
    g2 = row(norm2[0])
    wq = peer_wq[0].astype(BF16)
    keys = peer_keys[0].astype(BF16)
    u = peer_u[0].astype(BF16)
    vt = peer_v[0].astype(BF16).T
    gf = row(norm_f)

    za_m, zb_m = _in_proj(meta_tokens.astype(F32), g1, w_a, w_b, n_meta)
    _, hs_m = _hgrn(za_m[None], 1, n_meta, lbt, gn, jnp.zeros((1, ha_heads, dk, dv), F32), ha_heads, dk, dv)
    _, rs_m, sh_m = _rwkv(zb_m.reshape(1, n_meta, -1), jnp.zeros((1, zb_m.shape[1]), F32),
                          jnp.zeros((1, hb_heads // 2, hd, LANES), F32), rw, 1, n_meta, hd)

    def group(x, nseq, length, hs0, rs0, sh0, chunk_rows, hgrn_nb, tc):
        xf = x.reshape(nseq * length, d)
        n = xf.shape[0]
        tb = _block(n, 256)
        assert tb % LANES == 0
        za, zb = _in_proj(xf, g1, w_a, w_b, tb)
        oa, hs = _hgrn(za.reshape(nseq, length, -1), hgrn_nb, chunk_rows, lbt, gn, hs0, ha_heads, dk, dv)
        ob, rs, sh = _rwkv(zb.reshape(nseq, length, -1), sh0, rs0, rw, 8, tc, hd)
        x1, h2t, st = _out_proj(oa.reshape(n, -1), ob.reshape(n, hw), xf, woa, wob, g2, wq, keys, tb)
        t1, e1, e2 = _route(st, p_heads, nk, tb)
        y = _peer(h2t, st, t1, e1, e2, u, vt, x1, gf, p_heads, nk, tb)
        return y.reshape(nseq, length, d), hs, _unpack_pairs(rs), sh

    y_p, hs_p, rs_p, sh_p = group(
        x_prompt, bp, sp, hs_m, jnp.broadcast_to(rs_m, (bp,) + rs_m.shape[1:]),
        jnp.broadcast_to(sh_m, (bp, sh_m.shape[1])), HGRN_CHUNK, 4, 64)
    y_s, hs_s, rs_s, sh_s = group(
        x_sample, bs, ss, state_hgrn[0].astype(F32), _pack_pairs(state_rwkv[0].astype(F32)),
        state_shift[0].astype(F32), ss, 8, ss)

    return (y_p, y_s, hs_p[None], rs_p[None], sh_p[None], hs_s[None], rs_s[None], sh_s[None])
```

```python
import functools

import numpy as np
import jax
import jax.numpy as jnp
from jax import lax
from jax.experimental import pallas as pl
from jax.experimental.pallas import tpu as pltpu

F32 = jnp.float32
BF16 = jnp.bfloat16

EPS = 1e-6
GN_EPS = 64e-5
HGRN_CHUNK = 64
HGRN_SHORT_CHUNK = 16
PEER_TOPK = 16
RWKV_SLOT = 32
LANES = 128
SUBLANES = 8
VMEM_LIMIT_BYTES = 52 * 1024 * 1024

_NT = (((1,), (1,)), ((), ()))
_TN = (((0,), (0,)), ((), ()))


def _dot(a, b):
    return jnp.dot(a, b, preferred_element_type=F32)


def _split3(a):
    a1 = a.astype(BF16)
    r1 = a - a1.astype(F32)
    a2 = r1.astype(BF16)
    a3 = (r1 - a2.astype(F32)).astype(BF16)
    return a1, a2, a3


def _dot_exact_rhs(a, b_exact):
    a1, a2, a3 = _split3(a)
    return _dot(a1, b_exact) + _dot(a2, b_exact) + _dot(a3, b_exact)


def _dot_exact_lhs(a_exact, b):
    b1, b2, b3 = _split3(b)
    return _dot(a_exact, b1) + _dot(a_exact, b2) + _dot(a_exact, b3)


def _dot_hi(a, b):
    ah = a.astype(BF16)
    al = (a - ah.astype(F32)).astype(BF16)
    bh = b.astype(BF16)
    bl = (b - bh.astype(F32)).astype(BF16)
    return _dot(ah, bh) + _dot(ah, bl) + _dot(al, bh)


def _sigmoid(x):
    return 1.0 / (1.0 + jnp.exp(-x))


def _rms(x, g):
    return x * lax.rsqrt(jnp.mean(x * x, axis=-1, keepdims=True) + EPS) * g


def _params(*sem):
    return pltpu.CompilerParams(dimension_semantics=sem, vmem_limit_bytes=VMEM_LIMIT_BYTES)


def _const_spec(shape):
    nd = len(shape)
    return pl.BlockSpec(shape, lambda *_: (0,) * nd)


def _in_proj_body(x_ref, g_ref, wa_ref, wb_ref, za_ref, zb_ref):
    hb = _rms(x_ref[...], g_ref[...]).astype(BF16)
    za_ref[...] = _dot(hb, wa_ref[...])
    zb_ref[...] = _dot(hb, wb_ref[...])


def _in_proj(x, g, wa, wb, tb):
    n, d = x.shape
    na, nb = wa.shape[1], wb.shape[1]
    return pl.pallas_call(
        _in_proj_body,
        grid=(n // tb,),
        in_specs=[pl.BlockSpec((tb, d), lambda i: (i, 0)), _const_spec((1, d)),
                  _const_spec(wa.shape), _const_spec(wb.shape)],
        out_specs=[pl.BlockSpec((tb, na), lambda i: (i, 0)), pl.BlockSpec((tb, nb), lambda i: (i, 0))],
        out_shape=[jax.ShapeDtypeStruct((n, na), F32), jax.ShapeDtypeStruct((n, nb), F32)],
        compiler_params=_params("parallel"),
        name="in_proj",
    )(x, g, wa, wb)


def _hgrn_tables(c):
    nl = int(np.log2(c)) + 1
    mats = np.zeros((nl, c, c), np.float32)
    masks = np.zeros((nl, c, c), np.float32)
    idx = np.arange(c)
    mats[0] = (idx[None, :] <= idx[:, None])
    masks[0] = np.eye(c)
    for l in range(1, nl):
        blk = 1 << l
        half = blk >> 1
        for t in range(c):
            start = (t // blk) * blk
            ref = start + half - 1
            if t - start >= half:
                mats[l, t, ref + 1:t + 1] = 1.0
                masks[l, t, start:start + half] = 1.0
            else:
                mats[l, t, t + 1:ref + 1] = 1.0
    return mats.reshape(nl * c, c), masks


def _hgrn_body(za_ref, lbt_ref, gn_ref, s0_ref, mats_ref, masks_ref, oa_ref, sout_ref, st_scr,
               *, nb, rows, chunk, heads, dk, dv):
    c_id = pl.program_id(1)
    qk = heads * dk
    wd = heads * dv
    nl = masks_ref.shape[0]

    @pl.when(c_id == 0)
    def _():
        for s in range(nb):
            for h in range(heads):
                st_scr[s, h] = s0_ref[min(s, s0_ref.shape[0] - 1), h].T

    th = lbt_ref[...]
    ex = jnp.exp(th - jnp.max(th, axis=0, keepdims=True))
    lb = ex[0:1] / jnp.sum(ex, axis=0, keepdims=True)
    gn = gn_ref[...]

    for s in range(nb):
        za = za_ref[s]
        zq = za[:, :qk]
        zf = za[:, qk:2 * qk]
        zi = za[:, 2 * qk:2 * qk + wd]
        zo = za[:, 2 * qk + wd:]
        logf = jnp.log(lb + (1.0 - lb) * _sigmoid(zf))
        kin = (1.0 - lb) * _sigmoid(-zf)
        qa = zq * _sigmoid(zq)
        if rows < chunk:
            pad = jnp.zeros((chunk - rows, qk), F32)
            logf = jnp.concatenate([logf, pad], axis=0)
            kin = jnp.concatenate([kin, pad], axis=0)
            qa = jnp.concatenate([qa, pad], axis=0)
            zi = jnp.concatenate([zi, jnp.zeros((chunk - rows, wd), F32)], axis=0)

        dec = _dot_exact_lhs(mats_ref[...], logf)
        for h in range(heads):
            ks = slice(h * dk, (h + 1) * dk)
            vs = slice(h * dv, (h + 1) * dv)
            q = qa[:, ks]
            k = kin[:, ks]
            v = zi[:, vs].astype(BF16)
            cum = dec[0:chunk, ks]
            sc = lax.dot_general(q.astype(BF16), k.astype(BF16), _NT, preferred_element_type=F32)
            scores = jnp.where(masks_ref[0] != 0.0, sc, 0.0)
            for l in range(1, nl):
                e = jnp.exp(dec[l * chunk:(l + 1) * chunk, ks])
                sc = lax.dot_general((q * e).astype(BF16), (k * e).astype(BF16), _NT, preferred_element_type=F32)
                scores = scores + jnp.where(masks_ref[l] != 0.0, sc, 0.0)
            st = st_scr[s, h]
            o = _dot(scores.astype(BF16), v) + lax.dot_general(
                (q * jnp.exp(cum)).astype(BF16), st.astype(BF16), _NT, preferred_element_type=F32)
            last = cum[chunk - 1:chunk, :]
            kd = (k * jnp.exp(last - cum)).astype(BF16)
            st_scr[s, h] = jnp.exp(last) * st + lax.dot_general(v, kd, _TN, preferred_element_type=F32)

            o = o[:rows]
            o = o * lax.rsqrt(jnp.mean(o * o, axis=-1, keepdims=True) + EPS) * gn[:, vs]
            zoh = zo[:, vs]
            oa_ref[s, :, vs] = o * (zoh * _sigmoid(zoh))

    @pl.when(c_id == pl.num_programs(1) - 1)
    def _():
        for s in range(nb):
            for h in range(heads):
                sout_ref[s, h] = st_scr[s, h].T


def _hgrn(za, nb, rows, lb_theta, gn, s0, heads, dk, dv):
    nseq, length, cols = za.shape
    chunk = HGRN_CHUNK if rows > HGRN_SHORT_CHUNK else HGRN_SHORT_CHUNK
    mats, masks = _hgrn_tables(chunk)
    mats = jnp.asarray(mats, BF16)
    masks = jnp.asarray(masks, F32)
    wd = heads * dv
    shared = s0.shape[0] == 1
    s0_spec = (pl.BlockSpec((1, heads, dk, dv), lambda s, c: (0, 0, 0, 0)) if shared
               else pl.BlockSpec((nb, heads, dk, dv), lambda s, c: (s, 0, 0, 0)))
    body = functools.partial(_hgrn_body, nb=nb, rows=rows, chunk=chunk, heads=heads, dk=dk, dv=dv)
    return pl.pallas_call(
        body,
        grid=(nseq // nb, length // rows),
        in_specs=[pl.BlockSpec((nb, rows, cols), lambda s, c: (s, c, 0)),
                  _const_spec(lb_theta.shape), _const_spec(gn.shape), s0_spec,
                  _const_spec(mats.shape), _const_spec(masks.shape)],
        out_specs=[pl.BlockSpec((nb, rows, wd), lambda s, c: (s, c, 0)),
                   pl.BlockSpec((nb, heads, dk, dv), lambda s, c: (s, 0, 0, 0))],
        out_shape=[jax.ShapeDtypeStruct((nseq, length, wd), F32),
                   jax.ShapeDtypeStruct((nseq, heads, dk, dv), F32)],
        scratch_shapes=[pltpu.VMEM((nb, heads, dv, dk), F32)],
        compiler_params=_params("parallel", "arbitrary"),
        name="hgrn",
    )(za, lb_theta, gn, s0, mats, masks)


def _rwkv_body(zb_ref, sh0_ref, s0_ref, mu_ref, w0_ref, ww_ref, a0_ref, aw_ref, gw_ref, kk_ref, ka_ref,
               rk_ref, lnw_ref, lnb_ref, ones_ref, seg_ref,
               ob_ref, sout_ref, shout_ref,
               s_scr, prev_scr, al_scr, w_scr, be_scr, k_scr, r_scr, v_scr, g_scr, y_scr, vt_scr,
               *, nb, tc, hw, hd, lora):
    t_id = pl.program_id(1)
    pairs = hw // LANES
    rin = zb_ref.shape[-1]

    @pl.when(t_id == 0)
    def _():
        s_scr[...] = s0_ref[...]
        prev_scr[...] = sh0_ref[...]

    row = lax.broadcasted_iota(jnp.int32, (tc, rin), 0)
    mu = mu_ref[...]
    zms = []
    for b in range(nb):
        zb = zb_ref[b]
        prev = jnp.where(row == 0, prev_scr[b:b + 1, :], pltpu.roll(zb, 1, axis=0))
        prev_scr[b:b + 1, :] = zb[tc - 1:tc, :]
        zms.append(zb + (prev - zb) * mu)
    zm = jnp.concatenate(zms, axis=0)
    r = zm[:, :hw]
    k = zm[:, hw:2 * hw]
    v = zm[:, 2 * hw:3 * hw]
    dwa = zm[:, 3 * hw:3 * hw + lora]
    dg = zm[:, 3 * hw + lora:]
    w_raw = w0_ref[...] + _dot_hi(jnp.tanh(dwa), ww_ref[...])
    decay = jnp.exp(-np.float32(np.exp(-0.5)) * _sigmoid(w_raw))
    a = _sigmoid(a0_ref[...] + _dot_hi(dwa, aw_ref[...]))
    g = _dot_hi(_sigmoid(dg), gw_ref[...])
    kk = k * kk_ref[...]
    ss = _dot_exact_rhs(kk * kk, seg_ref[...])
    kk = kk / jnp.maximum(jnp.sqrt(ss), 1e-12)
    k2 = k * (1.0 + (a - 1.0) * ka_ref[...])
    shp = (nb, tc, hw)
    al_scr[...] = (-kk).reshape(shp)
    w_scr[...] = decay.reshape(shp)
    be_scr[...] = (kk * a).reshape(shp)
    k_scr[...] = k2.reshape(shp)
    r_scr[...] = r.reshape(shp)
    v_scr[...] = v.reshape(shp)
    g_scr[...] = g.reshape(shp)

    slot = RWKV_SLOT
    li2 = lax.broadcasted_iota(jnp.int32, (slot, LANES), 1)
    vi = lax.broadcasted_iota(jnp.int32, (hd, LANES), 0)
    sel = jnp.where(vi == lax.broadcasted_iota(jnp.int32, (hd, LANES), 1) % hd, 1.0, 0.0).astype(BF16)
    for b in range(nb):
        for p in range(pairs):
            for g0 in range(0, tc, slot):
                nt = min(slot, tc - g0)
                vb = v[b * tc + g0:b * tc + g0 + nt, p * LANES:(p + 1) * LANES]
                if nt < slot:
                    vb = jnp.concatenate([vb, jnp.zeros((slot - nt, LANES), F32)], axis=0)
                vh = vb.astype(BF16)
                vl = (vb - vh.astype(F32)).astype(BF16)
                zero = jnp.zeros_like(vh)
                stack = jnp.concatenate([jnp.where(li2 < hd, vh, zero), jnp.where(li2 >= hd, vh, zero),
                                         jnp.where(li2 < hd, vl, zero), jnp.where(li2 >= hd, vl, zero)], axis=0)
                vt_scr[g0 // slot, b, p] = lax.dot_general(
                    sel, stack, _NT, preferred_element_type=F32).astype(BF16)

    rows = pairs * hd
    ri = lax.broadcasted_iota(jnp.int32, (rows, LANES), 0)
    li = lax.broadcasted_iota(jnp.int32, (rows, LANES), 1)
    diag = (ri % hd) == (li % hd)
    ones1 = ones_ref[...]
    qi = lax.broadcasted_iota(jnp.int32, (4 * slot, LANES), 0)
    ql = lax.broadcasted_iota(jnp.int32, (4 * slot, LANES), 1)
    tok_of_row = jnp.where((qi // slot) % 2 == ql // hd, qi % slot, -1)

    def bcast(ref, b, t):
        rowv = ref[b, pl.ds(t, 1), :]
        return jnp.concatenate(
            [jnp.broadcast_to(rowv[:, p * LANES:(p + 1) * LANES], (hd, LANES)) for p in range(pairs)], axis=0)

    def step(t, carry):
        pick = jnp.where(tok_of_row == t % slot, 1.0, 0.0).astype(BF16)
        seqs = range(nb)
        vcols = [_dot(vt_scr[t // slot, b].reshape(rows, 4 * slot), pick) for b in seqs]
        for b in seqs:
            old = s_scr[b].reshape(rows, LANES)
            sa = _dot((old * bcast(al_scr, b, t)).astype(BF16), ones1)
            new = old * bcast(w_scr, b, t) + sa * bcast(be_scr, b, t) + vcols[b] * bcast(k_scr, b, t)
            s_scr[b] = new.reshape(pairs, hd, LANES)
            yrep = _dot((new * bcast(r_scr, b, t)).astype(BF16), ones1)
            ysel = jnp.where(diag, yrep, 0.0).reshape(pairs, hd, LANES)
            for p in range(pairs):
                y_scr[b, p, pl.ds(t, 1), :] = jnp.sum(ysel[p], axis=0, keepdims=True)
        return carry

    lax.fori_loop(0, tc, step, 0, unroll=4)

    y = jnp.concatenate([y_scr[:, p].reshape(nb * tc, LANES) for p in range(pairs)], axis=1)
    seg = seg_ref[...]
    inv = np.float32(1.0 / hd)
    mean = _dot_exact_rhs(y, seg) * inv
    yc = y - mean
    var = _dot_exact_rhs(yc * yc, seg) * inv
    yn = yc * lax.rsqrt(var + GN_EPS) * lnw_ref[...] + lnb_ref[...]
    r = r_scr[...].reshape(nb * tc, hw)
    k2 = k_scr[...].reshape(nb * tc, hw)
    v = v_scr[...].reshape(nb * tc, hw)
    bonus = _dot_exact_rhs(r * k2 * rk_ref[...], seg) * v
    ob_ref[...] = ((yn + bonus) * g_scr[...].reshape(nb * tc, hw)).reshape(shp)

    @pl.when(t_id == pl.num_programs(1) - 1)
    def _():
        sout_ref[...] = s_scr[...]
        shout_ref[...] = prev_scr[...]


def _rwkv(zb, sh0, s0, wts, nb, tc, hd):
    nseq, length, rin = zb.shape
    hw = wts["w0"].shape[1]
    pairs = hw // LANES
    lora = wts["ww"].shape[0]
    body = functools.partial(_rwkv_body, nb=nb, tc=tc, hw=hw, hd=hd, lora=lora)
    names = ("mu", "w0", "ww", "a0", "aw", "gw", "kk", "ka", "rk", "lnw", "lnb", "ones", "seg")
    consts = [wts[n] for n in names]
    tok = pltpu.VMEM((nb, tc, hw), F32)
    return pl.pallas_call(
        body,
        grid=(nseq // nb, length // tc),
        in_specs=[pl.BlockSpec((nb, tc, rin), lambda s, t: (s, t, 0)),
                  pl.BlockSpec((nb, rin), lambda s, t: (s, 0)),
                  pl.BlockSpec((nb, pairs, hd, LANES), lambda s, t: (s, 0, 0, 0))]
                 + [_const_spec(c.shape) for c in consts],
        out_specs=[pl.BlockSpec((nb, tc, hw), lambda s, t: (s, t, 0)),
                   pl.BlockSpec((nb, pairs, hd, LANES), lambda s, t: (s, 0, 0, 0)),
                   pl.BlockSpec((nb, rin), lambda s, t: (s, 0))],
        out_shape=[jax.ShapeDtypeStruct((nseq, length, hw), F32),
                   jax.ShapeDtypeStruct((nseq, pairs, hd, LANES), F32),
                   jax.ShapeDtypeStruct((nseq, rin), F32)],
        scratch_shapes=[pltpu.VMEM((nb, pairs, hd, LANES), F32), pltpu.VMEM((nb, rin), F32)]
                       + [tok] * 7 + [pltpu.VMEM((nb, pairs, tc, LANES), F32),
                                      pltpu.VMEM((-(-tc // RWKV_SLOT), nb, pairs, hd, 4 * RWKV_SLOT), BF16)],
        compiler_params=_params("parallel", "arbitrary"),
        name="rwkv",
    )(zb, sh0, s0, *consts)


def _pack_pairs(s):
    n, heads, hd, _ = s.shape
    return s.reshape(n, heads // 2, 2, hd, hd).transpose(0, 1, 3, 2, 4).reshape(n, heads // 2, hd, 2 * hd)


def _unpack_pairs(s):
    n, pairs, hd, _ = s.shape
    return s.reshape(n, pairs, hd, 2, hd).transpose(0, 1, 3, 2, 4).reshape(n, 2 * pairs, hd, hd)


def _out_proj_body(oa_ref, ob_ref, x_ref, woa_ref, wob_ref, g2_ref, wq_ref, keys_ref,
                   x1_ref, h2t_ref, st_ref, *, nk, dh):
    mix = _dot(oa_ref[...].astype(BF16), woa_ref[...]) + _dot(ob_ref[...].astype(BF16), wob_ref[...])
    x1 = x_ref[...] + mix
    x1_ref[...] = x1
    h2f = _rms(x1, g2_ref[...])
    h2t_ref[...] = h2f.T.astype(BF16)
    q = _dot(h2f.astype(BF16), wq_ref[...]).astype(BF16)
    for hp in range(q.shape[1] // dh):
        st_ref[hp * nk:(hp + 1) * nk, :] = lax.dot_general(
            keys_ref[hp % 2], q[:, hp * dh:(hp + 1) * dh], _NT, preferred_element_type=F32)


def _out_proj(oa, ob, x, woa, wob, g2, wq, keys, tb):
    n, d = x.shape
    nk, dh = keys.shape[1], keys.shape[2]
    nrow = (wq.shape[1] // dh) * nk
    body = functools.partial(_out_proj_body, nk=nk, dh=dh)
    return pl.pallas_call(
        body,
        grid=(n // tb,),
        in_specs=[pl.BlockSpec((tb, oa.shape[1]), lambda i: (i, 0)),
                  pl.BlockSpec((tb, ob.shape[1]), lambda i: (i, 0)),
                  pl.BlockSpec((tb, d), lambda i: (i, 0)),
                  _const_spec(woa.shape), _const_spec(wob.shape), _const_spec(g2.shape),
                  _const_spec(wq.shape), _const_spec(keys.shape)],
        out_specs=[pl.BlockSpec((tb, d), lambda i: (i, 0)), pl.BlockSpec((d, tb), lambda i: (0, i)),
                   pl.BlockSpec((nrow, tb), lambda i: (0, i))],
        out_shape=[jax.ShapeDtypeStruct((n, d), F32), jax.ShapeDtypeStruct((d, n), BF16),
                   jax.ShapeDtypeStruct((nrow, n), F32)],
        compiler_params=_params("parallel"),
        name="out_proj",
    )(oa, ob, x, woa, wob, g2, wq, keys)


def _route_body(st_ref, t1_ref, e1_ref, e2_ref, top_scr, cand_scr, *, nk, topk):
    neg = -jnp.inf
    nx = topk + 1
    half = topk // 2

    def extract(work, n, on_value):
        for i in range(n):
            m = jnp.max(work, axis=0, keepdims=True)
            on_value(i, m)
            work = jnp.where(work == m, neg, work)

    def keep(h):
        def on_value(i, m):
            top_scr[h, i:i + 1, :] = m
        return on_value

    s1 = st_ref[0:nk, :]
    s2 = st_ref[nk:2 * nk, :]
    extract(s1, nx, keep(0))
    extract(s2, nx, keep(1))
    cand_scr[...] = jnp.full(cand_scr.shape, neg, F32)
    sv2 = top_scr[1, 0:nx, :]
    cand_scr[0:nx, :] = top_scr[0, 0:1, :] + sv2
    base = 3 * SUBLANES
    for a in range(1, half):
        cand_scr[base + (a - 1) * half:base + a * half, :] = top_scr[0, a:a + 1, :] + sv2[0:half]
    base += (half - 1) * half
    cand_scr[base:base + nx - half, :] = top_scr[0, half:nx, :] + sv2[0:1]

    best0 = top_scr[0, 0:1, :] + sv2[0:1]
    acc = {"z": jnp.zeros_like(best0), "lo": best0, "hi": best0}

    def on_best(i, m):
        if i < topk:
            acc["z"] = acc["z"] + jnp.exp(m - best0)
            acc["hi"] = m
        acc["lo"] = m

    extract(cand_scr[...], nx, on_best)
    thr = jnp.where(acc["lo"] == neg, acc["hi"], 0.5 * (acc["hi"] + acc["lo"]))
    t1_ref[0] = thr - s1
    e1_ref[0] = jnp.exp(s1 - top_scr[0, 0:1, :]) / acc["z"]
    e2_ref[0] = jnp.exp(s2 - sv2[0:1])


def _route(st, heads, nk, tb):
    n = st.shape[1]
    topk = PEER_TOPK
    half = topk // 2
    ncand = 3 * SUBLANES + (half - 1) * half + 2 * SUBLANES
    assert topk + 1 <= 3 * SUBLANES and topk + 1 - half <= 2 * SUBLANES and half == SUBLANES
    body = functools.partial(_route_body, nk=nk, topk=topk)
    blk = pl.BlockSpec((1, nk, tb), lambda i, h: (h, 0, i))
    return pl.pallas_call(
        body,
        grid=(n // tb, heads),
        in_specs=[pl.BlockSpec((2 * nk, tb), lambda i, h: (h, i))],
        out_specs=[blk, blk, blk],
        out_shape=[jax.ShapeDtypeStruct((heads, nk, n), F32)] * 3,
        scratch_shapes=[pltpu.VMEM((2, 3 * SUBLANES, tb), F32), pltpu.VMEM((ncand, tb), F32)],
        compiler_params=_params("parallel", "parallel"),
        name="route",
    )(st)


PEER_SUB = 2 * LANES
PEER_ROW_SPLIT = 2


def _peer_body(h2t_ref, s2_ref, t1_ref, e1_ref, e2_ref, u_ref, vt_ref, x1_ref, gf_ref, y_ref,
               acc_scr, coef_scr, *, heads, nk, tb, n_eb):
    e = pl.program_id(1)
    eblk = 2 * u_ref.shape[0]
    i1_per_block = eblk // nk
    i1_per_sub = PEER_SUB // nk
    cur = e % 2

    @pl.when(e == 0)
    def _():
        acc_scr[...] = jnp.zeros_like(acc_scr)
        coef_scr[1] = jnp.zeros(coef_scr.shape[1:], BF16)

    i1_0 = pl.multiple_of(jnp.minimum(e, n_eb - 1) * i1_per_block, SUBLANES)
    n_sub = eblk // PEER_SUB
    h2t = h2t_ref[...]

    def matmuls(k):
        ks = slice(k * PEER_SUB, (k + 1) * PEER_SUB)
        u_rows = pltpu.bitcast(u_ref[k * PEER_SUB // 2:(k + 1) * PEER_SUB // 2, :], BF16)
        vt_cols = pltpu.bitcast(vt_ref[:, ks], BF16)
        return _dot(u_rows, h2t), _dot(vt_cols, coef_scr[1 - cur, ks, :])

    act, out = matmuls(0)
    for k in range(n_sub):
        if k + 1 < n_sub:
            act_next, part = matmuls(k + 1)
            out = out + part
        for lg in range(tb // LANES):
            ls = slice(lg * LANES, (lg + 1) * LANES)
            t1b = [t1_ref[h, pl.ds(i1_0, i1_per_block), ls] for h in range(heads)]
            e1b = [e1_ref[h, pl.ds(i1_0, i1_per_block), ls] for h in range(heads)]
            for rs in range(PEER_ROW_SPLIT):
                rsl = slice(rs * nk // PEER_ROW_SPLIT, (rs + 1) * nk // PEER_ROW_SPLIT)
                gates = [None] * i1_per_sub
                for h in range(heads):
                    s2 = s2_ref[h, rsl, ls]
                    e2 = e2_ref[h, rsl, ls]
                    for jj in range(i1_per_sub):
                        j = k * i1_per_sub + jj
                        term = jnp.where(s2 >= t1b[h][j:j + 1], e1b[h][j:j + 1] * e2, 0.0)
                        gates[jj] = term if gates[jj] is None else gates[jj] + term
                for jj in range(i1_per_sub):
                    a = act[jj * nk + rsl.start:jj * nk + rsl.stop, ls]
                    gelu = 0.5 * a * (1.0 + lax.erf(a * np.float32(np.sqrt(0.5))))
                    r0 = (k * i1_per_sub + jj) * nk
                    coef_scr[cur, r0 + rsl.start:r0 + rsl.stop, ls] = (gates[jj] * gelu).astype(BF16)
        if k + 1 < n_sub:
            act = act_next
    acc_scr[...] += out

    @pl.when(e == n_eb)
    def _():
        y_ref[...] = _rms(x1_ref[...] + acc_scr[...].T, gf_ref[...])


def _peer(h2t, st, t1, e1, e2, u, vt, x1, gf, heads, nk, tb):
    n, d = x1.shape
    ne = 2 * u.shape[0]
    eblk = SUBLANES * nk
    n_eb = ne // eblk
    body = functools.partial(_peer_body, heads=heads, nk=nk, tb=tb, n_eb=n_eb)
    tok = pl.BlockSpec((heads, nk, tb), lambda i, e: (0, 0, i))
    return pl.pallas_call(
        body,
        grid=(n // tb, n_eb + 1),
        in_specs=[pl.BlockSpec((d, tb), lambda i, e: (0, i)),
                  pl.BlockSpec((heads, None, nk, tb), lambda i, e: (0, 1, 0, i)),
                  tok, tok, tok,
                  pl.BlockSpec((eblk // 2, d), lambda i, e: (jnp.minimum(e, n_eb - 1), 0)),
                  pl.BlockSpec((d // 2, eblk), lambda i, e: (0, jnp.maximum(e - 1, 0))),
                  pl.BlockSpec((tb, d), lambda i, e: (i, 0)),
                  _const_spec(gf.shape)],
        out_specs=pl.BlockSpec((tb, d), lambda i, e: (i, 0)),
        out_shape=jax.ShapeDtypeStruct((n, d), F32),
        scratch_shapes=[pltpu.VMEM((d, tb), F32), pltpu.VMEM((2, eblk, tb), BF16)],
        compiler_params=_params("parallel", "arbitrary"),
        name="peer",
    )(h2t, st.reshape(heads, 2, nk, n), t1, e1, e2, u, vt, x1, gf)


def _pack_row_pairs(x):
    n2, c = x.shape
    b = lax.bitcast_convert_type(x, jnp.uint16).astype(jnp.uint32).reshape(n2 // 2, 2, c)
    return b[:, 0, :] | (b[:, 1, :] << 16)


def _block(n, pref):
    b = pref
    while n % b:
        b //= 2
    return b


def kernel(x_prompt, x_sample, state_hgrn, state_rwkv, state_shift, meta_tokens, norm1, w_in, lb_theta, hgrn_norm, mu_shift, w0, w_w2, a0, a_w2, g_w2, k_k, k_a, r_k, ln_w, ln_b, w_out, norm2, peer_wq, peer_keys, peer_u, peer_v, norm_f):
    assert w_in.shape[0] == 1, "single-layer trunk"
    bp, sp, d = x_prompt.shape
    bs, ss, _ = x_sample.shape
    n_meta = meta_tokens.shape[0]
    _, _, ha_heads, dk, dv = state_hgrn.shape
    _, _, hb_heads, hd, _ = state_rwkv.shape
    ha_in = 2 * ha_heads * dk + 2 * ha_heads * dv
    hw = hb_heads * hd
    lw, la = w_w2.shape[1], a_w2.shape[1]
    nk, dh = peer_keys.shape[2], peer_keys.shape[3]
    p_heads = peer_wq.shape[2] // (2 * dh)
    assert sp % HGRN_CHUNK == 0 and ss <= HGRN_CHUNK and n_meta <= HGRN_CHUNK
    assert hb_heads % 2 == 0 and 2 * hd == LANES and lw + la == LANES and nk == LANES

    row = lambda t: t.reshape(1, -1).astype(F32)
    w_a = w_in[0, :, :ha_in].astype(BF16)
    w_b = w_in[0, :, ha_in:].astype(BF16)
    g1 = row(norm1[0])
    lbt = lb_theta.astype(F32)
    gn = row(hgrn_norm[0])

    head_of_lane = np.arange(LANES) // hd
    head_of_col = np.arange(hw) // hd
    rw = {
        "mu": row(mu_shift[0]), "w0": row(w0[0]), "a0": row(a0[0]), "kk": row(k_k[0]), "ka": row(k_a[0]),
        "rk": row(r_k[0]), "lnw": row(ln_w[0]), "lnb": row(ln_b[0]), "gw": g_w2[0].astype(F32),
        "ww": jnp.concatenate([w_w2[0], jnp.zeros((la, hw), F32)], axis=0),
        "aw": jnp.concatenate([jnp.zeros((lw, hw), F32), a_w2[0]], axis=0),
        "ones": jnp.asarray(head_of_lane[:, None] == head_of_lane[None, :], BF16),
        "seg": jnp.asarray(head_of_col[:, None] == head_of_col[None, :], BF16),
    }
    woa = w_out[0, :ha_heads * dv].astype(BF16)
    wob = w_out[0, ha_heads * dv:].astype(BF16)
    g2 = row(norm2[0])
    wq = peer_wq[0].astype(BF16)
    keys = peer_keys[0].astype(BF16)
    u = _pack_row_pairs(peer_u[0].astype(BF16))
    vt = _pack_row_pairs(peer_v[0].astype(BF16).T)
    gf = row(norm_f)

    za_m, zb_m = _in_proj(meta_tokens.astype(F32), g1, w_a, w_b, n_meta)
    _, hs_m = _hgrn(za_m[None], 1, n_meta, lbt, gn, jnp.zeros((1, ha_heads, dk, dv), F32), ha_heads, dk, dv)
    _, rs_m, sh_m = _rwkv(zb_m.reshape(1, n_meta, -1), jnp.zeros((1, zb_m.shape[1]), F32),
                          jnp.zeros((1, hb_heads // 2, hd, LANES), F32), rw, 1, n_meta, hd)

    def group(x, nseq, length, hs0, rs0, sh0, chunk_rows, hgrn_nb, tc):
        xf = x.reshape(nseq * length, d)
        n = xf.shape[0]
        tb = _block(n, 256)
        assert tb % LANES == 0
        za, zb = _in_proj(xf, g1, w_a, w_b, tb)
        oa, hs = _hgrn(za.reshape(nseq, length, -1), hgrn_nb, chunk_rows, lbt, gn, hs0, ha_heads, dk, dv)
        ob, rs, sh = _rwkv(zb.reshape(nseq, length, -1), sh0, rs0, rw, 8, tc, hd)
        x1, h2t, st = _out_proj(oa.reshape(n, -1), ob.reshape(n, hw), xf, woa, wob, g2, wq, keys, tb)
        t1, e1, e2 = _route(st, p_heads, nk, tb)
        y = _peer(h2t, st, t1, e1, e2, u, vt, x1, gf, p_heads, nk, tb)
        return y.reshape(nseq, length, d), hs, _unpack_pairs(rs), sh

    y_p, hs_p, rs_p, sh_p = group(
        x_prompt, bp, sp, hs_m, jnp.broadcast_to(rs_m, (bp,) + rs_m.shape[1:]),
        jnp.broadcast_to(sh_m, (bp, sh_m.shape[1])), HGRN_CHUNK, 4, 64)
    y_s, hs_s, rs_s, sh_s = group(
        x_sample, bs, ss, state_hgrn[0].astype(F32), _pack_pairs(state_rwkv[0].astype(F32)),
        state_shift[0].astype(F32), ss, 8, ss)

    return (y_p, y_s, hs_p[None], rs_p[None], sh_p[None], hs_s[None], rs_s[None], sh_s[None])
```

```python
import functools

import numpy as np
import jax
import jax.numpy as jnp
from jax import lax
from jax.experimental import pallas as pl
from jax.experimental.pallas import tpu as pltpu

F32 = jnp.float32
BF16 = jnp.bfloat16

EPS = 1e-6
GN_EPS = 64e-5
HGRN_CHUNK = 64
HGRN_SHORT_CHUNK = 16
PEER_TOPK = 16
RWKV_SLOT = 32
LANES = 128
SUBLANES = 8
VMEM_LIMIT_BYTES = 52 * 1024 * 1024

_NT = (((1,), (1,)), ((), ()))
_TN = (((0,), (0,)), ((), ()))


def _dot(a, b):
    return jnp.dot(a, b, preferred_element_type=F32)


def _split3(a):
    a1 = a.astype(BF16)
    r1 = a - a1.astype(F32)
    a2 = r1.astype(BF16)
    a3 = (r1 - a2.astype(F32)).astype(BF16)
    return a1, a2, a3


def _dot_exact_rhs(a, b_exact):
    a1, a2, a3 = _split3(a)
    return _dot(a1, b_exact) + _dot(a2, b_exact) + _dot(a3, b_exact)


def _dot_exact_lhs(a_exact, b):
    b1, b2, b3 = _split3(b)
    return _dot(a_exact, b1) + _dot(a_exact, b2) + _dot(a_exact, b3)


def _dot_hi(a, b):
    ah = a.astype(BF16)
    al = (a - ah.astype(F32)).astype(BF16)
    bh = b.astype(BF16)
    bl = (b - bh.astype(F32)).astype(BF16)
    return _dot(ah, bh) + _dot(ah, bl) + _dot(al, bh)


def _sigmoid(x):
    return 1.0 / (1.0 + jnp.exp(-x))


def _rms(x, g):
    return x * lax.rsqrt(jnp.mean(x * x, axis=-1, keepdims=True) + EPS) * g


def _params(*sem):
    return pltpu.CompilerParams(dimension_semantics=sem, vmem_limit_bytes=VMEM_LIMIT_BYTES)


def _const_spec(shape):
    nd = len(shape)
    return pl.BlockSpec(shape, lambda *_: (0,) * nd)


def _in_proj_body(x_ref, g_ref, wa_ref, wb_ref, za_ref, zb_ref):
    hb = _rms(x_ref[...], g_ref[...]).astype(BF16)
    za_ref[...] = _dot(hb, wa_ref[...])
    zb_ref[...] = _dot(hb, wb_ref[...])


def _in_proj(x, g, wa, wb, tb):
    n, d = x.shape
    na, nb = wa.shape[1], wb.shape[1]
    return pl.pallas_call(
        _in_proj_body,
        grid=(n // tb,),
        in_specs=[pl.BlockSpec((tb, d), lambda i: (i, 0)), _const_spec((1, d)),
                  _const_spec(wa.shape), _const_spec(wb.shape)],
        out_specs=[pl.BlockSpec((tb, na), lambda i: (i, 0)), pl.BlockSpec((tb, nb), lambda i: (i, 0))],
        out_shape=[jax.ShapeDtypeStruct((n, na), F32), jax.ShapeDtypeStruct((n, nb), F32)],
        compiler_params=_params("parallel"),
        name="in_proj",
    )(x, g, wa, wb)


def _hgrn_tables(c):
    nl = int(np.log2(c)) + 1
    mats = np.zeros((nl, c, c), np.float32)
    masks = np.zeros((nl, c, c), np.float32)
    idx = np.arange(c)
    mats[0] = (idx[None, :] <= idx[:, None])
    masks[0] = np.eye(c)
    for l in range(1, nl):
        blk = 1 << l
        half = blk >> 1
        for t in range(c):
            start = (t // blk) * blk
            ref = start + half - 1
            if t - start >= half:
                mats[l, t, ref + 1:t + 1] = 1.0
                masks[l, t, start:start + half] = 1.0
            else:
                mats[l, t, t + 1:ref + 1] = 1.0
    return mats.reshape(nl * c, c), masks


def _hgrn_body(za_ref, lbt_ref, gn_ref, s0_ref, mats_ref, masks_ref, oa_ref, sout_ref, st_scr,
               *, nb, rows, chunk, heads, dk, dv):
    c_id = pl.program_id(1)
    qk = heads * dk
    wd = heads * dv
    nl = masks_ref.shape[0]

    @pl.when(c_id == 0)
    def _():
        for s in range(nb):
            for h in range(heads):
                st_scr[s, h] = s0_ref[min(s, s0_ref.shape[0] - 1), h].T

    th = lbt_ref[...]
    ex = jnp.exp(th - jnp.max(th, axis=0, keepdims=True))
    lb = ex[0:1] / jnp.sum(ex, axis=0, keepdims=True)
    gn = gn_ref[...]

    for s in range(nb):
        za = za_ref[s]
        zq = za[:, :qk]
        zf = za[:, qk:2 * qk]
        zi = za[:, 2 * qk:2 * qk + wd]
        zo = za[:, 2 * qk + wd:]
        logf = jnp.log(lb + (1.0 - lb) * _sigmoid(zf))
        kin = (1.0 - lb) * _sigmoid(-zf)
        qa = zq * _sigmoid(zq)
        if rows < chunk:
            pad = jnp.zeros((chunk - rows, qk), F32)
            logf = jnp.concatenate([logf, pad], axis=0)
            kin = jnp.concatenate([kin, pad], axis=0)
            qa = jnp.concatenate([qa, pad], axis=0)
            zi = jnp.concatenate([zi, jnp.zeros((chunk - rows, wd), F32)], axis=0)

        dec = _dot_exact_lhs(mats_ref[...], logf)
        for h in range(heads):
            ks = slice(h * dk, (h + 1) * dk)
            vs = slice(h * dv, (h + 1) * dv)
            q = qa[:, ks]
            k = kin[:, ks]
            v = zi[:, vs].astype(BF16)
            cum = dec[0:chunk, ks]
            sc = lax.dot_general(q.astype(BF16), k.astype(BF16), _NT, preferred_element_type=F32)
            scores = jnp.where(masks_ref[0] != 0.0, sc, 0.0)
            for l in range(1, nl):
                e = jnp.exp(dec[l * chunk:(l + 1) * chunk, ks])
                sc = lax.dot_general((q * e).astype(BF16), (k * e).astype(BF16), _NT, preferred_element_type=F32)
                scores = scores + jnp.where(masks_ref[l] != 0.0, sc, 0.0)
            st = st_scr[s, h]
            o = _dot(scores.astype(BF16), v) + lax.dot_general(
                (q * jnp.exp(cum)).astype(BF16), st.astype(BF16), _NT, preferred_element_type=F32)
            last = cum[chunk - 1:chunk, :]
            kd = (k * jnp.exp(last - cum)).astype(BF16)
            st_scr[s, h] = jnp.exp(last) * st + lax.dot_general(v, kd, _TN, preferred_element_type=F32)

            o = o[:rows]
            o = o * lax.rsqrt(jnp.mean(o * o, axis=-1, keepdims=True) + EPS) * gn[:, vs]
            zoh = zo[:, vs]
            oa_ref[s, :, vs] = o * (zoh * _sigmoid(zoh))

    @pl.when(c_id == pl.num_programs(1) - 1)
    def _():
        for s in range(nb):
            for h in range(heads):
                sout_ref[s, h] = st_scr[s, h].T


def _hgrn(za, nb, rows, lb_theta, gn, s0, heads, dk, dv):
    nseq, length, cols = za.shape
    chunk = HGRN_CHUNK if rows > HGRN_SHORT_CHUNK else HGRN_SHORT_CHUNK
    mats, masks = _hgrn_tables(chunk)
    mats = jnp.asarray(mats, BF16)
    masks = jnp.asarray(masks, F32)
    wd = heads * dv
    shared = s0.shape[0] == 1
    s0_spec = (pl.BlockSpec((1, heads, dk, dv), lambda s, c: (0, 0, 0, 0)) if shared
               else pl.BlockSpec((nb, heads, dk, dv), lambda s, c: (s, 0, 0, 0)))
    body = functools.partial(_hgrn_body, nb=nb, rows=rows, chunk=chunk, heads=heads, dk=dk, dv=dv)
    return pl.pallas_call(
        body,
        grid=(nseq // nb, length // rows),
        in_specs=[pl.BlockSpec((nb, rows, cols), lambda s, c: (s, c, 0)),
                  _const_spec(lb_theta.shape), _const_spec(gn.shape), s0_spec,
                  _const_spec(mats.shape), _const_spec(masks.shape)],
        out_specs=[pl.BlockSpec((nb, rows, wd), lambda s, c: (s, c, 0)),
                   pl.BlockSpec((nb, heads, dk, dv), lambda s, c: (s, 0, 0, 0))],
        out_shape=[jax.ShapeDtypeStruct((nseq, length, wd), F32),
                   jax.ShapeDtypeStruct((nseq, heads, dk, dv), F32)],
        scratch_shapes=[pltpu.VMEM((nb, heads, dv, dk), F32)],
        compiler_params=_params("parallel", "arbitrary"),
        name="hgrn",
    )(za, lb_theta, gn, s0, mats, masks)


def _rwkv_body(zb_ref, sh0_ref, s0_ref, mu_ref, w0_ref, ww_ref, a0_ref, aw_ref, gw_ref, kk_ref, ka_ref,
               rk_ref, lnw_ref, lnb_ref, ones_ref, seg_ref,
               ob_ref, sout_ref, shout_ref,
               s_scr, prev_scr, al_scr, w_scr, be_scr, k_scr, r_scr, v_scr, g_scr, y_scr, vt_scr,
               *, nb, tc, hw, hd, lora):
    t_id = pl.program_id(1)
    pairs = hw // LANES
    rin = zb_ref.shape[-1]

    @pl.when(t_id == 0)
    def _():
        s_scr[...] = s0_ref[...]
        prev_scr[...] = sh0_ref[...]

    row = lax.broadcasted_iota(jnp.int32, (tc, rin), 0)
    mu = mu_ref[...]
    zms = []
    for b in range(nb):
        zb = zb_ref[b]
        prev = jnp.where(row == 0, prev_scr[b:b + 1, :], pltpu.roll(zb, 1, axis=0))
        prev_scr[b:b + 1, :] = zb[tc - 1:tc, :]
        zms.append(zb + (prev - zb) * mu)
    zm = jnp.concatenate(zms, axis=0)
    r = zm[:, :hw]
    k = zm[:, hw:2 * hw]
    v = zm[:, 2 * hw:3 * hw]
    dwa = zm[:, 3 * hw:3 * hw + lora]
    dg = zm[:, 3 * hw + lora:]
    w_raw = w0_ref[...] + _dot_hi(jnp.tanh(dwa), ww_ref[...])
    decay = jnp.exp(-np.float32(np.exp(-0.5)) * _sigmoid(w_raw))
    a = _sigmoid(a0_ref[...] + _dot_hi(dwa, aw_ref[...]))
    g = _dot_hi(_sigmoid(dg), gw_ref[...])
    kk = k * kk_ref[...]
    ss = _dot_exact_rhs(kk * kk, seg_ref[...])
    kk = kk / jnp.maximum(jnp.sqrt(ss), 1e-12)
    k2 = k * (1.0 + (a - 1.0) * ka_ref[...])
    shp = (nb, tc, hw)
    al_scr[...] = (-kk).reshape(shp)
    w_scr[...] = decay.reshape(shp)
    be_scr[...] = (kk * a).reshape(shp)
    k_scr[...] = k2.reshape(shp)
    r_scr[...] = r.reshape(shp)
    v_scr[...] = v.reshape(shp)
    g_scr[...] = g.reshape(shp)

    slot = RWKV_SLOT
    li2 = lax.broadcasted_iota(jnp.int32, (slot, LANES), 1)
    vi = lax.broadcasted_iota(jnp.int32, (hd, LANES), 0)
    sel = jnp.where(vi == lax.broadcasted_iota(jnp.int32, (hd, LANES), 1) % hd, 1.0, 0.0).astype(BF16)
    for b in range(nb):
        for p in range(pairs):
            for g0 in range(0, tc, slot):
                nt = min(slot, tc - g0)
                vb = v[b * tc + g0:b * tc + g0 + nt, p * LANES:(p + 1) * LANES]
                if nt < slot:
                    vb = jnp.concatenate([vb, jnp.zeros((slot - nt, LANES), F32)], axis=0)
                vh = vb.astype(BF16)
                vl = (vb - vh.astype(F32)).astype(BF16)
                zero = jnp.zeros_like(vh)
                stack = jnp.concatenate([jnp.where(li2 < hd, vh, zero), jnp.where(li2 >= hd, vh, zero),
                                         jnp.where(li2 < hd, vl, zero), jnp.where(li2 >= hd, vl, zero)], axis=0)
                vt_scr[g0 // slot, b, p] = lax.dot_general(
                    sel, stack, _NT, preferred_element_type=F32).astype(BF16)

    rows = pairs * hd
    ri = lax.broadcasted_iota(jnp.int32, (rows, LANES), 0)
    li = lax.broadcasted_iota(jnp.int32, (rows, LANES), 1)
    diag = (ri % hd) == (li % hd)
    ones1 = ones_ref[...]
    qi = lax.broadcasted_iota(jnp.int32, (4 * slot, LANES), 0)
    ql = lax.broadcasted_iota(jnp.int32, (4 * slot, LANES), 1)
    tok_of_row = jnp.where((qi // slot) % 2 == ql // hd, qi % slot, -1)

    def bcast(ref, b, t):
        rowv = ref[b, pl.ds(t, 1), :]
        return jnp.concatenate(
            [jnp.broadcast_to(rowv[:, p * LANES:(p + 1) * LANES], (hd, LANES)) for p in range(pairs)], axis=0)

    def step(t, carry):
        pick = jnp.where(tok_of_row == t % slot, 1.0, 0.0).astype(BF16)
        seqs = range(nb)
        vcols = [_dot(vt_scr[t // slot, b].reshape(rows, 4 * slot), pick) for b in seqs]
        for b in seqs:
            old = s_scr[b].reshape(rows, LANES)
            sa = _dot((old * bcast(al_scr, b, t)).astype(BF16), ones1)
            new = old * bcast(w_scr, b, t) + sa * bcast(be_scr, b, t) + vcols[b] * bcast(k_scr, b, t)
            s_scr[b] = new.reshape(pairs, hd, LANES)
            yrep = _dot((new * bcast(r_scr, b, t)).astype(BF16), ones1)
            ysel = jnp.where(diag, yrep, 0.0).reshape(pairs, hd, LANES)
            for p in range(pairs):
                y_scr[b, p, pl.ds(t, 1), :] = jnp.sum(ysel[p], axis=0, keepdims=True)
        return carry

    lax.fori_loop(0, tc, step, 0, unroll=4)

    y = jnp.concatenate([y_scr[:, p].reshape(nb * tc, LANES) for p in range(pairs)], axis=1)
    seg = seg_ref[...]
    inv = np.float32(1.0 / hd)
    mean = _dot_exact_rhs(y, seg) * inv
    yc = y - mean
    var = _dot_exact_rhs(yc * yc, seg) * inv
    yn = yc * lax.rsqrt(var + GN_EPS) * lnw_ref[...] + lnb_ref[...]
    r = r_scr[...].reshape(nb * tc, hw)
    k2 = k_scr[...].reshape(nb * tc, hw)
    v = v_scr[...].reshape(nb * tc, hw)
    bonus = _dot_exact_rhs(r * k2 * rk_ref[...], seg) * v
    ob_ref[...] = ((yn + bonus) * g_scr[...].reshape(nb * tc, hw)).reshape(shp)

    @pl.when(t_id == pl.num_programs(1) - 1)
    def _():
        sout_ref[...] = s_scr[...]
        shout_ref[...] = prev_scr[...]


def _rwkv(zb, sh0, s0, wts, nb, tc, hd):
    nseq, length, rin = zb.shape
    hw = wts["w0"].shape[1]
    pairs = hw // LANES
    lora = wts["ww"].shape[0]
    body = functools.partial(_rwkv_body, nb=nb, tc=tc, hw=hw, hd=hd, lora=lora)
    names = ("mu", "w0", "ww", "a0", "aw", "gw", "kk", "ka", "rk", "lnw", "lnb", "ones", "seg")
    consts = [wts[n] for n in names]
    tok = pltpu.VMEM((nb, tc, hw), F32)
    return pl.pallas_call(
        body,
        grid=(nseq // nb, length // tc),
        in_specs=[pl.BlockSpec((nb, tc, rin), lambda s, t: (s, t, 0)),
                  pl.BlockSpec((nb, rin), lambda s, t: (s, 0)),
                  pl.BlockSpec((nb, pairs, hd, LANES), lambda s, t: (s, 0, 0, 0))]
                 + [_const_spec(c.shape) for c in consts],
        out_specs=[pl.BlockSpec((nb, tc, hw), lambda s, t: (s, t, 0)),
                   pl.BlockSpec((nb, pairs, hd, LANES), lambda s, t: (s, 0, 0, 0)),
                   pl.BlockSpec((nb, rin), lambda s, t: (s, 0))],
        out_shape=[jax.ShapeDtypeStruct((nseq, length, hw), F32),
                   jax.ShapeDtypeStruct((nseq, pairs, hd, LANES), F32),
                   jax.ShapeDtypeStruct((nseq, rin), F32)],
        scratch_shapes=[pltpu.VMEM((nb, pairs, hd, LANES), F32), pltpu.VMEM((nb, rin), F32)]
                       + [tok] * 7 + [pltpu.VMEM((nb, pairs, tc, LANES), F32),
                                      pltpu.VMEM((-(-tc // RWKV_SLOT), nb, pairs, hd, 4 * RWKV_SLOT), BF16)],
        compiler_params=_params("parallel", "arbitrary"),
        name="rwkv",
    )(zb, sh0, s0, *consts)


def _pack_pairs(s):
    n, heads, hd, _ = s.shape
    return s.reshape(n, heads // 2, 2, hd, hd).transpose(0, 1, 3, 2, 4).reshape(n, heads // 2, hd, 2 * hd)


def _unpack_pairs(s):
    n, pairs, hd, _ = s.shape
    return s.reshape(n, pairs, hd, 2, hd).transpose(0, 1, 3, 2, 4).reshape(n, 2 * pairs, hd, hd)


def _out_proj_body(oa_ref, ob_ref, x_ref, woa_ref, wob_ref, g2_ref, wq_ref, keys_ref,
                   x1_ref, h2t_ref, st_ref, *, nk, dh):
    mix = _dot(oa_ref[...].astype(BF16), woa_ref[...]) + _dot(ob_ref[...].astype(BF16), wob_ref[...])
    x1 = x_ref[...] + mix
    x1_ref[...] = x1
    h2f = _rms(x1, g2_ref[...])
    h2t_ref[...] = h2f.T.astype(BF16)
    q = _dot(h2f.astype(BF16), wq_ref[...]).astype(BF16)
    for hp in range(q.shape[1] // dh):
        st_ref[hp * nk:(hp + 1) * nk, :] = lax.dot_general(
            keys_ref[hp % 2], q[:, hp * dh:(hp + 1) * dh], _NT, preferred_element_type=F32)


def _out_proj(oa, ob, x, woa, wob, g2, wq, keys, tb):
    n, d = x.shape
    nk, dh = keys.shape[1], keys.shape[2]
    nrow = (wq.shape[1] // dh) * nk
    body = functools.partial(_out_proj_body, nk=nk, dh=dh)
    return pl.pallas_call(
        body,
        grid=(n // tb,),
        in_specs=[pl.BlockSpec((tb, oa.shape[1]), lambda i: (i, 0)),
                  pl.BlockSpec((tb, ob.shape[1]), lambda i: (i, 0)),
                  pl.BlockSpec((tb, d), lambda i: (i, 0)),
                  _const_spec(woa.shape), _const_spec(wob.shape), _const_spec(g2.shape),
                  _const_spec(wq.shape), _const_spec(keys.shape)],
        out_specs=[pl.BlockSpec((tb, d), lambda i: (i, 0)), pl.BlockSpec((d, tb), lambda i: (0, i)),
                   pl.BlockSpec((nrow, tb), lambda i: (0, i))],
        out_shape=[jax.ShapeDtypeStruct((n, d), F32), jax.ShapeDtypeStruct((d, n), BF16),
                   jax.ShapeDtypeStruct((nrow, n), F32)],
        compiler_params=_params("parallel"),
        name="out_proj",
    )(oa, ob, x, woa, wob, g2, wq, keys)


def _route_body(st_ref, t1_ref, e1_ref, e2_ref, top_scr, cand_scr, *, nk, topk):
    neg = -jnp.inf
    nx = topk + 1
    half = topk // 2

    def extract(work, n, on_value):
        for i in range(n):
            m = jnp.max(work, axis=0, keepdims=True)
            on_value(i, m)
            work = jnp.where(work == m, neg, work)

    def keep(h):
        def on_value(i, m):
            top_scr[h, i:i + 1, :] = m
        return on_value

    s1 = st_ref[0:nk, :]
    s2 = st_ref[nk:2 * nk, :]
    extract(s1, nx, keep(0))
    extract(s2, nx, keep(1))
    cand_scr[...] = jnp.full(cand_scr.shape, neg, F32)
    sv2 = top_scr[1, 0:nx, :]
    cand_scr[0:nx, :] = top_scr[0, 0:1, :] + sv2
    base = 3 * SUBLANES
    for a in range(1, half):
        cand_scr[base + (a - 1) * half:base + a * half, :] = top_scr[0, a:a + 1, :] + sv2[0:half]
    base += (half - 1) * half
    cand_scr[base:base + nx - half, :] = top_scr[0, half:nx, :] + sv2[0:1]

    best0 = top_scr[0, 0:1, :] + sv2[0:1]
    acc = {"z": jnp.zeros_like(best0), "lo": best0, "hi": best0}

    def on_best(i, m):
        if i < topk:
            acc["z"] = acc["z"] + jnp.exp(m - best0)
            acc["hi"] = m
        acc["lo"] = m

    extract(cand_scr[...], nx, on_best)
    thr = jnp.where(acc["lo"] == neg, acc["hi"], 0.5 * (acc["hi"] + acc["lo"]))
    t1_ref[0] = thr - s1
    e1_ref[0] = jnp.exp(s1 - top_scr[0, 0:1, :]) / acc["z"]
    e2_ref[0] = jnp.exp(s2 - sv2[0:1])


def _route(st, heads, nk, tb):
    n = st.shape[1]
    topk = PEER_TOPK
    half = topk // 2
    ncand = 3 * SUBLANES + (half - 1) * half + 2 * SUBLANES
    assert topk + 1 <= 3 * SUBLANES and topk + 1 - half <= 2 * SUBLANES and half == SUBLANES
    body = functools.partial(_route_body, nk=nk, topk=topk)
    blk = pl.BlockSpec((1, nk, tb), lambda i, h: (h, 0, i))
    return pl.pallas_call(
        body,
        grid=(n // tb, heads),
        in_specs=[pl.BlockSpec((2 * nk, tb), lambda i, h: (h, i))],
        out_specs=[blk, blk, blk],
        out_shape=[jax.ShapeDtypeStruct((heads, nk, n), F32)] * 3,
        scratch_shapes=[pltpu.VMEM((2, 3 * SUBLANES, tb), F32), pltpu.VMEM((ncand, tb), F32)],
        compiler_params=_params("parallel", "parallel"),
        name="route",
    )(st)


PEER_SUB = 2 * LANES
PEER_ROW_SPLIT = 2


def _peer_body(h2t_ref, s2_ref, t1_ref, e1_ref, e2_ref, u_ref, vt_ref, x1_ref, gf_ref, y_ref,
               acc_scr, coef_scr, *, heads, nk, tb, n_eb):
    e = pl.program_id(1)
    eblk = 2 * u_ref.shape[0]
    i1_per_block = eblk // nk
    i1_per_sub = PEER_SUB // nk
    cur = e % 2

    @pl.when(e == 0)
    def _():
        acc_scr[...] = jnp.zeros_like(acc_scr)
        coef_scr[1] = jnp.zeros(coef_scr.shape[1:], BF16)

    i1_0 = pl.multiple_of(jnp.minimum(e, n_eb - 1) * i1_per_block, SUBLANES)
    n_sub = eblk // PEER_SUB
    h2t = h2t_ref[...]

    def matmuls(k):
        ks = slice(k * PEER_SUB, (k + 1) * PEER_SUB)
        u_rows = pltpu.bitcast(u_ref[k * PEER_SUB // 2:(k + 1) * PEER_SUB // 2, :], BF16)
        vt_cols = pltpu.bitcast(vt_ref[:, ks], BF16)
        return _dot(u_rows, h2t), _dot(vt_cols, coef_scr[1 - cur, ks, :])

    act, out = matmuls(0)
    for k in range(n_sub):
        if k + 1 < n_sub:
            act_next, part = matmuls(k + 1)
            out = out + part
        for lg in range(tb // LANES):
            ls = slice(lg * LANES, (lg + 1) * LANES)
            t1b = [t1_ref[h, pl.ds(i1_0, i1_per_block), ls] for h in range(heads)]
            e1b = [e1_ref[h, pl.ds(i1_0, i1_per_block), ls] for h in range(heads)]
            for rs in range(PEER_ROW_SPLIT):
                rsl = slice(rs * nk // PEER_ROW_SPLIT, (rs + 1) * nk // PEER_ROW_SPLIT)
                gates = [None] * i1_per_sub
                for h in range(heads):
                    s2 = s2_ref[h, rsl, ls]
                    e2 = e2_ref[h, rsl, ls]
                    for jj in range(i1_per_sub):
                        j = k * i1_per_sub + jj
                        term = jnp.where(s2 >= t1b[h][j:j + 1], e1b[h][j:j + 1] * e2, 0.0)
                        gates[jj] = term if gates[jj] is None else gates[jj] + term
                for jj in range(i1_per_sub):
                    a = act[jj * nk + rsl.start:jj * nk + rsl.stop, ls]
                    gelu = 0.5 * a * (1.0 + lax.erf(a * np.float32(np.sqrt(0.5))))
                    r0 = (k * i1_per_sub + jj) * nk
                    coef_scr[cur, r0 + rsl.start:r0 + rsl.stop, ls] = (gates[jj] * gelu).astype(BF16)
        if k + 1 < n_sub:
            act = act_next
    acc_scr[...] += out

    @pl.when(e == n_eb)
    def _():
        y_ref[...] = _rms(x1_ref[...] + acc_scr[...].T, gf_ref[...])


def _peer(h2t, st, t1, e1, e2, u, vt, x1, gf, heads, nk, tb):
    n, d = x1.shape
    ne = 2 * u.shape[0]
    eblk = SUBLANES * nk
    n_eb = ne // eblk
    body = functools.partial(_peer_body, heads=heads, nk=nk, tb=tb, n_eb=n_eb)
    tok = pl.BlockSpec((heads, nk, tb), lambda i, e: (0, 0, i))
    return pl.pallas_call(
        body,
        grid=(n // tb, n_eb + 1),
        in_specs=[pl.BlockSpec((d, tb), lambda i, e: (0, i)),
                  pl.BlockSpec((heads, None, nk, tb), lambda i, e: (0, 1, 0, i)),
                  tok, tok, tok,
                  pl.BlockSpec((eblk // 2, d), lambda i, e: (jnp.minimum(e, n_eb - 1), 0)),
                  pl.BlockSpec((d // 2, eblk), lambda i, e: (0, jnp.maximum(e - 1, 0))),
                  pl.BlockSpec((tb, d), lambda i, e: (i, 0)),
                  _const_spec(gf.shape)],
        out_specs=pl.BlockSpec((tb, d), lambda i, e: (i, 0)),
        out_shape=jax.ShapeDtypeStruct((n, d), F32),
        scratch_shapes=[pltpu.VMEM((d, tb), F32), pltpu.VMEM((2, eblk, tb), BF16)],
        compiler_params=_params("parallel", "arbitrary"),
        name="peer",
    )(h2t, st.reshape(heads, 2, nk, n), t1, e1, e2, u, vt, x1, gf)


def _pack_body(x_ref, o_ref, *, transpose):
    x = x_ref[...]
    if transpose:
        x = x.T
    o_ref[...] = pltpu.bitcast(x.astype(BF16), jnp.uint32)


def _pack_table(x, transpose, rows=512):
    n, c = x.shape
    if transpose:
        out_shape, out_spec = (c // 2, n), pl.BlockSpec((c // 2, rows), lambda i: (0, i))
    else:
        out_shape, out_spec = (n // 2, c), pl.BlockSpec((rows // 2, c), lambda i: (i, 0))
    return pl.pallas_call(
        functools.partial(_pack_body, transpose=transpose),
        grid=(n // rows,),
        in_specs=[pl.BlockSpec((rows, c), lambda i: (i, 0))],
        out_specs=out_spec,
        out_shape=jax.ShapeDtypeStruct(out_shape, jnp.uint32),
        compiler_params=_params("parallel"),
        name="pack_table",
    )(x)


def _block(n, pref):
    b = pref
    while n % b:
        b //= 2
    return b


def kernel(x_prompt, x_sample, state_hgrn, state_rwkv, state_shift, meta_tokens, norm1, w_in, lb_theta, hgrn_norm, mu_shift, w0, w_w2, a0, a_w2, g_w2, k_k, k_a, r_k, ln_w, ln_b, w_out, norm2, peer_wq, peer_keys, peer_u, peer_v, norm_f):
    assert w_in.shape[0] == 1, "single-layer trunk"
    bp, sp, d = x_prompt.shape
    bs, ss, _ = x_sample.shape
    n_meta = meta_tokens.shape[0]
    _, _, ha_heads, dk, dv = state_hgrn.shape
    _, _, hb_heads, hd, _ = state_rwkv.shape
    ha_in = 2 * ha_heads * dk + 2 * ha_heads * dv
    hw = hb_heads * hd
    lw, la = w_w2.shape[1], a_w2.shape[1]
    nk, dh = peer_keys.shape[2], peer_keys.shape[3]
    p_heads = peer_wq.shape[2] // (2 * dh)
    assert sp % HGRN_CHUNK == 0 and ss <= HGRN_CHUNK and n_meta <= HGRN_CHUNK
    assert hb_heads % 2 == 0 and 2 * hd == LANES and lw + la == LANES and nk == LANES

    row = lambda t: t.reshape(1, -1).astype(F32)
    w_a = w_in[0, :, :ha_in].astype(BF16)
    w_b = w_in[0, :, ha_in:].astype(BF16)
    g1 = row(norm1[0])
    lbt = lb_theta.astype(F32)
    gn = row(hgrn_norm[0])

    head_of_lane = np.arange(LANES) // hd
    head_of_col = np.arange(hw) // hd
    rw = {
        "mu": row(mu_shift[0]), "w0": row(w0[0]), "a0": row(a0[0]), "kk": row(k_k[0]), "ka": row(k_a[0]),
        "rk": row(r_k[0]), "lnw": row(ln_w[0]), "lnb": row(ln_b[0]), "gw": g_w2[0].astype(F32),
        "ww": jnp.concatenate([w_w2[0], jnp.zeros((la, hw), F32)], axis=0),
        "aw": jnp.concatenate([jnp.zeros((lw, hw), F32), a_w2[0]], axis=0),
        "ones": jnp.asarray(head_of_lane[:, None] == head_of_lane[None, :], BF16),
        "seg": jnp.asarray(head_of_col[:, None] == head_of_col[None, :], BF16),
    }
    woa = w_out[0, :ha_heads * dv].astype(BF16)
    wob = w_out[0, ha_heads * dv:].astype(BF16)
    g2 = row(norm2[0])
    wq = peer_wq[0].astype(BF16)
    keys = peer_keys[0].astype(BF16)
    u = _pack_table(peer_u[0].astype(F32), False)
    vt = _pack_table(peer_v[0].astype(F32), True)
    gf = row(norm_f)

    za_m, zb_m = _in_proj(meta_tokens.astype(F32), g1, w_a, w_b, n_meta)
    _, hs_m = _hgrn(za_m[None], 1, n_meta, lbt, gn, jnp.zeros((1, ha_heads, dk, dv), F32), ha_heads, dk, dv)
    _, rs_m, sh_m = _rwkv(zb_m.reshape(1, n_meta, -1), jnp.zeros((1, zb_m.shape[1]), F32),
                          jnp.zeros((1, hb_heads // 2, hd, LANES), F32), rw, 1, n_meta, hd)

    def group(x, nseq, length, hs0, rs0, sh0, chunk_rows, hgrn_nb, tc):
        xf = x.reshape(nseq * length, d)
        n = xf.shape[0]
        tb = _block(n, 256)
        assert tb % LANES == 0
        za, zb = _in_proj(xf, g1, w_a, w_b, tb)
        oa, hs = _hgrn(za.reshape(nseq, length, -1), hgrn_nb, chunk_rows, lbt, gn, hs0, ha_heads, dk, dv)
        ob, rs, sh = _rwkv(zb.reshape(nseq, length, -1), sh0, rs0, rw, 8, tc, hd)
        x1, h2t, st = _out_proj(oa.reshape(n, -1), ob.reshape(n, hw), xf, woa, wob, g2, wq, keys, tb)
        t1, e1, e2 = _route(st, p_heads, nk, tb)
        y = _peer(h2t, st, t1, e1, e2, u, vt, x1, gf, p_heads, nk, tb)
        return y.reshape(nseq, length, d), hs, _unpack_pairs(rs), sh

    y_p, hs_p, rs_p, sh_p = group(
        x_prompt, bp, sp, hs_m, jnp.broadcast_to(rs_m, (bp,) + rs_m.shape[1:]),
        jnp.broadcast_to(sh_m, (bp, sh_m.shape[1])), HGRN_CHUNK, 4, 64)
    y_s, hs_s, rs_s, sh_s = group(
        x_sample, bs, ss, state_hgrn[0].astype(F32), _pack_pairs(state_rwkv[0].astype(F32)),
        state_shift[0].astype(F32), ss, 8, ss)

    return (y_p, y_s, hs_p[None], rs_p[None], sh_p[None], hs_s[None], rs_s[None], sh_s[None])
```

```python
import functools

import numpy as np
import jax
import jax.numpy as jnp
from jax import lax
from jax.experimental import pallas as pl
from jax.experimental.pallas import tpu as pltpu

F32 = jnp.float32
BF16 = jnp.bfloat16

EPS = 1e-6
GN_EPS = 64e-5
HGRN_CHUNK = 64
HGRN_SHORT_CHUNK = 16
PEER_TOPK = 16
RWKV_SLOT = 32
LANES = 128
SUBLANES = 8
VMEM_LIMIT_BYTES = 52 * 1024 * 1024

_NT = (((1,), (1,)), ((), ()))
_TN = (((0,), (0,)), ((), ()))


def _dot(a, b):
    return jnp.dot(a, b, preferred_element_type=F32)


def _split3(a):
    a1 = a.astype(BF16)
    r1 = a - a1.astype(F32)
    a2 = r1.astype(BF16)
    a3 = (r1 - a2.astype(F32)).astype(BF16)
    return a1, a2, a3


def _dot_exact_rhs(a, b_exact):
    a1, a2, a3 = _split3(a)
    return _dot(a1, b_exact) + _dot(a2, b_exact) + _dot(a3, b_exact)


def _dot_exact_lhs(a_exact, b):
    b1, b2, b3 = _split3(b)
    return _dot(a_exact, b1) + _dot(a_exact, b2) + _dot(a_exact, b3)


def _dot_hi(a, b):
    ah = a.astype(BF16)
    al = (a - ah.astype(F32)).astype(BF16)
    bh = b.astype(BF16)
    bl = (b - bh.astype(F32)).astype(BF16)
    return _dot(ah, bh) + _dot(ah, bl) + _dot(al, bh)


def _sigmoid(x):
    return 1.0 / (1.0 + jnp.exp(-x))


def _rms(x, g):
    return x * lax.rsqrt(jnp.mean(x * x, axis=-1, keepdims=True) + EPS) * g


def _params(*sem):
    return pltpu.CompilerParams(dimension_semantics=sem, vmem_limit_bytes=VMEM_LIMIT_BYTES)


def _const_spec(shape):
    nd = len(shape)
    return pl.BlockSpec(shape, lambda *_: (0,) * nd)


def _in_proj_body(x_ref, g_ref, wa_ref, wb_ref, za_ref, zb_ref):
    hb = _rms(x_ref[...], g_ref[...]).astype(BF16)
    za_ref[...] = _dot(hb, wa_ref[...])
    zb_ref[...] = _dot(hb, wb_ref[...])


def _in_proj(x, g, wa, wb, tb):
    n, d = x.shape
    na, nb = wa.shape[1], wb.shape[1]
    return pl.pallas_call(
        _in_proj_body,
        grid=(n // tb,),
        in_specs=[pl.BlockSpec((tb, d), lambda i: (i, 0)), _const_spec((1, d)),
                  _const_spec(wa.shape), _const_spec(wb.shape)],
        out_specs=[pl.BlockSpec((tb, na), lambda i: (i, 0)), pl.BlockSpec((tb, nb), lambda i: (i, 0))],
        out_shape=[jax.ShapeDtypeStruct((n, na), F32), jax.ShapeDtypeStruct((n, nb), F32)],
        compiler_params=_params("parallel"),
        name="in_proj",
    )(x, g, wa, wb)


def _hgrn_tables(c):
    nl = int(np.log2(c)) + 1
    mats = np.zeros((nl, c, c), np.float32)
    masks = np.zeros((nl, c, c), np.float32)
    idx = np.arange(c)
    mats[0] = (idx[None, :] <= idx[:, None])
    masks[0] = np.eye(c)
    for l in range(1, nl):
        blk = 1 << l
        half = blk >> 1
        for t in range(c):
            start = (t // blk) * blk
            ref = start + half - 1
            if t - start >= half:
                mats[l, t, ref + 1:t + 1] = 1.0
                masks[l, t, start:start + half] = 1.0
            else:
                mats[l, t, t + 1:ref + 1] = 1.0
    return mats.reshape(nl * c, c), masks


def _hgrn_body(za_ref, lbt_ref, gn_ref, s0_ref, mats_ref, masks_ref, oa_ref, sout_ref, st_scr,
               *, nb, rows, chunk, heads, dk, dv):
    c_id = pl.program_id(1)
    qk = heads * dk
    wd = heads * dv
    nl = masks_ref.shape[0]

    @pl.when(c_id == 0)
    def _():
        for s in range(nb):
            for h in range(heads):
                st_scr[s, h] = s0_ref[min(s, s0_ref.shape[0] - 1), h].T

    th = lbt_ref[...]
    ex = jnp.exp(th - jnp.max(th, axis=0, keepdims=True))
    lb = ex[0:1] / jnp.sum(ex, axis=0, keepdims=True)
    gn = gn_ref[...]

    for s in range(nb):
        za = za_ref[s]
        zq = za[:, :qk]
        zf = za[:, qk:2 * qk]
        zi = za[:, 2 * qk:2 * qk + wd]
        zo = za[:, 2 * qk + wd:]
        logf = jnp.log(lb + (1.0 - lb) * _sigmoid(zf))
        kin = (1.0 - lb) * _sigmoid(-zf)
        qa = zq * _sigmoid(zq)
        if rows < chunk:
            pad = jnp.zeros((chunk - rows, qk), F32)
            logf = jnp.concatenate([logf, pad], axis=0)
            kin = jnp.concatenate([kin, pad], axis=0)
            qa = jnp.concatenate([qa, pad], axis=0)
            zi = jnp.concatenate([zi, jnp.zeros((chunk - rows, wd), F32)], axis=0)

        dec = _dot_exact_lhs(mats_ref[...], logf)
        for h in range(heads):
            ks = slice(h * dk, (h + 1) * dk)
            vs = slice(h * dv, (h + 1) * dv)
            q = qa[:, ks]
            k = kin[:, ks]
            v = zi[:, vs].astype(BF16)
            cum = dec[0:chunk, ks]
            sc = lax.dot_general(q.astype(BF16), k.astype(BF16), _NT, preferred_element_type=F32)
            scores = jnp.where(masks_ref[0] != 0.0, sc, 0.0)
            for l in range(1, nl):
                e = jnp.exp(dec[l * chunk:(l + 1) * chunk, ks])
                sc = lax.dot_general((q * e).astype(BF16), (k * e).astype(BF16), _NT, preferred_element_type=F32)
                scores = scores + jnp.where(masks_ref[l] != 0.0, sc, 0.0)
            st = st_scr[s, h]
            o = _dot(scores.astype(BF16), v) + lax.dot_general(
                (q * jnp.exp(cum)).astype(BF16), st.astype(BF16), _NT, preferred_element_type=F32)
            last = cum[chunk - 1:chunk, :]
            kd = (k * jnp.exp(last - cum)).astype(BF16)
            st_scr[s, h] = jnp.exp(last) * st + lax.dot_general(v, kd, _TN, preferred_element_type=F32)

            o = o[:rows]
            o = o * lax.rsqrt(jnp.mean(o * o, axis=-1, keepdims=True) + EPS) * gn[:, vs]
            zoh = zo[:, vs]
            oa_ref[s, :, vs] = o * (zoh * _sigmoid(zoh))

    @pl.when(c_id == pl.num_programs(1) - 1)
    def _():
        for s in range(nb):
            for h in range(heads):
                sout_ref[s, h] = st_scr[s, h].T


def _hgrn(za, nb, rows, lb_theta, gn, s0, heads, dk, dv):
    nseq, length, cols = za.shape
    chunk = HGRN_CHUNK if rows > HGRN_SHORT_CHUNK else HGRN_SHORT_CHUNK
    mats, masks = _hgrn_tables(chunk)
    mats = jnp.asarray(mats, BF16)
    masks = jnp.asarray(masks, F32)
    wd = heads * dv
    shared = s0.shape[0] == 1
    s0_spec = (pl.BlockSpec((1, heads, dk, dv), lambda s, c: (0, 0, 0, 0)) if shared
               else pl.BlockSpec((nb, heads, dk, dv), lambda s, c: (s, 0, 0, 0)))
    body = functools.partial(_hgrn_body, nb=nb, rows=rows, chunk=chunk, heads=heads, dk=dk, dv=dv)
    return pl.pallas_call(
        body,
        grid=(nseq // nb, length // rows),
        in_specs=[pl.BlockSpec((nb, rows, cols), lambda s, c: (s, c, 0)),
                  _const_spec(lb_theta.shape), _const_spec(gn.shape), s0_spec,
                  _const_spec(mats.shape), _const_spec(masks.shape)],
        out_specs=[pl.BlockSpec((nb, rows, wd), lambda s, c: (s, c, 0)),
                   pl.BlockSpec((nb, heads, dk, dv), lambda s, c: (s, 0, 0, 0))],
        out_shape=[jax.ShapeDtypeStruct((nseq, length, wd), F32),
                   jax.ShapeDtypeStruct((nseq, heads, dk, dv), F32)],
        scratch_shapes=[pltpu.VMEM((nb, heads, dv, dk), F32)],
        compiler_params=_params("parallel", "arbitrary"),
        name="hgrn",
    )(za, lb_theta, gn, s0, mats, masks)


def _rwkv_body(zb_ref, sh0_ref, s0_ref, mu_ref, w0_ref, ww_ref, a0_ref, aw_ref, gw_ref, kk_ref, ka_ref,
               rk_ref, lnw_ref, lnb_ref, ones_ref, seg_ref,
               ob_ref, sout_ref, shout_ref,
               s_scr, prev_scr, al_scr, w_scr, be_scr, k_scr, r_scr, v_scr, g_scr, y_scr, vt_scr,
               *, nb, tc, hw, hd, lora):
    t_id = pl.program_id(1)
    pairs = hw // LANES
    rin = zb_ref.shape[-1]

    @pl.when(t_id == 0)
    def _():
        s_scr[...] = s0_ref[...]
        prev_scr[...] = sh0_ref[...]

    row = lax.broadcasted_iota(jnp.int32, (tc, rin), 0)
    mu = mu_ref[...]
    zms = []
    for b in range(nb):
        zb = zb_ref[b]
        prev = jnp.where(row == 0, prev_scr[b:b + 1, :], pltpu.roll(zb, 1, axis=0))
        prev_scr[b:b + 1, :] = zb[tc - 1:tc, :]
        zms.append(zb + (prev - zb) * mu)
    zm = jnp.concatenate(zms, axis=0)
    r = zm[:, :hw]
    k = zm[:, hw:2 * hw]
    v = zm[:, 2 * hw:3 * hw]
    dwa = zm[:, 3 * hw:3 * hw + lora]
    dg = zm[:, 3 * hw + lora:]
    w_raw = w0_ref[...] + _dot_hi(jnp.tanh(dwa), ww_ref[...])
    decay = jnp.exp(-np.float32(np.exp(-0.5)) * _sigmoid(w_raw))
    a = _sigmoid(a0_ref[...] + _dot_hi(dwa, aw_ref[...]))
    g = _dot_hi(_sigmoid(dg), gw_ref[...])
    kk = k * kk_ref[...]
    ss = _dot_exact_rhs(kk * kk, seg_ref[...])
    kk = kk / jnp.maximum(jnp.sqrt(ss), 1e-12)
    k2 = k * (1.0 + (a - 1.0) * ka_ref[...])
    shp = (nb, tc, hw)
    al_scr[...] = (-kk).reshape(shp)
    w_scr[...] = decay.reshape(shp)
    be_scr[...] = (kk * a).reshape(shp)
    k_scr[...] = k2.reshape(shp)
    r_scr[...] = r.reshape(shp)
    v_scr[...] = v.reshape(shp)
    g_scr[...] = g.reshape(shp)

    slot = RWKV_SLOT
    li2 = lax.broadcasted_iota(jnp.int32, (slot, LANES), 1)
    vi = lax.broadcasted_iota(jnp.int32, (hd, LANES), 0)
    sel = jnp.where(vi == lax.broadcasted_iota(jnp.int32, (hd, LANES), 1) % hd, 1.0, 0.0).astype(BF16)
    for b in range(nb):
        for p in range(pairs):
            for g0 in range(0, tc, slot):
                nt = min(slot, tc - g0)
                vb = v[b * tc + g0:b * tc + g0 + nt, p * LANES:(p + 1) * LANES]
                if nt < slot:
                    vb = jnp.concatenate([vb, jnp.zeros((slot - nt, LANES), F32)], axis=0)
                vh = vb.astype(BF16)
                vl = (vb - vh.astype(F32)).astype(BF16)
                zero = jnp.zeros_like(vh)
                stack = jnp.concatenate([jnp.where(li2 < hd, vh, zero), jnp.where(li2 >= hd, vh, zero),
                                         jnp.where(li2 < hd, vl, zero), jnp.where(li2 >= hd, vl, zero)], axis=0)
                vt_scr[g0 // slot, b, p] = lax.dot_general(
                    sel, stack, _NT, preferred_element_type=F32).astype(BF16)

    rows = pairs * hd
    ri = lax.broadcasted_iota(jnp.int32, (rows, LANES), 0)
    li = lax.broadcasted_iota(jnp.int32, (rows, LANES), 1)
    diag = (ri % hd) == (li % hd)
    ones1 = ones_ref[...]
    qi = lax.broadcasted_iota(jnp.int32, (4 * slot, LANES), 0)
    ql = lax.broadcasted_iota(jnp.int32, (4 * slot, LANES), 1)
    tok_of_row = jnp.where((qi // slot) % 2 == ql // hd, qi % slot, -1)

    def bcast(ref, b, t):
        rowv = ref[b, pl.ds(t, 1), :]
        return jnp.concatenate(
            [jnp.broadcast_to(rowv[:, p * LANES:(p + 1) * LANES], (hd, LANES)) for p in range(pairs)], axis=0)

    def step(t, carry):
        pick = jnp.where(tok_of_row == t % slot, 1.0, 0.0).astype(BF16)
        seqs = range(nb)
        vcols = [_dot(vt_scr[t // slot, b].reshape(rows, 4 * slot), pick) for b in seqs]
        for b in seqs:
            old = s_scr[b].reshape(rows, LANES)
            sa = _dot((old * bcast(al_scr, b, t)).astype(BF16), ones1)
            new = old * bcast(w_scr, b, t) + sa * bcast(be_scr, b, t) + vcols[b] * bcast(k_scr, b, t)
            s_scr[b] = new.reshape(pairs, hd, LANES)
            yrep = _dot((new * bcast(r_scr, b, t)).astype(BF16), ones1)
            ysel = jnp.where(diag, yrep, 0.0).reshape(pairs, hd, LANES)
            for p in range(pairs):
                y_scr[b, p, pl.ds(t, 1), :] = jnp.sum(ysel[p], axis=0, keepdims=True)
        return carry

    lax.fori_loop(0, tc, step, 0, unroll=4)

    y = jnp.concatenate([y_scr[:, p].reshape(nb * tc, LANES) for p in range(pairs)], axis=1)
    seg = seg_ref[...]
    inv = np.float32(1.0 / hd)
    mean = _dot_exact_rhs(y, seg) * inv
    yc = y - mean
    var = _dot_exact_rhs(yc * yc, seg) * inv
    yn = yc * lax.rsqrt(var + GN_EPS) * lnw_ref[...] + lnb_ref[...]
    r = r_scr[...].reshape(nb * tc, hw)
    k2 = k_scr[...].reshape(nb * tc, hw)
    v = v_scr[...].reshape(nb * tc, hw)
    bonus = _dot_exact_rhs(r * k2 * rk_ref[...], seg) * v
    ob_ref[...] = ((yn + bonus) * g_scr[...].reshape(nb * tc, hw)).reshape(shp)

    @pl.when(t_id == pl.num_programs(1) - 1)
    def _():
        sout_ref[...] = s_scr[...]
        shout_ref[...] = prev_scr[...]


def _rwkv(zb, sh0, s0, wts, nb, tc, hd):
    nseq, length, rin = zb.shape
    hw = wts["w0"].shape[1]
    pairs = hw // LANES
    lora = wts["ww"].shape[0]
    body = functools.partial(_rwkv_body, nb=nb, tc=tc, hw=hw, hd=hd, lora=lora)
    names = ("mu", "w0", "ww", "a0", "aw", "gw", "kk", "ka", "rk", "lnw", "lnb", "ones", "seg")
    consts = [wts[n] for n in names]
    tok = pltpu.VMEM((nb, tc, hw), F32)
    return pl.pallas_call(
        body,
        grid=(nseq // nb, length // tc),
        in_specs=[pl.BlockSpec((nb, tc, rin), lambda s, t: (s, t, 0)),
                  pl.BlockSpec((nb, rin), lambda s, t: (s, 0)),
                  pl.BlockSpec((nb, pairs, hd, LANES), lambda s, t: (s, 0, 0, 0))]
                 + [_const_spec(c.shape) for c in consts],
        out_specs=[pl.BlockSpec((nb, tc, hw), lambda s, t: (s, t, 0)),
                   pl.BlockSpec((nb, pairs, hd, LANES), lambda s, t: (s, 0, 0, 0)),
                   pl.BlockSpec((nb, rin), lambda s, t: (s, 0))],
        out_shape=[jax.ShapeDtypeStruct((nseq, length, hw), F32),
                   jax.ShapeDtypeStruct((nseq, pairs, hd, LANES), F32),
                   jax.ShapeDtypeStruct((nseq, rin), F32)],
        scratch_shapes=[pltpu.VMEM((nb, pairs, hd, LANES), F32), pltpu.VMEM((nb, rin), F32)]
                       + [tok] * 7 + [pltpu.VMEM((nb, pairs, tc, LANES), F32),
                                      pltpu.VMEM((-(-tc // RWKV_SLOT), nb, pairs, hd, 4 * RWKV_SLOT), BF16)],
        compiler_params=_params("parallel", "arbitrary"),
        name="rwkv",
    )(zb, sh0, s0, *consts)


def _pack_pairs(s):
    n, heads, hd, _ = s.shape
    return s.reshape(n, heads // 2, 2, hd, hd).transpose(0, 1, 3, 2, 4).reshape(n, heads // 2, hd, 2 * hd)


def _unpack_pairs(s):
    n, pairs, hd, _ = s.shape
    return s.reshape(n, pairs, hd, 2, hd).transpose(0, 1, 3, 2, 4).reshape(n, 2 * pairs, hd, hd)


def _out_proj_body(oa_ref, ob_ref, x_ref, woa_ref, wob_ref, g2_ref, wq_ref, keys_ref,
                   x1_ref, h2t_ref, st_ref, *, nk, dh):
    mix = _dot(oa_ref[...].astype(BF16), woa_ref[...]) + _dot(ob_ref[...].astype(BF16), wob_ref[...])
    x1 = x_ref[...] + mix
    x1_ref[...] = x1
    h2f = _rms(x1, g2_ref[...])
    h2t_ref[...] = h2f.T.astype(BF16)
    q = _dot(h2f.astype(BF16), wq_ref[...]).astype(BF16)
    for hp in range(q.shape[1] // dh):
        st_ref[hp * nk:(hp + 1) * nk, :] = lax.dot_general(
            keys_ref[hp % 2], q[:, hp * dh:(hp + 1) * dh], _NT, preferred_element_type=F32)


def _out_proj(oa, ob, x, woa, wob, g2, wq, keys, tb):
    n, d = x.shape
    nk, dh = keys.shape[1], keys.shape[2]
    nrow = (wq.shape[1] // dh) * nk
    body = functools.partial(_out_proj_body, nk=nk, dh=dh)
    return pl.pallas_call(
        body,
        grid=(n // tb,),
        in_specs=[pl.BlockSpec((tb, oa.shape[1]), lambda i: (i, 0)),
                  pl.BlockSpec((tb, ob.shape[1]), lambda i: (i, 0)),
                  pl.BlockSpec((tb, d), lambda i: (i, 0)),
                  _const_spec(woa.shape), _const_spec(wob.shape), _const_spec(g2.shape),
                  _const_spec(wq.shape), _const_spec(keys.shape)],
        out_specs=[pl.BlockSpec((tb, d), lambda i: (i, 0)), pl.BlockSpec((d, tb), lambda i: (0, i)),
                   pl.BlockSpec((nrow, tb), lambda i: (0, i))],
        out_shape=[jax.ShapeDtypeStruct((n, d), F32), jax.ShapeDtypeStruct((d, n), BF16),
                   jax.ShapeDtypeStruct((nrow, n), F32)],
        compiler_params=_params("parallel"),
        name="out_proj",
    )(oa, ob, x, woa, wob, g2, wq, keys)


def _sort16_network():
    n, pairs, p = 16, [], 1
    while p < n:
        k = p
        while k >= 1:
            for j in range(k % p, n - k, 2 * k):
                for i in range(min(k, n - j - k)):
                    if (i + j) // (2 * p) == (i + j + k) // (2 * p):
                        pairs.append((i + j, i + j + k))
            k //= 2
        p *= 2
    return pairs


def _top16_desc(tiles):
    t = list(tiles)

    def exchange(i, j):
        t[i], t[j] = jnp.maximum(t[i], t[j]), jnp.minimum(t[i], t[j])

    for i, j in _sort16_network():
        exchange(i, j)
    for shift in (4, 2, 1):
        t = [jnp.maximum(t[i], pltpu.roll(t[15 - i], shift, axis=0)) for i in range(16)]
        for d in (8, 4, 2, 1):
            for i in range(16):
                if not i & d:
                    exchange(i, i + d)
    return t


def _route_body(st_ref, t1_ref, e1_ref, e2_ref, top_scr, cand_scr, *, nk, topk):
    neg = -jnp.inf
    nx = topk + 1
    half = topk // 2

    def top_values(h, s):
        tiles = [s[i * SUBLANES:(i + 1) * SUBLANES] for i in range(nk // SUBLANES)]
        srt = _top16_desc(tiles)
        for i in range(topk):
            top_scr[h, i:i + 1, :] = srt[i][0:1]
        top_scr[h, topk:nx, :] = jnp.max(jnp.where(s < srt[topk - 1][0:1], s, neg), axis=0, keepdims=True)

    s1 = st_ref[0:nk, :]
    s2 = st_ref[nk:2 * nk, :]
    top_values(0, s1)
    top_values(1, s2)
    cand_scr[...] = jnp.full(cand_scr.shape, neg, F32)
    sv2 = top_scr[1, 0:nx, :]
    cand_scr[0:nx, :] = top_scr[0, 0:1, :] + sv2
    base = 3 * SUBLANES
    for a in range(1, half):
        cand_scr[base + (a - 1) * half:base + a * half, :] = top_scr[0, a:a + 1, :] + sv2[0:half]
    base += (half - 1) * half
    cand_scr[base:base + nx - half, :] = top_scr[0, half:nx, :] + sv2[0:1]

    cand = cand_scr[...]
    best = _top16_desc([cand[i * SUBLANES:(i + 1) * SUBLANES] for i in range(nk // SUBLANES)])
    z = jnp.zeros_like(best[0][0:1])
    for i in range(topk):
        z = z + jnp.exp(best[i][0:1] - best[0][0:1])
    hi = best[topk - 1][0:1]
    lo = jnp.max(jnp.where(cand < hi, cand, neg), axis=0, keepdims=True)
    thr = jnp.where(lo == neg, hi, 0.5 * (hi + lo))
    t1_ref[0] = thr - s1
    e1_ref[0] = jnp.exp(s1 - top_scr[0, 0:1, :]) * (np.float32(np.sqrt(0.5)) / z)
    e2_ref[0] = jnp.exp(s2 - sv2[0:1])


def _route(st, heads, nk, tb):
    n = st.shape[1]
    topk = PEER_TOPK
    half = topk // 2
    ncand = nk
    assert topk == 16 and nk == 16 * SUBLANES and half == SUBLANES
    assert 3 * SUBLANES + (half - 1) * half + topk + 1 - half <= ncand
    body = functools.partial(_route_body, nk=nk, topk=topk)
    blk = pl.BlockSpec((1, nk, tb), lambda i, h: (h, 0, i))
    return pl.pallas_call(
        body,
        grid=(n // tb, heads),
        in_specs=[pl.BlockSpec((2 * nk, tb), lambda i, h: (h, i))],
        out_specs=[blk, blk, blk],
        out_shape=[jax.ShapeDtypeStruct((heads, nk, n), F32)] * 3,
        scratch_shapes=[pltpu.VMEM((2, 3 * SUBLANES, tb), F32), pltpu.VMEM((ncand, tb), F32)],
        compiler_params=_params("parallel", "parallel"),
        name="route",
    )(st)


PEER_SUB = 2 * LANES
PEER_ROW_SPLIT = 2


def _peer_body(h2t_ref, s2_ref, t1_ref, e1_ref, e2_ref, u_ref, vt_ref, x1_ref, gf_ref, y_ref,
               acc_scr, coef_scr, *, heads, nk, tb, n_eb):
    e = pl.program_id(1)
    eblk = 2 * u_ref.shape[0]
    i1_per_block = eblk // nk
    i1_per_sub = PEER_SUB // nk
    cur = e % 2

    @pl.when(e == 0)
    def _():
        acc_scr[...] = jnp.zeros_like(acc_scr)
        coef_scr[1] = jnp.zeros(coef_scr.shape[1:], BF16)

    i1_0 = pl.multiple_of(jnp.minimum(e, n_eb - 1) * i1_per_block, SUBLANES)
    n_sub = eblk // PEER_SUB
    h2t = h2t_ref[...]

    def matmuls(k):
        ks = slice(k * PEER_SUB, (k + 1) * PEER_SUB)
        u_rows = pltpu.bitcast(u_ref[k * PEER_SUB // 2:(k + 1) * PEER_SUB // 2, :], BF16)
        vt_cols = pltpu.bitcast(vt_ref[:, ks], BF16)
        return _dot(u_rows, h2t), _dot(vt_cols, coef_scr[1 - cur, ks, :])

    act, out = matmuls(0)
    for k in range(n_sub):
        if k + 1 < n_sub:
            act_next, part = matmuls(k + 1)
            out = out + part
        for lg in range(tb // LANES):
            ls = slice(lg * LANES, (lg + 1) * LANES)
            t1b = [t1_ref[h, pl.ds(i1_0, i1_per_block), ls] for h in range(heads)]
            e1b = [e1_ref[h, pl.ds(i1_0, i1_per_block), ls] for h in range(heads)]
            for rs in range(PEER_ROW_SPLIT):
                rsl = slice(rs * nk // PEER_ROW_SPLIT, (rs + 1) * nk // PEER_ROW_SPLIT)
                gates = [None] * i1_per_sub
                for h in range(heads):
                    s2 = s2_ref[h, rsl, ls]
                    e2 = e2_ref[h, rsl, ls]
                    for jj in range(i1_per_sub):
                        j = k * i1_per_sub + jj
                        term = jnp.where(s2 >= t1b[h][j:j + 1], e1b[h][j:j + 1] * e2, 0.0)
                        gates[jj] = term if gates[jj] is None else gates[jj] + term
                for jj in range(i1_per_sub):
                    a = act[jj * nk + rsl.start:jj * nk + rsl.stop, ls]
                    r0 = (k * i1_per_sub + jj) * nk
                    coef_scr[cur, r0 + rsl.start:r0 + rsl.stop, ls] = (gates[jj] * (a + a * lax.erf(a))).astype(BF16)
        if k + 1 < n_sub:
            act = act_next
    acc_scr[...] += out

    @pl.when(e == n_eb)
    def _():
        y_ref[...] = _rms(x1_ref[...] + acc_scr[...].T, gf_ref[...])


def _peer(h2t, st, t1, e1, e2, u, vt, x1, gf, heads, nk, tb):
    n, d = x1.shape
    ne = 2 * u.shape[0]
    eblk = SUBLANES * nk
    n_eb = ne // eblk
    body = functools.partial(_peer_body, heads=heads, nk=nk, tb=tb, n_eb=n_eb)
    tok = pl.BlockSpec((heads, nk, tb), lambda i, e: (0, 0, i))
    return pl.pallas_call(
        body,
        grid=(n // tb, n_eb + 1),
        in_specs=[pl.BlockSpec((d, tb), lambda i, e: (0, i)),
                  pl.BlockSpec((heads, None, nk, tb), lambda i, e: (0, 1, 0, i)),
                  tok, tok, tok,
                  pl.BlockSpec((eblk // 2, d), lambda i, e: (jnp.minimum(e, n_eb - 1), 0)),
                  pl.BlockSpec((d // 2, eblk), lambda i, e: (0, jnp.maximum(e - 1, 0))),
                  pl.BlockSpec((tb, d), lambda i, e: (i, 0)),
                  _const_spec(gf.shape)],
        out_specs=pl.BlockSpec((tb, d), lambda i, e: (i, 0)),
        out_shape=jax.ShapeDtypeStruct((n, d), F32),
        scratch_shapes=[pltpu.VMEM((d, tb), F32), pltpu.VMEM((2, eblk, tb), BF16)],
        compiler_params=_params("parallel", "arbitrary"),
        name="peer",
    )(h2t, st.reshape(heads, 2, nk, n), t1, e1, e2, u, vt, x1, gf)


def _pack_body(x_ref, o_ref, *, transpose, scale):
    x = x_ref[...] * np.float32(scale)
    if transpose:
        x = x.T
    o_ref[...] = pltpu.bitcast(x.astype(BF16), jnp.uint32)


def _pack_table(x, transpose, scale=1.0, rows=512):
    n, c = x.shape
    if transpose:
        out_shape, out_spec = (c // 2, n), pl.BlockSpec((c // 2, rows), lambda i: (0, i))
    else:
        out_shape, out_spec = (n // 2, c), pl.BlockSpec((rows // 2, c), lambda i: (i, 0))
    return pl.pallas_call(
        functools.partial(_pack_body, transpose=transpose, scale=scale),
        grid=(n // rows,),
        in_specs=[pl.BlockSpec((rows, c), lambda i: (i, 0))],
        out_specs=out_spec,
        out_shape=jax.ShapeDtypeStruct(out_shape, jnp.uint32),
        compiler_params=_params("parallel"),
        name="pack_table",
    )(x)


def _block(n, pref):
    b = pref
    while n % b:
        b //= 2
    return b


def kernel(x_prompt, x_sample, state_hgrn, state_rwkv, state_shift, meta_tokens, norm1, w_in, lb_theta, hgrn_norm, mu_shift, w0, w_w2, a0, a_w2, g_w2, k_k, k_a, r_k, ln_w, ln_b, w_out, norm2, peer_wq, peer_keys, peer_u, peer_v, norm_f):
    assert w_in.shape[0] == 1, "single-layer trunk"
    bp, sp, d = x_prompt.shape
    bs, ss, _ = x_sample.shape
    n_meta = meta_tokens.shape[0]
    _, _, ha_heads, dk, dv = state_hgrn.shape
    _, _, hb_heads, hd, _ = state_rwkv.shape
    ha_in = 2 * ha_heads * dk + 2 * ha_heads * dv
    hw = hb_heads * hd
    lw, la = w_w2.shape[1], a_w2.shape[1]
    nk, dh = peer_keys.shape[2], peer_keys.shape[3]
    p_heads = peer_wq.shape[2] // (2 * dh)
    assert sp % HGRN_CHUNK == 0 and ss <= HGRN_CHUNK and n_meta <= HGRN_CHUNK
    assert hb_heads % 2 == 0 and 2 * hd == LANES and lw + la == LANES and nk == LANES

    row = lambda t: t.reshape(1, -1).astype(F32)
    w_a = w_in[0, :, :ha_in].astype(BF16)
    w_b = w_in[0, :, ha_in:].astype(BF16)
    g1 = row(norm1[0])
    lbt = lb_theta.astype(F32)
    gn = row(hgrn_norm[0])

    head_of_lane = np.arange(LANES) // hd
    head_of_col = np.arange(hw) // hd
    rw = {
        "mu": row(mu_shift[0]), "w0": row(w0[0]), "a0": row(a0[0]), "kk": row(k_k[0]), "ka": row(k_a[0]),
        "rk": row(r_k[0]), "lnw": row(ln_w[0]), "lnb": row(ln_b[0]), "gw": g_w2[0].astype(F32),
        "ww": jnp.concatenate([w_w2[0], jnp.zeros((la, hw), F32)], axis=0),
        "aw": jnp.concatenate([jnp.zeros((lw, hw), F32), a_w2[0]], axis=0),
        "ones": jnp.asarray(head_of_lane[:, None] == head_of_lane[None, :], BF16),
        "seg": jnp.asarray(head_of_col[:, None] == head_of_col[None, :], BF16),
    }
    woa = w_out[0, :ha_heads * dv].astype(BF16)
    wob = w_out[0, ha_heads * dv:].astype(BF16)
    g2 = row(norm2[0])
    wq = peer_wq[0].astype(BF16)
    keys = peer_keys[0].astype(BF16)
    u = _pack_table(peer_u[0].astype(F32), False, scale=np.sqrt(0.5))
    vt = _pack_table(peer_v[0].astype(F32), True)
    gf = row(norm_f)

    za_m, zb_m = _in_proj(meta_tokens.astype(F32), g1, w_a, w_b, n_meta)
    _, hs_m = _hgrn(za_m[None], 1, n_meta, lbt, gn, jnp.zeros((1, ha_heads, dk, dv), F32), ha_heads, dk, dv)
    _, rs_m, sh_m = _rwkv(zb_m.reshape(1, n_meta, -1), jnp.zeros((1, zb_m.shape[1]), F32),
                          jnp.zeros((1, hb_heads // 2, hd, LANES), F32), rw, 1, n_meta, hd)

    def group(x, nseq, length, hs0, rs0, sh0, chunk_rows, hgrn_nb, tc):
        xf = x.reshape(nseq * length, d)
        n = xf.shape[0]
        tb = _block(n, 256)
        assert tb % LANES == 0
        za, zb = _in_proj(xf, g1, w_a, w_b, tb)
        oa, hs = _hgrn(za.reshape(nseq, length, -1), hgrn_nb, chunk_rows, lbt, gn, hs0, ha_heads, dk, dv)
        ob, rs, sh = _rwkv(zb.reshape(nseq, length, -1), sh0, rs0, rw, 8, tc, hd)
        x1, h2t, st = _out_proj(oa.reshape(n, -1), ob.reshape(n, hw), xf, woa, wob, g2, wq, keys, tb)
        t1, e1, e2 = _route(st, p_heads, nk, tb)
        y = _peer(h2t, st, t1, e1, e2, u, vt, x1, gf, p_heads, nk, tb)
        return y.reshape(nseq, length, d), hs, _unpack_pairs(rs), sh

    y_p, hs_p, rs_p, sh_p = group(
        x_prompt, bp, sp, hs_m, jnp.broadcast_to(rs_m, (bp,) + rs_m.shape[1:]),
        jnp.broadcast_to(sh_m, (bp, sh_m.shape[1])), HGRN_CHUNK, 4, 64)
    y_s, hs_s, rs_s, sh_s = group(
        x_sample, bs, ss, state_hgrn[0].astype(F32), _pack_pairs(state_rwkv[0].astype(F32)),
        state_shift[0].astype(F32), ss, 8, ss)

    return (y_p, y_s, hs_p[None], rs_p[None], sh_p[None], hs_s[None], rs_s[None], sh_s[None])
```

```python
import functools

import numpy as np
import jax
import jax.numpy as jnp
from jax import lax
from jax.experimental import pallas as pl
from jax.experimental.pallas import tpu as pltpu

F32 = jnp.float32
BF16 = jnp.bfloat16

EPS = 1e-6
GN_EPS = 64e-5
HGRN_CHUNK = 64
HGRN_SHORT_CHUNK = 16
PEER_TOPK = 16
RWKV_SLOT = 32
LANES = 128
SUBLANES = 8
VMEM_LIMIT_BYTES = 52 * 1024 * 1024

_NT = (((1,), (1,)), ((), ()))
_TN = (((0,), (0,)), ((), ()))


def _dot(a, b):
    return jnp.dot(a, b, preferred_element_type=F32)


def _split3(a):
    a1 = a.astype(BF16)
    r1 = a - a1.astype(F32)
    a2 = r1.astype(BF16)
    a3 = (r1 - a2.astype(F32)).astype(BF16)
    return a1, a2, a3


def _dot_exact_rhs(a, b_exact):
    a1, a2, a3 = _split3(a)
    return _dot(a1, b_exact) + _dot(a2, b_exact) + _dot(a3, b_exact)


def _dot_exact_lhs(a_exact, b):
    b1, b2, b3 = _split3(b)
    return _dot(a_exact, b1) + _dot(a_exact, b2) + _dot(a_exact, b3)


def _dot_hi(a, b):
    ah = a.astype(BF16)
    al = (a - ah.astype(F32)).astype(BF16)
    bh = b.astype(BF16)
    bl = (b - bh.astype(F32)).astype(BF16)
    return _dot(ah, bh) + _dot(ah, bl) + _dot(al, bh)


def _sigmoid(x):
    return 1.0 / (1.0 + jnp.exp(-x))


def _rms(x, g):
    return x * lax.rsqrt(jnp.mean(x * x, axis=-1, keepdims=True) + EPS) * g


def _params(*sem):
    return pltpu.CompilerParams(dimension_semantics=sem, vmem_limit_bytes=VMEM_LIMIT_BYTES)


def _const_spec(shape):
    nd = len(shape)
    return pl.BlockSpec(shape, lambda *_: (0,) * nd)


def _in_proj_body(x_ref, g_ref, wa_ref, wb_ref, za_ref, zb_ref):
    hb = _rms(x_ref[...], g_ref[...]).astype(BF16)
    za_ref[...] = _dot(hb, wa_ref[...])
    zb_ref[...] = _dot(hb, wb_ref[...])


def _in_proj(x, g, wa, wb, tb):
    n, d = x.shape
    na, nb = wa.shape[1], wb.shape[1]
    return pl.pallas_call(
        _in_proj_body,
        grid=(n // tb,),
        in_specs=[pl.BlockSpec((tb, d), lambda i: (i, 0)), _const_spec((1, d)),
                  _const_spec(wa.shape), _const_spec(wb.shape)],
        out_specs=[pl.BlockSpec((tb, na), lambda i: (i, 0)), pl.BlockSpec((tb, nb), lambda i: (i, 0))],
        out_shape=[jax.ShapeDtypeStruct((n, na), F32), jax.ShapeDtypeStruct((n, nb), F32)],
        compiler_params=_params("parallel"),
        name="in_proj",
    )(x, g, wa, wb)


def _hgrn_tables(c):
    nl = int(np.log2(c)) + 1
    mats = np.zeros((nl, c, c), np.float32)
    masks = np.zeros((nl, c, c), np.float32)
    idx = np.arange(c)
    mats[0] = (idx[None, :] <= idx[:, None])
    masks[0] = np.eye(c)
    for l in range(1, nl):
        blk = 1 << l
        half = blk >> 1
        for t in range(c):
            start = (t // blk) * blk
            ref = start + half - 1
            if t - start >= half:
                mats[l, t, ref + 1:t + 1] = 1.0
                masks[l, t, start:start + half] = 1.0
            else:
                mats[l, t, t + 1:ref + 1] = 1.0
    return mats.reshape(nl * c, c), masks


def _hgrn_body(za_ref, lbt_ref, gn_ref, s0_ref, mats_ref, masks_ref, oa_ref, sout_ref, st_scr,
               *, nb, rows, chunk, heads, dk, dv):
    c_id = pl.program_id(1)
    qk = heads * dk
    wd = heads * dv
    nl = masks_ref.shape[0]

    @pl.when(c_id == 0)
    def _():
        for s in range(nb):
            for h in range(heads):
                st_scr[s, h] = s0_ref[min(s, s0_ref.shape[0] - 1), h].T

    th = lbt_ref[...]
    ex = jnp.exp(th - jnp.max(th, axis=0, keepdims=True))
    lb = ex[0:1] / jnp.sum(ex, axis=0, keepdims=True)
    gn = gn_ref[...]

    for s in range(nb):
        za = za_ref[s]
        zq = za[:, :qk]
        zf = za[:, qk:2 * qk]
        zi = za[:, 2 * qk:2 * qk + wd]
        zo = za[:, 2 * qk + wd:]
        logf = jnp.log(lb + (1.0 - lb) * _sigmoid(zf))
        kin = (1.0 - lb) * _sigmoid(-zf)
        qa = zq * _sigmoid(zq)
        if rows < chunk:
            pad = jnp.zeros((chunk - rows, qk), F32)
            logf = jnp.concatenate([logf, pad], axis=0)
            kin = jnp.concatenate([kin, pad], axis=0)
            qa = jnp.concatenate([qa, pad], axis=0)
            zi = jnp.concatenate([zi, jnp.zeros((chunk - rows, wd), F32)], axis=0)

        dec = _dot_exact_lhs(mats_ref[...], logf)
        for h in range(heads):
            ks = slice(h * dk, (h + 1) * dk)
            vs = slice(h * dv, (h + 1) * dv)
            q = qa[:, ks]
            k = kin[:, ks]
            v = zi[:, vs].astype(BF16)
            cum = dec[0:chunk, ks]
            sc = lax.dot_general(q.astype(BF16), k.astype(BF16), _NT, preferred_element_type=F32)
            scores = jnp.where(masks_ref[0] != 0.0, sc, 0.0)
            for l in range(1, nl):
                e = jnp.exp(dec[l * chunk:(l + 1) * chunk, ks])
                sc = lax.dot_general((q * e).astype(BF16), (k * e).astype(BF16), _NT, preferred_element_type=F32)
                scores = scores + jnp.where(masks_ref[l] != 0.0, sc, 0.0)
            st = st_scr[s, h]
            o = _dot(scores.astype(BF16), v) + lax.dot_general(
                (q * jnp.exp(cum)).astype(BF16), st.astype(BF16), _NT, preferred_element_type=F32)
            last = cum[chunk - 1:chunk, :]
            kd = (k * jnp.exp(last - cum)).astype(BF16)
            st_scr[s, h] = jnp.exp(last) * st + lax.dot_general(v, kd, _TN, preferred_element_type=F32)

            o = o[:rows]
            o = o * lax.rsqrt(jnp.mean(o * o, axis=-1, keepdims=True) + EPS) * gn[:, vs]
            zoh = zo[:, vs]
            oa_ref[s, :, vs] = o * (zoh * _sigmoid(zoh))

    @pl.when(c_id == pl.num_programs(1) - 1)
    def _():
        for s in range(nb):
            for h in range(heads):
                sout_ref[s, h] = st_scr[s, h].T


def _hgrn(za, nb, rows, lb_theta, gn, s0, heads, dk, dv):
    nseq, length, cols = za.shape
    chunk = HGRN_CHUNK if rows > HGRN_SHORT_CHUNK else HGRN_SHORT_CHUNK
    mats, masks = _hgrn_tables(chunk)
    mats = jnp.asarray(mats, BF16)
    masks = jnp.asarray(masks, F32)
    wd = heads * dv
    shared = s0.shape[0] == 1
    s0_spec = (pl.BlockSpec((1, heads, dk, dv), lambda s, c: (0, 0, 0, 0)) if shared
               else pl.BlockSpec((nb, heads, dk, dv), lambda s, c: (s, 0, 0, 0)))
    body = functools.partial(_hgrn_body, nb=nb, rows=rows, chunk=chunk, heads=heads, dk=dk, dv=dv)
    return pl.pallas_call(
        body,
        grid=(nseq // nb, length // rows),
        in_specs=[pl.BlockSpec((nb, rows, cols), lambda s, c: (s, c, 0)),
                  _const_spec(lb_theta.shape), _const_spec(gn.shape), s0_spec,
                  _const_spec(mats.shape), _const_spec(masks.shape)],
        out_specs=[pl.BlockSpec((nb, rows, wd), lambda s, c: (s, c, 0)),
                   pl.BlockSpec((nb, heads, dk, dv), lambda s, c: (s, 0, 0, 0))],
        out_shape=[jax.ShapeDtypeStruct((nseq, length, wd), F32),
                   jax.ShapeDtypeStruct((nseq, heads, dk, dv), F32)],
        scratch_shapes=[pltpu.VMEM((nb, heads, dv, dk), F32)],
        compiler_params=_params("parallel", "arbitrary"),
        name="hgrn",
    )(za, lb_theta, gn, s0, mats, masks)


def _rwkv_body(zb_ref, sh0_ref, s0_ref, mu_ref, w0_ref, ww_ref, a0_ref, aw_ref, gw_ref, kk_ref, ka_ref,
               rk_ref, lnw_ref, lnb_ref, ones_ref, seg_ref,
               ob_ref, sout_ref, shout_ref,
               s_scr, prev_scr, al_scr, w_scr, be_scr, k_scr, r_scr, v_scr, g_scr, y_scr, vt_scr,
               *, nb, tc, hw, hd, lora):
    t_id = pl.program_id(1)
    pairs = hw // LANES
    rin = zb_ref.shape[-1]

    @pl.when(t_id == 0)
    def _():
        s_scr[...] = s0_ref[...]
        prev_scr[...] = sh0_ref[...]

    row = lax.broadcasted_iota(jnp.int32, (tc, rin), 0)
    mu = mu_ref[...]
    zms = []
    for b in range(nb):
        zb = zb_ref[b]
        prev = jnp.where(row == 0, prev_scr[b:b + 1, :], pltpu.roll(zb, 1, axis=0))
        prev_scr[b:b + 1, :] = zb[tc - 1:tc, :]
        zms.append(zb + (prev - zb) * mu)
    zm = jnp.concatenate(zms, axis=0)
    r = zm[:, :hw]
    k = zm[:, hw:2 * hw]
    v = zm[:, 2 * hw:3 * hw]
    dwa = zm[:, 3 * hw:3 * hw + lora]
    dg = zm[:, 3 * hw + lora:]
    w_raw = w0_ref[...] + _dot_hi(jnp.tanh(dwa), ww_ref[...])
    decay = jnp.exp(-np.float32(np.exp(-0.5)) * _sigmoid(w_raw))
    a = _sigmoid(a0_ref[...] + _dot_hi(dwa, aw_ref[...]))
    g = _dot_hi(_sigmoid(dg), gw_ref[...])
    kk = k * kk_ref[...]
    ss = _dot_exact_rhs(kk * kk, seg_ref[...])
    kk = kk / jnp.maximum(jnp.sqrt(ss), 1e-12)
    k2 = k * (1.0 + (a - 1.0) * ka_ref[...])
    shp = (nb, tc, hw)
    al_scr[...] = (-kk).reshape(shp)
    w_scr[...] = decay.reshape(shp)
    be_scr[...] = (kk * a).reshape(shp)
    k_scr[...] = k2.reshape(shp)
    r_scr[...] = r.reshape(shp)
    v_scr[...] = v.reshape(shp)
    g_scr[...] = g.reshape(shp)

    slot = RWKV_SLOT
    li2 = lax.broadcasted_iota(jnp.int32, (slot, LANES), 1)
    vi = lax.broadcasted_iota(jnp.int32, (hd, LANES), 0)
    sel = jnp.where(vi == lax.broadcasted_iota(jnp.int32, (hd, LANES), 1) % hd, 1.0, 0.0).astype(BF16)
    for b in range(nb):
        for p in range(pairs):
            for g0 in range(0, tc, slot):
                nt = min(slot, tc - g0)
                vb = v[b * tc + g0:b * tc + g0 + nt, p * LANES:(p + 1) * LANES]
                if nt < slot:
                    vb = jnp.concatenate([vb, jnp.zeros((slot - nt, LANES), F32)], axis=0)
                vh = vb.astype(BF16)
                vl = (vb - vh.astype(F32)).astype(BF16)
                zero = jnp.zeros_like(vh)
                stack = jnp.concatenate([jnp.where(li2 < hd, vh, zero), jnp.where(li2 >= hd, vh, zero),
                                         jnp.where(li2 < hd, vl, zero), jnp.where(li2 >= hd, vl, zero)], axis=0)
                vt_scr[g0 // slot, b, p] = lax.dot_general(
                    sel, stack, _NT, preferred_element_type=F32).astype(BF16)

    rows = pairs * hd
    li = lax.broadcasted_iota(jnp.int32, (rows, LANES), 1)
    key_lane = li % hd
    ones1 = ones_ref[...]
    ones_f = ones1.astype(F32)
    qi = lax.broadcasted_iota(jnp.int32, (4 * slot, LANES), 0)
    ql = lax.broadcasted_iota(jnp.int32, (4 * slot, LANES), 1)
    tok_of_row = jnp.where((qi // slot) % 2 == ql // hd, qi % slot, -1)

    def bcast(ref, b, t):
        rowv = ref[b, pl.ds(t, 1), :]
        return jnp.concatenate(
            [jnp.broadcast_to(rowv[:, p * LANES:(p + 1) * LANES], (hd, LANES)) for p in range(pairs)], axis=0)

    def step(t, carry):
        pick = jnp.where(tok_of_row == t % slot, 1.0, 0.0).astype(BF16)
        seqs = range(nb)
        vcols = [_dot(vt_scr[t // slot, b].reshape(rows, 4 * slot), pick) for b in seqs]
        for b in seqs:
            old = s_scr[b].reshape(rows, LANES)
            sa = _dot(old * bcast(al_scr, b, t), ones_f)
            new = old * bcast(w_scr, b, t) + sa * bcast(be_scr, b, t) + vcols[b] * bcast(k_scr, b, t)
            s_scr[b] = new.reshape(pairs, hd, LANES)
            yrep = _dot(new * bcast(r_scr, b, t), ones_f)
            y_scr[b] = jnp.where(key_lane == t, yrep, y_scr[b])
        return carry

    y_scr[...] = jnp.zeros_like(y_scr)
    lax.fori_loop(0, tc, step, 0, unroll=4)

    def token_rows(b, p):
        yt = y_scr[b, p * hd:(p + 1) * hd, :].T
        return jnp.concatenate([yt[0:tc], yt[hd:hd + tc]], axis=1)

    y = jnp.concatenate(
        [jnp.concatenate([token_rows(b, p) for p in range(pairs)], axis=1) for b in range(nb)], axis=0)
    seg = seg_ref[...]
    inv = np.float32(1.0 / hd)
    mean = _dot_exact_rhs(y, seg) * inv
    yc = y - mean
    var = _dot_exact_rhs(yc * yc, seg) * inv
    yn = yc * lax.rsqrt(var + GN_EPS) * lnw_ref[...] + lnb_ref[...]
    r = r_scr[...].reshape(nb * tc, hw)
    k2 = k_scr[...].reshape(nb * tc, hw)
    v = v_scr[...].reshape(nb * tc, hw)
    bonus = _dot_exact_rhs(r * k2 * rk_ref[...], seg) * v
    ob_ref[...] = ((yn + bonus) * g_scr[...].reshape(nb * tc, hw)).reshape(shp)

    @pl.when(t_id == pl.num_programs(1) - 1)
    def _():
        sout_ref[...] = s_scr[...]
        shout_ref[...] = prev_scr[...]


def _rwkv(zb, sh0, s0, wts, nb, tc, hd):
    nseq, length, rin = zb.shape
    hw = wts["w0"].shape[1]
    pairs = hw // LANES
    lora = wts["ww"].shape[0]
    body = functools.partial(_rwkv_body, nb=nb, tc=tc, hw=hw, hd=hd, lora=lora)
    names = ("mu", "w0", "ww", "a0", "aw", "gw", "kk", "ka", "rk", "lnw", "lnb", "ones", "seg")
    consts = [wts[n] for n in names]
    tok = pltpu.VMEM((nb, tc, hw), F32)
    return pl.pallas_call(
        body,
        grid=(nseq // nb, length // tc),
        in_specs=[pl.BlockSpec((nb, tc, rin), lambda s, t: (s, t, 0)),
                  pl.BlockSpec((nb, rin), lambda s, t: (s, 0)),
                  pl.BlockSpec((nb, pairs, hd, LANES), lambda s, t: (s, 0, 0, 0))]
                 + [_const_spec(c.shape) for c in consts],
        out_specs=[pl.BlockSpec((nb, tc, hw), lambda s, t: (s, t, 0)),
                   pl.BlockSpec((nb, pairs, hd, LANES), lambda s, t: (s, 0, 0, 0)),
                   pl.BlockSpec((nb, rin), lambda s, t: (s, 0))],
        out_shape=[jax.ShapeDtypeStruct((nseq, length, hw), F32),
                   jax.ShapeDtypeStruct((nseq, pairs, hd, LANES), F32),
                   jax.ShapeDtypeStruct((nseq, rin), F32)],
        scratch_shapes=[pltpu.VMEM((nb, pairs, hd, LANES), F32), pltpu.VMEM((nb, rin), F32)]
                       + [tok] * 7 + [pltpu.VMEM((nb, pairs * hd, LANES), F32),
                                      pltpu.VMEM((-(-tc // RWKV_SLOT), nb, pairs, hd, 4 * RWKV_SLOT), BF16)],
        compiler_params=_params("parallel", "arbitrary"),
        name="rwkv",
    )(zb, sh0, s0, *consts)


def _pack_pairs(s):
    n, heads, hd, _ = s.shape
    return s.reshape(n, heads // 2, 2, hd, hd).transpose(0, 1, 3, 2, 4).reshape(n, heads // 2, hd, 2 * hd)


def _unpack_pairs(s):
    n, pairs, hd, _ = s.shape
    return s.reshape(n, pairs, hd, 2, hd).transpose(0, 1, 3, 2, 4).reshape(n, 2 * pairs, hd, hd)


def _out_proj_body(oa_ref, ob_ref, x_ref, woa_ref, wob_ref, g2_ref, wq_ref, keys_ref,
                   x1_ref, h2t_ref, st_ref, *, nk, dh):
    mix = _dot(oa_ref[...].astype(BF16), woa_ref[...]) + _dot(ob_ref[...].astype(BF16), wob_ref[...])
    x1 = x_ref[...] + mix
    x1_ref[...] = x1
    h2f = _rms(x1, g2_ref[...])
    h2t_ref[...] = h2f.T.astype(BF16)
    q = _dot(h2f.astype(BF16), wq_ref[...]).astype(BF16)
    for hp in range(q.shape[1] // dh):
        st_ref[hp * nk:(hp + 1) * nk, :] = lax.dot_general(
            keys_ref[hp % 2], q[:, hp * dh:(hp + 1) * dh], _NT, preferred_element_type=F32)


def _out_proj(oa, ob, x, woa, wob, g2, wq, keys, tb):
    n, d = x.shape
    nk, dh = keys.shape[1], keys.shape[2]
    nrow = (wq.shape[1] // dh) * nk
    body = functools.partial(_out_proj_body, nk=nk, dh=dh)
    return pl.pallas_call(
        body,
        grid=(n // tb,),
        in_specs=[pl.BlockSpec((tb, oa.shape[1]), lambda i: (i, 0)),
                  pl.BlockSpec((tb, ob.shape[1]), lambda i: (i, 0)),
                  pl.BlockSpec((tb, d), lambda i: (i, 0)),
                  _const_spec(woa.shape), _const_spec(wob.shape), _const_spec(g2.shape),
                  _const_spec(wq.shape), _const_spec(keys.shape)],
        out_specs=[pl.BlockSpec((tb, d), lambda i: (i, 0)), pl.BlockSpec((d, tb), lambda i: (0, i)),
                   pl.BlockSpec((nrow, tb), lambda i: (0, i))],
        out_shape=[jax.ShapeDtypeStruct((n, d), F32), jax.ShapeDtypeStruct((d, n), BF16),
                   jax.ShapeDtypeStruct((nrow, n), F32)],
        compiler_params=_params("parallel"),
        name="out_proj",
    )(oa, ob, x, woa, wob, g2, wq, keys)


def _sort16_network():
    n, pairs, p = 16, [], 1
    while p < n:
        k = p
        while k >= 1:
            for j in range(k % p, n - k, 2 * k):
                for i in range(min(k, n - j - k)):
                    if (i + j) // (2 * p) == (i + j + k) // (2 * p):
                        pairs.append((i + j, i + j + k))
            k //= 2
        p *= 2
    return pairs


def _top16_desc(tiles):
    t = list(tiles)

    def exchange(i, j):
        t[i], t[j] = jnp.maximum(t[i], t[j]), jnp.minimum(t[i], t[j])

    for i, j in _sort16_network():
        exchange(i, j)
    for shift in (4, 2, 1):
        t = [jnp.maximum(t[i], pltpu.roll(t[15 - i], shift, axis=0)) for i in range(16)]
        for d in (8, 4, 2, 1):
            for i in range(16):
                if not i & d:
                    exchange(i, i + d)
    return t


def _route_body(st_ref, t1_ref, e1_ref, e2_ref, top_scr, cand_scr, *, nk, topk):
    neg = -jnp.inf
    nx = topk + 1
    half = topk // 2

    def top_values(h, s):
        tiles = [s[i * SUBLANES:(i + 1) * SUBLANES] for i in range(nk // SUBLANES)]
        srt = _top16_desc(tiles)
        for i in range(topk):
            top_scr[h, i:i + 1, :] = srt[i][0:1]
        top_scr[h, topk:nx, :] = jnp.max(jnp.where(s < srt[topk - 1][0:1], s, neg), axis=0, keepdims=True)

    s1 = st_ref[0:nk, :]
    s2 = st_ref[nk:2 * nk, :]
    top_values(0, s1)
    top_values(1, s2)
    cand_scr[...] = jnp.full(cand_scr.shape, neg, F32)
    sv2 = top_scr[1, 0:nx, :]
    cand_scr[0:nx, :] = top_scr[0, 0:1, :] + sv2
    base = 3 * SUBLANES
    for a in range(1, half):
        cand_scr[base + (a - 1) * half:base + a * half, :] = top_scr[0, a:a + 1, :] + sv2[0:half]
    base += (half - 1) * half
    cand_scr[base:base + nx - half, :] = top_scr[0, half:nx, :] + sv2[0:1]

    cand = cand_scr[...]
    best = _top16_desc([cand[i * SUBLANES:(i + 1) * SUBLANES] for i in range(nk // SUBLANES)])
    z = jnp.zeros_like(best[0][0:1])
    for i in range(topk):
        z = z + jnp.exp(best[i][0:1] - best[0][0:1])
    hi = best[topk - 1][0:1]
    lo = jnp.max(jnp.where(cand < hi, cand, neg), axis=0, keepdims=True)
    thr = jnp.where(lo == neg, hi, 0.5 * (hi + lo))
    t1_ref[0] = thr - s1
    e1_ref[0] = jnp.exp(s1 - top_scr[0, 0:1, :]) * (np.float32(np.sqrt(0.5)) / z)
    e2_ref[0] = jnp.exp(s2 - sv2[0:1])


def _route(st, heads, nk, tb):
    n = st.shape[1]
    topk = PEER_TOPK
    half = topk // 2
    ncand = nk
    assert topk == 16 and nk == 16 * SUBLANES and half == SUBLANES
    assert 3 * SUBLANES + (half - 1) * half + topk + 1 - half <= ncand
    body = functools.partial(_route_body, nk=nk, topk=topk)
    blk = pl.BlockSpec((1, nk, tb), lambda i, h: (h, 0, i))
    return pl.pallas_call(
        body,
        grid=(n // tb, heads),
        in_specs=[pl.BlockSpec((2 * nk, tb), lambda i, h: (h, i))],
        out_specs=[blk, blk, blk],
        out_shape=[jax.ShapeDtypeStruct((heads, nk, n), F32)] * 3,
        scratch_shapes=[pltpu.VMEM((2, 3 * SUBLANES, tb), F32), pltpu.VMEM((ncand, tb), F32)],
        compiler_params=_params("parallel", "parallel"),
        name="route",
    )(st)


PEER_SUB = 2 * LANES
PEER_ROW_SPLIT = 2


def _peer_body(h2t_ref, s2_ref, t1_ref, e1_ref, e2_ref, u_ref, vt_ref, x1_ref, gf_ref, y_ref,
               acc_scr, coef_scr, *, heads, nk, tb, n_eb):
    e = pl.program_id(1)
    eblk = 2 * u_ref.shape[0]
    i1_per_block = eblk // nk
    i1_per_sub = PEER_SUB // nk
    cur = e % 2

    @pl.when(e == 0)
    def _():
        acc_scr[...] = jnp.zeros_like(acc_scr)
        coef_scr[1] = jnp.zeros(coef_scr.shape[1:], BF16)

    i1_0 = pl.multiple_of(jnp.minimum(e, n_eb - 1) * i1_per_block, SUBLANES)
    n_sub = eblk // PEER_SUB
    h2t = h2t_ref[...]

    def matmuls(k):
        ks = slice(k * PEER_SUB, (k + 1) * PEER_SUB)
        u_rows = pltpu.bitcast(u_ref[k * PEER_SUB // 2:(k + 1) * PEER_SUB // 2, :], BF16)
        vt_cols = pltpu.bitcast(vt_ref[:, ks], BF16)
        return _dot(u_rows, h2t), _dot(vt_cols, coef_scr[1 - cur, ks, :])

    act, out = matmuls(0)
    for k in range(n_sub):
        if k + 1 < n_sub:
            act_next, part = matmuls(k + 1)
            out = out + part
        for lg in range(tb // LANES):
            ls = slice(lg * LANES, (lg + 1) * LANES)
            t1b = [t1_ref[h, pl.ds(i1_0, i1_per_block), ls] for h in range(heads)]
            e1b = [e1_ref[h, pl.ds(i1_0, i1_per_block), ls] for h in range(heads)]
            for rs in range(PEER_ROW_SPLIT):
                rsl = slice(rs * nk // PEER_ROW_SPLIT, (rs + 1) * nk // PEER_ROW_SPLIT)
                gates = [None] * i1_per_sub
                for h in range(heads):
                    s2 = s2_ref[h, rsl, ls]
                    e2 = e2_ref[h, rsl, ls]
                    for jj in range(i1_per_sub):
                        j = k * i1_per_sub + jj
                        term = jnp.where(s2 >= t1b[h][j:j + 1], e1b[h][j:j + 1] * e2, 0.0)
                        gates[jj] = term if gates[jj] is None else gates[jj] + term
                for jj in range(i1_per_sub):
                    a = act[jj * nk + rsl.start:jj * nk + rsl.stop, ls]
                    r0 = (k * i1_per_sub + jj) * nk
                    coef_scr[cur, r0 + rsl.start:r0 + rsl.stop, ls] = (gates[jj] * (a + a * lax.erf(a))).astype(BF16)
        if k + 1 < n_sub:
            act = act_next
    acc_scr[...] += out

    @pl.when(e == n_eb)
    def _():
        y_ref[...] = _rms(x1_ref[...] + acc_scr[...].T, gf_ref[...])


def _peer(h2t, st, t1, e1, e2, u, vt, x1, gf, heads, nk, tb):
    n, d = x1.shape
    ne = 2 * u.shape[0]
    eblk = SUBLANES * nk
    n_eb = ne // eblk
    body = functools.partial(_peer_body, heads=heads, nk=nk, tb=tb, n_eb=n_eb)
    tok = pl.BlockSpec((heads, nk, tb), lambda i, e: (0, 0, i))
    return pl.pallas_call(
        body,
        grid=(n // tb, n_eb + 1),
        in_specs=[pl.BlockSpec((d, tb), lambda i, e: (0, i)),
                  pl.BlockSpec((heads, None, nk, tb), lambda i, e: (0, 1, 0, i)),
                  tok, tok, tok,
                  pl.BlockSpec((eblk // 2, d), lambda i, e: (jnp.minimum(e, n_eb - 1), 0)),
                  pl.BlockSpec((d // 2, eblk), lambda i, e: (0, jnp.maximum(e - 1, 0))),
                  pl.BlockSpec((tb, d), lambda i, e: (i, 0)),
                  _const_spec(gf.shape)],
        out_specs=pl.BlockSpec((tb, d), lambda i, e: (i, 0)),
        out_shape=jax.ShapeDtypeStruct((n, d), F32),
        scratch_shapes=[pltpu.VMEM((d, tb), F32), pltpu.VMEM((2, eblk, tb), BF16)],
        compiler_params=_params("parallel", "arbitrary"),
        name="peer",
    )(h2t, st.reshape(heads, 2, nk, n), t1, e1, e2, u, vt, x1, gf)


def _pack_body(x_ref, o_ref, *, transpose, scale):
    x = x_ref[...] * np.float32(scale)
    if transpose:
        x = x.T
    o_ref[...] = pltpu.bitcast(x.astype(BF16), jnp.uint32)


def _pack_table(x, transpose, scale=1.0, rows=512):
    n, c = x.shape
    if transpose:
        out_shape, out_spec = (c // 2, n), pl.BlockSpec((c // 2, rows), lambda i: (0, i))
    else:
        out_shape, out_spec = (n // 2, c), pl.BlockSpec((rows // 2, c), lambda i: (i, 0))
    return pl.pallas_call(
        functools.partial(_pack_body, transpose=transpose, scale=scale),
        grid=(n // rows,),
        in_specs=[pl.BlockSpec((rows, c), lambda i: (i, 0))],
        out_specs=out_spec,
        out_shape=jax.ShapeDtypeStruct(out_shape, jnp.uint32),
        compiler_params=_params("parallel"),
        name="pack_table",
    )(x)


def _block(n, pref):
    b = pref
    while n % b:
        b //= 2
    return b


def kernel(x_prompt, x_sample, state_hgrn, state_rwkv, state_shift, meta_tokens, norm1, w_in, lb_theta, hgrn_norm, mu_shift, w0, w_w2, a0, a_w2, g_w2, k_k, k_a, r_k, ln_w, ln_b, w_out, norm2, peer_wq, peer_keys, peer_u, peer_v, norm_f):
    assert w_in.shape[0] == 1, "single-layer trunk"
    bp, sp, d = x_prompt.shape
    bs, ss, _ = x_sample.shape
    n_meta = meta_tokens.shape[0]
    _, _, ha_heads, dk, dv = state_hgrn.shape
    _, _, hb_heads, hd, _ = state_rwkv.shape
    ha_in = 2 * ha_heads * dk + 2 * ha_heads * dv
    hw = hb_heads * hd
    lw, la = w_w2.shape[1], a_w2.shape[1]
    nk, dh = peer_keys.shape[2], peer_keys.shape[3]
    p_heads = peer_wq.shape[2] // (2 * dh)
    assert sp % HGRN_CHUNK == 0 and ss <= HGRN_CHUNK and n_meta <= HGRN_CHUNK
    assert hb_heads % 2 == 0 and 2 * hd == LANES and lw + la == LANES and nk == LANES

    row = lambda t: t.reshape(1, -1).astype(F32)
    w_a = w_in[0, :, :ha_in].astype(BF16)
    w_b = w_in[0, :, ha_in:].astype(BF16)
    g1 = row(norm1[0])
    lbt = lb_theta.astype(F32)
    gn = row(hgrn_norm[0])

    head_of_lane = np.arange(LANES) // hd
    head_of_col = np.arange(hw) // hd
    rw = {
        "mu": row(mu_shift[0]), "w0": row(w0[0]), "a0": row(a0[0]), "kk": row(k_k[0]), "ka": row(k_a[0]),
        "rk": row(r_k[0]), "lnw": row(ln_w[0]), "lnb": row(ln_b[0]), "gw": g_w2[0].astype(F32),
        "ww": jnp.concatenate([w_w2[0], jnp.zeros((la, hw), F32)], axis=0),
        "aw": jnp.concatenate([jnp.zeros((lw, hw), F32), a_w2[0]], axis=0),
        "ones": jnp.asarray(head_of_lane[:, None] == head_of_lane[None, :], BF16),
        "seg": jnp.asarray(head_of_col[:, None] == head_of_col[None, :], BF16),
    }
    woa = w_out[0, :ha_heads * dv].astype(BF16)
    wob = w_out[0, ha_heads * dv:].astype(BF16)
    g2 = row(norm2[0])
    wq = peer_wq[0].astype(BF16)
    keys = peer_keys[0].astype(BF16)
    u = _pack_table(peer_u[0].astype(F32), False, scale=np.sqrt(0.5))
    vt = _pack_table(peer_v[0].astype(F32), True)
    gf = row(norm_f)

    za_m, zb_m = _in_proj(meta_tokens.astype(F32), g1, w_a, w_b, n_meta)
    _, hs_m = _hgrn(za_m[None], 1, n_meta, lbt, gn, jnp.zeros((1, ha_heads, dk, dv), F32), ha_heads, dk, dv)
    _, rs_m, sh_m = _rwkv(zb_m.reshape(1, n_meta, -1), jnp.zeros((1, zb_m.shape[1]), F32),
                          jnp.zeros((1, hb_heads // 2, hd, LANES), F32), rw, 1, n_meta, hd)

    def group(x, nseq, length, hs0, rs0, sh0, chunk_rows, hgrn_nb, tc):
        xf = x.reshape(nseq * length, d)
        n = xf.shape[0]
        tb = _block(n, 256)
        assert tb % LANES == 0
        za, zb = _in_proj(xf, g1, w_a, w_b, tb)
        oa, hs = _hgrn(za.reshape(nseq, length, -1), hgrn_nb, chunk_rows, lbt, gn, hs0, ha_heads, dk, dv)
        ob, rs, sh = _rwkv(zb.reshape(nseq, length, -1), sh0, rs0, rw, 8, tc, hd)
        x1, h2t, st = _out_proj(oa.reshape(n, -1), ob.reshape(n, hw), xf, woa, wob, g2, wq, keys, tb)
        t1, e1, e2 = _route(st, p_heads, nk, _block(n, 512))
        y = _peer(h2t, st, t1, e1, e2, u, vt, x1, gf, p_heads, nk, tb)
        return y.reshape(nseq, length, d), hs, _unpack_pairs(rs), sh

    y_p, hs_p, rs_p, sh_p = group(
        x_prompt, bp, sp, hs_m, jnp.broadcast_to(rs_m, (bp,) + rs_m.shape[1:]),
        jnp.broadcast_to(sh_m, (bp, sh_m.shape[1])), HGRN_CHUNK, 4, 64)
    y_s, hs_s, rs_s, sh_s = group(
        x_sample, bs, ss, state_hgrn[0].astype(F32), _pack_pairs(state_rwkv[0].astype(F32)),
        state_shift[0].astype(F32), ss, 8, ss)

    return (y_p, y_s, hs_p[None], rs_p[None], sh_p[None], hs_s[None], rs_s[None], sh_s[None])
```

```python
import functools

import numpy as np
import jax
import jax.numpy as jnp
from jax import lax
from jax.experimental import pallas as pl
from jax.experimental.pallas import tpu as pltpu

F32 = jnp.float32
BF16 = jnp.bfloat16

EPS = 1e-6
GN_EPS = 64e-5
HGRN_CHUNK = 64
HGRN_SHORT_CHUNK = 16
PEER_TOPK = 16
RWKV_SLOT = 32
LANES = 128
SUBLANES = 8
VMEM_LIMIT_BYTES = 52 * 1024 * 1024

_NT = (((1,), (1,)), ((), ()))
_TN = (((0,), (0,)), ((), ()))


def _dot(a, b):
    return jnp.dot(a, b, preferred_element_type=F32)


def _split3(a):
    a1 = a.astype(BF16)
    r1 = a - a1.astype(F32)
    a2 = r1.astype(BF16)
    a3 = (r1 - a2.astype(F32)).astype(BF16)
    return a1, a2, a3


def _head_sums(x, ones2):
    out = []
    for c0 in range(0, x.shape[1], LANES):
        xs = x[:, c0:c0 + LANES]
        hi = xs.astype(BF16)
        lo = (xs - hi.astype(F32)).astype(BF16)
        out.append(_dot(jnp.concatenate([hi, lo], axis=1), ones2))
    return jnp.concatenate(out, axis=1)


def _dot_exact_lhs(a_exact3, b):
    return _dot(a_exact3, jnp.concatenate(_split3(b), axis=0))


def _dot_hi(a, b):
    ah = a.astype(BF16)
    al = (a - ah.astype(F32)).astype(BF16)
    bh = b.astype(BF16)
    bl = (b - bh.astype(F32)).astype(BF16)
    return _dot(ah, bh) + _dot(ah, bl) + _dot(al, bh)


def _sigmoid(x):
    return 1.0 / (1.0 + jnp.exp(-x))


def _rms(x, g):
    return x * lax.rsqrt(jnp.mean(x * x, axis=-1, keepdims=True) + EPS) * g


def _params(*sem):
    return pltpu.CompilerParams(dimension_semantics=sem, vmem_limit_bytes=VMEM_LIMIT_BYTES)


def _const_spec(shape):
    nd = len(shape)
    return pl.BlockSpec(shape, lambda *_: (0,) * nd)


def _in_proj_body(x_ref, g_ref, wa_ref, wb_ref, za_ref, zb_ref):
    hb = _rms(x_ref[...], g_ref[...]).astype(BF16)
    za_ref[...] = _dot(hb, wa_ref[...])
    zb_ref[...] = _dot(hb, wb_ref[...])


def _in_proj(x, g, wa, wb, tb):
    n, d = x.shape
    na, nb = wa.shape[1], wb.shape[1]
    return pl.pallas_call(
        _in_proj_body,
        grid=(n // tb,),
        in_specs=[pl.BlockSpec((tb, d), lambda i: (i, 0)), _const_spec((1, d)),
                  _const_spec(wa.shape), _const_spec(wb.shape)],
        out_specs=[pl.BlockSpec((tb, na), lambda i: (i, 0)), pl.BlockSpec((tb, nb), lambda i: (i, 0))],
        out_shape=[jax.ShapeDtypeStruct((n, na), F32), jax.ShapeDtypeStruct((n, nb), F32)],
        compiler_params=_params("parallel"),
        name="in_proj",
    )(x, g, wa, wb)


def _hgrn_tables(c):
    nl = int(np.log2(c)) + 1
    mats = np.zeros((nl, c, c), np.float32)
    masks = np.zeros((nl, c, c), np.float32)
    idx = np.arange(c)
    mats[0] = (idx[None, :] <= idx[:, None])
    masks[0] = np.eye(c)
    for l in range(1, nl):
        blk = 1 << l
        half = blk >> 1
        for t in range(c):
            start = (t // blk) * blk
            ref = start + half - 1
            if t - start >= half:
                mats[l, t, ref + 1:t + 1] = 1.0
                masks[l, t, start:start + half] = 1.0
            else:
                mats[l, t, t + 1:ref + 1] = 1.0
    return mats.reshape(nl * c, c), masks


def _hgrn_body(za_ref, lbt_ref, gn_ref, s0_ref, mats_ref, masks_ref, oa_ref, sout_ref, st_scr,
               *, nb, rows, chunk, heads, dk, dv):
    c_id = pl.program_id(1)
    qk = heads * dk
    wd = heads * dv
    nl = masks_ref.shape[0]

    @pl.when(c_id == 0)
    def _():
        for s in range(nb):
            for h in range(heads):
                st_scr[s, h] = s0_ref[min(s, s0_ref.shape[0] - 1), h].T

    th = lbt_ref[...]
    ex = jnp.exp(th - jnp.max(th, axis=0, keepdims=True))
    lb = ex[0:1] / jnp.sum(ex, axis=0, keepdims=True)
    gn = gn_ref[...]

    for s in range(nb):
        za = za_ref[s]
        zq = za[:, :qk]
        zf = za[:, qk:2 * qk]
        zi = za[:, 2 * qk:2 * qk + wd]
        zo = za[:, 2 * qk + wd:]
        logf = jnp.log(lb + (1.0 - lb) * _sigmoid(zf))
        kin = (1.0 - lb) * _sigmoid(-zf)
        qa = zq * _sigmoid(zq)
        if rows < chunk:
            pad = jnp.zeros((chunk - rows, qk), F32)
            logf = jnp.concatenate([logf, pad], axis=0)
            kin = jnp.concatenate([kin, pad], axis=0)
            qa = jnp.concatenate([qa, pad], axis=0)
            zi = jnp.concatenate([zi, jnp.zeros((chunk - rows, wd), F32)], axis=0)

        dec = _dot_exact_lhs(mats_ref[...], logf)
        for h in range(heads):
            ks = slice(h * dk, (h + 1) * dk)
            vs = slice(h * dv, (h + 1) * dv)
            q = qa[:, ks]
            k = kin[:, ks]
            v = zi[:, vs].astype(BF16)
            cum = dec[0:chunk, ks]
            sc = lax.dot_general(q.astype(BF16), k.astype(BF16), _NT, preferred_element_type=F32)
            scores = jnp.where(masks_ref[0] != 0.0, sc, 0.0)
            for l in range(1, nl):
                e = jnp.exp(dec[l * chunk:(l + 1) * chunk, ks])
                sc = lax.dot_general((q * e).astype(BF16), (k * e).astype(BF16), _NT, preferred_element_type=F32)
                scores = scores + jnp.where(masks_ref[l] != 0.0, sc, 0.0)
            st = st_scr[s, h]
            o = _dot(scores.astype(BF16), v) + lax.dot_general(
                (q * jnp.exp(cum)).astype(BF16), st.astype(BF16), _NT, preferred_element_type=F32)
            last = cum[chunk - 1:chunk, :]
            kd = (k * jnp.exp(last - cum)).astype(BF16)
            st_scr[s, h] = jnp.exp(last) * st + lax.dot_general(v, kd, _TN, preferred_element_type=F32)

            o = o[:rows]
            o = o * lax.rsqrt(jnp.mean(o * o, axis=-1, keepdims=True) + EPS) * gn[:, vs]
            zoh = zo[:, vs]
            oa_ref[s, :, vs] = o * (zoh * _sigmoid(zoh))

    @pl.when(c_id == pl.num_programs(1) - 1)
    def _():
        for s in range(nb):
            for h in range(heads):
                sout_ref[s, h] = st_scr[s, h].T


def _hgrn(za, nb, rows, lb_theta, gn, s0, heads, dk, dv):
    nseq, length, cols = za.shape
    chunk = HGRN_CHUNK if rows > HGRN_SHORT_CHUNK else HGRN_SHORT_CHUNK
    mats, masks = _hgrn_tables(chunk)
    mats = jnp.asarray(np.tile(mats, (1, 3)), BF16)
    masks = jnp.asarray(masks, F32)
    wd = heads * dv
    shared = s0.shape[0] == 1
    s0_spec = (pl.BlockSpec((1, heads, dk, dv), lambda s, c: (0, 0, 0, 0)) if shared
               else pl.BlockSpec((nb, heads, dk, dv), lambda s, c: (s, 0, 0, 0)))
    body = functools.partial(_hgrn_body, nb=nb, rows=rows, chunk=chunk, heads=heads, dk=dk, dv=dv)
    return pl.pallas_call(
        body,
        grid=(nseq // nb, length // rows),
        in_specs=[pl.BlockSpec((nb, rows, cols), lambda s, c: (s, c, 0)),
                  _const_spec(lb_theta.shape), _const_spec(gn.shape), s0_spec,
                  _const_spec(mats.shape), _const_spec(masks.shape)],
        out_specs=[pl.BlockSpec((nb, rows, wd), lambda s, c: (s, c, 0)),
                   pl.BlockSpec((nb, heads, dk, dv), lambda s, c: (s, 0, 0, 0))],
        out_shape=[jax.ShapeDtypeStruct((nseq, length, wd), F32),
                   jax.ShapeDtypeStruct((nseq, heads, dk, dv), F32)],
        scratch_shapes=[pltpu.VMEM((nb, heads, dv, dk), F32)],
        compiler_params=_params("parallel", "arbitrary"),
        name="hgrn",
    )(za, lb_theta, gn, s0, mats, masks)


def _rwkv_body(zb_ref, sh0_ref, s0_ref, mu_ref, w0_ref, ww_ref, a0_ref, aw_ref, gw_ref, kk_ref, ka_ref,
               rk_ref, lnw_ref, lnb_ref, ones_ref, seg_ref,
               ob_ref, sout_ref, shout_ref,
               s_scr, prev_scr, al_scr, w_scr, be_scr, k_scr, r_scr, v_scr, g_scr, y_scr, vt_scr,
               *, nb, tc, hw, hd, lora):
    t_id = pl.program_id(1)
    pairs = hw // LANES
    rin = zb_ref.shape[-1]

    @pl.when(t_id == 0)
    def _():
        s_scr[...] = s0_ref[...]
        prev_scr[...] = sh0_ref[...]

    row = lax.broadcasted_iota(jnp.int32, (tc, rin), 0)
    mu = mu_ref[...]
    zms = []
    for b in range(nb):
        zb = zb_ref[b]
        prev = jnp.where(row == 0, prev_scr[b:b + 1, :], pltpu.roll(zb, 1, axis=0))
        prev_scr[b:b + 1, :] = zb[tc - 1:tc, :]
        zms.append(zb + (prev - zb) * mu)
    zm = jnp.concatenate(zms, axis=0)
    r = zm[:, :hw]
    k = zm[:, hw:2 * hw]
    v = zm[:, 2 * hw:3 * hw]
    dwa = zm[:, 3 * hw:3 * hw + lora]
    dg = zm[:, 3 * hw + lora:]
    w_raw = w0_ref[...] + _dot_hi(jnp.tanh(dwa), ww_ref[...])
    decay = jnp.exp(-np.float32(np.exp(-0.5)) * _sigmoid(w_raw))
    a = _sigmoid(a0_ref[...] + _dot_hi(dwa, aw_ref[...]))
    g = _dot_hi(_sigmoid(dg), gw_ref[...])
    kk = k * kk_ref[...]
    ss = _head_sums(kk * kk, seg_ref[...])
    kk = kk / jnp.maximum(jnp.sqrt(ss), 1e-12)
    k2 = k * (1.0 + (a - 1.0) * ka_ref[...])
    shp = (nb, tc, hw)
    al_scr[...] = (-kk).reshape(shp)
    w_scr[...] = decay.reshape(shp)
    be_scr[...] = (kk * a).reshape(shp)
    k_scr[...] = k2.reshape(shp)
    r_scr[...] = r.reshape(shp)
    v_scr[...] = v.reshape(shp)
    g_scr[...] = g.reshape(shp)

    slot = RWKV_SLOT
    li2 = lax.broadcasted_iota(jnp.int32, (slot, LANES), 1)
    vi = lax.broadcasted_iota(jnp.int32, (hd, LANES), 0)
    sel = jnp.where(vi == lax.broadcasted_iota(jnp.int32, (hd, LANES), 1) % hd, 1.0, 0.0).astype(BF16)
    for b in range(nb):
        for p in range(pairs):
            for g0 in range(0, tc, slot):
                nt = min(slot, tc - g0)
                vb = v[b * tc + g0:b * tc + g0 + nt, p * LANES:(p + 1) * LANES]
                if nt < slot:
                    vb = jnp.concatenate([vb, jnp.zeros((slot - nt, LANES), F32)], axis=0)
                vh = vb.astype(BF16)
                vl = (vb - vh.astype(F32)).astype(BF16)
                zero = jnp.zeros_like(vh)
                stack = jnp.concatenate([jnp.where(li2 < hd, vh, zero), jnp.where(li2 >= hd, vh, zero),
                                         jnp.where(li2 < hd, vl, zero), jnp.where(li2 >= hd, vl, zero)], axis=0)
                vt_scr[g0 // slot, b, p] = lax.dot_general(
                    sel, stack, _NT, preferred_element_type=F32).astype(BF16)

    rows = pairs * hd
    li = lax.broadcasted_iota(jnp.int32, (rows, LANES), 1)
    key_lane = li % hd
    ones1 = ones_ref[...]
    ones_f = ones1.astype(F32)
    qi = lax.broadcasted_iota(jnp.int32, (4 * slot, LANES), 0)
    ql = lax.broadcasted_iota(jnp.int32, (4 * slot, LANES), 1)
    tok_of_row = jnp.where((qi // slot) % 2 == ql // hd, qi % slot, -1)

    def bcast(ref, b, t):
        rowv = ref[b, pl.ds(t, 1), :]
        return jnp.concatenate(
            [jnp.broadcast_to(rowv[:, p * LANES:(p + 1) * LANES], (hd, LANES)) for p in range(pairs)], axis=0)

    def step(t, carry):
        pick = jnp.where(tok_of_row == t % slot, 1.0, 0.0).astype(BF16)
        seqs = range(nb)
        vcols = [_dot(vt_scr[t // slot, b].reshape(rows, 4 * slot), pick) for b in seqs]
        for b in seqs:
            old = s_scr[b].reshape(rows, LANES)
            sa = _dot(old * bcast(al_scr, b, t), ones_f)
            new = old * bcast(w_scr, b, t) + sa * bcast(be_scr, b, t) + vcols[b] * bcast(k_scr, b, t)
            s_scr[b] = new.reshape(pairs, hd, LANES)
            yrep = _dot(new * bcast(r_scr, b, t), ones_f)
            y_scr[b] = jnp.where(key_lane == t, yrep, y_scr[b])
        return carry

    y_scr[...] = jnp.zeros_like(y_scr)
    lax.fori_loop(0, tc, step, 0, unroll=4)

    def token_rows(b, p):
        yt = y_scr[b, p * hd:(p + 1) * hd, :].T
        return jnp.concatenate([yt[0:tc], yt[hd:hd + tc]], axis=1)

    y = jnp.concatenate(
        [jnp.concatenate([token_rows(b, p) for p in range(pairs)], axis=1) for b in range(nb)], axis=0)
    seg = seg_ref[...]
    inv = np.float32(1.0 / hd)
    mean = _head_sums(y, seg) * inv
    yc = y - mean
    var = _head_sums(yc * yc, seg) * inv
    yn = yc * lax.rsqrt(var + GN_EPS) * lnw_ref[...] + lnb_ref[...]
    r = r_scr[...].reshape(nb * tc, hw)
    k2 = k_scr[...].reshape(nb * tc, hw)
    v = v_scr[...].reshape(nb * tc, hw)
    bonus = _head_sums(r * k2 * rk_ref[...], seg) * v
    ob_ref[...] = ((yn + bonus) * g_scr[...].reshape(nb * tc, hw)).reshape(shp)

    @pl.when(t_id == pl.num_programs(1) - 1)
    def _():
        sout_ref[...] = s_scr[...]
        shout_ref[...] = prev_scr[...]


def _rwkv(zb, sh0, s0, wts, nb, tc, hd):
    nseq, length, rin = zb.shape
    hw = wts["w0"].shape[1]
    pairs = hw // LANES
    lora = wts["ww"].shape[0]
    body = functools.partial(_rwkv_body, nb=nb, tc=tc, hw=hw, hd=hd, lora=lora)
    names = ("mu", "w0", "ww", "a0", "aw", "gw", "kk", "ka", "rk", "lnw", "lnb", "ones", "seg")
    consts = [wts[n] for n in names]
    tok = pltpu.VMEM((nb, tc, hw), F32)
    return pl.pallas_call(
        body,
        grid=(nseq // nb, length // tc),
        in_specs=[pl.BlockSpec((nb, tc, rin), lambda s, t: (s, t, 0)),
                  pl.BlockSpec((nb, rin), lambda s, t: (s, 0)),
                  pl.BlockSpec((nb, pairs, hd, LANES), lambda s, t: (s, 0, 0, 0))]
                 + [_const_spec(c.shape) for c in consts],
        out_specs=[pl.BlockSpec((nb, tc, hw), lambda s, t: (s, t, 0)),
                   pl.BlockSpec((nb, pairs, hd, LANES), lambda s, t: (s, 0, 0, 0)),
                   pl.BlockSpec((nb, rin), lambda s, t: (s, 0))],
        out_shape=[jax.ShapeDtypeStruct((nseq, length, hw), F32),
                   jax.ShapeDtypeStruct((nseq, pairs, hd, LANES), F32),
                   jax.ShapeDtypeStruct((nseq, rin), F32)],
        scratch_shapes=[pltpu.VMEM((nb, pairs, hd, LANES), F32), pltpu.VMEM((nb, rin), F32)]
                       + [tok] * 7 + [pltpu.VMEM((nb, pairs * hd, LANES), F32),
                                      pltpu.VMEM((-(-tc // RWKV_SLOT), nb, pairs, hd, 4 * RWKV_SLOT), BF16)],
        compiler_params=_params("parallel", "arbitrary"),
        name="rwkv",
    )(zb, sh0, s0, *consts)


def _pack_pairs(s):
    n, heads, hd, _ = s.shape
    return s.reshape(n, heads // 2, 2, hd, hd).transpose(0, 1, 3, 2, 4).reshape(n, heads // 2, hd, 2 * hd)


def _unpack_pairs(s):
    n, pairs, hd, _ = s.shape
    return s.reshape(n, pairs, hd, 2, hd).transpose(0, 1, 3, 2, 4).reshape(n, 2 * pairs, hd, hd)


def _out_proj_body(oa_ref, ob_ref, x_ref, woa_ref, wob_ref, g2_ref, wq_ref, keys_ref,
                   x1_ref, h2t_ref, st_ref, *, nk, dh):
    mix = _dot(oa_ref[...].astype(BF16), woa_ref[...]) + _dot(ob_ref[...].astype(BF16), wob_ref[...])
    x1 = x_ref[...] + mix
    x1_ref[...] = x1
    h2f = _rms(x1, g2_ref[...])
    h2t_ref[...] = h2f.T.astype(BF16)
    q = _dot(h2f.astype(BF16), wq_ref[...]).astype(BF16)
    for hp in range(q.shape[1] // dh):
        st_ref[hp * nk:(hp + 1) * nk, :] = lax.dot_general(
            keys_ref[hp % 2], q[:, hp * dh:(hp + 1) * dh], _NT, preferred_element_type=F32)


def _out_proj(oa, ob, x, woa, wob, g2, wq, keys, tb):
    n, d = x.shape
    nk, dh = keys.shape[1], keys.shape[2]
    nrow = (wq.shape[1] // dh) * nk
    body = functools.partial(_out_proj_body, nk=nk, dh=dh)
    return pl.pallas_call(
        body,
        grid=(n // tb,),
        in_specs=[pl.BlockSpec((tb, oa.shape[1]), lambda i: (i, 0)),
                  pl.BlockSpec((tb, ob.shape[1]), lambda i: (i, 0)),
                  pl.BlockSpec((tb, d), lambda i: (i, 0)),
                  _const_spec(woa.shape), _const_spec(wob.shape), _const_spec(g2.shape),
                  _const_spec(wq.shape), _const_spec(keys.shape)],
        out_specs=[pl.BlockSpec((tb, d), lambda i: (i, 0)), pl.BlockSpec((d, tb), lambda i: (0, i)),
                   pl.BlockSpec((nrow, tb), lambda i: (0, i))],
        out_shape=[jax.ShapeDtypeStruct((n, d), F32), jax.ShapeDtypeStruct((d, n), BF16),
                   jax.ShapeDtypeStruct((nrow, n), F32)],
        compiler_params=_params("parallel"),
        name="out_proj",
    )(oa, ob, x, woa, wob, g2, wq, keys)


def _sort16_network():
    n, pairs, p = 16, [], 1
    while p < n:
        k = p
        while k >= 1:
            for j in range(k % p, n - k, 2 * k):
                for i in range(min(k, n - j - k)):
                    if (i + j) // (2 * p) == (i + j + k) // (2 * p):
                        pairs.append((i + j, i + j + k))
            k //= 2
        p *= 2
    return pairs


def _top16_desc(tiles):
    t = list(tiles)

    def exchange(i, j):
        t[i], t[j] = jnp.maximum(t[i], t[j]), jnp.minimum(t[i], t[j])

    for i, j in _sort16_network():
        exchange(i, j)
    for shift in (4, 2, 1):
        t = [jnp.maximum(t[i], pltpu.roll(t[15 - i], shift, axis=0)) for i in range(16)]
        for d in (8, 4, 2, 1):
            for i in range(16):
                if not i & d:
                    exchange(i, i + d)
    return t


def _route_body(st_ref, t1_ref, e1_ref, e2_ref, top_scr, cand_scr, *, nk, topk):
    neg = -jnp.inf
    nx = topk + 1
    half = topk // 2

    def top_values(h, s):
        tiles = [s[i * SUBLANES:(i + 1) * SUBLANES] for i in range(nk // SUBLANES)]
        srt = _top16_desc(tiles)
        for i in range(topk):
            top_scr[h, i:i + 1, :] = srt[i][0:1]
        top_scr[h, topk:nx, :] = jnp.max(jnp.where(s < srt[topk - 1][0:1], s, neg), axis=0, keepdims=True)

    s1 = st_ref[0:nk, :]
    s2 = st_ref[nk:2 * nk, :]
    top_values(0, s1)
    top_values(1, s2)
    cand_scr[...] = jnp.full(cand_scr.shape, neg, F32)
    sv2 = top_scr[1, 0:nx, :]
    cand_scr[0:nx, :] = top_scr[0, 0:1, :] + sv2
    base = 3 * SUBLANES
    for a in range(1, half):
        cand_scr[base + (a - 1) * half:base + a * half, :] = top_scr[0, a:a + 1, :] + sv2[0:half]
    base += (half - 1) * half
    cand_scr[base:base + nx - half, :] = top_scr[0, half:nx, :] + sv2[0:1]

    cand = cand_scr[...]
    best = _top16_desc([cand[i * SUBLANES:(i + 1) * SUBLANES] for i in range(nk // SUBLANES)])
    z = jnp.zeros_like(best[0][0:1])
    for i in range(topk):
        z = z + jnp.exp(best[i][0:1] - best[0][0:1])
    hi = best[topk - 1][0:1]
    lo = jnp.max(jnp.where(cand < hi, cand, neg), axis=0, keepdims=True)
    thr = jnp.where(lo == neg, hi, 0.5 * (hi + lo))
    t1_ref[0] = thr - s1
    e1_ref[0] = jnp.exp(s1 - top_scr[0, 0:1, :]) * (np.float32(np.sqrt(0.5)) / z)
    e2_ref[0] = jnp.exp(s2 - sv2[0:1])


def _route(st, heads, nk, tb):
    n = st.shape[1]
    topk = PEER_TOPK
    half = topk // 2
    ncand = nk
    assert topk == 16 and nk == 16 * SUBLANES and half == SUBLANES
    assert 3 * SUBLANES + (half - 1) * half + topk + 1 - half <= ncand
    body = functools.partial(_route_body, nk=nk, topk=topk)
    blk = pl.BlockSpec((1, nk, tb), lambda i, h: (h, 0, i))
    return pl.pallas_call(
        body,
        grid=(n // tb, heads),
        in_specs=[pl.BlockSpec((2 * nk, tb), lambda i, h: (h, i))],
        out_specs=[blk, blk, blk],
        out_shape=[jax.ShapeDtypeStruct((heads, nk, n), F32)] * 3,
        scratch_shapes=[pltpu.VMEM((2, 3 * SUBLANES, tb), F32), pltpu.VMEM((ncand, tb), F32)],
        compiler_params=_params("parallel", "parallel"),
        name="route",
    )(st)


PEER_SUB = 2 * LANES
PEER_ROW_SPLIT = 2


def _peer_body(h2t_ref, s2_ref, t1_ref, e1_ref, e2_ref, u_ref, vt_ref, x1_ref, gf_ref, y_ref,
               acc_scr, coef_scr, *, heads, nk, tb, n_eb):
    e = pl.program_id(1)
    eblk = 2 * u_ref.shape[0]
    i1_per_block = eblk // nk
    i1_per_sub = PEER_SUB // nk
    cur = e % 2

    @pl.when(e == 0)
    def _():
        acc_scr[...] = jnp.zeros_like(acc_scr)
        coef_scr[1] = jnp.zeros(coef_scr.shape[1:], BF16)

    i1_0 = pl.multiple_of(jnp.minimum(e, n_eb - 1) * i1_per_block, SUBLANES)
    n_sub = eblk // PEER_SUB
    h2t = h2t_ref[...]

    def matmuls(k):
        ks = slice(k * PEER_SUB, (k + 1) * PEER_SUB)
        u_rows = pltpu.bitcast(u_ref[k * PEER_SUB // 2:(k + 1) * PEER_SUB // 2, :], BF16)
        vt_cols = pltpu.bitcast(vt_ref[:, ks], BF16)
        return _dot(u_rows, h2t), _dot(vt_cols, coef_scr[1 - cur, ks, :])

    act, out = matmuls(0)
    for k in range(n_sub):
        if k + 1 < n_sub:
            act_next, part = matmuls(k + 1)
            out = out + part
        for lg in range(tb // LANES):
            ls = slice(lg * LANES, (lg + 1) * LANES)
            t1b = [t1_ref[h, pl.ds(i1_0, i1_per_block), ls] for h in range(heads)]
            e1b = [e1_ref[h, pl.ds(i1_0, i1_per_block), ls] for h in range(heads)]
            for rs in range(PEER_ROW_SPLIT):
                rsl = slice(rs * nk // PEER_ROW_SPLIT, (rs + 1) * nk // PEER_ROW_SPLIT)
                gates = [None] * i1_per_sub
                for h in range(heads):
                    s2 = s2_ref[h, rsl, ls]
                    e2 = e2_ref[h, rsl, ls]
                    for jj in range(i1_per_sub):
                        j = k * i1_per_sub + jj
                        term = jnp.where(s2 >= t1b[h][j:j + 1], e1b[h][j:j + 1] * e2, 0.0)
                        gates[jj] = term if gates[jj] is None else gates[jj] + term
                for jj in range(i1_per_sub):
                    a = act[jj * nk + rsl.start:jj * nk + rsl.stop, ls]
                    r0 = (k * i1_per_sub + jj) * nk
                    coef_scr[cur, r0 + rsl.start:r0 + rsl.stop, ls] = (gates[jj] * (a + a * lax.erf(a))).astype(BF16)
        if k + 1 < n_sub:
            act = act_next
    acc_scr[...] += out

    @pl.when(e == n_eb)
    def _():
        y_ref[...] = _rms(x1_ref[...] + acc_scr[...].T, gf_ref[...])


def _peer(h2t, st, t1, e1, e2, u, vt, x1, gf, heads, nk, tb):
    n, d = x1.shape
    ne = 2 * u.shape[0]
    eblk = SUBLANES * nk
    n_eb = ne // eblk
    body = functools.partial(_peer_body, heads=heads, nk=nk, tb=tb, n_eb=n_eb)
    tok = pl.BlockSpec((heads, nk, tb), lambda i, e: (0, 0, i))
    return pl.pallas_call(
        body,
        grid=(n // tb, n_eb + 1),
        in_specs=[pl.BlockSpec((d, tb), lambda i, e: (0, i)),
                  pl.BlockSpec((heads, None, nk, tb), lambda i, e: (0, 1, 0, i)),
                  tok, tok, tok,
                  pl.BlockSpec((eblk // 2, d), lambda i, e: (jnp.minimum(e, n_eb - 1), 0)),
                  pl.BlockSpec((d // 2, eblk), lambda i, e: (0, jnp.maximum(e - 1, 0))),
                  pl.BlockSpec((tb, d), lambda i, e: (i, 0)),
                  _const_spec(gf.shape)],
        out_specs=pl.BlockSpec((tb, d), lambda i, e: (i, 0)),
        out_shape=jax.ShapeDtypeStruct((n, d), F32),
        scratch_shapes=[pltpu.VMEM((d, tb), F32), pltpu.VMEM((2, eblk, tb), BF16)],
        compiler_params=_params("parallel", "arbitrary"),
        name="peer",
    )(h2t, st.reshape(heads, 2, nk, n), t1, e1, e2, u, vt, x1, gf)


def _pack_body(x_ref, o_ref, *, transpose, scale):
    x = x_ref[...] * np.float32(scale)
    if transpose:
        x = x.T
    o_ref[...] = pltpu.bitcast(x.astype(BF16), jnp.uint32)


def _pack_table(x, transpose, scale=1.0, rows=512):
    n, c = x.shape
    if transpose:
        out_shape, out_spec = (c // 2, n), pl.BlockSpec((c // 2, rows), lambda i: (0, i))
    else:
        out_shape, out_spec = (n // 2, c), pl.BlockSpec((rows // 2, c), lambda i: (i, 0))
    return pl.pallas_call(
        functools.partial(_pack_body, transpose=transpose, scale=scale),
        grid=(n // rows,),
        in_specs=[pl.BlockSpec((rows, c), lambda i: (i, 0))],
        out_specs=out_spec,
        out_shape=jax.ShapeDtypeStruct(out_shape, jnp.uint32),
        compiler_params=_params("parallel"),
        name="pack_table",
    )(x)


def _block(n, pref):
    b = pref
    while n % b:
        b //= 2
    return b


def kernel(x_prompt, x_sample, state_hgrn, state_rwkv, state_shift, meta_tokens, norm1, w_in, lb_theta, hgrn_norm, mu_shift, w0, w_w2, a0, a_w2, g_w2, k_k, k_a, r_k, ln_w, ln_b, w_out, norm2, peer_wq, peer_keys, peer_u, peer_v, norm_f):
    assert w_in.shape[0] == 1, "single-layer trunk"
    bp, sp, d = x_prompt.shape
    bs, ss, _ = x_sample.shape
    n_meta = meta_tokens.shape[0]
    _, _, ha_heads, dk, dv = state_hgrn.shape
    _, _, hb_heads, hd, _ = state_rwkv.shape
    ha_in = 2 * ha_heads * dk + 2 * ha_heads * dv
    hw = hb_heads * hd
    lw, la = w_w2.shape[1], a_w2.shape[1]
    nk, dh = peer_keys.shape[2], peer_keys.shape[3]
    p_heads = peer_wq.shape[2] // (2 * dh)
    assert sp % HGRN_CHUNK == 0 and ss <= HGRN_CHUNK and n_meta <= HGRN_CHUNK
    assert hb_heads % 2 == 0 and 2 * hd == LANES and lw + la == LANES and nk == LANES

    row = lambda t: t.reshape(1, -1).astype(F32)
    w_a = w_in[0, :, :ha_in].astype(BF16)
    w_b = w_in[0, :, ha_in:].astype(BF16)
    g1 = row(norm1[0])
    lbt = lb_theta.astype(F32)
    gn = row(hgrn_norm[0])

    head_of_lane = np.arange(LANES) // hd
    rw = {
        "mu": row(mu_shift[0]), "w0": row(w0[0]), "a0": row(a0[0]), "kk": row(k_k[0]), "ka": row(k_a[0]),
        "rk": row(r_k[0]), "lnw": row(ln_w[0]), "lnb": row(ln_b[0]), "gw": g_w2[0].astype(F32),
        "ww": jnp.concatenate([w_w2[0], jnp.zeros((la, hw), F32)], axis=0),
        "aw": jnp.concatenate([jnp.zeros((lw, hw), F32), a_w2[0]], axis=0),
        "ones": jnp.asarray(head_of_lane[:, None] == head_of_lane[None, :], BF16),
        "seg": jnp.asarray(np.tile(head_of_lane[:, None] == head_of_lane[None, :], (2, 1)), BF16),
    }
    woa = w_out[0, :ha_heads * dv].astype(BF16)
    wob = w_out[0, ha_heads * dv:].astype(BF16)
    g2 = row(norm2[0])
    wq = peer_wq[0].astype(BF16)
    keys = peer_keys[0].astype(BF16)
    u = _pack_table(peer_u[0].astype(F32), False, scale=np.sqrt(0.5))
    vt = _pack_table(peer_v[0].astype(F32), True)
    gf = row(norm_f)

    za_m, zb_m = _in_proj(meta_tokens.astype(F32), g1, w_a, w_b, n_meta)
    _, hs_m = _hgrn(za_m[None], 1, n_meta, lbt, gn, jnp.zeros((1, ha_heads, dk, dv), F32), ha_heads, dk, dv)
    _, rs_m, sh_m = _rwkv(zb_m.reshape(1, n_meta, -1), jnp.zeros((1, zb_m.shape[1]), F32),
                          jnp.zeros((1, hb_heads // 2, hd, LANES), F32), rw, 1, n_meta, hd)

    def group(x, nseq, length, hs0, rs0, sh0, chunk_rows, hgrn_nb, tc):
        xf = x.reshape(nseq * length, d)
        n = xf.shape[0]
        tb = _block(n, 256)
        assert tb % LANES == 0
        za, zb = _in_proj(xf, g1, w_a, w_b, tb)
        oa, hs = _hgrn(za.reshape(nseq, length, -1), hgrn_nb, chunk_rows, lbt, gn, hs0, ha_heads, dk, dv)
        ob, rs, sh = _rwkv(zb.reshape(nseq, length, -1), sh0, rs0, rw, 8, tc, hd)
        x1, h2t, st = _out_proj(oa.reshape(n, -1), ob.reshape(n, hw), xf, woa, wob, g2, wq, keys, tb)
        t1, e1, e2 = _route(st, p_heads, nk, _block(n, 512))
        y = _peer(h2t, st, t1, e1, e2, u, vt, x1, gf, p_heads, nk, tb)
        return y.reshape(nseq, length, d), hs, _unpack_pairs(rs), sh

    y_p, hs_p, rs_p, sh_p = group(
        x_prompt, bp, sp, hs_m, jnp.broadcast_to(rs_m, (bp,) + rs_m.shape[1:]),
        jnp.broadcast_to(sh_m, (bp, sh_m.shape[1])), HGRN_CHUNK, 4, 64)
    y_s, hs_s, rs_s, sh_s = group(
        x_sample, bs, ss, state_hgrn[0].astype(F32), _pack_pairs(state_rwkv[0].astype(F32)),
        state_shift[0].astype(F32), ss, 8, ss)

    return (y_p, y_s, hs_p[None], rs_p[None], sh_p[None], hs_s[None], rs_s[None], sh_s[None])
```

```python
import functools

import numpy as np
import jax
import jax.numpy as jnp
from jax import lax
from jax.experimental import pallas as pl
from jax.experimental.pallas import tpu as pltpu

F32 = jnp.float32
BF16 = jnp.bfloat16

EPS = 1e-6
GN_EPS = 64e-5
HGRN_CHUNK = 64
HGRN_SHORT_CHUNK = 16
PEER_TOPK = 16
RWKV_SLOT = 32
LANES = 128
SUBLANES = 8
VMEM_LIMIT_BYTES = 52 * 1024 * 1024

_NT = (((1,), (1,)), ((), ()))
_TN = (((0,), (0,)), ((), ()))


def _dot(a, b):
    return jnp.dot(a, b, preferred_element_type=F32)


def _split3(a):
    a1 = a.astype(BF16)
    r1 = a - a1.astype(F32)
    a2 = r1.astype(BF16)
    a3 = (r1 - a2.astype(F32)).astype(BF16)
    return a1, a2, a3


def _head_sums(x, ones2):
    out = []
    for c0 in range(0, x.shape[1], LANES):
        xs = x[:, c0:c0 + LANES]
        hi = xs.astype(BF16)
        lo = (xs - hi.astype(F32)).astype(BF16)
        out.append(_dot(jnp.concatenate([hi, lo], axis=1), ones2))
    return jnp.concatenate(out, axis=1)


def _dot_exact_lhs(a_exact3, b):
    return _dot(a_exact3, jnp.concatenate(_split3(b), axis=0))


def _dot_hi(a, b):
    ah = a.astype(BF16)
    al = (a - ah.astype(F32)).astype(BF16)
    bh = b.astype(BF16)
    bl = (b - bh.astype(F32)).astype(BF16)
    return _dot(ah, bh) + _dot(ah, bl) + _dot(al, bh)


def _sigmoid(x):
    return 1.0 / (1.0 + jnp.exp(-x))


def _rms(x, g):
    return x * lax.rsqrt(jnp.mean(x * x, axis=-1, keepdims=True) + EPS) * g


def _params(*sem):
    return pltpu.CompilerParams(dimension_semantics=sem, vmem_limit_bytes=VMEM_LIMIT_BYTES)


def _const_spec(shape):
    nd = len(shape)
    return pl.BlockSpec(shape, lambda *_: (0,) * nd)


def _in_proj_body(x_ref, g_ref, wa_ref, wb_ref, za_ref, zb_ref):
    hb = _rms(x_ref[...], g_ref[...]).astype(BF16)
    za_ref[...] = _dot(hb, wa_ref[...])
    zb_ref[...] = _dot(hb, wb_ref[...])


def _in_proj(x, g, wa, wb, tb):
    n, d = x.shape
    na, nb = wa.shape[1], wb.shape[1]
    return pl.pallas_call(
        _in_proj_body,
        grid=(n // tb,),
        in_specs=[pl.BlockSpec((tb, d), lambda i: (i, 0)), _const_spec((1, d)),
                  _const_spec(wa.shape), _const_spec(wb.shape)],
        out_specs=[pl.BlockSpec((tb, na), lambda i: (i, 0)), pl.BlockSpec((tb, nb), lambda i: (i, 0))],
        out_shape=[jax.ShapeDtypeStruct((n, na), F32), jax.ShapeDtypeStruct((n, nb), F32)],
        compiler_params=_params("parallel"),
        name="in_proj",
    )(x, g, wa, wb)


def _hgrn_tables(c):
    nl = int(np.log2(c)) + 1
    mats = np.zeros((nl, c, c), np.float32)
    masks = np.zeros((nl, c, c), np.float32)
    idx = np.arange(c)
    mats[0] = (idx[None, :] <= idx[:, None])
    masks[0] = np.eye(c)
    for l in range(1, nl):
        blk = 1 << l
        half = blk >> 1
        for t in range(c):
            start = (t // blk) * blk
            ref = start + half - 1
            if t - start >= half:
                mats[l, t, ref + 1:t + 1] = 1.0
                masks[l, t, start:start + half] = 1.0
            else:
                mats[l, t, t + 1:ref + 1] = 1.0
    return mats.reshape(nl * c, c), masks


def _hgrn_body(za_ref, lbt_ref, gn_ref, s0_ref, mats_ref, masks_ref, oa_ref, sout_ref, st_scr,
               *, nb, rows, chunk, heads, dk, dv):
    c_id = pl.program_id(1)
    qk = heads * dk
    wd = heads * dv
    nl = masks_ref.shape[0]

    @pl.when(c_id == 0)
    def _():
        for s in range(nb):
            for h in range(heads):
                st_scr[s, h] = s0_ref[min(s, s0_ref.shape[0] - 1), h].T

    th = lbt_ref[...]
    ex = jnp.exp(th - jnp.max(th, axis=0, keepdims=True))
    lb = ex[0:1] / jnp.sum(ex, axis=0, keepdims=True)
    gn = gn_ref[...]

    for s in range(nb):
        za = za_ref[s]
        zq = za[:, :qk]
        zf = za[:, qk:2 * qk]
        zi = za[:, 2 * qk:2 * qk + wd]
        zo = za[:, 2 * qk + wd:]
        logf = jnp.log(lb + (1.0 - lb) * _sigmoid(zf))
        kin = (1.0 - lb) * _sigmoid(-zf)
        qa = zq * _sigmoid(zq)
        if rows < chunk:
            pad = jnp.zeros((chunk - rows, qk), F32)
            logf = jnp.concatenate([logf, pad], axis=0)
            kin = jnp.concatenate([kin, pad], axis=0)
            qa = jnp.concatenate([qa, pad], axis=0)
            zi = jnp.concatenate([zi, jnp.zeros((chunk - rows, wd), F32)], axis=0)

        dec = _dot_exact_lhs(mats_ref[...], logf)
        for h in range(heads):
            ks = slice(h * dk, (h + 1) * dk)
            vs = slice(h * dv, (h + 1) * dv)
            q = qa[:, ks]
            k = kin[:, ks]
            v = zi[:, vs].astype(BF16)
            cum = dec[0:chunk, ks]
            sc = lax.dot_general(q.astype(BF16), k.astype(BF16), _NT, preferred_element_type=F32)
            scores = jnp.where(masks_ref[0] != 0.0, sc, 0.0)
            for l in range(1, nl):
                e = jnp.exp(dec[l * chunk:(l + 1) * chunk, ks])
                sc = lax.dot_general((q * e).astype(BF16), (k * e).astype(BF16), _NT, preferred_element_type=F32)
                scores = scores + jnp.where(masks_ref[l] != 0.0, sc, 0.0)
            st = st_scr[s, h]
            o = _dot(scores.astype(BF16), v) + lax.dot_general(
                (q * jnp.exp(cum)).astype(BF16), st.astype(BF16), _NT, preferred_element_type=F32)
            last = cum[chunk - 1:chunk, :]
            kd = (k * jnp.exp(last - cum)).astype(BF16)
            st_scr[s, h] = jnp.exp(last) * st + lax.dot_general(v, kd, _TN, preferred_element_type=F32)

            o = o[:rows]
            o = o * lax.rsqrt(jnp.mean(o * o, axis=-1, keepdims=True) + EPS) * gn[:, vs]
            zoh = zo[:, vs]
            oa_ref[s, :, vs] = o * (zoh * _sigmoid(zoh))

    @pl.when(c_id == pl.num_programs(1) - 1)
    def _():
        for s in range(nb):
            for h in range(heads):
                sout_ref[s, h] = st_scr[s, h].T


def _hgrn(za, nb, rows, lb_theta, gn, s0, heads, dk, dv):
    nseq, length, cols = za.shape
    chunk = HGRN_CHUNK if rows > HGRN_SHORT_CHUNK else HGRN_SHORT_CHUNK
    mats, masks = _hgrn_tables(chunk)
    mats = jnp.asarray(np.tile(mats, (1, 3)), BF16)
    masks = jnp.asarray(masks, F32)
    wd = heads * dv
    shared = s0.shape[0] == 1
    s0_spec = (pl.BlockSpec((1, heads, dk, dv), lambda s, c: (0, 0, 0, 0)) if shared
               else pl.BlockSpec((nb, heads, dk, dv), lambda s, c: (s, 0, 0, 0)))
    body = functools.partial(_hgrn_body, nb=nb, rows=rows, chunk=chunk, heads=heads, dk=dk, dv=dv)
    return pl.pallas_call(
        body,
        grid=(nseq // nb, length // rows),
        in_specs=[pl.BlockSpec((nb, rows, cols), lambda s, c: (s, c, 0)),
                  _const_spec(lb_theta.shape), _const_spec(gn.shape), s0_spec,
                  _const_spec(mats.shape), _const_spec(masks.shape)],
        out_specs=[pl.BlockSpec((nb, rows, wd), lambda s, c: (s, c, 0)),
                   pl.BlockSpec((nb, heads, dk, dv), lambda s, c: (s, 0, 0, 0))],
        out_shape=[jax.ShapeDtypeStruct((nseq, length, wd), F32),
                   jax.ShapeDtypeStruct((nseq, heads, dk, dv), F32)],
        scratch_shapes=[pltpu.VMEM((nb, heads, dv, dk), F32)],
        compiler_params=_params("parallel", "arbitrary"),
        name="hgrn",
    )(za, lb_theta, gn, s0, mats, masks)


def _rwkv_body(zb_ref, sh0_ref, s0_ref, mu_ref, w0_ref, ww_ref, a0_ref, aw_ref, gw_ref, kk_ref, ka_ref,
               rk_ref, lnw_ref, lnb_ref, ones_ref, seg_ref,
               ob_ref, sout_ref, shout_ref,
               s_scr, prev_scr, al_scr, w_scr, be_scr, k_scr, r_scr, v_scr, g_scr, y_scr, vt_scr,
               *, nb, tc, hw, hd, lora):
    t_id = pl.program_id(1)
    pairs = hw // LANES
    rin = zb_ref.shape[-1]

    @pl.when(t_id == 0)
    def _():
        s_scr[...] = s0_ref[...]
        prev_scr[...] = sh0_ref[...]

    row = lax.broadcasted_iota(jnp.int32, (tc, rin), 0)
    mu = mu_ref[...]
    zms = []
    for b in range(nb):
        zb = zb_ref[b]
        prev = jnp.where(row == 0, prev_scr[b:b + 1, :], pltpu.roll(zb, 1, axis=0))
        prev_scr[b:b + 1, :] = zb[tc - 1:tc, :]
        zms.append(zb + (prev - zb) * mu)
    zm = jnp.concatenate(zms, axis=0)
    r = zm[:, :hw]
    k = zm[:, hw:2 * hw]
    v = zm[:, 2 * hw:3 * hw]
    dwa = zm[:, 3 * hw:3 * hw + lora]
    dg = zm[:, 3 * hw + lora:]
    w_raw = w0_ref[...] + _dot_hi(jnp.tanh(dwa), ww_ref[...])
    decay = jnp.exp(-np.float32(np.exp(-0.5)) * _sigmoid(w_raw))
    a = _sigmoid(a0_ref[...] + _dot_hi(dwa, aw_ref[...]))
    g = _dot_hi(_sigmoid(dg), gw_ref[...])
    kk = k * kk_ref[...]
    ss = _head_sums(kk * kk, seg_ref[...])
    kk = kk / jnp.maximum(jnp.sqrt(ss), 1e-12)
    k2 = k * (1.0 + (a - 1.0) * ka_ref[...])
    shp = (nb, tc, hw)
    al_scr[...] = (-kk).reshape(shp)
    w_scr[...] = decay.reshape(shp)
    be_scr[...] = (kk * a).reshape(shp)
    k_scr[...] = k2.reshape(shp)
    r_scr[...] = r.reshape(shp)
    v_scr[...] = v.reshape(shp)
    g_scr[...] = g.reshape(shp)

    slot = RWKV_SLOT
    li2 = lax.broadcasted_iota(jnp.int32, (slot, LANES), 1)
    vi = lax.broadcasted_iota(jnp.int32, (hd, LANES), 0)
    sel = jnp.where(vi == lax.broadcasted_iota(jnp.int32, (hd, LANES), 1) % hd, 1.0, 0.0).astype(BF16)
    for b in range(nb):
        for p in range(pairs):
            for g0 in range(0, tc, slot):
                nt = min(slot, tc - g0)
                vb = v[b * tc + g0:b * tc + g0 + nt, p * LANES:(p + 1) * LANES]
                if nt < slot:
                    vb = jnp.concatenate([vb, jnp.zeros((slot - nt, LANES), F32)], axis=0)
                vh = vb.astype(BF16)
                vl = (vb - vh.astype(F32)).astype(BF16)
                zero = jnp.zeros_like(vh)
                stack = jnp.concatenate([jnp.where(li2 < hd, vh, zero), jnp.where(li2 >= hd, vh, zero),
                                         jnp.where(li2 < hd, vl, zero), jnp.where(li2 >= hd, vl, zero)], axis=0)
                vt_scr[g0 // slot, b, p] = lax.dot_general(
                    sel, stack, _NT, preferred_element_type=F32).astype(BF16)

    rows = pairs * hd
    li = lax.broadcasted_iota(jnp.int32, (rows, LANES), 1)
    key_lane = li % hd
    ones1 = ones_ref[...]
    ones_f = ones1.astype(F32)
    qi = lax.broadcasted_iota(jnp.int32, (4 * slot, LANES), 0)
    ql = lax.broadcasted_iota(jnp.int32, (4 * slot, LANES), 1)
    tok_of_row = jnp.where((qi // slot) % 2 == ql // hd, qi % slot, -1)

    def bcast(ref, b, t):
        rowv = ref[b, pl.ds(t, 1), :]
        return jnp.concatenate(
            [jnp.broadcast_to(rowv[:, p * LANES:(p + 1) * LANES], (hd, LANES)) for p in range(pairs)], axis=0)

    def step(t, carry):
        pick = jnp.where(tok_of_row == t % slot, 1.0, 0.0).astype(BF16)
        seqs = range(nb)
        vcols = [_dot(vt_scr[t // slot, b].reshape(rows, 4 * slot), pick) for b in seqs]
        for b in seqs:
            old = s_scr[b].reshape(rows, LANES)
            sa = _dot(old * bcast(al_scr, b, t), ones_f)
            new = old * bcast(w_scr, b, t) + sa * bcast(be_scr, b, t) + vcols[b] * bcast(k_scr, b, t)
            s_scr[b] = new.reshape(pairs, hd, LANES)
            yrep = _dot(new * bcast(r_scr, b, t), ones_f)
            y_scr[b] = jnp.where(key_lane == t, yrep, y_scr[b])
        return carry

    y_scr[...] = jnp.zeros_like(y_scr)
    lax.fori_loop(0, tc, step, 0, unroll=4)

    def token_rows(b, p):
        yt = y_scr[b, p * hd:(p + 1) * hd, :].T
        return jnp.concatenate([yt[0:tc], yt[hd:hd + tc]], axis=1)

    y = jnp.concatenate(
        [jnp.concatenate([token_rows(b, p) for p in range(pairs)], axis=1) for b in range(nb)], axis=0)
    seg = seg_ref[...]
    inv = np.float32(1.0 / hd)
    mean = _head_sums(y, seg) * inv
    yc = y - mean
    var = _head_sums(yc * yc, seg) * inv
    yn = yc * lax.rsqrt(var + GN_EPS) * lnw_ref[...] + lnb_ref[...]
    r = r_scr[...].reshape(nb * tc, hw)
    k2 = k_scr[...].reshape(nb * tc, hw)
    v = v_scr[...].reshape(nb * tc, hw)
    bonus = _head_sums(r * k2 * rk_ref[...], seg) * v
    ob_ref[...] = ((yn + bonus) * g_scr[...].reshape(nb * tc, hw)).reshape(shp)

    @pl.when(t_id == pl.num_programs(1) - 1)
    def _():
        sout_ref[...] = s_scr[...]
        shout_ref[...] = prev_scr[...]


def _rwkv(zb, sh0, s0, wts, nb, tc, hd):
    nseq, length, rin = zb.shape
    hw = wts["w0"].shape[1]
    pairs = hw // LANES
    lora = wts["ww"].shape[0]
    body = functools.partial(_rwkv_body, nb=nb, tc=tc, hw=hw, hd=hd, lora=lora)
    names = ("mu", "w0", "ww", "a0", "aw", "gw", "kk", "ka", "rk", "lnw", "lnb", "ones", "seg")
    consts = [wts[n] for n in names]
    tok = pltpu.VMEM((nb, tc, hw), F32)
    return pl.pallas_call(
        body,
        grid=(nseq // nb, length // tc),
        in_specs=[pl.BlockSpec((nb, tc, rin), lambda s, t: (s, t, 0)),
                  pl.BlockSpec((nb, rin), lambda s, t: (s, 0)),
                  pl.BlockSpec((nb, pairs, hd, LANES), lambda s, t: (s, 0, 0, 0))]
                 + [_const_spec(c.shape) for c in consts],
        out_specs=[pl.BlockSpec((nb, tc, hw), lambda s, t: (s, t, 0)),
                   pl.BlockSpec((nb, pairs, hd, LANES), lambda s, t: (s, 0, 0, 0)),
                   pl.BlockSpec((nb, rin), lambda s, t: (s, 0))],
        out_shape=[jax.ShapeDtypeStruct((nseq, length, hw), F32),
                   jax.ShapeDtypeStruct((nseq, pairs, hd, LANES), F32),
                   jax.ShapeDtypeStruct((nseq, rin), F32)],
        scratch_shapes=[pltpu.VMEM((nb, pairs, hd, LANES), F32), pltpu.VMEM((nb, rin), F32)]
                       + [tok] * 7 + [pltpu.VMEM((nb, pairs * hd, LANES), F32),
                                      pltpu.VMEM((-(-tc // RWKV_SLOT), nb, pairs, hd, 4 * RWKV_SLOT), BF16)],
        compiler_params=_params("parallel", "arbitrary"),
        name="rwkv",
    )(zb, sh0, s0, *consts)


def _pack_pairs(s):
    n, heads, hd, _ = s.shape
    return s.reshape(n, heads // 2, 2, hd, hd).transpose(0, 1, 3, 2, 4).reshape(n, heads // 2, hd, 2 * hd)


def _unpack_pairs(s):
    n, pairs, hd, _ = s.shape
    return s.reshape(n, pairs, hd, 2, hd).transpose(0, 1, 3, 2, 4).reshape(n, 2 * pairs, hd, hd)


def _out_proj_body(oa_ref, ob_ref, x_ref, woa_ref, wob_ref, g2_ref, wq_ref, keys_ref,
                   x1_ref, h2t_ref, st_ref, *, nk, dh):
    mix = _dot(oa_ref[...].astype(BF16), woa_ref[...]) + _dot(ob_ref[...].astype(BF16), wob_ref[...])
    x1 = x_ref[...] + mix
    x1_ref[...] = x1
    h2f = _rms(x1, g2_ref[...])
    h2t_ref[...] = h2f.T.astype(BF16)
    q = _dot(h2f.astype(BF16), wq_ref[...]).astype(BF16)
    for hp in range(q.shape[1] // dh):
        st_ref[hp * nk:(hp + 1) * nk, :] = lax.dot_general(
            keys_ref[hp % 2], q[:, hp * dh:(hp + 1) * dh], _NT, preferred_element_type=F32)


def _out_proj(oa, ob, x, woa, wob, g2, wq, keys, tb):
    n, d = x.shape
    nk, dh = keys.shape[1], keys.shape[2]
    nrow = (wq.shape[1] // dh) * nk
    body = functools.partial(_out_proj_body, nk=nk, dh=dh)
    return pl.pallas_call(
        body,
        grid=(n // tb,),
        in_specs=[pl.BlockSpec((tb, oa.shape[1]), lambda i: (i, 0)),
                  pl.BlockSpec((tb, ob.shape[1]), lambda i: (i, 0)),
                  pl.BlockSpec((tb, d), lambda i: (i, 0)),
                  _const_spec(woa.shape), _const_spec(wob.shape), _const_spec(g2.shape),
                  _const_spec(wq.shape), _const_spec(keys.shape)],
        out_specs=[pl.BlockSpec((tb, d), lambda i: (i, 0)), pl.BlockSpec((d, tb), lambda i: (0, i)),
                   pl.BlockSpec((nrow, tb), lambda i: (0, i))],
        out_shape=[jax.ShapeDtypeStruct((n, d), F32), jax.ShapeDtypeStruct((d, n), BF16),
                   jax.ShapeDtypeStruct((nrow, n), F32)],
        compiler_params=_params("parallel"),
        name="out_proj",
    )(oa, ob, x, woa, wob, g2, wq, keys)


def _sort16_network():
    n, pairs, p = 16, [], 1
    while p < n:
        k = p
        while k >= 1:
            for j in range(k % p, n - k, 2 * k):
                for i in range(min(k, n - j - k)):
                    if (i + j) // (2 * p) == (i + j + k) // (2 * p):
                        pairs.append((i + j, i + j + k))
            k //= 2
        p *= 2
    return pairs


def _top16_desc(tiles):
    t = list(tiles)

    def exchange(i, j):
        t[i], t[j] = jnp.maximum(t[i], t[j]), jnp.minimum(t[i], t[j])

    for i, j in _sort16_network():
        exchange(i, j)
    for shift in (4, 2, 1):
        t = [jnp.maximum(t[i], pltpu.roll(t[15 - i], shift, axis=0)) for i in range(16)]
        for d in (8, 4, 2, 1):
            for i in range(16):
                if not i & d:
                    exchange(i, i + d)
    return t


def _route_body(st_ref, t1_ref, e1_ref, e2_ref, top_scr, cand_scr, *, nk, topk):
    neg = -jnp.inf
    nx = topk + 1
    half = topk // 2

    def top_values(h, s):
        tiles = [s[i * SUBLANES:(i + 1) * SUBLANES] for i in range(nk // SUBLANES)]
        srt = _top16_desc(tiles)
        for i in range(topk):
            top_scr[h, i:i + 1, :] = srt[i][0:1]
        top_scr[h, topk:nx, :] = jnp.max(jnp.where(s < srt[topk - 1][0:1], s, neg), axis=0, keepdims=True)

    s1 = st_ref[0:nk, :]
    s2 = st_ref[nk:2 * nk, :]
    top_values(0, s1)
    top_values(1, s2)
    cand_scr[...] = jnp.full(cand_scr.shape, neg, F32)
    sv2 = top_scr[1, 0:nx, :]
    cand_scr[0:nx, :] = top_scr[0, 0:1, :] + sv2
    base = 3 * SUBLANES
    for a in range(1, half):
        cand_scr[base + (a - 1) * half:base + a * half, :] = top_scr[0, a:a + 1, :] + sv2[0:half]
    base += (half - 1) * half
    cand_scr[base:base + nx - half, :] = top_scr[0, half:nx, :] + sv2[0:1]

    cand = cand_scr[...]
    best = _top16_desc([cand[i * SUBLANES:(i + 1) * SUBLANES] for i in range(nk // SUBLANES)])
    z = jnp.zeros_like(best[0][0:1])
    for i in range(topk):
        z = z + jnp.exp(best[i][0:1] - best[0][0:1])
    hi = best[topk - 1][0:1]
    lo = jnp.max(jnp.where(cand < hi, cand, neg), axis=0, keepdims=True)
    thr = jnp.where(lo == neg, hi, 0.5 * (hi + lo))
    t1_ref[0] = thr - s1
    e1_ref[0] = jnp.exp(s1 - top_scr[0, 0:1, :]) * (np.float32(np.sqrt(0.5)) / z)
    e2_ref[0] = jnp.exp(s2 - sv2[0:1])


def _route(st, heads, nk, tb):
    n = st.shape[1]
    topk = PEER_TOPK
    half = topk // 2
    ncand = nk
    assert topk == 16 and nk == 16 * SUBLANES and half == SUBLANES
    assert 3 * SUBLANES + (half - 1) * half + topk + 1 - half <= ncand
    body = functools.partial(_route_body, nk=nk, topk=topk)
    blk = pl.BlockSpec((1, nk, tb), lambda i, h: (h, 0, i))
    return pl.pallas_call(
        body,
        grid=(n // tb, heads),
        in_specs=[pl.BlockSpec((2 * nk, tb), lambda i, h: (h, i))],
        out_specs=[blk, blk, blk],
        out_shape=[jax.ShapeDtypeStruct((heads, nk, n), F32)] * 3,
        scratch_shapes=[pltpu.VMEM((2, 3 * SUBLANES, tb), F32), pltpu.VMEM((ncand, tb), F32)],
        compiler_params=_params("parallel", "parallel"),
        name="route",
    )(st)


PEER_SUB = 2 * LANES
PEER_ROW_SPLIT = 2


def _peer_body(h2t_ref, s2_ref, t1_ref, e1_ref, e2_ref, u_ref, vt_ref, x1_ref, gf_ref, y_ref,
               acc_scr, coef_scr, *, heads, nk, tb, n_eb):
    i = pl.program_id(0)
    e = pl.program_id(1)
    eblk = 2 * u_ref.shape[0]
    i1_per_block = eblk // nk
    i1_per_sub = PEER_SUB // nk
    cur = e % 2

    @pl.when((i == 0) & (e == 0))
    def _():
        acc_scr[...] = jnp.zeros_like(acc_scr)
        coef_scr[1] = jnp.zeros(coef_scr.shape[1:], BF16)

    i1_0 = pl.multiple_of(e * i1_per_block, SUBLANES)
    n_sub = eblk // PEER_SUB
    h2t = h2t_ref[...]

    def matmuls(k):
        ks = slice(k * PEER_SUB, (k + 1) * PEER_SUB)
        u_rows = pltpu.bitcast(u_ref[k * PEER_SUB // 2:(k + 1) * PEER_SUB // 2, :], BF16)
        vt_cols = pltpu.bitcast(vt_ref[:, ks], BF16)
        return _dot(u_rows, h2t), _dot(vt_cols, coef_scr[1 - cur, ks, :])

    act, out = matmuls(0)
    for k in range(n_sub):
        if k + 1 < n_sub:
            act_next, part = matmuls(k + 1)
            out = out + part
        for lg in range(tb // LANES):
            ls = slice(lg * LANES, (lg + 1) * LANES)
            t1b = [t1_ref[h, pl.ds(i1_0, i1_per_block), ls] for h in range(heads)]
            e1b = [e1_ref[h, pl.ds(i1_0, i1_per_block), ls] for h in range(heads)]
            for rs in range(PEER_ROW_SPLIT):
                rsl = slice(rs * nk // PEER_ROW_SPLIT, (rs + 1) * nk // PEER_ROW_SPLIT)
                gates = [None] * i1_per_sub
                for h in range(heads):
                    s2 = s2_ref[h, rsl, ls]
                    e2 = e2_ref[h, rsl, ls]
                    for jj in range(i1_per_sub):
                        j = k * i1_per_sub + jj
                        term = jnp.where(s2 >= t1b[h][j:j + 1], e1b[h][j:j + 1] * e2, 0.0)
                        gates[jj] = term if gates[jj] is None else gates[jj] + term
                for jj in range(i1_per_sub):
                    a = act[jj * nk + rsl.start:jj * nk + rsl.stop, ls]
                    r0 = (k * i1_per_sub + jj) * nk
                    coef_scr[cur, r0 + rsl.start:r0 + rsl.stop, ls] = (gates[jj] * (a + a * lax.erf(a))).astype(BF16)
        if k + 1 < n_sub:
            act = act_next
    acc_scr[...] += out

    @pl.when(e == 0)
    def _():
        @pl.when(i > 0)
        def _():
            y_ref[...] = _rms(x1_ref[...] + acc_scr[...].T, gf_ref[...])

        acc_scr[...] = jnp.zeros_like(acc_scr)


def _peer(h2t, st, t1, e1, e2, u, vt, x1, gf, heads, nk, tb):
    n, d = x1.shape
    ne = 2 * u.shape[0]
    eblk = SUBLANES * nk
    n_eb = ne // eblk
    n_tb = n // tb
    assert n_eb % 2 == 0
    body = functools.partial(_peer_body, heads=heads, nk=nk, tb=tb, n_eb=n_eb)
    this = lambda i: jnp.minimum(i, n_tb - 1)
    done = lambda i: jnp.maximum(i - 1, 0)
    tok = pl.BlockSpec((heads, nk, tb), lambda i, e: (0, 0, this(i)))
    return pl.pallas_call(
        body,
        grid=(n_tb + 1, n_eb),
        in_specs=[pl.BlockSpec((d, tb), lambda i, e: (0, this(i))),
                  pl.BlockSpec((heads, None, nk, tb), lambda i, e: (0, 1, 0, this(i))),
                  tok, tok, tok,
                  pl.BlockSpec((eblk // 2, d), lambda i, e: (e, 0)),
                  pl.BlockSpec((d // 2, eblk), lambda i, e: (0, (e + n_eb - 1) % n_eb)),
                  pl.BlockSpec((tb, d), lambda i, e: (done(i), 0)),
                  _const_spec(gf.shape)],
        out_specs=pl.BlockSpec((tb, d), lambda i, e: (done(i), 0)),
        out_shape=jax.ShapeDtypeStruct((n, d), F32),
        scratch_shapes=[pltpu.VMEM((d, tb), F32), pltpu.VMEM((2, eblk, tb), BF16)],
        compiler_params=_params("arbitrary", "arbitrary"),
        name="peer",
    )(h2t, st.reshape(heads, 2, nk, n), t1, e1, e2, u, vt, x1, gf)


def _pack_body(x_ref, o_ref, *, transpose, scale):
    x = x_ref[...] * np.float32(scale)
    if transpose:
        x = x.T
    o_ref[...] = pltpu.bitcast(x.astype(BF16), jnp.uint32)


def _pack_table(x, transpose, scale=1.0, rows=512):
    n, c = x.shape
    if transpose:
        out_shape, out_spec = (c // 2, n), pl.BlockSpec((c // 2, rows), lambda i: (0, i))
    else:
        out_shape, out_spec = (n // 2, c), pl.BlockSpec((rows // 2, c), lambda i: (i, 0))
    return pl.pallas_call(
        functools.partial(_pack_body, transpose=transpose, scale=scale),
        grid=(n // rows,),
        in_specs=[pl.BlockSpec((rows, c), lambda i: (i, 0))],
        out_specs=out_spec,
        out_shape=jax.ShapeDtypeStruct(out_shape, jnp.uint32),
        compiler_params=_params("parallel"),
        name="pack_table",
    )(x)


def _block(n, pref):
    b = pref
    while n % b:
        b //= 2
    return b


def kernel(x_prompt, x_sample, state_hgrn, state_rwkv, state_shift, meta_tokens, norm1, w_in, lb_theta, hgrn_norm, mu_shift, w0, w_w2, a0, a_w2, g_w2, k_k, k_a, r_k, ln_w, ln_b, w_out, norm2, peer_wq, peer_keys, peer_u, peer_v, norm_f):
    assert w_in.shape[0] == 1, "single-layer trunk"
    bp, sp, d = x_prompt.shape
    bs, ss, _ = x_sample.shape
    n_meta = meta_tokens.shape[0]
    _, _, ha_heads, dk, dv = state_hgrn.shape
    _, _, hb_heads, hd, _ = state_rwkv.shape
    ha_in = 2 * ha_heads * dk + 2 * ha_heads * dv
    hw = hb_heads * hd
    lw, la = w_w2.shape[1], a_w2.shape[1]
    nk, dh = peer_keys.shape[2], peer_keys.shape[3]
    p_heads = peer_wq.shape[2] // (2 * dh)
    assert sp % HGRN_CHUNK == 0 and ss <= HGRN_CHUNK and n_meta <= HGRN_CHUNK
    assert hb_heads % 2 == 0 and 2 * hd == LANES and lw + la == LANES and nk == LANES

    row = lambda t: t.reshape(1, -1).astype(F32)
    w_a = w_in[0, :, :ha_in].astype(BF16)
    w_b = w_in[0, :, ha_in:].astype(BF16)
    g1 = row(norm1[0])
    lbt = lb_theta.astype(F32)
    gn = row(hgrn_norm[0])

    head_of_lane = np.arange(LANES) // hd
    rw = {
        "mu": row(mu_shift[0]), "w0": row(w0[0]), "a0": row(a0[0]), "kk": row(k_k[0]), "ka": row(k_a[0]),
        "rk": row(r_k[0]), "lnw": row(ln_w[0]), "lnb": row(ln_b[0]), "gw": g_w2[0].astype(F32),
        "ww": jnp.concatenate([w_w2[0], jnp.zeros((la, hw), F32)], axis=0),
        "aw": jnp.concatenate([jnp.zeros((lw, hw), F32), a_w2[0]], axis=0),
        "ones": jnp.asarray(head_of_lane[:, None] == head_of_lane[None, :], BF16),
        "seg": jnp.asarray(np.tile(head_of_lane[:, None] == head_of_lane[None, :], (2, 1)), BF16),
    }
    woa = w_out[0, :ha_heads * dv].astype(BF16)
    wob = w_out[0, ha_heads * dv:].astype(BF16)
    g2 = row(norm2[0])
    wq = peer_wq[0].astype(BF16)
    keys = peer_keys[0].astype(BF16)
    u = _pack_table(peer_u[0].astype(F32), False, scale=np.sqrt(0.5))
    vt = _pack_table(peer_v[0].astype(F32), True)
    gf = row(norm_f)

    za_m, zb_m = _in_proj(meta_tokens.astype(F32), g1, w_a, w_b, n_meta)
    _, hs_m = _hgrn(za_m[None], 1, n_meta, lbt, gn, jnp.zeros((1, ha_heads, dk, dv), F32), ha_heads, dk, dv)
    _, rs_m, sh_m = _rwkv(zb_m.reshape(1, n_meta, -1), jnp.zeros((1, zb_m.shape[1]), F32),
                          jnp.zeros((1, hb_heads // 2, hd, LANES), F32), rw, 1, n_meta, hd)

    def group(x, nseq, length, hs0, rs0, sh0, chunk_rows, hgrn_nb, tc):
        xf = x.reshape(nseq * length, d)
        n = xf.shape[0]
        tb = _block(n, 256)
        assert tb % LANES == 0
        za, zb = _in_proj(xf, g1, w_a, w_b, tb)
        oa, hs = _hgrn(za.reshape(nseq, length, -1), hgrn_nb, chunk_rows, lbt, gn, hs0, ha_heads, dk, dv)
        ob, rs, sh = _rwkv(zb.reshape(nseq, length, -1), sh0, rs0, rw, 8, tc, hd)
        x1, h2t, st = _out_proj(oa.reshape(n, -1), ob.reshape(n, hw), xf, woa, wob, g2, wq, keys, tb)
        t1, e1, e2 = _route(st, p_heads, nk, _block(n, 512))
        y = _peer(h2t, st, t1, e1, e2, u, vt, x1, gf, p_heads, nk, tb)
        return y.reshape(nseq, length, d), hs, _unpack_pairs(rs), sh

    y_p, hs_p, rs_p, sh_p = group(
        x_prompt, bp, sp, hs_m, jnp.broadcast_to(rs_m, (bp,) + rs_m.shape[1:]),
        jnp.broadcast_to(sh_m, (bp, sh_m.shape[1])), HGRN_CHUNK, 4, 64)
    y_s, hs_s, rs_s, sh_s = group(
        x_sample, bs, ss, state_hgrn[0].astype(F32), _pack_pairs(state_rwkv[0].astype(F32)),
        state_shift[0].astype(F32), ss, 8, ss)

    return (y_p, y_s, hs_p[None], rs_p[None], sh_p[None], hs_s[None], rs_s[None], sh_s[None])
```

```python
import functools

import numpy as np
import jax
import jax.numpy as jnp
from jax import lax
from jax.experimental import pallas as pl
from jax.experimental.pallas import tpu as pltpu

F32 = jnp.float32
BF16 = jnp.bfloat16

EPS = 1e-6
GN_EPS = 64e-5
HGRN_CHUNK = 64
HGRN_SHORT_CHUNK = 16
PEER_TOPK = 16
RWKV_SLOT = 32
LANES = 128
SUBLANES = 8
VMEM_LIMIT_BYTES = 52 * 1024 * 1024

_NT = (((1,), (1,)), ((), ()))
_TN = (((0,), (0,)), ((), ()))


def _dot(a, b):
    return jnp.dot(a, b, preferred_element_type=F32)


def _split3(a):
    a1 = a.astype(BF16)
    r1 = a - a1.astype(F32)
    a2 = r1.astype(BF16)
    a3 = (r1 - a2.astype(F32)).astype(BF16)
    return a1, a2, a3


def _head_sums(x, ones2):
    out = []
    for c0 in range(0, x.shape[1], LANES):
        xs = x[:, c0:c0 + LANES]
        hi = xs.astype(BF16)
        lo = (xs - hi.astype(F32)).astype(BF16)
        out.append(_dot(jnp.concatenate([hi, lo], axis=1), ones2))
    return jnp.concatenate(out, axis=1)


def _dot_exact_lhs(a_exact3, b):
    return _dot(a_exact3, jnp.concatenate(_split3(b), axis=0))


def _dot_hi(a, b):
    ah = a.astype(BF16)
    al = (a - ah.astype(F32)).astype(BF16)
    bh = b.astype(BF16)
    bl = (b - bh.astype(F32)).astype(BF16)
    return _dot(ah, bh) + _dot(ah, bl) + _dot(al, bh)


def _sigmoid(x):
    return 1.0 / (1.0 + jnp.exp(-x))


def _rms(x, g):
    return x * lax.rsqrt(jnp.mean(x * x, axis=-1, keepdims=True) + EPS) * g


def _params(*sem):
    return pltpu.CompilerParams(dimension_semantics=sem, vmem_limit_bytes=VMEM_LIMIT_BYTES)


def _const_spec(shape):
    nd = len(shape)
    return pl.BlockSpec(shape, lambda *_: (0,) * nd)


def _in_proj_body(x_ref, g_ref, wa_ref, wb_ref, za_ref, zb_ref):
    hb = _rms(x_ref[...], g_ref[...]).astype(BF16)
    za_ref[...] = _dot(hb, wa_ref[...])
    zb_ref[...] = _dot(hb, wb_ref[...])


def _in_proj(x, g, wa, wb, tb):
    n, d = x.shape
    na, nb = wa.shape[1], wb.shape[1]
    return pl.pallas_call(
        _in_proj_body,
        grid=(n // tb,),
        in_specs=[pl.BlockSpec((tb, d), lambda i: (i, 0)), _const_spec((1, d)),
                  _const_spec(wa.shape), _const_spec(wb.shape)],
        out_specs=[pl.BlockSpec((tb, na), lambda i: (i, 0)), pl.BlockSpec((tb, nb), lambda i: (i, 0))],
        out_shape=[jax.ShapeDtypeStruct((n, na), F32), jax.ShapeDtypeStruct((n, nb), F32)],
        compiler_params=_params("parallel"),
        name="in_proj",
    )(x, g, wa, wb)


def _hgrn_tables(c):
    nl = int(np.log2(c)) + 1
    mats = np.zeros((nl, c, c), np.float32)
    masks = np.zeros((nl, c, c), np.float32)
    idx = np.arange(c)
    mats[0] = (idx[None, :] <= idx[:, None])
    masks[0] = np.eye(c)
    for l in range(1, nl):
        blk = 1 << l
        half = blk >> 1
        for t in range(c):
            start = (t // blk) * blk
            ref = start + half - 1
            if t - start >= half:
                mats[l, t, ref + 1:t + 1] = 1.0
                masks[l, t, start:start + half] = 1.0
            else:
                mats[l, t, t + 1:ref + 1] = 1.0
    return mats.reshape(nl * c, c), masks


def _hgrn_body(za_ref, lbt_ref, gn_ref, s0_ref, mats_ref, masks_ref, oa_ref, sout_ref, st_scr,
               *, nb, rows, chunk, heads, dk, dv):
    c_id = pl.program_id(1)
    qk = heads * dk
    wd = heads * dv
    nl = masks_ref.shape[0]

    @pl.when(c_id == 0)
    def _():
        for s in range(nb):
            for h in range(heads):
                st_scr[s, h] = s0_ref[min(s, s0_ref.shape[0] - 1), h].T

    th = lbt_ref[...]
    ex = jnp.exp(th - jnp.max(th, axis=0, keepdims=True))
    lb = ex[0:1] / jnp.sum(ex, axis=0, keepdims=True)
    gn = gn_ref[...]

    for s in range(nb):
        za = za_ref[s]
        zq = za[:, :qk]
        zf = za[:, qk:2 * qk]
        zi = za[:, 2 * qk:2 * qk + wd]
        zo = za[:, 2 * qk + wd:]
        logf = jnp.log(lb + (1.0 - lb) * _sigmoid(zf))
        kin = (1.0 - lb) * _sigmoid(-zf)
        qa = zq * _sigmoid(zq)
        if rows < chunk:
            pad = jnp.zeros((chunk - rows, qk), F32)
            logf = jnp.concatenate([logf, pad], axis=0)
            kin = jnp.concatenate([kin, pad], axis=0)
            qa = jnp.concatenate([qa, pad], axis=0)
            zi = jnp.concatenate([zi, jnp.zeros((chunk - rows, wd), F32)], axis=0)

        dec = _dot_exact_lhs(mats_ref[...], logf)
        for h in range(heads):
            ks = slice(h * dk, (h + 1) * dk)
            vs = slice(h * dv, (h + 1) * dv)
            q = qa[:, ks]
            k = kin[:, ks]
            v = zi[:, vs].astype(BF16)
            cum = dec[0:chunk, ks]
            sc = lax.dot_general(q.astype(BF16), k.astype(BF16), _NT, preferred_element_type=F32)
            scores = jnp.where(masks_ref[0] != 0.0, sc, 0.0)
            for l in range(1, nl):
                e = jnp.exp(dec[l * chunk:(l + 1) * chunk, ks])
                sc = lax.dot_general((q * e).astype(BF16), (k * e).astype(BF16), _NT, preferred_element_type=F32)
                scores = scores + jnp.where(masks_ref[l] != 0.0, sc, 0.0)
            st = st_scr[s, h]
            o = _dot(scores.astype(BF16), v) + lax.dot_general(
                (q * jnp.exp(cum)).astype(BF16), st.astype(BF16), _NT, preferred_element_type=F32)
            last = cum[chunk - 1:chunk, :]
            kd = (k * jnp.exp(last - cum)).astype(BF16)
            st_scr[s, h] = jnp.exp(last) * st + lax.dot_general(v, kd, _TN, preferred_element_type=F32)

            o = o[:rows]
            o = o * lax.rsqrt(jnp.mean(o * o, axis=-1, keepdims=True) + EPS) * gn[:, vs]
            zoh = zo[:, vs]
            oa_ref[s, :, vs] = o * (zoh * _sigmoid(zoh))

    @pl.when(c_id == pl.num_programs(1) - 1)
    def _():
        for s in range(nb):
            for h in range(heads):
                sout_ref[s, h] = st_scr[s, h].T


def _hgrn(za, nb, rows, lb_theta, gn, s0, heads, dk, dv):
    nseq, length, cols = za.shape
    chunk = HGRN_CHUNK if rows > HGRN_SHORT_CHUNK else HGRN_SHORT_CHUNK
    mats, masks = _hgrn_tables(chunk)
    mats = jnp.asarray(np.tile(mats, (1, 3)), BF16)
    masks = jnp.asarray(masks, F32)
    wd = heads * dv
    shared = s0.shape[0] == 1
    s0_spec = (pl.BlockSpec((1, heads, dk, dv), lambda s, c: (0, 0, 0, 0)) if shared
               else pl.BlockSpec((nb, heads, dk, dv), lambda s, c: (s, 0, 0, 0)))
    body = functools.partial(_hgrn_body, nb=nb, rows=rows, chunk=chunk, heads=heads, dk=dk, dv=dv)
    return pl.pallas_call(
        body,
        grid=(nseq // nb, length // rows),
        in_specs=[pl.BlockSpec((nb, rows, cols), lambda s, c: (s, c, 0)),
                  _const_spec(lb_theta.shape), _const_spec(gn.shape), s0_spec,
                  _const_spec(mats.shape), _const_spec(masks.shape)],
        out_specs=[pl.BlockSpec((nb, rows, wd), lambda s, c: (s, c, 0)),
                   pl.BlockSpec((nb, heads, dk, dv), lambda s, c: (s, 0, 0, 0))],
        out_shape=[jax.ShapeDtypeStruct((nseq, length, wd), F32),
                   jax.ShapeDtypeStruct((nseq, heads, dk, dv), F32)],
        scratch_shapes=[pltpu.VMEM((nb, heads, dv, dk), F32)],
        compiler_params=_params("parallel", "arbitrary"),
        name="hgrn",
    )(za, lb_theta, gn, s0, mats, masks)


def _rwkv_body(zb_ref, sh0_ref, s0_ref, mu_ref, w0_ref, ww_ref, a0_ref, aw_ref, gw_ref, kk_ref, ka_ref,
               rk_ref, lnw_ref, lnb_ref, ones_ref, seg_ref,
               ob_ref, sout_ref, shout_ref,
               s_scr, prev_scr, al_scr, w_scr, be_scr, k_scr, r_scr, v_scr, g_scr, y_scr, vt_scr,
               *, nb, tc, hw, hd, lora):
    t_id = pl.program_id(1)
    pairs = hw // LANES
    rin = zb_ref.shape[-1]

    @pl.when(t_id == 0)
    def _():
        s_scr[...] = s0_ref[...]
        prev_scr[...] = sh0_ref[...]

    row = lax.broadcasted_iota(jnp.int32, (tc, rin), 0)
    mu = mu_ref[...]
    zms = []
    for b in range(nb):
        zb = zb_ref[b]
        prev = jnp.where(row == 0, prev_scr[b:b + 1, :], pltpu.roll(zb, 1, axis=0))
        prev_scr[b:b + 1, :] = zb[tc - 1:tc, :]
        zms.append(zb + (prev - zb) * mu)
    zm = jnp.concatenate(zms, axis=0)
    r = zm[:, :hw]
    k = zm[:, hw:2 * hw]
    v = zm[:, 2 * hw:3 * hw]
    dwa = zm[:, 3 * hw:3 * hw + lora]
    dg = zm[:, 3 * hw + lora:]
    w_raw = w0_ref[...] + _dot_hi(jnp.tanh(dwa), ww_ref[...])
    decay = jnp.exp(-np.float32(np.exp(-0.5)) * _sigmoid(w_raw))
    a = _sigmoid(a0_ref[...] + _dot_hi(dwa, aw_ref[...]))
    g = _dot_hi(_sigmoid(dg), gw_ref[...])
    kk = k * kk_ref[...]
    ss = _head_sums(kk * kk, seg_ref[...])
    kk = kk / jnp.maximum(jnp.sqrt(ss), 1e-12)
    k2 = k * (1.0 + (a - 1.0) * ka_ref[...])
    shp = (nb, tc, hw)
    al_scr[...] = (-kk).reshape(shp)
    w_scr[...] = decay.reshape(shp)
    be_scr[...] = (kk * a).reshape(shp)
    k_scr[...] = k2.reshape(shp)
    r_scr[...] = r.reshape(shp)
    v_scr[...] = v.reshape(shp)
    g_scr[...] = g.reshape(shp)

    slot = RWKV_SLOT
    li2 = lax.broadcasted_iota(jnp.int32, (slot, LANES), 1)
    vi = lax.broadcasted_iota(jnp.int32, (hd, LANES), 0)
    sel = jnp.where(vi == lax.broadcasted_iota(jnp.int32, (hd, LANES), 1) % hd, 1.0, 0.0).astype(BF16)
    for b in range(nb):
        for p in range(pairs):
            for g0 in range(0, tc, slot):
                nt = min(slot, tc - g0)
                vb = v[b * tc + g0:b * tc + g0 + nt, p * LANES:(p + 1) * LANES]
                if nt < slot:
                    vb = jnp.concatenate([vb, jnp.zeros((slot - nt, LANES), F32)], axis=0)
                vh = vb.astype(BF16)
                vl = (vb - vh.astype(F32)).astype(BF16)
                zero = jnp.zeros_like(vh)
                stack = jnp.concatenate([jnp.where(li2 < hd, vh, zero), jnp.where(li2 >= hd, vh, zero),
                                         jnp.where(li2 < hd, vl, zero), jnp.where(li2 >= hd, vl, zero)], axis=0)
                vt_scr[g0 // slot, b, p] = lax.dot_general(
                    sel, stack, _NT, preferred_element_type=F32).astype(BF16)

    rows = pairs * hd
    li = lax.broadcasted_iota(jnp.int32, (rows, LANES), 1)
    key_lane = li % hd
    ones1 = ones_ref[...]
    ones_f = ones1.astype(F32)
    qi = lax.broadcasted_iota(jnp.int32, (4 * slot, LANES), 0)
    ql = lax.broadcasted_iota(jnp.int32, (4 * slot, LANES), 1)
    tok_of_row = jnp.where((qi // slot) % 2 == ql // hd, qi % slot, -1)

    def bcast(ref, b, t):
        rowv = ref[b, pl.ds(t, 1), :]
        return jnp.concatenate(
            [jnp.broadcast_to(rowv[:, p * LANES:(p + 1) * LANES], (hd, LANES)) for p in range(pairs)], axis=0)

    def step(t, carry):
        pick = jnp.where(tok_of_row == t % slot, 1.0, 0.0).astype(BF16)
        seqs = range(nb)
        vcols = [_dot(vt_scr[t // slot, b].reshape(rows, 4 * slot), pick) for b in seqs]
        for b in seqs:
            old = s_scr[b].reshape(rows, LANES)
            sa = _dot(old * bcast(al_scr, b, t), ones_f)
            new = old * bcast(w_scr, b, t) + sa * bcast(be_scr, b, t) + vcols[b] * bcast(k_scr, b, t)
            s_scr[b] = new.reshape(pairs, hd, LANES)
            yrep = _dot(new * bcast(r_scr, b, t), ones_f)
            y_scr[b] = jnp.where(key_lane == t, yrep, y_scr[b])
        return carry

    y_scr[...] = jnp.zeros_like(y_scr)
    lax.fori_loop(0, tc, step, 0, unroll=4)

    def token_rows(b, p):
        yt = y_scr[b, p * hd:(p + 1) * hd, :].T
        return jnp.concatenate([yt[0:tc], yt[hd:hd + tc]], axis=1)

    y = jnp.concatenate(
        [jnp.concatenate([token_rows(b, p) for p in range(pairs)], axis=1) for b in range(nb)], axis=0)
    seg = seg_ref[...]
    inv = np.float32(1.0 / hd)
    mean = _head_sums(y, seg) * inv
    yc = y - mean
    var = _head_sums(yc * yc, seg) * inv
    yn = yc * lax.rsqrt(var + GN_EPS) * lnw_ref[...] + lnb_ref[...]
    r = r_scr[...].reshape(nb * tc, hw)
    k2 = k_scr[...].reshape(nb * tc, hw)
    v = v_scr[...].reshape(nb * tc, hw)
    bonus = _head_sums(r * k2 * rk_ref[...], seg) * v
    ob_ref[...] = ((yn + bonus) * g_scr[...].reshape(nb * tc, hw)).reshape(shp)

    @pl.when(t_id == pl.num_programs(1) - 1)
    def _():
        sout_ref[...] = s_scr[...]
        shout_ref[...] = prev_scr[...]


def _rwkv(zb, sh0, s0, wts, nb, tc, hd):
    nseq, length, rin = zb.shape
    hw = wts["w0"].shape[1]
    pairs = hw // LANES
    lora = wts["ww"].shape[0]
    body = functools.partial(_rwkv_body, nb=nb, tc=tc, hw=hw, hd=hd, lora=lora)
    names = ("mu", "w0", "ww", "a0", "aw", "gw", "kk", "ka", "rk", "lnw", "lnb", "ones", "seg")
    consts = [wts[n] for n in names]
    tok = pltpu.VMEM((nb, tc, hw), F32)
    return pl.pallas_call(
        body,
        grid=(nseq // nb, length // tc),
        in_specs=[pl.BlockSpec((nb, tc, rin), lambda s, t: (s, t, 0)),
                  pl.BlockSpec((nb, rin), lambda s, t: (s, 0)),
                  pl.BlockSpec((nb, pairs, hd, LANES), lambda s, t: (s, 0, 0, 0))]
                 + [_const_spec(c.shape) for c in consts],
        out_specs=[pl.BlockSpec((nb, tc, hw), lambda s, t: (s, t, 0)),
                   pl.BlockSpec((nb, pairs, hd, LANES), lambda s, t: (s, 0, 0, 0)),
                   pl.BlockSpec((nb, rin), lambda s, t: (s, 0))],
        out_shape=[jax.ShapeDtypeStruct((nseq, length, hw), F32),
                   jax.ShapeDtypeStruct((nseq, pairs, hd, LANES), F32),
                   jax.ShapeDtypeStruct((nseq, rin), F32)],
        scratch_shapes=[pltpu.VMEM((nb, pairs, hd, LANES), F32), pltpu.VMEM((nb, rin), F32)]
                       + [tok] * 7 + [pltpu.VMEM((nb, pairs * hd, LANES), F32),
                                      pltpu.VMEM((-(-tc // RWKV_SLOT), nb, pairs, hd, 4 * RWKV_SLOT), BF16)],
        compiler_params=_params("parallel", "arbitrary"),
        name="rwkv",
    )(zb, sh0, s0, *consts)


def _pack_pairs(s):
    n, heads, hd, _ = s.shape
    return s.reshape(n, heads // 2, 2, hd, hd).transpose(0, 1, 3, 2, 4).reshape(n, heads // 2, hd, 2 * hd)


def _unpack_pairs(s):
    n, pairs, hd, _ = s.shape
    return s.reshape(n, pairs, hd, 2, hd).transpose(0, 1, 3, 2, 4).reshape(n, 2 * pairs, hd, hd)


def _out_proj_body(oa_ref, ob_ref, x_ref, woa_ref, wob_ref, g2_ref, wq_ref, keys_ref,
                   x1_ref, h2t_ref, st_ref, *, nk, dh):
    mix = _dot(oa_ref[...].astype(BF16), woa_ref[...]) + _dot(ob_ref[...].astype(BF16), wob_ref[...])
    x1 = x_ref[...] + mix
    x1_ref[...] = x1
    h2f = _rms(x1, g2_ref[...])
    h2t_ref[...] = h2f.T.astype(BF16)
    q = _dot(h2f.astype(BF16), wq_ref[...]).astype(BF16)
    for hp in range(q.shape[1] // dh):
        st_ref[hp * nk:(hp + 1) * nk, :] = lax.dot_general(
            keys_ref[hp % 2], q[:, hp * dh:(hp + 1) * dh], _NT, preferred_element_type=F32)


def _out_proj(oa, ob, x, woa, wob, g2, wq, keys, tb):
    n, d = x.shape
    nk, dh = keys.shape[1], keys.shape[2]
    nrow = (wq.shape[1] // dh) * nk
    body = functools.partial(_out_proj_body, nk=nk, dh=dh)
    return pl.pallas_call(
        body,
        grid=(n // tb,),
        in_specs=[pl.BlockSpec((tb, oa.shape[1]), lambda i: (i, 0)),
                  pl.BlockSpec((tb, ob.shape[1]), lambda i: (i, 0)),
                  pl.BlockSpec((tb, d), lambda i: (i, 0)),
                  _const_spec(woa.shape), _const_spec(wob.shape), _const_spec(g2.shape),
                  _const_spec(wq.shape), _const_spec(keys.shape)],
        out_specs=[pl.BlockSpec((tb, d), lambda i: (i, 0)), pl.BlockSpec((d, tb), lambda i: (0, i)),
                   pl.BlockSpec((nrow, tb), lambda i: (0, i))],
        out_shape=[jax.ShapeDtypeStruct((n, d), F32), jax.ShapeDtypeStruct((d, n), BF16),
                   jax.ShapeDtypeStruct((nrow, n), F32)],
        compiler_params=_params("parallel"),
        name="out_proj",
    )(oa, ob, x, woa, wob, g2, wq, keys)


def _sort16_network():
    n, pairs, p = 16, [], 1
    while p < n:
        k = p
        while k >= 1:
            for j in range(k % p, n - k, 2 * k):
                for i in range(min(k, n - j - k)):
                    if (i + j) // (2 * p) == (i + j + k) // (2 * p):
                        pairs.append((i + j, i + j + k))
            k //= 2
        p *= 2
    return pairs


def _top16_desc(tiles):
    t = list(tiles)

    def exchange(i, j):
        t[i], t[j] = jnp.maximum(t[i], t[j]), jnp.minimum(t[i], t[j])

    for i, j in _sort16_network():
        exchange(i, j)
    for shift in (4, 2, 1):
        t = [jnp.maximum(t[i], pltpu.roll(t[15 - i], shift, axis=0)) for i in range(16)]
        for d in (8, 4, 2, 1):
            for i in range(16):
                if not i & d:
                    exchange(i, i + d)
    return t


def _route_body(st_ref, t1_ref, e1_ref, e2_ref, top_scr, cand_scr, *, nk, topk):
    neg = -jnp.inf
    nx = topk + 1
    half = topk // 2

    def top_values(h, s):
        tiles = [s[i * SUBLANES:(i + 1) * SUBLANES] for i in range(nk // SUBLANES)]
        srt = _top16_desc(tiles)
        for i in range(topk):
            top_scr[h, i:i + 1, :] = srt[i][0:1]
        top_scr[h, topk:nx, :] = jnp.max(jnp.where(s < srt[topk - 1][0:1], s, neg), axis=0, keepdims=True)

    s1 = st_ref[0:nk, :]
    s2 = st_ref[nk:2 * nk, :]
    top_values(0, s1)
    top_values(1, s2)
    cand_scr[...] = jnp.full(cand_scr.shape, neg, F32)
    sv2 = top_scr[1, 0:nx, :]
    cand_scr[0:nx, :] = top_scr[0, 0:1, :] + sv2
    base = 3 * SUBLANES
    for a in range(1, half):
        cand_scr[base + (a - 1) * half:base + a * half, :] = top_scr[0, a:a + 1, :] + sv2[0:half]
    base += (half - 1) * half
    cand_scr[base:base + nx - half, :] = top_scr[0, half:nx, :] + sv2[0:1]

    cand = cand_scr[...]
    best = _top16_desc([cand[i * SUBLANES:(i + 1) * SUBLANES] for i in range(nk // SUBLANES)])
    z = jnp.zeros_like(best[0][0:1])
    for i in range(topk):
        z = z + jnp.exp(best[i][0:1] - best[0][0:1])
    hi = best[topk - 1][0:1]
    lo = jnp.max(jnp.where(cand < hi, cand, neg), axis=0, keepdims=True)
    thr = jnp.where(lo == neg, hi, 0.5 * (hi + lo))
    t1_ref[0] = thr - s1
    e1_ref[0] = jnp.exp(s1 - top_scr[0, 0:1, :]) * (np.float32(np.sqrt(0.5)) / z)
    e2_ref[0] = jnp.exp(s2 - sv2[0:1])


def _route(st, heads, nk, tb):
    n = st.shape[1]
    topk = PEER_TOPK
    half = topk // 2
    ncand = nk
    assert topk == 16 and nk == 16 * SUBLANES and half == SUBLANES
    assert 3 * SUBLANES + (half - 1) * half + topk + 1 - half <= ncand
    body = functools.partial(_route_body, nk=nk, topk=topk)
    blk = pl.BlockSpec((1, nk, tb), lambda i, h: (h, 0, i))
    return pl.pallas_call(
        body,
        grid=(n // tb, heads),
        in_specs=[pl.BlockSpec((2 * nk, tb), lambda i, h: (h, i))],
        out_specs=[blk, blk, blk],
        out_shape=[jax.ShapeDtypeStruct((heads, nk, n), F32)] * 3,
        scratch_shapes=[pltpu.VMEM((2, 3 * SUBLANES, tb), F32), pltpu.VMEM((ncand, tb), F32)],
        compiler_params=_params("parallel", "parallel"),
        name="route",
    )(st)


PEER_SUB = 2 * LANES
PEER_ROW_SPLIT = 2


def _peer_body(h2t_ref, s2_ref, t1_ref, e1_ref, e2_ref, u_ref, vt_ref, x1_ref, gf_ref, y_ref,
               acc_scr, coef_scr, *, heads, nk, tb, n_eb, n_tb):
    i = pl.program_id(0)
    e = pl.program_id(1)
    eblk = 2 * u_ref.shape[0]
    i1_per_block = eblk // nk
    i1_per_sub = PEER_SUB // nk
    cur = e % 2

    @pl.when((i == 0) & (e == 0))
    def _():
        acc_scr[...] = jnp.zeros_like(acc_scr)
        coef_scr[1] = jnp.zeros(coef_scr.shape[1:], BF16)

    @pl.when((i < n_tb) | (e == 0))
    def _():
        i1_0 = pl.multiple_of(e * i1_per_block, SUBLANES)
        n_sub = eblk // PEER_SUB
        h2t = h2t_ref[...]

        def matmuls(k):
            ks = slice(k * PEER_SUB, (k + 1) * PEER_SUB)
            u_rows = pltpu.bitcast(u_ref[k * PEER_SUB // 2:(k + 1) * PEER_SUB // 2, :], BF16)
            vt_cols = pltpu.bitcast(vt_ref[:, ks], BF16)
            return _dot(u_rows, h2t), _dot(vt_cols, coef_scr[1 - cur, ks, :])

        act, out = matmuls(0)
        for k in range(n_sub):
            if k + 1 < n_sub:
                act_next, part = matmuls(k + 1)
                out = out + part
            for lg in range(tb // LANES):
                ls = slice(lg * LANES, (lg + 1) * LANES)
                t1b = [t1_ref[h, pl.ds(i1_0, i1_per_block), ls] for h in range(heads)]
                e1b = [e1_ref[h, pl.ds(i1_0, i1_per_block), ls] for h in range(heads)]
                for rs in range(PEER_ROW_SPLIT):
                    rsl = slice(rs * nk // PEER_ROW_SPLIT, (rs + 1) * nk // PEER_ROW_SPLIT)
                    gates = [None] * i1_per_sub
                    for h in range(heads):
                        s2 = s2_ref[h, rsl, ls]
                        e2 = e2_ref[h, rsl, ls]
                        for jj in range(i1_per_sub):
                            j = k * i1_per_sub + jj
                            term = jnp.where(s2 >= t1b[h][j:j + 1], e1b[h][j:j + 1] * e2, 0.0)
                            gates[jj] = term if gates[jj] is None else gates[jj] + term
                    for jj in range(i1_per_sub):
                        a = act[jj * nk + rsl.start:jj * nk + rsl.stop, ls]
                        r0 = (k * i1_per_sub + jj) * nk
                        coef_scr[cur, r0 + rsl.start:r0 + rsl.stop, ls] = (gates[jj] * (a + a * lax.erf(a))).astype(BF16)
            if k + 1 < n_sub:
                act = act_next
        acc_scr[...] += out

    @pl.when(e == 0)
    def _():
        @pl.when(i > 0)
        def _():
            y_ref[...] = _rms(x1_ref[...] + acc_scr[...].T, gf_ref[...])

        acc_scr[...] = jnp.zeros_like(acc_scr)


def _peer(h2t, st, t1, e1, e2, u, vt, x1, gf, heads, nk, tb):
    n, d = x1.shape
    ne = 2 * u.shape[0]
    eblk = SUBLANES * nk
    n_eb = ne // eblk
    n_tb = n // tb
    assert n_eb % 2 == 0
    body = functools.partial(_peer_body, heads=heads, nk=nk, tb=tb, n_eb=n_eb, n_tb=n_tb)
    this = lambda i: jnp.minimum(i, n_tb - 1)
    done = lambda i: jnp.maximum(i - 1, 0)
    tok = pl.BlockSpec((heads, nk, tb), lambda i, e: (0, 0, this(i)))
    return pl.pallas_call(
        body,
        grid=(n_tb + 1, n_eb),
        in_specs=[pl.BlockSpec((d, tb), lambda i, e: (0, this(i))),
                  pl.BlockSpec((heads, None, nk, tb), lambda i, e: (0, 1, 0, this(i))),
                  tok, tok, tok,
                  pl.BlockSpec((eblk // 2, d), lambda i, e: (jnp.where(i < n_tb, e, 0), 0)),
                  pl.BlockSpec((d // 2, eblk), lambda i, e: (0, jnp.where(i < n_tb, (e + n_eb - 1) % n_eb, n_eb - 1))),
                  pl.BlockSpec((tb, d), lambda i, e: (done(i), 0)),
                  _const_spec(gf.shape)],
        out_specs=pl.BlockSpec((tb, d), lambda i, e: (done(i), 0)),
        out_shape=jax.ShapeDtypeStruct((n, d), F32),
        scratch_shapes=[pltpu.VMEM((d, tb), F32), pltpu.VMEM((2, eblk, tb), BF16)],
        compiler_params=_params("arbitrary", "arbitrary"),
        name="peer",
    )(h2t, st.reshape(heads, 2, nk, n), t1, e1, e2, u, vt, x1, gf)


def _pack_body(x_ref, o_ref, *, transpose, scale):
    x = x_ref[...] * np.float32(scale)
    if transpose:
        x = x.T
    o_ref[...] = pltpu.bitcast(x.astype(BF16), jnp.uint32)


def _pack_table(x, transpose, scale=1.0, rows=512):
    n, c = x.shape
    if transpose:
        out_shape, out_spec = (c // 2, n), pl.BlockSpec((c // 2, rows), lambda i: (0, i))
    else:
        out_shape, out_spec = (n // 2, c), pl.BlockSpec((rows // 2, c), lambda i: (i, 0))
    return pl.pallas_call(
        functools.partial(_pack_body, transpose=transpose, scale=scale),
        grid=(n // rows,),
        in_specs=[pl.BlockSpec((rows, c), lambda i: (i, 0))],
        out_specs=out_spec,
        out_shape=jax.ShapeDtypeStruct(out_shape, jnp.uint32),
        compiler_params=_params("parallel"),
        name="pack_table",
    )(x)


def _block(n, pref):
    b = pref
    while n % b:
        b //= 2
    return b


def kernel(x_prompt, x_sample, state_hgrn, state_rwkv, state_shift, meta_tokens, norm1, w_in, lb_theta, hgrn_norm, mu_shift, w0, w_w2, a0, a_w2, g_w2, k_k, k_a, r_k, ln_w, ln_b, w_out, norm2, peer_wq, peer_keys, peer_u, peer_v, norm_f):
    assert w_in.shape[0] == 1, "single-layer trunk"
    bp, sp, d = x_prompt.shape
    bs, ss, _ = x_sample.shape
    n_meta = meta_tokens.shape[0]
    _, _, ha_heads, dk, dv = state_hgrn.shape
    _, _, hb_heads, hd, _ = state_rwkv.shape
    ha_in = 2 * ha_heads * dk + 2 * ha_heads * dv
    hw = hb_heads * hd
    lw, la = w_w2.shape[1], a_w2.shape[1]
    nk, dh = peer_keys.shape[2], peer_keys.shape[3]
    p_heads = peer_wq.shape[2] // (2 * dh)
    assert sp % HGRN_CHUNK == 0 and ss <= HGRN_CHUNK and n_meta <= HGRN_CHUNK
    assert hb_heads % 2 == 0 and 2 * hd == LANES and lw + la == LANES and nk == LANES

    row = lambda t: t.reshape(1, -1).astype(F32)
    w_a = w_in[0, :, :ha_in].astype(BF16)
    w_b = w_in[0, :, ha_in:].astype(BF16)
    g1 = row(norm1[0])
    lbt = lb_theta.astype(F32)
    gn = row(hgrn_norm[0])

    head_of_lane = np.arange(LANES) // hd
    rw = {
        "mu": row(mu_shift[0]), "w0": row(w0[0]), "a0": row(a0[0]), "kk": row(k_k[0]), "ka": row(k_a[0]),
        "rk": row(r_k[0]), "lnw": row(ln_w[0]), "lnb": row(ln_b[0]), "gw": g_w2[0].astype(F32),
        "ww": jnp.concatenate([w_w2[0], jnp.zeros((la, hw), F32)], axis=0),
        "aw": jnp.concatenate([jnp.zeros((lw, hw), F32), a_w2[0]], axis=0),
        "ones": jnp.asarray(head_of_lane[:, None] == head_of_lane[None, :], BF16),
        "seg": jnp.asarray(np.tile(head_of_lane[:, None] == head_of_lane[None, :], (2, 1)), BF16),
    }
    woa = w_out[0, :ha_heads * dv].astype(BF16)
    wob = w_out[0, ha_heads * dv:].astype(BF16)
    g2 = row(norm2[0])
    wq = peer_wq[0].astype(BF16)
    keys = peer_keys[0].astype(BF16)
    u = _pack_table(peer_u[0].astype(F32), False, scale=np.sqrt(0.5))
    vt = _pack_table(peer_v[0].astype(F32), True)
    gf = row(norm_f)

    za_m, zb_m = _in_proj(meta_tokens.astype(F32), g1, w_a, w_b, n_meta)
    _, hs_m = _hgrn(za_m[None], 1, n_meta, lbt, gn, jnp.zeros((1, ha_heads, dk, dv), F32), ha_heads, dk, dv)
    _, rs_m, sh_m = _rwkv(zb_m.reshape(1, n_meta, -1), jnp.zeros((1, zb_m.shape[1]), F32),
                          jnp.zeros((1, hb_heads // 2, hd, LANES), F32), rw, 1, n_meta, hd)

    def group(x, nseq, length, hs0, rs0, sh0, chunk_rows, hgrn_nb, tc):
        xf = x.reshape(nseq * length, d)
        n = xf.shape[0]
        tb = _block(n, 256)
        assert tb % LANES == 0
        za, zb = _in_proj(xf, g1, w_a, w_b, tb)
        oa, hs = _hgrn(za.reshape(nseq, length, -1), hgrn_nb, chunk_rows, lbt, gn, hs0, ha_heads, dk, dv)
        ob, rs, sh = _rwkv(zb.reshape(nseq, length, -1), sh0, rs0, rw, 8, tc, hd)
        x1, h2t, st = _out_proj(oa.reshape(n, -1), ob.reshape(n, hw), xf, woa, wob, g2, wq, keys, tb)
        t1, e1, e2 = _route(st, p_heads, nk, _block(n, 512))
        y = _peer(h2t, st, t1, e1, e2, u, vt, x1, gf, p_heads, nk, tb)
        return y.reshape(nseq, length, d), hs, _unpack_pairs(rs), sh

    y_p, hs_p, rs_p, sh_p = group(
        x_prompt, bp, sp, hs_m, jnp.broadcast_to(rs_m, (bp,) + rs_m.shape[1:]),
        jnp.broadcast_to(sh_m, (bp, sh_m.shape[1])), HGRN_CHUNK, 4, 64)
    y_s, hs_s, rs_s, sh_s = group(
        x_sample, bs, ss, state_hgrn[0].astype(F32), _pack_pairs(state_rwkv[0].astype(F32)),
        state_shift[0].astype(F32), ss, 8, ss)

    return (y_p, y_s, hs_p[None], rs_p[None], sh_p[None], hs_s[None], rs_s[None], sh_s[None])
```

```python
import functools

import numpy as np
import jax
import jax.numpy as jnp
from jax import lax
from jax.experimental import pallas as pl
from jax.experimental.pallas import tpu as pltpu

F32 = jnp.float32
BF16 = jnp.bfloat16

EPS = 1e-6
GN_EPS = 64e-5
HGRN_CHUNK = 64
HGRN_SHORT_CHUNK = 16
PEER_TOPK = 16
RWKV_SLOT = 32
LANES = 128
SUBLANES = 8
VMEM_LIMIT_BYTES = 52 * 1024 * 1024

_NT = (((1,), (1,)), ((), ()))
_TN = (((0,), (0,)), ((), ()))


def _dot(a, b):
    return jnp.dot(a, b, preferred_element_type=F32)


def _split3(a):
    a1 = a.astype(BF16)
    r1 = a - a1.astype(F32)
    a2 = r1.astype(BF16)
    a3 = (r1 - a2.astype(F32)).astype(BF16)
    return a1, a2, a3


def _head_sums(x, ones2):
    out = []
    for c0 in range(0, x.shape[1], LANES):
        xs = x[:, c0:c0 + LANES]
        hi = xs.astype(BF16)
        lo = (xs - hi.astype(F32)).astype(BF16)
        out.append(_dot(jnp.concatenate([hi, lo], axis=1), ones2))
    return jnp.concatenate(out, axis=1)


def _dot_exact_lhs(a_exact3, b):
    return _dot(a_exact3, jnp.concatenate(_split3(b), axis=0))


def _dot_hi(a, b):
    ah = a.astype(BF16)
    al = (a - ah.astype(F32)).astype(BF16)
    bh = b.astype(BF16)
    bl = (b - bh.astype(F32)).astype(BF16)
    return _dot(ah, bh) + _dot(ah, bl) + _dot(al, bh)


def _sigmoid(x):
    return 1.0 / (1.0 + jnp.exp(-x))


def _rms(x, g):
    return x * lax.rsqrt(jnp.mean(x * x, axis=-1, keepdims=True) + EPS) * g


def _params(*sem):
    return pltpu.CompilerParams(dimension_semantics=sem, vmem_limit_bytes=VMEM_LIMIT_BYTES)


def _const_spec(shape):
    nd = len(shape)
    return pl.BlockSpec(shape, lambda *_: (0,) * nd)


def _in_proj_body(x_ref, g_ref, wa_ref, wb_ref, za_ref, zb_ref):
    hb = _rms(x_ref[...], g_ref[...]).astype(BF16)
    za_ref[...] = _dot(hb, wa_ref[...])
    zb_ref[...] = _dot(hb, wb_ref[...])


def _in_proj(x, g, wa, wb, tb):
    n, d = x.shape
    na, nb = wa.shape[1], wb.shape[1]
    return pl.pallas_call(
        _in_proj_body,
        grid=(n // tb,),
        in_specs=[pl.BlockSpec((tb, d), lambda i: (i, 0)), _const_spec((1, d)),
                  _const_spec(wa.shape), _const_spec(wb.shape)],
        out_specs=[pl.BlockSpec((tb, na), lambda i: (i, 0)), pl.BlockSpec((tb, nb), lambda i: (i, 0))],
        out_shape=[jax.ShapeDtypeStruct((n, na), F32), jax.ShapeDtypeStruct((n, nb), F32)],
        compiler_params=_params("parallel"),
        name="in_proj",
    )(x, g, wa, wb)


def _hgrn_tables(c):
    nl = int(np.log2(c)) + 1
    mats = np.zeros((nl, c, c), np.float32)
    masks = np.zeros((nl, c, c), np.float32)
    idx = np.arange(c)
    mats[0] = (idx[None, :] <= idx[:, None])
    masks[0] = np.eye(c)
    for l in range(1, nl):
        blk = 1 << l
        half = blk >> 1
        for t in range(c):
            start = (t // blk) * blk
            ref = start + half - 1
            if t - start >= half:
                mats[l, t, ref + 1:t + 1] = 1.0
                masks[l, t, start:start + half] = 1.0
            else:
                mats[l, t, t + 1:ref + 1] = 1.0
    return mats.reshape(nl * c, c), masks


def _hgrn_body(za_ref, lbt_ref, gn_ref, s0_ref, mats_ref, masks_ref, oa_ref, sout_ref, st_scr,
               *, nb, rows, chunk, heads, dk, dv):
    c_id = pl.program_id(1)
    qk = heads * dk
    wd = heads * dv
    nl = masks_ref.shape[0]

    @pl.when(c_id == 0)
    def _():
        for s in range(nb):
            for h in range(heads):
                st_scr[s, h] = s0_ref[min(s, s0_ref.shape[0] - 1), h].T

    th = lbt_ref[...]
    ex = jnp.exp(th - jnp.max(th, axis=0, keepdims=True))
    lb = ex[0:1] / jnp.sum(ex, axis=0, keepdims=True)
    gn = gn_ref[...]

    for s in range(nb):
        za = za_ref[s]
        zq = za[:, :qk]
        zf = za[:, qk:2 * qk]
        zi = za[:, 2 * qk:2 * qk + wd]
        zo = za[:, 2 * qk + wd:]
        logf = jnp.log(lb + (1.0 - lb) * _sigmoid(zf))
        kin = (1.0 - lb) * _sigmoid(-zf)
        qa = zq * _sigmoid(zq)
        if rows < chunk:
            pad = jnp.zeros((chunk - rows, qk), F32)
            logf = jnp.concatenate([logf, pad], axis=0)
            kin = jnp.concatenate([kin, pad], axis=0)
            qa = jnp.concatenate([qa, pad], axis=0)
            zi = jnp.concatenate([zi, jnp.zeros((chunk - rows, wd), F32)], axis=0)

        dec = _dot_exact_lhs(mats_ref[...], logf)
        for h in range(heads):
            ks = slice(h * dk, (h + 1) * dk)
            vs = slice(h * dv, (h + 1) * dv)
            q = qa[:, ks]
            k = kin[:, ks]
            v = zi[:, vs].astype(BF16)
            cum = dec[0:chunk, ks]
            sc = lax.dot_general(q.astype(BF16), k.astype(BF16), _NT, preferred_element_type=F32)
            scores = jnp.where(masks_ref[0] != 0.0, sc, 0.0)
            for l in range(1, nl):
                e = jnp.exp(dec[l * chunk:(l + 1) * chunk, ks])
                sc = lax.dot_general((q * e).astype(BF16), (k * e).astype(BF16), _NT, preferred_element_type=F32)
                scores = scores + jnp.where(masks_ref[l] != 0.0, sc, 0.0)
            st = st_scr[s, h]
            o = _dot(scores.astype(BF16), v) + lax.dot_general(
                (q * jnp.exp(cum)).astype(BF16), st.astype(BF16), _NT, preferred_element_type=F32)
            last = cum[chunk - 1:chunk, :]
            kd = (k * jnp.exp(last - cum)).astype(BF16)
            st_scr[s, h] = jnp.exp(last) * st + lax.dot_general(v, kd, _TN, preferred_element_type=F32)

            o = o[:rows]
            o = o * lax.rsqrt(jnp.mean(o * o, axis=-1, keepdims=True) + EPS) * gn[:, vs]
            zoh = zo[:, vs]
            oa_ref[s, :, vs] = o * (zoh * _sigmoid(zoh))

    @pl.when(c_id == pl.num_programs(1) - 1)
    def _():
        for s in range(nb):
            for h in range(heads):
                sout_ref[s, h] = st_scr[s, h].T


def _hgrn(za, nb, rows, lb_theta, gn, s0, heads, dk, dv):
    nseq, length, cols = za.shape
    chunk = HGRN_CHUNK if rows > HGRN_SHORT_CHUNK else HGRN_SHORT_CHUNK
    mats, masks = _hgrn_tables(chunk)
    mats = jnp.asarray(np.tile(mats, (1, 3)), BF16)
    masks = jnp.asarray(masks, F32)
    wd = heads * dv
    shared = s0.shape[0] == 1
    s0_spec = (pl.BlockSpec((1, heads, dk, dv), lambda s, c: (0, 0, 0, 0)) if shared
               else pl.BlockSpec((nb, heads, dk, dv), lambda s, c: (s, 0, 0, 0)))
    body = functools.partial(_hgrn_body, nb=nb, rows=rows, chunk=chunk, heads=heads, dk=dk, dv=dv)
    return pl.pallas_call(
        body,
        grid=(nseq // nb, length // rows),
        in_specs=[pl.BlockSpec((nb, rows, cols), lambda s, c: (s, c, 0)),
                  _const_spec(lb_theta.shape), _const_spec(gn.shape), s0_spec,
                  _const_spec(mats.shape), _const_spec(masks.shape)],
        out_specs=[pl.BlockSpec((nb, rows, wd), lambda s, c: (s, c, 0)),
                   pl.BlockSpec((nb, heads, dk, dv), lambda s, c: (s, 0, 0, 0))],
        out_shape=[jax.ShapeDtypeStruct((nseq, length, wd), F32),
                   jax.ShapeDtypeStruct((nseq, heads, dk, dv), F32)],
        scratch_shapes=[pltpu.VMEM((nb, heads, dv, dk), F32)],
        compiler_params=_params("parallel", "arbitrary"),
        name="hgrn",
    )(za, lb_theta, gn, s0, mats, masks)


def _rwkv_body(zb_ref, sh0_ref, s0_ref, mu_ref, w0_ref, ww_ref, a0_ref, aw_ref, gw_ref, kk_ref, ka_ref,
               rk_ref, lnw_ref, lnb_ref, ones_ref, seg_ref,
               ob_ref, sout_ref, shout_ref,
               s_scr, prev_scr, al_scr, w_scr, be_scr, k_scr, r_scr, v_scr, g_scr, y_scr, vt_scr,
               *, nb, tc, hw, hd, lora):
    t_id = pl.program_id(1)
    pairs = hw // LANES
    rin = zb_ref.shape[-1]

    @pl.when(t_id == 0)
    def _():
        s_scr[...] = s0_ref[...]
        prev_scr[...] = sh0_ref[...]

    row = lax.broadcasted_iota(jnp.int32, (tc, rin), 0)
    mu = mu_ref[...]
    zms = []
    for b in range(nb):
        zb = zb_ref[b]
        prev = jnp.where(row == 0, prev_scr[b:b + 1, :], pltpu.roll(zb, 1, axis=0))
        prev_scr[b:b + 1, :] = zb[tc - 1:tc, :]
        zms.append(zb + (prev - zb) * mu)
    zm = jnp.concatenate(zms, axis=0)
    r = zm[:, :hw]
    k = zm[:, hw:2 * hw]
    v = zm[:, 2 * hw:3 * hw]
    dwa = zm[:, 3 * hw:3 * hw + lora]
    dg = zm[:, 3 * hw + lora:]
    w_raw = w0_ref[...] + _dot_hi(jnp.tanh(dwa), ww_ref[...])
    decay = jnp.exp(-np.float32(np.exp(-0.5)) * _sigmoid(w_raw))
    a = _sigmoid(a0_ref[...] + _dot_hi(dwa, aw_ref[...]))
    g = _dot_hi(_sigmoid(dg), gw_ref[...])
    kk = k * kk_ref[...]
    ss = _head_sums(kk * kk, seg_ref[...])
    kk = kk / jnp.maximum(jnp.sqrt(ss), 1e-12)
    k2 = k * (1.0 + (a - 1.0) * ka_ref[...])
    shp = (nb, tc, hw)
    al_scr[...] = (-kk).reshape(shp)
    w_scr[...] = decay.reshape(shp)
    be_scr[...] = (kk * a).reshape(shp)
    k_scr[...] = k2.reshape(shp)
    r_scr[...] = r.reshape(shp)
    v_scr[...] = v.reshape(shp)
    g_scr[...] = g.reshape(shp)

    slot = RWKV_SLOT
    li2 = lax.broadcasted_iota(jnp.int32, (slot, LANES), 1)
    vi = lax.broadcasted_iota(jnp.int32, (hd, LANES), 0)
    sel = jnp.where(vi == lax.broadcasted_iota(jnp.int32, (hd, LANES), 1) % hd, 1.0, 0.0).astype(BF16)
    for b in range(nb):
        for p in range(pairs):
            for g0 in range(0, tc, slot):
                nt = min(slot, tc - g0)
                vb = v[b * tc + g0:b * tc + g0 + nt, p * LANES:(p + 1) * LANES]
                if nt < slot:
                    vb = jnp.concatenate([vb, jnp.zeros((slot - nt, LANES), F32)], axis=0)
                vh = vb.astype(BF16)
                vl = (vb - vh.astype(F32)).astype(BF16)
                zero = jnp.zeros_like(vh)
                stack = jnp.concatenate([jnp.where(li2 < hd, vh, zero), jnp.where(li2 >= hd, vh, zero),
                                         jnp.where(li2 < hd, vl, zero), jnp.where(li2 >= hd, vl, zero)], axis=0)
                vt_scr[g0 // slot, b, p] = lax.dot_general(
                    sel, stack, _NT, preferred_element_type=F32).astype(BF16)

    rows = pairs * hd
    li = lax.broadcasted_iota(jnp.int32, (rows, LANES), 1)
    key_lane = li % hd
    ones1 = ones_ref[...]
    ones_f = ones1.astype(F32)
    qi = lax.broadcasted_iota(jnp.int32, (4 * slot, LANES), 0)
    ql = lax.broadcasted_iota(jnp.int32, (4 * slot, LANES), 1)
    tok_of_row = jnp.where((qi // slot) % 2 == ql // hd, qi % slot, -1)

    def bcast(ref, b, t):
        rowv = ref[b, pl.ds(t, 1), :]
        return jnp.concatenate(
            [jnp.broadcast_to(rowv[:, p * LANES:(p + 1) * LANES], (hd, LANES)) for p in range(pairs)], axis=0)

    def step(t, carry):
        pick = jnp.where(tok_of_row == t % slot, 1.0, 0.0).astype(BF16)
        seqs = range(nb)
        vcols = [_dot(vt_scr[t // slot, b].reshape(rows, 4 * slot), pick) for b in seqs]
        for b in seqs:
            old = s_scr[b].reshape(rows, LANES)
            sa = _dot(old * bcast(al_scr, b, t), ones_f)
            new = old * bcast(w_scr, b, t) + sa * bcast(be_scr, b, t) + vcols[b] * bcast(k_scr, b, t)
            s_scr[b] = new.reshape(pairs, hd, LANES)
            yrep = _dot(new * bcast(r_scr, b, t), ones_f)
            y_scr[b] = jnp.where(key_lane == t, yrep, y_scr[b])
        return carry

    y_scr[...] = jnp.zeros_like(y_scr)
    lax.fori_loop(0, tc, step, 0, unroll=4)

    def token_rows(b, p):
        yt = y_scr[b, p * hd:(p + 1) * hd, :].T
        return jnp.concatenate([yt[0:tc], yt[hd:hd + tc]], axis=1)

    y = jnp.concatenate(
        [jnp.concatenate([token_rows(b, p) for p in range(pairs)], axis=1) for b in range(nb)], axis=0)
    seg = seg_ref[...]
    inv = np.float32(1.0 / hd)
    mean = _head_sums(y, seg) * inv
    yc = y - mean
    var = _head_sums(yc * yc, seg) * inv
    yn = yc * lax.rsqrt(var + GN_EPS) * lnw_ref[...] + lnb_ref[...]
    r = r_scr[...].reshape(nb * tc, hw)
    k2 = k_scr[...].reshape(nb * tc, hw)
    v = v_scr[...].reshape(nb * tc, hw)
    bonus = _head_sums(r * k2 * rk_ref[...], seg) * v
    ob_ref[...] = ((yn + bonus) * g_scr[...].reshape(nb * tc, hw)).reshape(shp)

    @pl.when(t_id == pl.num_programs(1) - 1)
    def _():
        sout_ref[...] = s_scr[...]
        shout_ref[...] = prev_scr[...]


def _rwkv(zb, sh0, s0, wts, nb, tc, hd):
    nseq, length, rin = zb.shape
    hw = wts["w0"].shape[1]
    pairs = hw // LANES
    lora = wts["ww"].shape[0]
    body = functools.partial(_rwkv_body, nb=nb, tc=tc, hw=hw, hd=hd, lora=lora)
    names = ("mu", "w0", "ww", "a0", "aw", "gw", "kk", "ka", "rk", "lnw", "lnb", "ones", "seg")
    consts = [wts[n] for n in names]
    tok = pltpu.VMEM((nb, tc, hw), F32)
    return pl.pallas_call(
        body,
        grid=(nseq // nb, length // tc),
        in_specs=[pl.BlockSpec((nb, tc, rin), lambda s, t: (s, t, 0)),
                  pl.BlockSpec((nb, rin), lambda s, t: (s, 0)),
                  pl.BlockSpec((nb, pairs, hd, LANES), lambda s, t: (s, 0, 0, 0))]
                 + [_const_spec(c.shape) for c in consts],
        out_specs=[pl.BlockSpec((nb, tc, hw), lambda s, t: (s, t, 0)),
                   pl.BlockSpec((nb, pairs, hd, LANES), lambda s, t: (s, 0, 0, 0)),
                   pl.BlockSpec((nb, rin), lambda s, t: (s, 0))],
        out_shape=[jax.ShapeDtypeStruct((nseq, length, hw), F32),
                   jax.ShapeDtypeStruct((nseq, pairs, hd, LANES), F32),
                   jax.ShapeDtypeStruct((nseq, rin), F32)],
        scratch_shapes=[pltpu.VMEM((nb, pairs, hd, LANES), F32), pltpu.VMEM((nb, rin), F32)]
                       + [tok] * 7 + [pltpu.VMEM((nb, pairs * hd, LANES), F32),
                                      pltpu.VMEM((-(-tc // RWKV_SLOT), nb, pairs, hd, 4 * RWKV_SLOT), BF16)],
        compiler_params=_params("parallel", "arbitrary"),
        name="rwkv",
    )(zb, sh0, s0, *consts)


def _pack_pairs(s):
    n, heads, hd, _ = s.shape
    return s.reshape(n, heads // 2, 2, hd, hd).transpose(0, 1, 3, 2, 4).reshape(n, heads // 2, hd, 2 * hd)


def _unpack_pairs(s):
    n, pairs, hd, _ = s.shape
    return s.reshape(n, pairs, hd, 2, hd).transpose(0, 1, 3, 2, 4).reshape(n, 2 * pairs, hd, hd)


def _out_proj_body(oa_ref, ob_ref, x_ref, woa_ref, wob_ref, g2_ref, wq_ref, keys_ref,
                   x1_ref, h2t_ref, st_ref, *, nk, dh):
    mix = _dot(oa_ref[...].astype(BF16), woa_ref[...]) + _dot(ob_ref[...].astype(BF16), wob_ref[...])
    x1 = x_ref[...] + mix
    x1_ref[...] = x1
    h2f = _rms(x1, g2_ref[...])
    h2t_ref[...] = h2f.T.astype(BF16)
    q = _dot(h2f.astype(BF16), wq_ref[...]).astype(BF16)
    for hp in range(q.shape[1] // dh):
        st_ref[hp * nk:(hp + 1) * nk, :] = lax.dot_general(
            keys_ref[hp % 2], q[:, hp * dh:(hp + 1) * dh], _NT, preferred_element_type=F32)


def _out_proj(oa, ob, x, woa, wob, g2, wq, keys, tb):
    n, d = x.shape
    nk, dh = keys.shape[1], keys.shape[2]
    nrow = (wq.shape[1] // dh) * nk
    body = functools.partial(_out_proj_body, nk=nk, dh=dh)
    return pl.pallas_call(
        body,
        grid=(n // tb,),
        in_specs=[pl.BlockSpec((tb, oa.shape[1]), lambda i: (i, 0)),
                  pl.BlockSpec((tb, ob.shape[1]), lambda i: (i, 0)),
                  pl.BlockSpec((tb, d), lambda i: (i, 0)),
                  _const_spec(woa.shape), _const_spec(wob.shape), _const_spec(g2.shape),
                  _const_spec(wq.shape), _const_spec(keys.shape)],
        out_specs=[pl.BlockSpec((tb, d), lambda i: (i, 0)), pl.BlockSpec((d, tb), lambda i: (0, i)),
                   pl.BlockSpec((nrow, tb), lambda i: (0, i))],
        out_shape=[jax.ShapeDtypeStruct((n, d), F32), jax.ShapeDtypeStruct((d, n), BF16),
                   jax.ShapeDtypeStruct((nrow, n), F32)],
        compiler_params=_params("parallel"),
        name="out_proj",
    )(oa, ob, x, woa, wob, g2, wq, keys)


def _sort16_network():
    n, pairs, p = 16, [], 1
    while p < n:
        k = p
        while k >= 1:
            for j in range(k % p, n - k, 2 * k):
                for i in range(min(k, n - j - k)):
                    if (i + j) // (2 * p) == (i + j + k) // (2 * p):
                        pairs.append((i + j, i + j + k))
            k //= 2
        p *= 2
    return pairs


def _top16_desc(tiles):
    t = list(tiles)

    def exchange(i, j):
        t[i], t[j] = jnp.maximum(t[i], t[j]), jnp.minimum(t[i], t[j])

    for i, j in _sort16_network():
        exchange(i, j)
    for shift in (4, 2, 1):
        t = [jnp.maximum(t[i], pltpu.roll(t[15 - i], shift, axis=0)) for i in range(16)]
        for d in (8, 4, 2, 1):
            for i in range(16):
                if not i & d:
                    exchange(i, i + d)
    return t


def _route_body(st_ref, t1_ref, e1_ref, e2_ref, top_scr, cand_scr, *, nk, topk):
    neg = -jnp.inf
    nx = topk + 1
    half = topk // 2

    def top_values(h, s):
        tiles = [s[i * SUBLANES:(i + 1) * SUBLANES] for i in range(nk // SUBLANES)]
        srt = _top16_desc(tiles)
        for i in range(topk):
            top_scr[h, i:i + 1, :] = srt[i][0:1]
        top_scr[h, topk:nx, :] = jnp.max(jnp.where(s < srt[topk - 1][0:1], s, neg), axis=0, keepdims=True)

    s1 = st_ref[0:nk, :]
    s2 = st_ref[nk:2 * nk, :]
    top_values(0, s1)
    top_values(1, s2)
    cand_scr[...] = jnp.full(cand_scr.shape, neg, F32)
    sv2 = top_scr[1, 0:nx, :]
    cand_scr[0:nx, :] = top_scr[0, 0:1, :] + sv2
    base = 3 * SUBLANES
    for a in range(1, half):
        cand_scr[base + (a - 1) * half:base + a * half, :] = top_scr[0, a:a + 1, :] + sv2[0:half]
    base += (half - 1) * half
    cand_scr[base:base + nx - half, :] = top_scr[0, half:nx, :] + sv2[0:1]

    cand = cand_scr[...]
    best = _top16_desc([cand[i * SUBLANES:(i + 1) * SUBLANES] for i in range(nk // SUBLANES)])
    z = jnp.zeros_like(best[0][0:1])
    for i in range(topk):
        z = z + jnp.exp(best[i][0:1] - best[0][0:1])
    hi = best[topk - 1][0:1]
    lo = jnp.max(jnp.where(cand < hi, cand, neg), axis=0, keepdims=True)
    thr = jnp.where(lo == neg, hi, 0.5 * (hi + lo))
    t1_ref[0] = thr - s1
    e1_ref[0] = jnp.exp(s1 - top_scr[0, 0:1, :]) * (np.float32(np.sqrt(0.5)) / z)
    e2_ref[0] = jnp.exp(s2 - sv2[0:1])


def _route(st, heads, nk, tb):
    n = st.shape[1]
    topk = PEER_TOPK
    half = topk // 2
    ncand = nk
    assert topk == 16 and nk == 16 * SUBLANES and half == SUBLANES
    assert 3 * SUBLANES + (half - 1) * half + topk + 1 - half <= ncand
    body = functools.partial(_route_body, nk=nk, topk=topk)
    blk = pl.BlockSpec((1, nk, tb), lambda i, h: (h, 0, i))
    return pl.pallas_call(
        body,
        grid=(n // tb, heads),
        in_specs=[pl.BlockSpec((2 * nk, tb), lambda i, h: (h, i))],
        out_specs=[blk, blk, blk],
        out_shape=[jax.ShapeDtypeStruct((heads, nk, n), F32)] * 3,
        scratch_shapes=[pltpu.VMEM((2, 3 * SUBLANES, tb), F32), pltpu.VMEM((ncand, tb), F32)],
        compiler_params=_params("parallel", "parallel"),
        name="route",
    )(st)


PEER_SUB = 2 * LANES
PEER_ROW_SPLIT = 2


def _peer_body(h2t_ref, s2_ref, t1_ref, e1_ref, e2_ref, u_ref, vt_ref, x1_ref, gf_ref, y_ref,
               acc_scr, coef_scr, *, heads, nk, tb, n_tb):
    i = pl.program_id(0)
    e = pl.program_id(1)
    eblk = 2 * u_ref.shape[0]
    i1_per_block = eblk // nk
    i1_per_sub = PEER_SUB // nk
    cur = e % 2

    @pl.when((i == 0) & (e == 0))
    def _():
        acc_scr[...] = jnp.zeros_like(acc_scr)
        coef_scr[1] = jnp.zeros(coef_scr.shape[1:], BF16)

    @pl.when((i < n_tb) | (e == 0))
    def _():
        i1_0 = pl.multiple_of(e * i1_per_block, SUBLANES)
        n_sub = eblk // PEER_SUB
        h2t = h2t_ref[...]

        def matmuls(k):
            ks = slice(k * PEER_SUB, (k + 1) * PEER_SUB)
            u_rows = pltpu.bitcast(u_ref[k * PEER_SUB // 2:(k + 1) * PEER_SUB // 2, :], BF16)
            vt_cols = pltpu.bitcast(vt_ref[:, ks], BF16)
            return _dot(u_rows, h2t), _dot(vt_cols, coef_scr[1 - cur, ks, :])

        act, out = matmuls(0)
        for k in range(n_sub):
            if k + 1 < n_sub:
                act_next, part = matmuls(k + 1)
                out = out + part
            for lg in range(tb // LANES):
                ls = slice(lg * LANES, (lg + 1) * LANES)
                t1b = [t1_ref[h, pl.ds(i1_0, i1_per_block), ls] for h in range(heads)]
                e1b = [e1_ref[h, pl.ds(i1_0, i1_per_block), ls] for h in range(heads)]
                for rs in range(PEER_ROW_SPLIT):
                    rsl = slice(rs * nk // PEER_ROW_SPLIT, (rs + 1) * nk // PEER_ROW_SPLIT)
                    gates = [None] * i1_per_sub
                    for h in range(heads):
                        s2 = s2_ref[h, rsl, ls]
                        e2 = e2_ref[h, rsl, ls]
                        for jj in range(i1_per_sub):
                            j = k * i1_per_sub + jj
                            term = jnp.where(s2 >= t1b[h][j:j + 1], e1b[h][j:j + 1] * e2, 0.0)
                            gates[jj] = term if gates[jj] is None else gates[jj] + term
                    for jj in range(i1_per_sub):
                        a = act[jj * nk + rsl.start:jj * nk + rsl.stop, ls]
                        r0 = (k * i1_per_sub + jj) * nk
                        coef_scr[cur, r0 + rsl.start:r0 + rsl.stop, ls] = (gates[jj] * (a + a * lax.erf(a))).astype(BF16)
            if k + 1 < n_sub:
                act = act_next
        acc_scr[...] += out

    @pl.when(e == 0)
    def _():
        @pl.when(i > 0)
        def _():
            y_ref[...] = _rms(x1_ref[...] + acc_scr[...].T, gf_ref[...])

        acc_scr[...] = jnp.zeros_like(acc_scr)


def _peer(h2t, st, t1, e1, e2, u, vt, x1, gf, heads, nk, tb):
    n, d = x1.shape
    ne = 2 * u.shape[0]
    eblk = SUBLANES * nk
    n_eb = ne // eblk
    n_tb = n // tb
    assert n_eb % 2 == 0
    body = functools.partial(_peer_body, heads=heads, nk=nk, tb=tb, n_tb=n_tb)
    this = lambda i: jnp.minimum(i, n_tb - 1)
    done = lambda i: jnp.maximum(i - 1, 0)
    tok = pl.BlockSpec((heads, nk, tb), lambda i, e: (0, 0, this(i)))
    return pl.pallas_call(
        body,
        grid=(n_tb + 1, n_eb),
        in_specs=[pl.BlockSpec((d, tb), lambda i, e: (0, this(i))),
                  pl.BlockSpec((heads, None, nk, tb), lambda i, e: (0, 1, 0, this(i))),
                  tok, tok, tok,
                  pl.BlockSpec((eblk // 2, d), lambda i, e: (jnp.where(i < n_tb, e, 0), 0)),
                  pl.BlockSpec((d // 2, eblk), lambda i, e: (0, jnp.where(i < n_tb, (e + n_eb - 1) % n_eb, n_eb - 1))),
                  pl.BlockSpec((tb, d), lambda i, e: (done(i), 0)),
                  _const_spec(gf.shape)],
        out_specs=pl.BlockSpec((tb, d), lambda i, e: (done(i), 0)),
        out_shape=jax.ShapeDtypeStruct((n, d), F32),
        scratch_shapes=[pltpu.VMEM((d, tb), F32), pltpu.VMEM((2, eblk, tb), BF16)],
        compiler_params=_params("arbitrary", "arbitrary"),
        name="peer",
    )(h2t, st.reshape(heads, 2, nk, n), t1, e1, e2, u, vt, x1, gf)


def _pack_body(x_ref, o_ref, *, transpose, scale):
    x = x_ref[...] * np.float32(scale)
    if transpose:
        x = x.T
    o_ref[...] = pltpu.bitcast(x.astype(BF16), jnp.uint32)


def _pack_table(x, transpose, scale=1.0, rows=512):
    n, c = x.shape
    if transpose:
        out_shape, out_spec = (c // 2, n), pl.BlockSpec((c // 2, rows), lambda i: (0, i))
    else:
        out_shape, out_spec = (n // 2, c), pl.BlockSpec((rows // 2, c), lambda i: (i, 0))
    return pl.pallas_call(
        functools.partial(_pack_body, transpose=transpose, scale=scale),
        grid=(n // rows,),
        in_specs=[pl.BlockSpec((rows, c), lambda i: (i, 0))],
        out_specs=out_spec,
        out_shape=jax.ShapeDtypeStruct(out_shape, jnp.uint32),
        compiler_params=_params("parallel"),
        name="pack_table",
    )(x)


def _block(n, pref):
    b = pref
    while n % b:
        b //= 2
    return b


def kernel(x_prompt, x_sample, state_hgrn, state_rwkv, state_shift, meta_tokens, norm1, w_in, lb_theta, hgrn_norm, mu_shift, w0, w_w2, a0, a_w2, g_w2, k_k, k_a, r_k, ln_w, ln_b, w_out, norm2, peer_wq, peer_keys, peer_u, peer_v, norm_f):
    assert w_in.shape[0] == 1, "single-layer trunk"
    bp, sp, d = x_prompt.shape
    bs, ss, _ = x_sample.shape
    n_meta = meta_tokens.shape[0]
    _, _, ha_heads, dk, dv = state_hgrn.shape
    _, _, hb_heads, hd, _ = state_rwkv.shape
    ha_in = 2 * ha_heads * dk + 2 * ha_heads * dv
    hw = hb_heads * hd
    lw, la = w_w2.shape[1], a_w2.shape[1]
    nk, dh = peer_keys.shape[2], peer_keys.shape[3]
    p_heads = peer_wq.shape[2] // (2 * dh)
    assert sp % HGRN_CHUNK == 0 and ss <= HGRN_CHUNK and n_meta <= HGRN_CHUNK
    assert hb_heads % 2 == 0 and 2 * hd == LANES and lw + la == LANES and nk == LANES

    row = lambda t: t.reshape(1, -1).astype(F32)
    w_a = w_in[0, :, :ha_in].astype(BF16)
    w_b = w_in[0, :, ha_in:].astype(BF16)
    g1 = row(norm1[0])
    lbt = lb_theta.astype(F32)
    gn = row(hgrn_norm[0])

    head_of_lane = np.arange(LANES) // hd
    rw = {
        "mu": row(mu_shift[0]), "w0": row(w0[0]), "a0": row(a0[0]), "kk": row(k_k[0]), "ka": row(k_a[0]),
        "rk": row(r_k[0]), "lnw": row(ln_w[0]), "lnb": row(ln_b[0]), "gw": g_w2[0].astype(F32),
        "ww": jnp.concatenate([w_w2[0], jnp.zeros((la, hw), F32)], axis=0),
        "aw": jnp.concatenate([jnp.zeros((lw, hw), F32), a_w2[0]], axis=0),
        "ones": jnp.asarray(head_of_lane[:, None] == head_of_lane[None, :], BF16),
        "seg": jnp.asarray(np.tile(head_of_lane[:, None] == head_of_lane[None, :], (2, 1)), BF16),
    }
    woa = w_out[0, :ha_heads * dv].astype(BF16)
    wob = w_out[0, ha_heads * dv:].astype(BF16)
    g2 = row(norm2[0])
    wq = peer_wq[0].astype(BF16)
    keys = peer_keys[0].astype(BF16)
    u = _pack_table(peer_u[0].astype(F32), False, scale=np.sqrt(0.5))
    vt = _pack_table(peer_v[0].astype(F32), True)
    gf = row(norm_f)

    za_m, zb_m = _in_proj(meta_tokens.astype(F32), g1, w_a, w_b, n_meta)
    _, hs_m = _hgrn(za_m[None], 1, n_meta, lbt, gn, jnp.zeros((1, ha_heads, dk, dv), F32), ha_heads, dk, dv)
    _, rs_m, sh_m = _rwkv(zb_m.reshape(1, n_meta, -1), jnp.zeros((1, zb_m.shape[1]), F32),
                          jnp.zeros((1, hb_heads // 2, hd, LANES), F32), rw, 1, n_meta, hd)

    def group(x, nseq, length, hs0, rs0, sh0, chunk_rows, hgrn_nb, tc):
        xf = x.reshape(nseq * length, d)
        n = xf.shape[0]
        tb = _block(n, 256)
        assert tb % LANES == 0
        za, zb = _in_proj(xf, g1, w_a, w_b, tb)
        oa, hs = _hgrn(za.reshape(nseq, length, -1), hgrn_nb, chunk_rows, lbt, gn, hs0, ha_heads, dk, dv)
        ob, rs, sh = _rwkv(zb.reshape(nseq, length, -1), sh0, rs0, rw, 8, tc, hd)
        x1, h2t, st = _out_proj(oa.reshape(n, -1), ob.reshape(n, hw), xf, woa, wob, g2, wq, keys, tb)
        t1, e1, e2 = _route(st, p_heads, nk, _block(n, 512))
        y = _peer(h2t, st, t1, e1, e2, u, vt, x1, gf, p_heads, nk, tb)
        return y.reshape(nseq, length, d), hs, _unpack_pairs(rs), sh

    y_p, hs_p, rs_p, sh_p = group(
        x_prompt, bp, sp, hs_m, jnp.broadcast_to(rs_m, (bp,) + rs_m.shape[1:]),
        jnp.broadcast_to(sh_m, (bp, sh_m.shape[1])), HGRN_CHUNK, 8, 64)
    y_s, hs_s, rs_s, sh_s = group(
        x_sample, bs, ss, state_hgrn[0].astype(F32), _pack_pairs(state_rwkv[0].astype(F32)),
        state_shift[0].astype(F32), ss, 8, ss)

    return (y_p, y_s, hs_p[None], rs_p[None], sh_p[None], hs_s[None], rs_s[None], sh_s[None])
```

```python
import functools

import numpy as np
import jax
import jax.numpy as jnp
from jax import lax
from jax.experimental import pallas as pl
from jax.experimental.pallas import tpu as pltpu

F32 = jnp.float32
BF16 = jnp.bfloat16

EPS = 1e-6
GN_EPS = 64e-5
HGRN_CHUNK = 64
HGRN_SHORT_CHUNK = 16
PEER_TOPK = 16
RWKV_SLOT = 32
LANES = 128
SUBLANES = 8
VMEM_LIMIT_BYTES = 52 * 1024 * 1024

_NT = (((1,), (1,)), ((), ()))
_TN = (((0,), (0,)), ((), ()))


def _dot(a, b):
    return jnp.dot(a, b, preferred_element_type=F32)


def _split3(a):
    a1 = a.astype(BF16)
    r1 = a - a1.astype(F32)
    a2 = r1.astype(BF16)
    a3 = (r1 - a2.astype(F32)).astype(BF16)
    return a1, a2, a3


def _head_sums(x, ones2):
    out = []
    for c0 in range(0, x.shape[1], LANES):
        xs = x[:, c0:c0 + LANES]
        hi = xs.astype(BF16)
        lo = (xs - hi.astype(F32)).astype(BF16)
        out.append(_dot(jnp.concatenate([hi, lo], axis=1), ones2))
    return jnp.concatenate(out, axis=1)


def _dot_exact_lhs(a_exact3, b):
    return _dot(a_exact3, jnp.concatenate(_split3(b), axis=0))


def _dot_hi(a, b):
    ah = a.astype(BF16)
    al = (a - ah.astype(F32)).astype(BF16)
    bh = b.astype(BF16)
    bl = (b - bh.astype(F32)).astype(BF16)
    return _dot(ah, bh) + _dot(ah, bl) + _dot(al, bh)


def _sigmoid(x):
    return 1.0 / (1.0 + jnp.exp(-x))


def _rms(x, g):
    return x * lax.rsqrt(jnp.mean(x * x, axis=-1, keepdims=True) + EPS) * g


def _params(*sem):
    return pltpu.CompilerParams(dimension_semantics=sem, vmem_limit_bytes=VMEM_LIMIT_BYTES)


def _const_spec(shape):
    nd = len(shape)
    return pl.BlockSpec(shape, lambda *_: (0,) * nd)


def _in_proj_body(x_ref, g_ref, wa_ref, wb_ref, za_ref, zb_ref):
    hb = _rms(x_ref[...], g_ref[...]).astype(BF16)
    za_ref[...] = _dot(hb, wa_ref[...])
    zb_ref[...] = _dot(hb, wb_ref[...])


def _in_proj(x, g, wa, wb, tb):
    n, d = x.shape
    na, nb = wa.shape[1], wb.shape[1]
    return pl.pallas_call(
        _in_proj_body,
        grid=(n // tb,),
        in_specs=[pl.BlockSpec((tb, d), lambda i: (i, 0)), _const_spec((1, d)),
                  _const_spec(wa.shape), _const_spec(wb.shape)],
        out_specs=[pl.BlockSpec((tb, na), lambda i: (i, 0)), pl.BlockSpec((tb, nb), lambda i: (i, 0))],
        out_shape=[jax.ShapeDtypeStruct((n, na), F32), jax.ShapeDtypeStruct((n, nb), F32)],
        compiler_params=_params("parallel"),
        name="in_proj",
    )(x, g, wa, wb)


def _hgrn_tables(c):
    nl = int(np.log2(c)) + 1
    mats = np.zeros((nl, c, c), np.float32)
    masks = np.zeros((nl, c, c), np.float32)
    idx = np.arange(c)
    mats[0] = (idx[None, :] <= idx[:, None])
    masks[0] = np.eye(c)
    for l in range(1, nl):
        blk = 1 << l
        half = blk >> 1
        for t in range(c):
            start = (t // blk) * blk
            ref = start + half - 1
            if t - start >= half:
                mats[l, t, ref + 1:t + 1] = 1.0
                masks[l, t, start:start + half] = 1.0
            else:
                mats[l, t, t + 1:ref + 1] = 1.0
    return mats.reshape(nl * c, c), masks


def _hgrn_body(za_ref, lbt_ref, gn_ref, s0_ref, mats_ref, masks_ref, oa_ref, sout_ref, st_scr,
               *, nb, rows, chunk, heads, dk, dv):
    c_id = pl.program_id(1)
    qk = heads * dk
    wd = heads * dv
    nl = masks_ref.shape[0]

    @pl.when(c_id == 0)
    def _():
        for s in range(nb):
            for h in range(heads):
                st_scr[s, h] = s0_ref[min(s, s0_ref.shape[0] - 1), h].T

    th = lbt_ref[...]
    ex = jnp.exp(th - jnp.max(th, axis=0, keepdims=True))
    lb = ex[0:1] / jnp.sum(ex, axis=0, keepdims=True)
    gn = gn_ref[...]

    for s in range(nb):
        za = za_ref[s]
        zq = za[:, :qk]
        zf = za[:, qk:2 * qk]
        zi = za[:, 2 * qk:2 * qk + wd]
        zo = za[:, 2 * qk + wd:]
        logf = jnp.log(lb + (1.0 - lb) * _sigmoid(zf))
        kin = (1.0 - lb) * _sigmoid(-zf)
        qa = zq * _sigmoid(zq)
        if rows < chunk:
            pad = jnp.zeros((chunk - rows, qk), F32)
            logf = jnp.concatenate([logf, pad], axis=0)
            kin = jnp.concatenate([kin, pad], axis=0)
            qa = jnp.concatenate([qa, pad], axis=0)
            zi = jnp.concatenate([zi, jnp.zeros((chunk - rows, wd), F32)], axis=0)

        dec = _dot_exact_lhs(mats_ref[...], logf)
        for h in range(heads):
            ks = slice(h * dk, (h + 1) * dk)
            vs = slice(h * dv, (h + 1) * dv)
            q = qa[:, ks]
            k = kin[:, ks]
            v = zi[:, vs].astype(BF16)
            cum = dec[0:chunk, ks]
            sc = lax.dot_general(q.astype(BF16), k.astype(BF16), _NT, preferred_element_type=F32)
            scores = jnp.where(masks_ref[0] != 0.0, sc, 0.0)
            for l in range(1, nl):
                e = jnp.exp(dec[l * chunk:(l + 1) * chunk, ks])
                sc = lax.dot_general((q * e).astype(BF16), (k * e).astype(BF16), _NT, preferred_element_type=F32)
                scores = scores + jnp.where(masks_ref[l] != 0.0, sc, 0.0)
            st = st_scr[s, h]
            o = _dot(scores.astype(BF16), v) + lax.dot_general(
                (q * jnp.exp(cum)).astype(BF16), st.astype(BF16), _NT, preferred_element_type=F32)
            last = cum[chunk - 1:chunk, :]
            kd = (k * jnp.exp(last - cum)).astype(BF16)
            st_scr[s, h] = jnp.exp(last) * st + lax.dot_general(v, kd, _TN, preferred_element_type=F32)

            o = o[:rows]
            o = o * lax.rsqrt(jnp.mean(o * o, axis=-1, keepdims=True) + EPS) * gn[:, vs]
            zoh = zo[:, vs]
            oa_ref[s, :, vs] = o * (zoh * _sigmoid(zoh))

    @pl.when(c_id == pl.num_programs(1) - 1)
    def _():
        for s in range(nb):
            for h in range(heads):
                sout_ref[s, h] = st_scr[s, h].T


def _hgrn(za, nb, rows, lb_theta, gn, s0, heads, dk, dv):
    nseq, length, cols = za.shape
    chunk = HGRN_CHUNK if rows > HGRN_SHORT_CHUNK else HGRN_SHORT_CHUNK
    mats, masks = _hgrn_tables(chunk)
    mats = jnp.asarray(np.tile(mats, (1, 3)), BF16)
    masks = jnp.asarray(masks, F32)
    wd = heads * dv
    shared = s0.shape[0] == 1
    s0_spec = (pl.BlockSpec((1, heads, dk, dv), lambda s, c: (0, 0, 0, 0)) if shared
               else pl.BlockSpec((nb, heads, dk, dv), lambda s, c: (s, 0, 0, 0)))
    body = functools.partial(_hgrn_body, nb=nb, rows=rows, chunk=chunk, heads=heads, dk=dk, dv=dv)
    return pl.pallas_call(
        body,
        grid=(nseq // nb, length // rows),
        in_specs=[pl.BlockSpec((nb, rows, cols), lambda s, c: (s, c, 0)),
                  _const_spec(lb_theta.shape), _const_spec(gn.shape), s0_spec,
                  _const_spec(mats.shape), _const_spec(masks.shape)],
        out_specs=[pl.BlockSpec((nb, rows, wd), lambda s, c: (s, c, 0)),
                   pl.BlockSpec((nb, heads, dk, dv), lambda s, c: (s, 0, 0, 0))],
        out_shape=[jax.ShapeDtypeStruct((nseq, length, wd), F32),
                   jax.ShapeDtypeStruct((nseq, heads, dk, dv), F32)],
        scratch_shapes=[pltpu.VMEM((nb, heads, dv, dk), F32)],
        compiler_params=_params("parallel", "arbitrary"),
        name="hgrn",
    )(za, lb_theta, gn, s0, mats, masks)


def _rwkv_body(zb_ref, sh0_ref, s0_ref, mu_ref, w0_ref, ww_ref, a0_ref, aw_ref, gw_ref, kk_ref, ka_ref,
               rk_ref, lnw_ref, lnb_ref, ones_ref, seg_ref,
               ob_ref, sout_ref, shout_ref,
               s_scr, prev_scr, al_scr, w_scr, be_scr, k_scr, r_scr, v_scr, g_scr, y_scr, vt_scr,
               *, nb, tc, hw, hd, lora):
    t_id = pl.program_id(1)
    pairs = hw // LANES
    rin = zb_ref.shape[-1]

    @pl.when(t_id == 0)
    def _():
        for b in range(nb):
            for p in range(pairs):
                s_scr[b, p] = jnp.concatenate([s0_ref[b, 2 * p], s0_ref[b, 2 * p + 1]], axis=1)
        prev_scr[...] = sh0_ref[...]

    row = lax.broadcasted_iota(jnp.int32, (tc, rin), 0)
    mu = mu_ref[...]
    zms = []
    for b in range(nb):
        zb = zb_ref[b]
        prev = jnp.where(row == 0, prev_scr[b:b + 1, :], pltpu.roll(zb, 1, axis=0))
        prev_scr[b:b + 1, :] = zb[tc - 1:tc, :]
        zms.append(zb + (prev - zb) * mu)
    zm = jnp.concatenate(zms, axis=0)
    r = zm[:, :hw]
    k = zm[:, hw:2 * hw]
    v = zm[:, 2 * hw:3 * hw]
    dwa = zm[:, 3 * hw:3 * hw + lora]
    dg = zm[:, 3 * hw + lora:]
    w_raw = w0_ref[...] + _dot_hi(jnp.tanh(dwa), ww_ref[...])
    decay = jnp.exp(-np.float32(np.exp(-0.5)) * _sigmoid(w_raw))
    a = _sigmoid(a0_ref[...] + _dot_hi(dwa, aw_ref[...]))
    g = _dot_hi(_sigmoid(dg), gw_ref[...])
    kk = k * kk_ref[...]
    ss = _head_sums(kk * kk, seg_ref[...])
    kk = kk / jnp.maximum(jnp.sqrt(ss), 1e-12)
    k2 = k * (1.0 + (a - 1.0) * ka_ref[...])
    shp = (nb, tc, hw)
    al_scr[...] = (-kk).reshape(shp)
    w_scr[...] = decay.reshape(shp)
    be_scr[...] = (kk * a).reshape(shp)
    k_scr[...] = k2.reshape(shp)
    r_scr[...] = r.reshape(shp)
    v_scr[...] = v.reshape(shp)
    g_scr[...] = g.reshape(shp)

    slot = RWKV_SLOT
    li2 = lax.broadcasted_iota(jnp.int32, (slot, LANES), 1)
    vi = lax.broadcasted_iota(jnp.int32, (hd, LANES), 0)
    sel = jnp.where(vi == lax.broadcasted_iota(jnp.int32, (hd, LANES), 1) % hd, 1.0, 0.0).astype(BF16)
    for b in range(nb):
        for p in range(pairs):
            for g0 in range(0, tc, slot):
                nt = min(slot, tc - g0)
                vb = v[b * tc + g0:b * tc + g0 + nt, p * LANES:(p + 1) * LANES]
                if nt < slot:
                    vb = jnp.concatenate([vb, jnp.zeros((slot - nt, LANES), F32)], axis=0)
                vh = vb.astype(BF16)
                vl = (vb - vh.astype(F32)).astype(BF16)
                zero = jnp.zeros_like(vh)
                stack = jnp.concatenate([jnp.where(li2 < hd, vh, zero), jnp.where(li2 >= hd, vh, zero),
                                         jnp.where(li2 < hd, vl, zero), jnp.where(li2 >= hd, vl, zero)], axis=0)
                vt_scr[g0 // slot, b, p] = lax.dot_general(
                    sel, stack, _NT, preferred_element_type=F32).astype(BF16)

    rows = pairs * hd
    li = lax.broadcasted_iota(jnp.int32, (rows, LANES), 1)
    key_lane = li % hd
    ones1 = ones_ref[...]
    ones_f = ones1.astype(F32)
    qi = lax.broadcasted_iota(jnp.int32, (4 * slot, LANES), 0)
    ql = lax.broadcasted_iota(jnp.int32, (4 * slot, LANES), 1)
    tok_of_row = jnp.where((qi // slot) % 2 == ql // hd, qi % slot, -1)

    def bcast(ref, b, t):
        rowv = ref[b, pl.ds(t, 1), :]
        return jnp.concatenate(
            [jnp.broadcast_to(rowv[:, p * LANES:(p + 1) * LANES], (hd, LANES)) for p in range(pairs)], axis=0)

    def step(t, carry):
        pick = jnp.where(tok_of_row == t % slot, 1.0, 0.0).astype(BF16)
        seqs = range(nb)
        vcols = [_dot(vt_scr[t // slot, b].reshape(rows, 4 * slot), pick) for b in seqs]
        for b in seqs:
            old = s_scr[b].reshape(rows, LANES)
            sa = _dot(old * bcast(al_scr, b, t), ones_f)
            new = old * bcast(w_scr, b, t) + sa * bcast(be_scr, b, t) + vcols[b] * bcast(k_scr, b, t)
            s_scr[b] = new.reshape(pairs, hd, LANES)
            yrep = _dot(new * bcast(r_scr, b, t), ones_f)
            y_scr[b] = jnp.where(key_lane == t, yrep, y_scr[b])
        return carry

    y_scr[...] = jnp.zeros_like(y_scr)
    lax.fori_loop(0, tc, step, 0, unroll=4)

    def token_rows(b, p):
        yt = y_scr[b, p * hd:(p + 1) * hd, :].T
        return jnp.concatenate([yt[0:tc], yt[hd:hd + tc]], axis=1)

    y = jnp.concatenate(
        [jnp.concatenate([token_rows(b, p) for p in range(pairs)], axis=1) for b in range(nb)], axis=0)
    seg = seg_ref[...]
    inv = np.float32(1.0 / hd)
    mean = _head_sums(y, seg) * inv
    yc = y - mean
    var = _head_sums(yc * yc, seg) * inv
    yn = yc * lax.rsqrt(var + GN_EPS) * lnw_ref[...] + lnb_ref[...]
    r = r_scr[...].reshape(nb * tc, hw)
    k2 = k_scr[...].reshape(nb * tc, hw)
    v = v_scr[...].reshape(nb * tc, hw)
    bonus = _head_sums(r * k2 * rk_ref[...], seg) * v
    ob_ref[...] = ((yn + bonus) * g_scr[...].reshape(nb * tc, hw)).reshape(shp)

    @pl.when(t_id == pl.num_programs(1) - 1)
    def _():
        for b in range(nb):
            for p in range(pairs):
                sout_ref[b, 2 * p] = s_scr[b, p, :, 0:hd]
                sout_ref[b, 2 * p + 1] = s_scr[b, p, :, hd:2 * hd]
        shout_ref[...] = prev_scr[...]


def _rwkv(zb, sh0, s0, wts, nb, tc, hd):
    nseq, length, rin = zb.shape
    hw = wts["w0"].shape[1]
    pairs = hw // LANES
    lora = wts["ww"].shape[0]
    body = functools.partial(_rwkv_body, nb=nb, tc=tc, hw=hw, hd=hd, lora=lora)
    names = ("mu", "w0", "ww", "a0", "aw", "gw", "kk", "ka", "rk", "lnw", "lnb", "ones", "seg")
    consts = [wts[n] for n in names]
    tok = pltpu.VMEM((nb, tc, hw), F32)
    return pl.pallas_call(
        body,
        grid=(nseq // nb, length // tc),
        in_specs=[pl.BlockSpec((nb, tc, rin), lambda s, t: (s, t, 0)),
                  pl.BlockSpec((nb, rin), lambda s, t: (s, 0)),
                  pl.BlockSpec((nb, 2 * pairs, hd, hd), lambda s, t: (s, 0, 0, 0))]
                 + [_const_spec(c.shape) for c in consts],
        out_specs=[pl.BlockSpec((nb, tc, hw), lambda s, t: (s, t, 0)),
                   pl.BlockSpec((nb, 2 * pairs, hd, hd), lambda s, t: (s, 0, 0, 0)),
                   pl.BlockSpec((nb, rin), lambda s, t: (s, 0))],
        out_shape=[jax.ShapeDtypeStruct((nseq, length, hw), F32),
                   jax.ShapeDtypeStruct((nseq, 2 * pairs, hd, hd), F32),
                   jax.ShapeDtypeStruct((nseq, rin), F32)],
        scratch_shapes=[pltpu.VMEM((nb, pairs, hd, LANES), F32), pltpu.VMEM((nb, rin), F32)]
                       + [tok] * 7 + [pltpu.VMEM((nb, pairs * hd, LANES), F32),
                                      pltpu.VMEM((-(-tc // RWKV_SLOT), nb, pairs, hd, 4 * RWKV_SLOT), BF16)],
        compiler_params=_params("parallel", "arbitrary"),
        name="rwkv",
    )(zb, sh0, s0, *consts)


def _out_proj_body(oa_ref, ob_ref, x_ref, woa_ref, wob_ref, g2_ref, wq_ref, keys_ref,
                   x1_ref, h2t_ref, st_ref, *, nk, dh):
    mix = _dot(oa_ref[...].astype(BF16), woa_ref[...]) + _dot(ob_ref[...].astype(BF16), wob_ref[...])
    x1 = x_ref[...] + mix
    x1_ref[...] = x1
    h2f = _rms(x1, g2_ref[...])
    h2t_ref[...] = h2f.T.astype(BF16)
    q = _dot(h2f.astype(BF16), wq_ref[...]).astype(BF16)
    for hp in range(q.shape[1] // dh):
        st_ref[hp * nk:(hp + 1) * nk, :] = lax.dot_general(
            keys_ref[hp % 2], q[:, hp * dh:(hp + 1) * dh], _NT, preferred_element_type=F32)


def _out_proj(oa, ob, x, woa, wob, g2, wq, keys, tb):
    n, d = x.shape
    nk, dh = keys.shape[1], keys.shape[2]
    nrow = (wq.shape[1] // dh) * nk
    body = functools.partial(_out_proj_body, nk=nk, dh=dh)
    return pl.pallas_call(
        body,
        grid=(n // tb,),
        in_specs=[pl.BlockSpec((tb, oa.shape[1]), lambda i: (i, 0)),
                  pl.BlockSpec((tb, ob.shape[1]), lambda i: (i, 0)),
                  pl.BlockSpec((tb, d), lambda i: (i, 0)),
                  _const_spec(woa.shape), _const_spec(wob.shape), _const_spec(g2.shape),
                  _const_spec(wq.shape), _const_spec(keys.shape)],
        out_specs=[pl.BlockSpec((tb, d), lambda i: (i, 0)), pl.BlockSpec((d, tb), lambda i: (0, i)),
                   pl.BlockSpec((nrow, tb), lambda i: (0, i))],
        out_shape=[jax.ShapeDtypeStruct((n, d), F32), jax.ShapeDtypeStruct((d, n), BF16),
                   jax.ShapeDtypeStruct((nrow, n), F32)],
        compiler_params=_params("parallel"),
        name="out_proj",
    )(oa, ob, x, woa, wob, g2, wq, keys)


def _sort16_network():
    n, pairs, p = 16, [], 1
    while p < n:
        k = p
        while k >= 1:
            for j in range(k % p, n - k, 2 * k):
                for i in range(min(k, n - j - k)):
                    if (i + j) // (2 * p) == (i + j + k) // (2 * p):
                        pairs.append((i + j, i + j + k))
            k //= 2
        p *= 2
    return pairs


def _top16_desc(tiles):
    t = list(tiles)

    def exchange(i, j):
        t[i], t[j] = jnp.maximum(t[i], t[j]), jnp.minimum(t[i], t[j])

    for i, j in _sort16_network():
        exchange(i, j)
    for shift in (4, 2, 1):
        t = [jnp.maximum(t[i], pltpu.roll(t[15 - i], shift, axis=0)) for i in range(16)]
        for d in (8, 4, 2, 1):
            for i in range(16):
                if not i & d:
                    exchange(i, i + d)
    return t


def _route_body(st_ref, t1_ref, e1_ref, e2_ref, top_scr, cand_scr, *, nk, topk):
    neg = -jnp.inf
    nx = topk + 1
    half = topk // 2

    def top_values(h, s):
        tiles = [s[i * SUBLANES:(i + 1) * SUBLANES] for i in range(nk // SUBLANES)]
        srt = _top16_desc(tiles)
        for i in range(topk):
            top_scr[h, i:i + 1, :] = srt[i][0:1]
        top_scr[h, topk:nx, :] = jnp.max(jnp.where(s < srt[topk - 1][0:1], s, neg), axis=0, keepdims=True)

    s1 = st_ref[0:nk, :]
    s2 = st_ref[nk:2 * nk, :]
    top_values(0, s1)
    top_values(1, s2)
    cand_scr[...] = jnp.full(cand_scr.shape, neg, F32)
    sv2 = top_scr[1, 0:nx, :]
    cand_scr[0:nx, :] = top_scr[0, 0:1, :] + sv2
    base = 3 * SUBLANES
    for a in range(1, half):
        cand_scr[base + (a - 1) * half:base + a * half, :] = top_scr[0, a:a + 1, :] + sv2[0:half]
    base += (half - 1) * half
    cand_scr[base:base + nx - half, :] = top_scr[0, half:nx, :] + sv2[0:1]

    cand = cand_scr[...]
    best = _top16_desc([cand[i * SUBLANES:(i + 1) * SUBLANES] for i in range(nk // SUBLANES)])
    z = jnp.zeros_like(best[0][0:1])
    for i in range(topk):
        z = z + jnp.exp(best[i][0:1] - best[0][0:1])
    hi = best[topk - 1][0:1]
    lo = jnp.max(jnp.where(cand < hi, cand, neg), axis=0, keepdims=True)
    thr = jnp.where(lo == neg, hi, 0.5 * (hi + lo))
    t1_ref[0] = thr - s1
    e1_ref[0] = jnp.exp(s1 - top_scr[0, 0:1, :]) * (np.float32(np.sqrt(0.5)) / z)
    e2_ref[0] = jnp.exp(s2 - sv2[0:1])


def _route(st, heads, nk, tb):
    n = st.shape[1]
    topk = PEER_TOPK
    half = topk // 2
    ncand = nk
    assert topk == 16 and nk == 16 * SUBLANES and half == SUBLANES
    assert 3 * SUBLANES + (half - 1) * half + topk + 1 - half <= ncand
    body = functools.partial(_route_body, nk=nk, topk=topk)
    blk = pl.BlockSpec((1, nk, tb), lambda i, h: (h, 0, i))
    return pl.pallas_call(
        body,
        grid=(n // tb, heads),
        in_specs=[pl.BlockSpec((2 * nk, tb), lambda i, h: (h, i))],
        out_specs=[blk, blk, blk],
        out_shape=[jax.ShapeDtypeStruct((heads, nk, n), F32)] * 3,
        scratch_shapes=[pltpu.VMEM((2, 3 * SUBLANES, tb), F32), pltpu.VMEM((ncand, tb), F32)],
        compiler_params=_params("parallel", "parallel"),
        name="route",
    )(st)


PEER_SUB = 2 * LANES
PEER_ROW_SPLIT = 2


def _peer_body(h2t_ref, s2_ref, t1_ref, e1_ref, e2_ref, u_ref, vt_ref, x1_ref, gf_ref, y_ref,
               acc_scr, coef_scr, *, heads, nk, tb, n_tb):
    i = pl.program_id(0)
    e = pl.program_id(1)
    eblk = 2 * u_ref.shape[0]
    i1_per_block = eblk // nk
    i1_per_sub = PEER_SUB // nk
    cur = e % 2

    @pl.when((i == 0) & (e == 0))
    def _():
        acc_scr[...] = jnp.zeros_like(acc_scr)
        coef_scr[1] = jnp.zeros(coef_scr.shape[1:], BF16)

    @pl.when((i < n_tb) | (e == 0))
    def _():
        i1_0 = pl.multiple_of(e * i1_per_block, SUBLANES)
        n_sub = eblk // PEER_SUB
        h2t = h2t_ref[...]

        def matmuls(k):
            ks = slice(k * PEER_SUB, (k + 1) * PEER_SUB)
            u_rows = pltpu.bitcast(u_ref[k * PEER_SUB // 2:(k + 1) * PEER_SUB // 2, :], BF16)
            vt_cols = pltpu.bitcast(vt_ref[:, ks], BF16)
            return _dot(u_rows, h2t), _dot(vt_cols, coef_scr[1 - cur, ks, :])

        act, out = matmuls(0)
        for k in range(n_sub):
            if k + 1 < n_sub:
                act_next, part = matmuls(k + 1)
                out = out + part
            for lg in range(tb // LANES):
                ls = slice(lg * LANES, (lg + 1) * LANES)
                t1b = [t1_ref[h, pl.ds(i1_0, i1_per_block), ls] for h in range(heads)]
                e1b = [e1_ref[h, pl.ds(i1_0, i1_per_block), ls] for h in range(heads)]
                for rs in range(PEER_ROW_SPLIT):
                    rsl = slice(rs * nk // PEER_ROW_SPLIT, (rs + 1) * nk // PEER_ROW_SPLIT)
                    gates = [None] * i1_per_sub
                    for h in range(heads):
                        s2 = s2_ref[h, rsl, ls]
                        e2 = e2_ref[h, rsl, ls]
                        for jj in range(i1_per_sub):
                            j = k * i1_per_sub + jj
                            term = jnp.where(s2 >= t1b[h][j:j + 1], e1b[h][j:j + 1] * e2, 0.0)
                            gates[jj] = term if gates[jj] is None else gates[jj] + term
                    for jj in range(i1_per_sub):
                        a = act[jj * nk + rsl.start:jj * nk + rsl.stop, ls]
                        r0 = (k * i1_per_sub + jj) * nk
                        coef_scr[cur, r0 + rsl.start:r0 + rsl.stop, ls] = (gates[jj] * (a + a * lax.erf(a))).astype(BF16)
            if k + 1 < n_sub:
                act = act_next
        acc_scr[...] += out

    @pl.when(e == 0)
    def _():
        @pl.when(i > 0)
        def _():
            y_ref[...] = _rms(x1_ref[...] + acc_scr[...].T, gf_ref[...])

        acc_scr[...] = jnp.zeros_like(acc_scr)


def _peer(h2t, st, t1, e1, e2, u, vt, x1, gf, heads, nk, tb):
    n, d = x1.shape
    ne = 2 * u.shape[0]
    eblk = SUBLANES * nk
    n_eb = ne // eblk
    n_tb = n // tb
    assert n_eb % 2 == 0
    body = functools.partial(_peer_body, heads=heads, nk=nk, tb=tb, n_tb=n_tb)
    this = lambda i: jnp.minimum(i, n_tb - 1)
    done = lambda i: jnp.maximum(i - 1, 0)
    tok = pl.BlockSpec((heads, nk, tb), lambda i, e: (0, 0, this(i)))
    return pl.pallas_call(
        body,
        grid=(n_tb + 1, n_eb),
        in_specs=[pl.BlockSpec((d, tb), lambda i, e: (0, this(i))),
                  pl.BlockSpec((heads, None, nk, tb), lambda i, e: (0, 1, 0, this(i))),
                  tok, tok, tok,
                  pl.BlockSpec((eblk // 2, d), lambda i, e: (jnp.where(i < n_tb, e, 0), 0)),
                  pl.BlockSpec((d // 2, eblk), lambda i, e: (0, jnp.where(i < n_tb, (e + n_eb - 1) % n_eb, n_eb - 1))),
                  pl.BlockSpec((tb, d), lambda i, e: (done(i), 0)),
                  _const_spec(gf.shape)],
        out_specs=pl.BlockSpec((tb, d), lambda i, e: (done(i), 0)),
        out_shape=jax.ShapeDtypeStruct((n, d), F32),
        scratch_shapes=[pltpu.VMEM((d, tb), F32), pltpu.VMEM((2, eblk, tb), BF16)],
        compiler_params=_params("arbitrary", "arbitrary"),
        name="peer",
    )(h2t, st.reshape(heads, 2, nk, n), t1, e1, e2, u, vt, x1, gf)


def _pack_body(x_ref, o_ref, *, transpose, scale):
    x = x_ref[...] * np.float32(scale)
    if transpose:
        x = x.T
    o_ref[...] = pltpu.bitcast(x.astype(BF16), jnp.uint32)


def _pack_table(x, transpose, scale=1.0, rows=512):
    n, c = x.shape
    if transpose:
        out_shape, out_spec = (c // 2, n), pl.BlockSpec((c // 2, rows), lambda i: (0, i))
    else:
        out_shape, out_spec = (n // 2, c), pl.BlockSpec((rows // 2, c), lambda i: (i, 0))
    return pl.pallas_call(
        functools.partial(_pack_body, transpose=transpose, scale=scale),
        grid=(n // rows,),
        in_specs=[pl.BlockSpec((rows, c), lambda i: (i, 0))],
        out_specs=out_spec,
        out_shape=jax.ShapeDtypeStruct(out_shape, jnp.uint32),
        compiler_params=_params("parallel"),
        name="pack_table",
    )(x)


def _block(n, pref):
    b = pref
    while n % b:
        b //= 2
    return b


def kernel(x_prompt, x_sample, state_hgrn, state_rwkv, state_shift, meta_tokens, norm1, w_in, lb_theta, hgrn_norm, mu_shift, w0, w_w2, a0, a_w2, g_w2, k_k, k_a, r_k, ln_w, ln_b, w_out, norm2, peer_wq, peer_keys, peer_u, peer_v, norm_f):
    assert w_in.shape[0] == 1, "single-layer trunk"
    bp, sp, d = x_prompt.shape
    bs, ss, _ = x_sample.shape
    n_meta = meta_tokens.shape[0]
    _, _, ha_heads, dk, dv = state_hgrn.shape
    _, _, hb_heads, hd, _ = state_rwkv.shape
    ha_in = 2 * ha_heads * dk + 2 * ha_heads * dv
    hw = hb_heads * hd
    lw, la = w_w2.shape[1], a_w2.shape[1]
    nk, dh = peer_keys.shape[2], peer_keys.shape[3]
    p_heads = peer_wq.shape[2] // (2 * dh)
    assert sp % HGRN_CHUNK == 0 and ss <= HGRN_CHUNK and n_meta <= HGRN_CHUNK
    assert hb_heads % 2 == 0 and 2 * hd == LANES and lw + la == LANES and nk == LANES

    row = lambda t: t.reshape(1, -1).astype(F32)
    w_a = w_in[0, :, :ha_in].astype(BF16)
    w_b = w_in[0, :, ha_in:].astype(BF16)
    g1 = row(norm1[0])
    lbt = lb_theta.astype(F32)
    gn = row(hgrn_norm[0])

    head_of_lane = np.arange(LANES) // hd
    rw = {
        "mu": row(mu_shift[0]), "w0": row(w0[0]), "a0": row(a0[0]), "kk": row(k_k[0]), "ka": row(k_a[0]),
        "rk": row(r_k[0]), "lnw": row(ln_w[0]), "lnb": row(ln_b[0]), "gw": g_w2[0].astype(F32),
        "ww": jnp.concatenate([w_w2[0], jnp.zeros((la, hw), F32)], axis=0),
        "aw": jnp.concatenate([jnp.zeros((lw, hw), F32), a_w2[0]], axis=0),
        "ones": jnp.asarray(head_of_lane[:, None] == head_of_lane[None, :], BF16),
        "seg": jnp.asarray(np.tile(head_of_lane[:, None] == head_of_lane[None, :], (2, 1)), BF16),
    }
    woa = w_out[0, :ha_heads * dv].astype(BF16)
    wob = w_out[0, ha_heads * dv:].astype(BF16)
    g2 = row(norm2[0])
    wq = peer_wq[0].astype(BF16)
    keys = peer_keys[0].astype(BF16)
    u = _pack_table(peer_u[0].astype(F32), False, scale=np.sqrt(0.5))
    vt = _pack_table(peer_v[0].astype(F32), True)
    gf = row(norm_f)

    za_m, zb_m = _in_proj(meta_tokens.astype(F32), g1, w_a, w_b, n_meta)
    _, hs_m = _hgrn(za_m[None], 1, n_meta, lbt, gn, jnp.zeros((1, ha_heads, dk, dv), F32), ha_heads, dk, dv)
    _, rs_m, sh_m = _rwkv(zb_m.reshape(1, n_meta, -1), jnp.zeros((1, zb_m.shape[1]), F32),
                          jnp.zeros((1, hb_heads, hd, hd), F32), rw, 1, n_meta, hd)

    def group(x, nseq, length, hs0, rs0, sh0, chunk_rows, hgrn_nb, tc):
        xf = x.reshape(nseq * length, d)
        n = xf.shape[0]
        tb = _block(n, 256)
        assert tb % LANES == 0
        za, zb = _in_proj(xf, g1, w_a, w_b, tb)
        oa, hs = _hgrn(za.reshape(nseq, length, -1), hgrn_nb, chunk_rows, lbt, gn, hs0, ha_heads, dk, dv)
        ob, rs, sh = _rwkv(zb.reshape(nseq, length, -1), sh0, rs0, rw, 8, tc, hd)
        x1, h2t, st = _out_proj(oa.reshape(n, -1), ob.reshape(n, hw), xf, woa, wob, g2, wq, keys, tb)
        t1, e1, e2 = _route(st, p_heads, nk, _block(n, 512))
        y = _peer(h2t, st, t1, e1, e2, u, vt, x1, gf, p_heads, nk, tb)
        return y.reshape(nseq, length, d), hs, rs, sh

    y_p, hs_p, rs_p, sh_p = group(
        x_prompt, bp, sp, hs_m, jnp.broadcast_to(rs_m, (bp,) + rs_m.shape[1:]),
        jnp.broadcast_to(sh_m, (bp, sh_m.shape[1])), HGRN_CHUNK, 8, 64)
    y_s, hs_s, rs_s, sh_s = group(
        x_sample, bs, ss, state_hgrn[0].astype(F32), state_rwkv[0].astype(F32),
        state_shift[0].astype(F32), ss, 8, ss)

    return (y_p, y_s, hs_p[None], rs_p[None], sh_p[None], hs_s[None], rs_s[None], sh_s[None])
```

```python
import functools

import numpy as np
import jax
import jax.numpy as jnp
from jax import lax
from jax.experimental import pallas as pl
from jax.experimental.pallas import tpu as pltpu

F32 = jnp.float32
BF16 = jnp.bfloat16

EPS = 1e-6
GN_EPS = 64e-5
HGRN_CHUNK = 64
HGRN_SHORT_CHUNK = 16
PEER_TOPK = 16
PEER_TOKENS = 256
WIDE_TOKENS = 512
RWKV_SLOT = 32
LANES = 128
SUBLANES = 8
VMEM_LIMIT_BYTES = 52 * 1024 * 1024

_NT = (((1,), (1,)), ((), ()))
_TN = (((0,), (0,)), ((), ()))


def _dot(a, b):
    return jnp.dot(a, b, preferred_element_type=F32)


def _split3(a):
    a1 = a.astype(BF16)
    r1 = a - a1.astype(F32)
    a2 = r1.astype(BF16)
    a3 = (r1 - a2.astype(F32)).astype(BF16)
    return a1, a2, a3


def _head_sums(x, ones2):
    out = []
    for c0 in range(0, x.shape[1], LANES):
        xs = x[:, c0:c0 + LANES]
        hi = xs.astype(BF16)
        lo = (xs - hi.astype(F32)).astype(BF16)
        out.append(_dot(jnp.concatenate([hi, lo], axis=1), ones2))
    return jnp.concatenate(out, axis=1)


def _dot_exact_lhs(a_exact3, b):
    return _dot(a_exact3, jnp.concatenate(_split3(b), axis=0))


def _dot_hi(a, b):
    ah = a.astype(BF16)
    al = (a - ah.astype(F32)).astype(BF16)
    bh = b.astype(BF16)
    bl = (b - bh.astype(F32)).astype(BF16)
    return _dot(ah, bh) + _dot(ah, bl) + _dot(al, bh)


def _sigmoid(x):
    return 1.0 / (1.0 + jnp.exp(-x))


def _rms(x, g):
    return x * lax.rsqrt(jnp.mean(x * x, axis=-1, keepdims=True) + EPS) * g


def _params(*sem):
    return pltpu.CompilerParams(dimension_semantics=sem, vmem_limit_bytes=VMEM_LIMIT_BYTES)


def _const_spec(shape):
    nd = len(shape)
    return pl.BlockSpec(shape, lambda *_: (0,) * nd)


def _in_proj_body(x_ref, g_ref, wa_ref, wb_ref, za_ref, zb_ref):
    hb = _rms(x_ref[...], g_ref[...]).astype(BF16)
    za_ref[...] = _dot(hb, wa_ref[...])
    zb_ref[...] = _dot(hb, wb_ref[...])


def _in_proj(x, g, wa, wb, tb):
    n, d = x.shape
    na, nb = wa.shape[1], wb.shape[1]
    return pl.pallas_call(
        _in_proj_body,
        grid=(n // tb,),
        in_specs=[pl.BlockSpec((tb, d), lambda i: (i, 0)), _const_spec((1, d)),
                  _const_spec(wa.shape), _const_spec(wb.shape)],
        out_specs=[pl.BlockSpec((tb, na), lambda i: (i, 0)), pl.BlockSpec((tb, nb), lambda i: (i, 0))],
        out_shape=[jax.ShapeDtypeStruct((n, na), F32), jax.ShapeDtypeStruct((n, nb), F32)],
        compiler_params=_params("parallel"),
        name="in_proj",
    )(x, g, wa, wb)


def _hgrn_tables(c):
    nl = int(np.log2(c)) + 1
    mats = np.zeros((nl, c, c), np.float32)
    masks = np.zeros((nl, c, c), np.float32)
    idx = np.arange(c)
    mats[0] = (idx[None, :] <= idx[:, None])
    masks[0] = np.eye(c)
    for l in range(1, nl):
        blk = 1 << l
        half = blk >> 1
        for t in range(c):
            start = (t // blk) * blk
            ref = start + half - 1
            if t - start >= half:
                mats[l, t, ref + 1:t + 1] = 1.0
                masks[l, t, start:start + half] = 1.0
            else:
                mats[l, t, t + 1:ref + 1] = 1.0
    return mats.reshape(nl * c, c), masks


def _hgrn_body(za_ref, lbt_ref, gn_ref, s0_ref, mats_ref, masks_ref, oa_ref, sout_ref, st_scr,
               *, nb, rows, chunk, heads, dk, dv):
    c_id = pl.program_id(1)
    qk = heads * dk
    wd = heads * dv
    nl = masks_ref.shape[0]

    @pl.when(c_id == 0)
    def _():
        for s in range(nb):
            for h in range(heads):
                st_scr[s, h] = s0_ref[min(s, s0_ref.shape[0] - 1), h].T

    th = lbt_ref[...]
    ex = jnp.exp(th - jnp.max(th, axis=0, keepdims=True))
    lb = ex[0:1] / jnp.sum(ex, axis=0, keepdims=True)
    gn = gn_ref[...]

    for s in range(nb):
        za = za_ref[s]
        zq = za[:, :qk]
        zf = za[:, qk:2 * qk]
        zi = za[:, 2 * qk:2 * qk + wd]
        zo = za[:, 2 * qk + wd:]
        logf = jnp.log(lb + (1.0 - lb) * _sigmoid(zf))
        kin = (1.0 - lb) * _sigmoid(-zf)
        qa = zq * _sigmoid(zq)
        if rows < chunk:
            pad = jnp.zeros((chunk - rows, qk), F32)
            logf = jnp.concatenate([logf, pad], axis=0)
            kin = jnp.concatenate([kin, pad], axis=0)
            qa = jnp.concatenate([qa, pad], axis=0)
            zi = jnp.concatenate([zi, jnp.zeros((chunk - rows, wd), F32)], axis=0)

        dec = _dot_exact_lhs(mats_ref[...], logf)
        for h in range(heads):
            ks = slice(h * dk, (h + 1) * dk)
            vs = slice(h * dv, (h + 1) * dv)
            q = qa[:, ks]
            k = kin[:, ks]
            v = zi[:, vs].astype(BF16)
            cum = dec[0:chunk, ks]
            sc = lax.dot_general(q.astype(BF16), k.astype(BF16), _NT, preferred_element_type=F32)
            scores = jnp.where(masks_ref[0] != 0.0, sc, 0.0)
            for l in range(1, nl):
                e = jnp.exp(dec[l * chunk:(l + 1) * chunk, ks])
                sc = lax.dot_general((q * e).astype(BF16), (k * e).astype(BF16), _NT, preferred_element_type=F32)
                scores = scores + jnp.where(masks_ref[l] != 0.0, sc, 0.0)
            st = st_scr[s, h]
            o = _dot(scores.astype(BF16), v) + lax.dot_general(
                (q * jnp.exp(cum)).astype(BF16), st.astype(BF16), _NT, preferred_element_type=F32)
            last = cum[chunk - 1:chunk, :]
            kd = (k * jnp.exp(last - cum)).astype(BF16)
            st_scr[s, h] = jnp.exp(last) * st + lax.dot_general(v, kd, _TN, preferred_element_type=F32)

            o = o[:rows]
            o = o * lax.rsqrt(jnp.mean(o * o, axis=-1, keepdims=True) + EPS) * gn[:, vs]
            zoh = zo[:, vs]
            oa_ref[s, :, vs] = o * (zoh * _sigmoid(zoh))

    @pl.when(c_id == pl.num_programs(1) - 1)
    def _():
        for s in range(nb):
            for h in range(heads):
                sout_ref[s, h] = st_scr[s, h].T


def _hgrn(za, nb, rows, lb_theta, gn, s0, heads, dk, dv):
    nseq, length, cols = za.shape
    chunk = HGRN_CHUNK if rows > HGRN_SHORT_CHUNK else HGRN_SHORT_CHUNK
    mats, masks = _hgrn_tables(chunk)
    mats = jnp.asarray(np.tile(mats, (1, 3)), BF16)
    masks = jnp.asarray(masks, F32)
    wd = heads * dv
    shared = s0.shape[0] == 1
    s0_spec = (pl.BlockSpec((1, heads, dk, dv), lambda s, c: (0, 0, 0, 0)) if shared
               else pl.BlockSpec((nb, heads, dk, dv), lambda s, c: (s, 0, 0, 0)))
    body = functools.partial(_hgrn_body, nb=nb, rows=rows, chunk=chunk, heads=heads, dk=dk, dv=dv)
    return pl.pallas_call(
        body,
        grid=(nseq // nb, length // rows),
        in_specs=[pl.BlockSpec((nb, rows, cols), lambda s, c: (s, c, 0)),
                  _const_spec(lb_theta.shape), _const_spec(gn.shape), s0_spec,
                  _const_spec(mats.shape), _const_spec(masks.shape)],
        out_specs=[pl.BlockSpec((nb, rows, wd), lambda s, c: (s, c, 0)),
                   pl.BlockSpec((nb, heads, dk, dv), lambda s, c: (s, 0, 0, 0))],
        out_shape=[jax.ShapeDtypeStruct((nseq, length, wd), F32),
                   jax.ShapeDtypeStruct((nseq, heads, dk, dv), F32)],
        scratch_shapes=[pltpu.VMEM((nb, heads, dv, dk), F32)],
        compiler_params=_params("parallel", "arbitrary"),
        name="hgrn",
    )(za, lb_theta, gn, s0, mats, masks)


def _rwkv_body(zb_ref, sh0_ref, s0_ref, mu_ref, w0_ref, ww_ref, a0_ref, aw_ref, gw_ref, kk_ref, ka_ref,
               rk_ref, lnw_ref, lnb_ref, ones_ref, seg_ref,
               ob_ref, sout_ref, shout_ref,
               s_scr, prev_scr, al_scr, w_scr, be_scr, k_scr, r_scr, v_scr, g_scr, y_scr, vt_scr,
               *, nb, tc, hw, hd, lora):
    t_id = pl.program_id(1)
    pairs = hw // LANES
    rin = zb_ref.shape[-1]

    @pl.when(t_id == 0)
    def _():
        for b in range(nb):
            for p in range(pairs):
                s_scr[b, p] = jnp.concatenate([s0_ref[b, 2 * p], s0_ref[b, 2 * p + 1]], axis=1)
        prev_scr[...] = sh0_ref[...]

    row = lax.broadcasted_iota(jnp.int32, (tc, rin), 0)
    mu = mu_ref[...]
    zms = []
    for b in range(nb):
        zb = zb_ref[b]
        prev = jnp.where(row == 0, prev_scr[b:b + 1, :], pltpu.roll(zb, 1, axis=0))
        prev_scr[b:b + 1, :] = zb[tc - 1:tc, :]
        zms.append(zb + (prev - zb) * mu)
    zm = jnp.concatenate(zms, axis=0)
    r = zm[:, :hw]
    k = zm[:, hw:2 * hw]
    v = zm[:, 2 * hw:3 * hw]
    dwa = zm[:, 3 * hw:3 * hw + lora]
    dg = zm[:, 3 * hw + lora:]
    w_raw = w0_ref[...] + _dot_hi(jnp.tanh(dwa), ww_ref[...])
    decay = jnp.exp(-np.float32(np.exp(-0.5)) * _sigmoid(w_raw))
    a = _sigmoid(a0_ref[...] + _dot_hi(dwa, aw_ref[...]))
    g = _dot_hi(_sigmoid(dg), gw_ref[...])
    kk = k * kk_ref[...]
    ss = _head_sums(kk * kk, seg_ref[...])
    kk = kk / jnp.maximum(jnp.sqrt(ss), 1e-12)
    k2 = k * (1.0 + (a - 1.0) * ka_ref[...])
    shp = (nb, tc, hw)
    al_scr[...] = (-kk).reshape(shp)
    w_scr[...] = decay.reshape(shp)
    be_scr[...] = (kk * a).reshape(shp)
    k_scr[...] = k2.reshape(shp)
    r_scr[...] = r.reshape(shp)
    v_scr[...] = v.reshape(shp)
    g_scr[...] = g.reshape(shp)

    slot = RWKV_SLOT
    li2 = lax.broadcasted_iota(jnp.int32, (slot, LANES), 1)
    vi = lax.broadcasted_iota(jnp.int32, (hd, LANES), 0)
    sel = jnp.where(vi == lax.broadcasted_iota(jnp.int32, (hd, LANES), 1) % hd, 1.0, 0.0).astype(BF16)
    for b in range(nb):
        for p in range(pairs):
            for g0 in range(0, tc, slot):
                nt = min(slot, tc - g0)
                vb = v[b * tc + g0:b * tc + g0 + nt, p * LANES:(p + 1) * LANES]
                if nt < slot:
                    vb = jnp.concatenate([vb, jnp.zeros((slot - nt, LANES), F32)], axis=0)
                vh = vb.astype(BF16)
                vl = (vb - vh.astype(F32)).astype(BF16)
                zero = jnp.zeros_like(vh)
                stack = jnp.concatenate([jnp.where(li2 < hd, vh, zero), jnp.where(li2 >= hd, vh, zero),
                                         jnp.where(li2 < hd, vl, zero), jnp.where(li2 >= hd, vl, zero)], axis=0)
                vt_scr[g0 // slot, b, p] = lax.dot_general(
                    sel, stack, _NT, preferred_element_type=F32).astype(BF16)

    rows = pairs * hd
    li = lax.broadcasted_iota(jnp.int32, (rows, LANES), 1)
    key_lane = li % hd
    ones1 = ones_ref[...]
    ones_f = ones1.astype(F32)
    qi = lax.broadcasted_iota(jnp.int32, (4 * slot, LANES), 0)
    ql = lax.broadcasted_iota(jnp.int32, (4 * slot, LANES), 1)
    tok_of_row = jnp.where((qi // slot) % 2 == ql // hd, qi % slot, -1)

    def bcast(ref, b, t):
        rowv = ref[b, pl.ds(t, 1), :]
        return jnp.concatenate(
            [jnp.broadcast_to(rowv[:, p * LANES:(p + 1) * LANES], (hd, LANES)) for p in range(pairs)], axis=0)

    def step(t, carry):
        pick = jnp.where(tok_of_row == t % slot, 1.0, 0.0).astype(BF16)
        seqs = range(nb)
        vcols = [_dot(vt_scr[t // slot, b].reshape(rows, 4 * slot), pick) for b in seqs]
        for b in seqs:
            old = s_scr[b].reshape(rows, LANES)
            sa = _dot(old * bcast(al_scr, b, t), ones_f)
            new = old * bcast(w_scr, b, t) + sa * bcast(be_scr, b, t) + vcols[b] * bcast(k_scr, b, t)
            s_scr[b] = new.reshape(pairs, hd, LANES)
            yrep = _dot(new * bcast(r_scr, b, t), ones_f)
            y_scr[b] = jnp.where(key_lane == t, yrep, y_scr[b])
        return carry

    y_scr[...] = jnp.zeros_like(y_scr)
    lax.fori_loop(0, tc, step, 0, unroll=8)

    def token_rows(b, p):
        yt = y_scr[b, p * hd:(p + 1) * hd, :].T
        return jnp.concatenate([yt[0:tc], yt[hd:hd + tc]], axis=1)

    y = jnp.concatenate(
        [jnp.concatenate([token_rows(b, p) for p in range(pairs)], axis=1) for b in range(nb)], axis=0)
    seg = seg_ref[...]
    inv = np.float32(1.0 / hd)
    mean = _head_sums(y, seg) * inv
    yc = y - mean
    var = _head_sums(yc * yc, seg) * inv
    yn = yc * lax.rsqrt(var + GN_EPS) * lnw_ref[...] + lnb_ref[...]
    r = r_scr[...].reshape(nb * tc, hw)
    k2 = k_scr[...].reshape(nb * tc, hw)
    v = v_scr[...].reshape(nb * tc, hw)
    bonus = _head_sums(r * k2 * rk_ref[...], seg) * v
    ob_ref[...] = ((yn + bonus) * g_scr[...].reshape(nb * tc, hw)).reshape(shp)

    @pl.when(t_id == pl.num_programs(1) - 1)
    def _():
        for b in range(nb):
            for p in range(pairs):
                sout_ref[b, 2 * p] = s_scr[b, p, :, 0:hd]
                sout_ref[b, 2 * p + 1] = s_scr[b, p, :, hd:2 * hd]
        shout_ref[...] = prev_scr[...]


def _rwkv(zb, sh0, s0, wts, nb, tc, hd):
    nseq, length, rin = zb.shape
    hw = wts["w0"].shape[1]
    pairs = hw // LANES
    lora = wts["ww"].shape[0]
    body = functools.partial(_rwkv_body, nb=nb, tc=tc, hw=hw, hd=hd, lora=lora)
    names = ("mu", "w0", "ww", "a0", "aw", "gw", "kk", "ka", "rk", "lnw", "lnb", "ones", "seg")
    consts = [wts[n] for n in names]
    tok = pltpu.VMEM((nb, tc, hw), F32)
    return pl.pallas_call(
        body,
        grid=(nseq // nb, length // tc),
        in_specs=[pl.BlockSpec((nb, tc, rin), lambda s, t: (s, t, 0)),
                  pl.BlockSpec((nb, rin), lambda s, t: (s, 0)),
                  pl.BlockSpec((nb, 2 * pairs, hd, hd), lambda s, t: (s, 0, 0, 0))]
                 + [_const_spec(c.shape) for c in consts],
        out_specs=[pl.BlockSpec((nb, tc, hw), lambda s, t: (s, t, 0)),
                   pl.BlockSpec((nb, 2 * pairs, hd, hd), lambda s, t: (s, 0, 0, 0)),
                   pl.BlockSpec((nb, rin), lambda s, t: (s, 0))],
        out_shape=[jax.ShapeDtypeStruct((nseq, length, hw), F32),
                   jax.ShapeDtypeStruct((nseq, 2 * pairs, hd, hd), F32),
                   jax.ShapeDtypeStruct((nseq, rin), F32)],
        scratch_shapes=[pltpu.VMEM((nb, pairs, hd, LANES), F32), pltpu.VMEM((nb, rin), F32)]
                       + [tok] * 7 + [pltpu.VMEM((nb, pairs * hd, LANES), F32),
                                      pltpu.VMEM((-(-tc // RWKV_SLOT), nb, pairs, hd, 4 * RWKV_SLOT), BF16)],
        compiler_params=_params("parallel", "arbitrary"),
        name="rwkv",
    )(zb, sh0, s0, *consts)


def _out_proj_body(oa_ref, ob_ref, x_ref, woa_ref, wob_ref, g2_ref, wq_ref, keys_ref,
                   x1_ref, h2t_ref, st_ref, *, nk, dh):
    mix = _dot(oa_ref[...].astype(BF16), woa_ref[...]) + _dot(ob_ref[...].astype(BF16), wob_ref[...])
    x1 = x_ref[...] + mix
    x1_ref[...] = x1
    h2f = _rms(x1, g2_ref[...])
    h2t_ref[...] = h2f.T.astype(BF16)
    q = _dot(h2f.astype(BF16), wq_ref[...]).astype(BF16)
    for hp in range(q.shape[1] // dh):
        st_ref[hp * nk:(hp + 1) * nk, :] = lax.dot_general(
            keys_ref[hp % 2], q[:, hp * dh:(hp + 1) * dh], _NT, preferred_element_type=F32)


def _out_proj(oa, ob, x, woa, wob, g2, wq, keys, tb):
    n, d = x.shape
    nk, dh = keys.shape[1], keys.shape[2]
    nrow = (wq.shape[1] // dh) * nk
    body = functools.partial(_out_proj_body, nk=nk, dh=dh)
    return pl.pallas_call(
        body,
        grid=(n // tb,),
        in_specs=[pl.BlockSpec((tb, oa.shape[1]), lambda i: (i, 0)),
                  pl.BlockSpec((tb, ob.shape[1]), lambda i: (i, 0)),
                  pl.BlockSpec((tb, d), lambda i: (i, 0)),
                  _const_spec(woa.shape), _const_spec(wob.shape), _const_spec(g2.shape),
                  _const_spec(wq.shape), _const_spec(keys.shape)],
        out_specs=[pl.BlockSpec((tb, d), lambda i: (i, 0)), pl.BlockSpec((d, tb), lambda i: (0, i)),
                   pl.BlockSpec((nrow, tb), lambda i: (0, i))],
        out_shape=[jax.ShapeDtypeStruct((n, d), F32), jax.ShapeDtypeStruct((d, n), BF16),
                   jax.ShapeDtypeStruct((nrow, n), F32)],
        compiler_params=_params("parallel"),
        name="out_proj",
    )(oa, ob, x, woa, wob, g2, wq, keys)


def _sort16_network():
    n, pairs, p = 16, [], 1
    while p < n:
        k = p
        while k >= 1:
            for j in range(k % p, n - k, 2 * k):
                for i in range(min(k, n - j - k)):
                    if (i + j) // (2 * p) == (i + j + k) // (2 * p):
                        pairs.append((i + j, i + j + k))
            k //= 2
        p *= 2
    return pairs


def _top16_desc(tiles):
    t = list(tiles)

    def exchange(i, j):
        t[i], t[j] = jnp.maximum(t[i], t[j]), jnp.minimum(t[i], t[j])

    for i, j in _sort16_network():
        exchange(i, j)
    for shift in (4, 2, 1):
        t = [jnp.maximum(t[i], pltpu.roll(t[15 - i], shift, axis=0)) for i in range(16)]
        for d in (8, 4, 2, 1):
            for i in range(16):
                if not i & d:
                    exchange(i, i + d)
    return t


def _route_body(st_ref, t1_ref, e1_ref, e2_ref, top_scr, cand_scr, *, nk, topk):
    neg = -jnp.inf
    nx = topk + 1
    half = topk // 2

    def top_values(h, s):
        tiles = [s[i * SUBLANES:(i + 1) * SUBLANES] for i in range(nk // SUBLANES)]
        srt = _top16_desc(tiles)
        for i in range(topk):
            top_scr[h, i:i + 1, :] = srt[i][0:1]
        top_scr[h, topk:nx, :] = jnp.max(jnp.where(s < srt[topk - 1][0:1], s, neg), axis=0, keepdims=True)

    s1 = st_ref[0:nk, :]
    s2 = st_ref[nk:2 * nk, :]
    top_values(0, s1)
    top_values(1, s2)
    cand_scr[...] = jnp.full(cand_scr.shape, neg, F32)
    sv2 = top_scr[1, 0:nx, :]
    cand_scr[0:nx, :] = top_scr[0, 0:1, :] + sv2
    base = 3 * SUBLANES
    for a in range(1, half):
        cand_scr[base + (a - 1) * half:base + a * half, :] = top_scr[0, a:a + 1, :] + sv2[0:half]
    base += (half - 1) * half
    cand_scr[base:base + nx - half, :] = top_scr[0, half:nx, :] + sv2[0:1]

    cand = cand_scr[...]
    best = _top16_desc([cand[i * SUBLANES:(i + 1) * SUBLANES] for i in range(nk // SUBLANES)])
    z = jnp.zeros_like(best[0][0:1])
    for i in range(topk):
        z = z + jnp.exp(best[i][0:1] - best[0][0:1])
    hi = best[topk - 1][0:1]
    lo = jnp.max(jnp.where(cand < hi, cand, neg), axis=0, keepdims=True)
    thr = jnp.where(lo == neg, hi, 0.5 * (hi + lo))
    t1_ref[0] = thr - s1
    e1_ref[0] = jnp.exp(s1 - top_scr[0, 0:1, :]) * (np.float32(np.sqrt(0.5)) / z)
    e2_ref[0] = jnp.exp(s2 - sv2[0:1])


def _route(st, heads, nk, tb):
    n = st.shape[1]
    topk = PEER_TOPK
    half = topk // 2
    ncand = nk
    assert topk == 16 and nk == 16 * SUBLANES and half == SUBLANES
    assert 3 * SUBLANES + (half - 1) * half + topk + 1 - half <= ncand
    body = functools.partial(_route_body, nk=nk, topk=topk)
    blk = pl.BlockSpec((1, nk, tb), lambda i, h: (h, 0, i))
    return pl.pallas_call(
        body,
        grid=(n // tb, heads),
        in_specs=[pl.BlockSpec((2 * nk, tb), lambda i, h: (h, i))],
        out_specs=[blk, blk, blk],
        out_shape=[jax.ShapeDtypeStruct((heads, nk, n), F32)] * 3,
        scratch_shapes=[pltpu.VMEM((2, 3 * SUBLANES, tb), F32), pltpu.VMEM((ncand, tb), F32)],
        compiler_params=_params("parallel", "parallel"),
        name="route",
    )(st)


PEER_SUB = 2 * LANES
PEER_ROW_SPLIT = 2


def _peer_body(h2t_ref, s2_ref, t1_ref, e1_ref, e2_ref, u_ref, vt_ref, x1_ref, gf_ref, y_ref,
               acc_scr, coef_scr, *, heads, nk, tb, n_tb):
    i = pl.program_id(0)
    e = pl.program_id(1)
    eblk = 2 * u_ref.shape[0]
    i1_per_block = eblk // nk
    i1_per_sub = PEER_SUB // nk
    cur = e % 2

    @pl.when((i == 0) & (e == 0))
    def _():
        acc_scr[...] = jnp.zeros_like(acc_scr)
        coef_scr[1] = jnp.zeros(coef_scr.shape[1:], BF16)

    @pl.when((i < n_tb) | (e == 0))
    def _():
        i1_0 = pl.multiple_of(e * i1_per_block, SUBLANES)
        n_sub = eblk // PEER_SUB
        h2t = h2t_ref[...]

        def matmuls(k):
            ks = slice(k * PEER_SUB, (k + 1) * PEER_SUB)
            u_rows = pltpu.bitcast(u_ref[k * PEER_SUB // 2:(k + 1) * PEER_SUB // 2, :], BF16)
            vt_cols = pltpu.bitcast(vt_ref[:, ks], BF16)
            return _dot(u_rows, h2t), _dot(vt_cols, coef_scr[1 - cur, ks, :])

        act, out = matmuls(0)
        for k in range(n_sub):
            if k + 1 < n_sub:
                act_next, part = matmuls(k + 1)
                out = out + part
            for lg in range(tb // LANES):
                ls = slice(lg * LANES, (lg + 1) * LANES)
                t1b = [t1_ref[h, pl.ds(i1_0, i1_per_block), ls] for h in range(heads)]
                e1b = [e1_ref[h, pl.ds(i1_0, i1_per_block), ls] for h in range(heads)]
                for rs in range(PEER_ROW_SPLIT):
                    rsl = slice(rs * nk // PEER_ROW_SPLIT, (rs + 1) * nk // PEER_ROW_SPLIT)
                    gates = [None] * i1_per_sub
                    for h in range(heads):
                        s2 = s2_ref[h, rsl, ls]
                        e2 = e2_ref[h, rsl, ls]
                        for jj in range(i1_per_sub):
                            j = k * i1_per_sub + jj
                            term = jnp.where(s2 >= t1b[h][j:j + 1], e1b[h][j:j + 1] * e2, 0.0)
                            gates[jj] = term if gates[jj] is None else gates[jj] + term
                    for jj in range(i1_per_sub):
                        a = act[jj * nk + rsl.start:jj * nk + rsl.stop, ls]
                        r0 = (k * i1_per_sub + jj) * nk
                        coef_scr[cur, r0 + rsl.start:r0 + rsl.stop, ls] = (gates[jj] * (a + a * lax.erf(a))).astype(BF16)
            if k + 1 < n_sub:
                act = act_next
        acc_scr[...] += out

    @pl.when(e == 0)
    def _():
        @pl.when(i > 0)
        def _():
            y_ref[...] = _rms(x1_ref[...] + acc_scr[...].T, gf_ref[...])

        acc_scr[...] = jnp.zeros_like(acc_scr)


def _peer(h2t, st, t1, e1, e2, u, vt, x1, gf, heads, nk, tb):
    n, d = x1.shape
    ne = 2 * u.shape[0]
    eblk = SUBLANES * nk
    n_eb = ne // eblk
    n_tb = n // tb
    assert n_eb % 2 == 0
    body = functools.partial(_peer_body, heads=heads, nk=nk, tb=tb, n_tb=n_tb)
    this = lambda i: jnp.minimum(i, n_tb - 1)
    done = lambda i: jnp.maximum(i - 1, 0)
    tok = pl.BlockSpec((heads, nk, tb), lambda i, e: (0, 0, this(i)))
    return pl.pallas_call(
        body,
        grid=(n_tb + 1, n_eb),
        in_specs=[pl.BlockSpec((d, tb), lambda i, e: (0, this(i))),
                  pl.BlockSpec((heads, None, nk, tb), lambda i, e: (0, 1, 0, this(i))),
                  tok, tok, tok,
                  pl.BlockSpec((eblk // 2, d), lambda i, e: (jnp.where(i < n_tb, e, 0), 0)),
                  pl.BlockSpec((d // 2, eblk), lambda i, e: (0, jnp.where(i < n_tb, (e + n_eb - 1) % n_eb, n_eb - 1))),
                  pl.BlockSpec((tb, d), lambda i, e: (done(i), 0)),
                  _const_spec(gf.shape)],
        out_specs=pl.BlockSpec((tb, d), lambda i, e: (done(i), 0)),
        out_shape=jax.ShapeDtypeStruct((n, d), F32),
        scratch_shapes=[pltpu.VMEM((d, tb), F32), pltpu.VMEM((2, eblk, tb), BF16)],
        compiler_params=_params("arbitrary", "arbitrary"),
        name="peer",
    )(h2t, st.reshape(heads, 2, nk, n), t1, e1, e2, u, vt, x1, gf)


def _pack_body(x_ref, o_ref, *, transpose, scale):
    x = x_ref[...] * np.float32(scale)
    if transpose:
        x = x.T
    o_ref[...] = pltpu.bitcast(x.astype(BF16), jnp.uint32)


def _pack_table(x, transpose, scale=1.0, rows=512):
    n, c = x.shape
    if transpose:
        out_shape, out_spec = (c // 2, n), pl.BlockSpec((c // 2, rows), lambda i: (0, i))
    else:
        out_shape, out_spec = (n // 2, c), pl.BlockSpec((rows // 2, c), lambda i: (i, 0))
    return pl.pallas_call(
        functools.partial(_pack_body, transpose=transpose, scale=scale),
        grid=(n // rows,),
        in_specs=[pl.BlockSpec((rows, c), lambda i: (i, 0))],
        out_specs=out_spec,
        out_shape=jax.ShapeDtypeStruct(out_shape, jnp.uint32),
        compiler_params=_params("parallel"),
        name="pack_table",
    )(x)


def _block(n, pref):
    b = pref
    while n % b:
        b //= 2
    return b


def kernel(x_prompt, x_sample, state_hgrn, state_rwkv, state_shift, meta_tokens, norm1, w_in, lb_theta, hgrn_norm, mu_shift, w0, w_w2, a0, a_w2, g_w2, k_k, k_a, r_k, ln_w, ln_b, w_out, norm2, peer_wq, peer_keys, peer_u, peer_v, norm_f):
    assert w_in.shape[0] == 1, "single-layer trunk"
    bp, sp, d = x_prompt.shape
    bs, ss, _ = x_sample.shape
    n_meta = meta_tokens.shape[0]
    _, _, ha_heads, dk, dv = state_hgrn.shape
    _, _, hb_heads, hd, _ = state_rwkv.shape
    ha_in = 2 * ha_heads * dk + 2 * ha_heads * dv
    hw = hb_heads * hd
    lw, la = w_w2.shape[1], a_w2.shape[1]
    nk, dh = peer_keys.shape[2], peer_keys.shape[3]
    p_heads = peer_wq.shape[2] // (2 * dh)
    assert sp % HGRN_CHUNK == 0 and ss <= HGRN_CHUNK and n_meta <= HGRN_CHUNK
    assert hb_heads % 2 == 0 and 2 * hd == LANES and lw + la == LANES and nk == LANES

    row = lambda t: t.reshape(1, -1).astype(F32)
    w_a = w_in[0, :, :ha_in].astype(BF16)
    w_b = w_in[0, :, ha_in:].astype(BF16)
    g1 = row(norm1[0])
    lbt = lb_theta.astype(F32)
    gn = row(hgrn_norm[0])

    head_of_lane = np.arange(LANES) // hd
    rw = {
        "mu": row(mu_shift[0]), "w0": row(w0[0]), "a0": row(a0[0]), "kk": row(k_k[0]), "ka": row(k_a[0]),
        "rk": row(r_k[0]), "lnw": row(ln_w[0]), "lnb": row(ln_b[0]), "gw": g_w2[0].astype(F32),
        "ww": jnp.concatenate([w_w2[0], jnp.zeros((la, hw), F32)], axis=0),
        "aw": jnp.concatenate([jnp.zeros((lw, hw), F32), a_w2[0]], axis=0),
        "ones": jnp.asarray(head_of_lane[:, None] == head_of_lane[None, :], BF16),
        "seg": jnp.asarray(np.tile(head_of_lane[:, None] == head_of_lane[None, :], (2, 1)), BF16),
    }
    woa = w_out[0, :ha_heads * dv].astype(BF16)
    wob = w_out[0, ha_heads * dv:].astype(BF16)
    g2 = row(norm2[0])
    wq = peer_wq[0].astype(BF16)
    keys = peer_keys[0].astype(BF16)
    u = _pack_table(peer_u[0].astype(F32), False, scale=np.sqrt(0.5))
    vt = _pack_table(peer_v[0].astype(F32), True)
    gf = row(norm_f)

    za_m, zb_m = _in_proj(meta_tokens.astype(F32), g1, w_a, w_b, n_meta)
    _, hs_m = _hgrn(za_m[None], 1, n_meta, lbt, gn, jnp.zeros((1, ha_heads, dk, dv), F32), ha_heads, dk, dv)
    _, rs_m, sh_m = _rwkv(zb_m.reshape(1, n_meta, -1), jnp.zeros((1, zb_m.shape[1]), F32),
                          jnp.zeros((1, hb_heads, hd, hd), F32), rw, 1, n_meta, hd)

    def group(x, nseq, length, hs0, rs0, sh0, chunk_rows, hgrn_nb, tc):
        xf = x.reshape(nseq * length, d)
        n = xf.shape[0]
        tb = _block(n, PEER_TOKENS)
        tb_wide = _block(n, WIDE_TOKENS)
        assert tb % LANES == 0
        za, zb = _in_proj(xf, g1, w_a, w_b, tb_wide)
        oa, hs = _hgrn(za.reshape(nseq, length, -1), hgrn_nb, chunk_rows, lbt, gn, hs0, ha_heads, dk, dv)
        ob, rs, sh = _rwkv(zb.reshape(nseq, length, -1), sh0, rs0, rw, 8, tc, hd)
        x1, h2t, st = _out_proj(oa.reshape(n, -1), ob.reshape(n, hw), xf, woa, wob, g2, wq, keys, tb_wide)
        t1, e1, e2 = _route(st, p_heads, nk, tb_wide)
        y = _peer(h2t, st, t1, e1, e2, u, vt, x1, gf, p_heads, nk, tb)
        return y.reshape(nseq, length, d), hs, rs, sh

    y_p, hs_p, rs_p, sh_p = group(
        x_prompt, bp, sp, hs_m, jnp.broadcast_to(rs_m, (bp,) + rs_m.shape[1:]),
        jnp.broadcast_to(sh_m, (bp, sh_m.shape[1])), HGRN_CHUNK, 8, 64)
    y_s, hs_s, rs_s, sh_s = group(
        x_sample, bs, ss, state_hgrn[0].astype(F32), state_rwkv[0].astype(F32),
        state_shift[0].astype(F32), ss, 8, ss)

    return (y_p, y_s, hs_p[None], rs_p[None], sh_p[None], hs_s[None], rs_s[None], sh_s[None])
```

```python
import functools

import numpy as np
import jax
import jax.numpy as jnp
from jax import lax
from jax.experimental import pallas as pl
from jax.experimental.pallas import tpu as pltpu

F32 = jnp.float32
BF16 = jnp.bfloat16

EPS = 1e-6
GN_EPS = 64e-5
HGRN_CHUNK = 64
HGRN_SHORT_CHUNK = 16
PEER_TOPK = 16
PEER_TOKENS = 256
WIDE_TOKENS = 512
RWKV_SLOT = 32
LANES = 128
SUBLANES = 8
VMEM_LIMIT_BYTES = 52 * 1024 * 1024

_NT = (((1,), (1,)), ((), ()))
_TN = (((0,), (0,)), ((), ()))


def _dot(a, b):
    return jnp.dot(a, b, preferred_element_type=F32)


def _split3(a):
    a1 = a.astype(BF16)
    r1 = a - a1.astype(F32)
    a2 = r1.astype(BF16)
    a3 = (r1 - a2.astype(F32)).astype(BF16)
    return a1, a2, a3


def _head_sums(x, ones2):
    out = []
    for c0 in range(0, x.shape[1], LANES):
        xs = x[:, c0:c0 + LANES]
        hi = xs.astype(BF16)
        lo = (xs - hi.astype(F32)).astype(BF16)
        out.append(_dot(jnp.concatenate([hi, lo], axis=1), ones2))
    return jnp.concatenate(out, axis=1)


def _dot_exact_lhs(a_exact3, b):
    return _dot(a_exact3, jnp.concatenate(_split3(b), axis=0))


def _dot_hi(a, b):
    ah = a.astype(BF16)
    al = (a - ah.astype(F32)).astype(BF16)
    bh = b.astype(BF16)
    bl = (b - bh.astype(F32)).astype(BF16)
    return _dot(ah, bh) + _dot(ah, bl) + _dot(al, bh)


def _sigmoid(x):
    return 1.0 / (1.0 + jnp.exp(-x))


def _rms(x, g):
    return x * lax.rsqrt(jnp.mean(x * x, axis=-1, keepdims=True) + EPS) * g


def _params(*sem):
    return pltpu.CompilerParams(dimension_semantics=sem, vmem_limit_bytes=VMEM_LIMIT_BYTES)


def _const_spec(shape):
    nd = len(shape)
    return pl.BlockSpec(shape, lambda *_: (0,) * nd)


def _in_proj_body(x_ref, g_ref, wa_ref, wb_ref, za_ref, zb_ref):
    hb = _rms(x_ref[...], g_ref[...]).astype(BF16)
    za_ref[...] = _dot(hb, wa_ref[...])
    zb_ref[...] = _dot(hb, wb_ref[...])


def _in_proj(x, g, wa, wb, tb):
    n, d = x.shape
    na, nb = wa.shape[1], wb.shape[1]
    return pl.pallas_call(
        _in_proj_body,
        grid=(n // tb,),
        in_specs=[pl.BlockSpec((tb, d), lambda i: (i, 0)), _const_spec((1, d)),
                  _const_spec(wa.shape), _const_spec(wb.shape)],
        out_specs=[pl.BlockSpec((tb, na), lambda i: (i, 0)), pl.BlockSpec((tb, nb), lambda i: (i, 0))],
        out_shape=[jax.ShapeDtypeStruct((n, na), F32), jax.ShapeDtypeStruct((n, nb), F32)],
        compiler_params=_params("parallel"),
        name="in_proj",
    )(x, g, wa, wb)


def _hgrn_tables(c):
    nl = int(np.log2(c)) + 1
    mats = np.zeros((nl, c, c), np.float32)
    masks = np.zeros((nl, c, c), np.float32)
    idx = np.arange(c)
    mats[0] = (idx[None, :] <= idx[:, None])
    masks[0] = np.eye(c)
    for l in range(1, nl):
        blk = 1 << l
        half = blk >> 1
        for t in range(c):
            start = (t // blk) * blk
            ref = start + half - 1
            if t - start >= half:
                mats[l, t, ref + 1:t + 1] = 1.0
                masks[l, t, start:start + half] = 1.0
            else:
                mats[l, t, t + 1:ref + 1] = 1.0
    return mats.reshape(nl * c, c), masks


def _hgrn_body(za_ref, lbt_ref, gn_ref, s0_ref, mats_ref, masks_ref, oa_ref, sout_ref, st_scr,
               *, nb, rows, chunk, heads, dk, dv):
    c_id = pl.program_id(1)
    qk = heads * dk
    wd = heads * dv
    nl = masks_ref.shape[0]

    @pl.when(c_id == 0)
    def _():
        for s in range(nb):
            for h in range(heads):
                st_scr[s, h] = s0_ref[min(s, s0_ref.shape[0] - 1), h].T

    th = lbt_ref[...]
    ex = jnp.exp(th - jnp.max(th, axis=0, keepdims=True))
    lb = ex[0:1] / jnp.sum(ex, axis=0, keepdims=True)
    gn = gn_ref[...]

    for s in range(nb):
        za = za_ref[s]
        zq = za[:, :qk]
        zf = za[:, qk:2 * qk]
        zi = za[:, 2 * qk:2 * qk + wd]
        zo = za[:, 2 * qk + wd:]
        logf = jnp.log(lb + (1.0 - lb) * _sigmoid(zf))
        kin = (1.0 - lb) * _sigmoid(-zf)
        qa = zq * _sigmoid(zq)
        if rows < chunk:
            pad = jnp.zeros((chunk - rows, qk), F32)
            logf = jnp.concatenate([logf, pad], axis=0)
            kin = jnp.concatenate([kin, pad], axis=0)
            qa = jnp.concatenate([qa, pad], axis=0)
            zi = jnp.concatenate([zi, jnp.zeros((chunk - rows, wd), F32)], axis=0)

        dec = _dot_exact_lhs(mats_ref[...], logf)
        for h in range(heads):
            ks = slice(h * dk, (h + 1) * dk)
            vs = slice(h * dv, (h + 1) * dv)
            q = qa[:, ks]
            k = kin[:, ks]
            v = zi[:, vs].astype(BF16)
            cum = dec[0:chunk, ks]
            sc = lax.dot_general(q.astype(BF16), k.astype(BF16), _NT, preferred_element_type=F32)
            scores = jnp.where(masks_ref[0] != 0.0, sc, 0.0)
            for l in range(1, nl):
                e = jnp.exp(dec[l * chunk:(l + 1) * chunk, ks])
                sc = lax.dot_general((q * e).astype(BF16), (k * e).astype(BF16), _NT, preferred_element_type=F32)
                scores = scores + jnp.where(masks_ref[l] != 0.0, sc, 0.0)
            st = st_scr[s, h]
            o = _dot(scores.astype(BF16), v) + lax.dot_general(
                (q * jnp.exp(cum)).astype(BF16), st.astype(BF16), _NT, preferred_element_type=F32)
            last = cum[chunk - 1:chunk, :]
            kd = (k * jnp.exp(last - cum)).astype(BF16)
            st_scr[s, h] = jnp.exp(last) * st + lax.dot_general(v, kd, _TN, preferred_element_type=F32)

            o = o[:rows]
            o = o * lax.rsqrt(jnp.mean(o * o, axis=-1, keepdims=True) + EPS) * gn[:, vs]
            zoh = zo[:, vs]
            oa_ref[s, :, vs] = o * (zoh * _sigmoid(zoh))

    @pl.when(c_id == pl.num_programs(1) - 1)
    def _():
        for s in range(nb):
            for h in range(heads):
                sout_ref[s, h] = st_scr[s, h].T


def _hgrn(za, nb, rows, lb_theta, gn, s0, heads, dk, dv):
    nseq, length, cols = za.shape
    chunk = HGRN_CHUNK if rows > HGRN_SHORT_CHUNK else HGRN_SHORT_CHUNK
    mats, masks = _hgrn_tables(chunk)
    mats = jnp.asarray(np.tile(mats, (1, 3)), BF16)
    masks = jnp.asarray(masks, F32)
    wd = heads * dv
    shared = s0.shape[0] == 1
    s0_spec = (pl.BlockSpec((1, heads, dk, dv), lambda s, c: (0, 0, 0, 0)) if shared
               else pl.BlockSpec((nb, heads, dk, dv), lambda s, c: (s, 0, 0, 0)))
    body = functools.partial(_hgrn_body, nb=nb, rows=rows, chunk=chunk, heads=heads, dk=dk, dv=dv)
    return pl.pallas_call(
        body,
        grid=(nseq // nb, length // rows),
        in_specs=[pl.BlockSpec((nb, rows, cols), lambda s, c: (s, c, 0)),
                  _const_spec(lb_theta.shape), _const_spec(gn.shape), s0_spec,
                  _const_spec(mats.shape), _const_spec(masks.shape)],
        out_specs=[pl.BlockSpec((nb, rows, wd), lambda s, c: (s, c, 0)),
                   pl.BlockSpec((nb, heads, dk, dv), lambda s, c: (s, 0, 0, 0))],
        out_shape=[jax.ShapeDtypeStruct((nseq, length, wd), F32),
                   jax.ShapeDtypeStruct((nseq, heads, dk, dv), F32)],
        scratch_shapes=[pltpu.VMEM((nb, heads, dv, dk), F32)],
        compiler_params=_params("parallel", "arbitrary"),
        name="hgrn",
    )(za, lb_theta, gn, s0, mats, masks)


def _rwkv_body(zb_ref, sh0_ref, s0_ref, mu_ref, w0_ref, ww_ref, a0_ref, aw_ref, gw_ref, kk_ref, ka_ref,
               rk_ref, lnw_ref, lnb_ref, ones_ref, seg_ref,
               ob_ref, sout_ref, shout_ref,
               s_scr, prev_scr, al_scr, w_scr, be_scr, k_scr, r_scr, v_scr, g_scr, y_scr, vt_scr,
               *, nb, tc, hw, hd, lora):
    t_id = pl.program_id(1)
    pairs = hw // LANES
    rin = zb_ref.shape[-1]

    @pl.when(t_id == 0)
    def _():
        for b in range(nb):
            for p in range(pairs):
                s_scr[b, p] = jnp.concatenate([s0_ref[b, 2 * p], s0_ref[b, 2 * p + 1]], axis=1)
        prev_scr[...] = sh0_ref[...]

    row = lax.broadcasted_iota(jnp.int32, (tc, rin), 0)
    mu = mu_ref[...]
    zms = []
    for b in range(nb):
        zb = zb_ref[b]
        prev = jnp.where(row == 0, prev_scr[b:b + 1, :], pltpu.roll(zb, 1, axis=0))
        prev_scr[b:b + 1, :] = zb[tc - 1:tc, :]
        zms.append(zb + (prev - zb) * mu)
    zm = jnp.concatenate(zms, axis=0)
    r = zm[:, :hw]
    k = zm[:, hw:2 * hw]
    v = zm[:, 2 * hw:3 * hw]
    dwa = zm[:, 3 * hw:3 * hw + lora]
    dg = zm[:, 3 * hw + lora:]
    w_raw = w0_ref[...] + _dot_hi(jnp.tanh(dwa), ww_ref[...])
    decay = jnp.exp(-np.float32(np.exp(-0.5)) * _sigmoid(w_raw))
    a = _sigmoid(a0_ref[...] + _dot_hi(dwa, aw_ref[...]))
    g = _dot_hi(_sigmoid(dg), gw_ref[...])
    kk = k * kk_ref[...]
    ss = _head_sums(kk * kk, seg_ref[...])
    kk = kk / jnp.maximum(jnp.sqrt(ss), 1e-12)
    k2 = k * (1.0 + (a - 1.0) * ka_ref[...])
    shp = (nb, tc, hw)
    al_scr[...] = (-kk).reshape(shp)
    w_scr[...] = decay.reshape(shp)
    be_scr[...] = (kk * a).reshape(shp)
    k_scr[...] = k2.reshape(shp)
    r_scr[...] = r.reshape(shp)
    v_scr[...] = v.reshape(shp)
    g_scr[...] = g.reshape(shp)

    slot = RWKV_SLOT
    li2 = lax.broadcasted_iota(jnp.int32, (slot, LANES), 1)
    vi = lax.broadcasted_iota(jnp.int32, (hd, LANES), 0)
    sel = jnp.where(vi == lax.broadcasted_iota(jnp.int32, (hd, LANES), 1) % hd, 1.0, 0.0).astype(BF16)
    for b in range(nb):
        for p in range(pairs):
            for g0 in range(0, tc, slot):
                nt = min(slot, tc - g0)
                vb = v[b * tc + g0:b * tc + g0 + nt, p * LANES:(p + 1) * LANES]
                if nt < slot:
                    vb = jnp.concatenate([vb, jnp.zeros((slot - nt, LANES), F32)], axis=0)
                vh = vb.astype(BF16)
                vl = (vb - vh.astype(F32)).astype(BF16)
                zero = jnp.zeros_like(vh)
                stack = jnp.concatenate([jnp.where(li2 < hd, vh, zero), jnp.where(li2 >= hd, vh, zero),
                                         jnp.where(li2 < hd, vl, zero), jnp.where(li2 >= hd, vl, zero)], axis=0)
                vt_scr[g0 // slot, b, p] = lax.dot_general(
                    sel, stack, _NT, preferred_element_type=F32).astype(BF16)

    rows = pairs * hd
    li = lax.broadcasted_iota(jnp.int32, (rows, LANES), 1)
    key_lane = li % hd
    ones1 = ones_ref[...]
    ones_f = ones1.astype(F32)
    qi = lax.broadcasted_iota(jnp.int32, (4 * slot, LANES), 0)
    ql = lax.broadcasted_iota(jnp.int32, (4 * slot, LANES), 1)
    tok_of_row = jnp.where((qi // slot) % 2 == ql // hd, qi % slot, -1)

    def bcast(ref, b, t):
        rowv = ref[b, pl.ds(t, 1), :]
        return jnp.concatenate(
            [jnp.broadcast_to(rowv[:, p * LANES:(p + 1) * LANES], (hd, LANES)) for p in range(pairs)], axis=0)

    def step(t, carry):
        pick = jnp.where(tok_of_row == t % slot, 1.0, 0.0).astype(BF16)
        seqs = range(nb)
        vcols = [_dot(vt_scr[t // slot, b].reshape(rows, 4 * slot), pick) for b in seqs]
        for b in seqs:
            old = s_scr[b].reshape(rows, LANES)
            sa = _dot(old * bcast(al_scr, b, t), ones_f)
            new = old * bcast(w_scr, b, t) + sa * bcast(be_scr, b, t) + vcols[b] * bcast(k_scr, b, t)
            s_scr[b] = new.reshape(pairs, hd, LANES)
            yrep = _dot(new * bcast(r_scr, b, t), ones_f)
            y_scr[b] = jnp.where(key_lane == t, yrep, y_scr[b])
        return carry

    y_scr[...] = jnp.zeros_like(y_scr)
    lax.fori_loop(0, tc, step, 0, unroll=8)

    def token_rows(b, p):
        yt = y_scr[b, p * hd:(p + 1) * hd, :].T
        return jnp.concatenate([yt[0:tc], yt[hd:hd + tc]], axis=1)

    y = jnp.concatenate(
        [jnp.concatenate([token_rows(b, p) for p in range(pairs)], axis=1) for b in range(nb)], axis=0)
    seg = seg_ref[...]
    inv = np.float32(1.0 / hd)
    mean = _head_sums(y, seg) * inv
    yc = y - mean
    var = _head_sums(yc * yc, seg) * inv
    yn = yc * lax.rsqrt(var + GN_EPS) * lnw_ref[...] + lnb_ref[...]
    r = r_scr[...].reshape(nb * tc, hw)
    k2 = k_scr[...].reshape(nb * tc, hw)
    v = v_scr[...].reshape(nb * tc, hw)
    bonus = _head_sums(r * k2 * rk_ref[...], seg) * v
    ob_ref[...] = ((yn + bonus) * g_scr[...].reshape(nb * tc, hw)).reshape(shp)

    @pl.when(t_id == pl.num_programs(1) - 1)
    def _():
        for b in range(nb):
            for p in range(pairs):
                sout_ref[b, 2 * p] = s_scr[b, p, :, 0:hd]
                sout_ref[b, 2 * p + 1] = s_scr[b, p, :, hd:2 * hd]
        shout_ref[...] = prev_scr[...]


def _rwkv(zb, sh0, s0, wts, nb, tc, hd):
    nseq, length, rin = zb.shape
    hw = wts["w0"].shape[1]
    pairs = hw // LANES
    lora = wts["ww"].shape[0]
    body = functools.partial(_rwkv_body, nb=nb, tc=tc, hw=hw, hd=hd, lora=lora)
    names = ("mu", "w0", "ww", "a0", "aw", "gw", "kk", "ka", "rk", "lnw", "lnb", "ones", "seg")
    consts = [wts[n] for n in names]
    tok = pltpu.VMEM((nb, tc, hw), F32)
    return pl.pallas_call(
        body,
        grid=(nseq // nb, length // tc),
        in_specs=[pl.BlockSpec((nb, tc, rin), lambda s, t: (s, t, 0)),
                  pl.BlockSpec((nb, rin), lambda s, t: (s, 0)),
                  pl.BlockSpec((nb, 2 * pairs, hd, hd), lambda s, t: (s, 0, 0, 0))]
                 + [_const_spec(c.shape) for c in consts],
        out_specs=[pl.BlockSpec((nb, tc, hw), lambda s, t: (s, t, 0)),
                   pl.BlockSpec((nb, 2 * pairs, hd, hd), lambda s, t: (s, 0, 0, 0)),
                   pl.BlockSpec((nb, rin), lambda s, t: (s, 0))],
        out_shape=[jax.ShapeDtypeStruct((nseq, length, hw), F32),
                   jax.ShapeDtypeStruct((nseq, 2 * pairs, hd, hd), F32),
                   jax.ShapeDtypeStruct((nseq, rin), F32)],
        scratch_shapes=[pltpu.VMEM((nb, pairs, hd, LANES), F32), pltpu.VMEM((nb, rin), F32)]
                       + [tok] * 7 + [pltpu.VMEM((nb, pairs * hd, LANES), F32),
                                      pltpu.VMEM((-(-tc // RWKV_SLOT), nb, pairs, hd, 4 * RWKV_SLOT), BF16)],
        compiler_params=_params("parallel", "arbitrary"),
        name="rwkv",
    )(zb, sh0, s0, *consts)


def _out_proj_body(oa_ref, ob_ref, x_ref, woa_ref, wob_ref, g2_ref, wq_ref, keys_ref,
                   x1_ref, h2t_ref, st_ref, *, nk, dh):
    mix = _dot(oa_ref[...].astype(BF16), woa_ref[...]) + _dot(ob_ref[...].astype(BF16), wob_ref[...])
    x1 = x_ref[...] + mix
    x1_ref[...] = x1
    h2f = _rms(x1, g2_ref[...])
    h2t_ref[...] = h2f.T.astype(BF16)
    q = _dot(h2f.astype(BF16), wq_ref[...]).astype(BF16)
    for hp in range(q.shape[1] // dh):
        st_ref[hp * nk:(hp + 1) * nk, :] = lax.dot_general(
            keys_ref[hp % 2], q[:, hp * dh:(hp + 1) * dh], _NT, preferred_element_type=F32)


def _out_proj(oa, ob, x, woa, wob, g2, wq, keys, tb):
    n, d = x.shape
    nk, dh = keys.shape[1], keys.shape[2]
    nrow = (wq.shape[1] // dh) * nk
    body = functools.partial(_out_proj_body, nk=nk, dh=dh)
    return pl.pallas_call(
        body,
        grid=(n // tb,),
        in_specs=[pl.BlockSpec((tb, oa.shape[1]), lambda i: (i, 0)),
                  pl.BlockSpec((tb, ob.shape[1]), lambda i: (i, 0)),
                  pl.BlockSpec((tb, d), lambda i: (i, 0)),
                  _const_spec(woa.shape), _const_spec(wob.shape), _const_spec(g2.shape),
                  _const_spec(wq.shape), _const_spec(keys.shape)],
        out_specs=[pl.BlockSpec((tb, d), lambda i: (i, 0)), pl.BlockSpec((d, tb), lambda i: (0, i)),
                   pl.BlockSpec((nrow, tb), lambda i: (0, i))],
        out_shape=[jax.ShapeDtypeStruct((n, d), F32), jax.ShapeDtypeStruct((d, n), BF16),
                   jax.ShapeDtypeStruct((nrow, n), F32)],
        compiler_params=_params("parallel"),
        name="out_proj",
    )(oa, ob, x, woa, wob, g2, wq, keys)


def _sort16_network():
    n, pairs, p = 16, [], 1
    while p < n:
        k = p
        while k >= 1:
            for j in range(k % p, n - k, 2 * k):
                for i in range(min(k, n - j - k)):
                    if (i + j) // (2 * p) == (i + j + k) // (2 * p):
                        pairs.append((i + j, i + j + k))
            k //= 2
        p *= 2
    return pairs


def _top16_desc(tiles):
    t = list(tiles)

    def exchange(i, j):
        t[i], t[j] = jnp.maximum(t[i], t[j]), jnp.minimum(t[i], t[j])

    for i, j in _sort16_network():
        exchange(i, j)
    for shift in (4, 2, 1):
        t = [jnp.maximum(t[i], pltpu.roll(t[15 - i], shift, axis=0)) for i in range(16)]
        for d in (8, 4, 2, 1):
            for i in range(16):
                if not i & d:
                    exchange(i, i + d)
    return t


def _route_body(st_ref, t1_ref, e1_ref, e2_ref, top_scr, cand_scr, *, nk, topk):
    neg = -jnp.inf
    nx = topk + 1
    half = topk // 2

    def top_values(h, s):
        tiles = [s[i * SUBLANES:(i + 1) * SUBLANES] for i in range(nk // SUBLANES)]
        srt = _top16_desc(tiles)
        for i in range(topk):
            top_scr[h, i:i + 1, :] = srt[i][0:1]
        top_scr[h, topk:nx, :] = jnp.max(jnp.where(s < srt[topk - 1][0:1], s, neg), axis=0, keepdims=True)

    s1 = st_ref[0:nk, :]
    s2 = st_ref[nk:2 * nk, :]
    top_values(0, s1)
    top_values(1, s2)
    cand_scr[...] = jnp.full(cand_scr.shape, neg, F32)
    sv2 = top_scr[1, 0:nx, :]
    cand_scr[0:nx, :] = top_scr[0, 0:1, :] + sv2
    base = 3 * SUBLANES
    for a in range(1, half):
        cand_scr[base + (a - 1) * half:base + a * half, :] = top_scr[0, a:a + 1, :] + sv2[0:half]
    base += (half - 1) * half
    cand_scr[base:base + nx - half, :] = top_scr[0, half:nx, :] + sv2[0:1]

    cand = cand_scr[...]
    best = _top16_desc([cand[i * SUBLANES:(i + 1) * SUBLANES] for i in range(nk // SUBLANES)])
    z = jnp.zeros_like(best[0][0:1])
    for i in range(topk):
        z = z + jnp.exp(best[i][0:1] - best[0][0:1])
    hi = best[topk - 1][0:1]
    lo = jnp.max(jnp.where(cand < hi, cand, neg), axis=0, keepdims=True)
    thr = jnp.where(lo == neg, hi, 0.5 * (hi + lo))
    t1_ref[0] = thr - s1
    e1_ref[0] = jnp.exp(s1 - top_scr[0, 0:1, :]) * (np.float32(np.sqrt(0.5)) / z)
    e2_ref[0] = jnp.exp(s2 - sv2[0:1])


def _route(st, heads, nk, tb):
    n = st.shape[1]
    topk = PEER_TOPK
    half = topk // 2
    ncand = nk
    assert topk == 16 and nk == 16 * SUBLANES and half == SUBLANES
    assert 3 * SUBLANES + (half - 1) * half + topk + 1 - half <= ncand
    body = functools.partial(_route_body, nk=nk, topk=topk)
    blk = pl.BlockSpec((1, nk, tb), lambda i, h: (h, 0, i))
    return pl.pallas_call(
        body,
        grid=(n // tb, heads),
        in_specs=[pl.BlockSpec((2 * nk, tb), lambda i, h: (h, i))],
        out_specs=[blk, blk, blk],
        out_shape=[jax.ShapeDtypeStruct((heads, nk, n), F32)] * 3,
        scratch_shapes=[pltpu.VMEM((2, 3 * SUBLANES, tb), F32), pltpu.VMEM((ncand, tb), F32)],
        compiler_params=_params("parallel", "parallel"),
        name="route",
    )(st)


PEER_SUB = 2 * LANES
PEER_ROW_SPLIT = 2


def _peer_body(h2t_ref, s2_ref, t1_ref, e1_ref, e2_ref, u_ref, vt_ref, x1_ref, gf_ref, y_ref,
               acc_scr, coef_scr, *, heads, nk, tb, n_tb):
    i = pl.program_id(0)
    e = pl.program_id(1)
    eblk = 2 * u_ref.shape[0]
    i1_per_block = eblk // nk
    i1_per_sub = PEER_SUB // nk
    cur = e % 2

    @pl.when((i == 0) & (e == 0))
    def _():
        acc_scr[...] = jnp.zeros_like(acc_scr)
        coef_scr[1] = jnp.zeros(coef_scr.shape[1:], BF16)

    @pl.when((i < n_tb) | (e == 0))
    def _():
        i1_0 = pl.multiple_of(e * i1_per_block, SUBLANES)
        n_sub = eblk // PEER_SUB
        h2t = h2t_ref[...]

        acc_scr[...] += _dot(pltpu.bitcast(vt_ref[...], BF16), coef_scr[1 - cur])

        def activations(k):
            u_rows = pltpu.bitcast(u_ref[k * PEER_SUB // 2:(k + 1) * PEER_SUB // 2, :], BF16)
            return _dot(u_rows, h2t)

        act = activations(0)
        for k in range(n_sub):
            if k + 1 < n_sub:
                act_next = activations(k + 1)
            for lg in range(tb // LANES):
                ls = slice(lg * LANES, (lg + 1) * LANES)
                t1b = [t1_ref[h, pl.ds(i1_0, i1_per_block), ls] for h in range(heads)]
                e1b = [e1_ref[h, pl.ds(i1_0, i1_per_block), ls] for h in range(heads)]
                for rs in range(PEER_ROW_SPLIT):
                    rsl = slice(rs * nk // PEER_ROW_SPLIT, (rs + 1) * nk // PEER_ROW_SPLIT)
                    gates = [None] * i1_per_sub
                    for h in range(heads):
                        s2 = s2_ref[h, rsl, ls]
                        e2 = e2_ref[h, rsl, ls]
                        for jj in range(i1_per_sub):
                            j = k * i1_per_sub + jj
                            term = jnp.where(s2 >= t1b[h][j:j + 1], e1b[h][j:j + 1] * e2, 0.0)
                            gates[jj] = term if gates[jj] is None else gates[jj] + term
                    for jj in range(i1_per_sub):
                        a = act[jj * nk + rsl.start:jj * nk + rsl.stop, ls]
                        r0 = (k * i1_per_sub + jj) * nk
                        coef_scr[cur, r0 + rsl.start:r0 + rsl.stop, ls] = (gates[jj] * (a + a * lax.erf(a))).astype(BF16)
            if k + 1 < n_sub:
                act = act_next

    @pl.when(e == 0)
    def _():
        @pl.when(i > 0)
        def _():
            y_ref[...] = _rms(x1_ref[...] + acc_scr[...].T, gf_ref[...])

        acc_scr[...] = jnp.zeros_like(acc_scr)


def _peer(h2t, st, t1, e1, e2, u, vt, x1, gf, heads, nk, tb):
    n, d = x1.shape
    ne = 2 * u.shape[0]
    eblk = SUBLANES * nk
    n_eb = ne // eblk
    n_tb = n // tb
    assert n_eb % 2 == 0
    body = functools.partial(_peer_body, heads=heads, nk=nk, tb=tb, n_tb=n_tb)
    this = lambda i: jnp.minimum(i, n_tb - 1)
    done = lambda i: jnp.maximum(i - 1, 0)
    tok = pl.BlockSpec((heads, nk, tb), lambda i, e: (0, 0, this(i)))
    return pl.pallas_call(
        body,
        grid=(n_tb + 1, n_eb),
        in_specs=[pl.BlockSpec((d, tb), lambda i, e: (0, this(i))),
                  pl.BlockSpec((heads, None, nk, tb), lambda i, e: (0, 1, 0, this(i))),
                  tok, tok, tok,
                  pl.BlockSpec((eblk // 2, d), lambda i, e: (jnp.where(i < n_tb, e, 0), 0)),
                  pl.BlockSpec((d // 2, eblk), lambda i, e: (0, jnp.where(i < n_tb, (e + n_eb - 1) % n_eb, n_eb - 1))),
                  pl.BlockSpec((tb, d), lambda i, e: (done(i), 0)),
                  _const_spec(gf.shape)],
        out_specs=pl.BlockSpec((tb, d), lambda i, e: (done(i), 0)),
        out_shape=jax.ShapeDtypeStruct((n, d), F32),
        scratch_shapes=[pltpu.VMEM((d, tb), F32), pltpu.VMEM((2, eblk, tb), BF16)],
        compiler_params=_params("arbitrary", "arbitrary"),
        name="peer",
    )(h2t, st.reshape(heads, 2, nk, n), t1, e1, e2, u, vt, x1, gf)


def _pack_body(x_ref, o_ref, *, transpose, scale):
    x = x_ref[...] * np.float32(scale)
    if transpose:
        x = x.T
    o_ref[...] = pltpu.bitcast(x.astype(BF16), jnp.uint32)


def _pack_table(x, transpose, scale=1.0, rows=512):
    n, c = x.shape
    if transpose:
        out_shape, out_spec = (c // 2, n), pl.BlockSpec((c // 2, rows), lambda i: (0, i))
    else:
        out_shape, out_spec = (n // 2, c), pl.BlockSpec((rows // 2, c), lambda i: (i, 0))
    return pl.pallas_call(
        functools.partial(_pack_body, transpose=transpose, scale=scale),
        grid=(n // rows,),
        in_specs=[pl.BlockSpec((rows, c), lambda i: (i, 0))],
        out_specs=out_spec,
        out_shape=jax.ShapeDtypeStruct(out_shape, jnp.uint32),
        compiler_params=_params("parallel"),
        name="pack_table",
    )(x)


def _block(n, pref):
    b = pref
    while n % b:
        b //= 2
    return b


def kernel(x_prompt, x_sample, state_hgrn, state_rwkv, state_shift, meta_tokens, norm1, w_in, lb_theta, hgrn_norm, mu_shift, w0, w_w2, a0, a_w2, g_w2, k_k, k_a, r_k, ln_w, ln_b, w_out, norm2, peer_wq, peer_keys, peer_u, peer_v, norm_f):
    assert w_in.shape[0] == 1, "single-layer trunk"
    bp, sp, d = x_prompt.shape
    bs, ss, _ = x_sample.shape
    n_meta = meta_tokens.shape[0]
    _, _, ha_heads, dk, dv = state_hgrn.shape
    _, _, hb_heads, hd, _ = state_rwkv.shape
    ha_in = 2 * ha_heads * dk + 2 * ha_heads * dv
    hw = hb_heads * hd
    lw, la = w_w2.shape[1], a_w2.shape[1]
    nk, dh = peer_keys.shape[2], peer_keys.shape[3]
    p_heads = peer_wq.shape[2] // (2 * dh)
    assert sp % HGRN_CHUNK == 0 and ss <= HGRN_CHUNK and n_meta <= HGRN_CHUNK
    assert hb_heads % 2 == 0 and 2 * hd == LANES and lw + la == LANES and nk == LANES

    row = lambda t: t.reshape(1, -1).astype(F32)
    w_a = w_in[0, :, :ha_in].astype(BF16)
    w_b = w_in[0, :, ha_in:].astype(BF16)
    g1 = row(norm1[0])
    lbt = lb_theta.astype(F32)
    gn = row(hgrn_norm[0])

    head_of_lane = np.arange(LANES) // hd
    rw = {
        "mu": row(mu_shift[0]), "w0": row(w0[0]), "a0": row(a0[0]), "kk": row(k_k[0]), "ka": row(k_a[0]),
        "rk": row(r_k[0]), "lnw": row(ln_w[0]), "lnb": row(ln_b[0]), "gw": g_w2[0].astype(F32),
        "ww": jnp.concatenate([w_w2[0], jnp.zeros((la, hw), F32)], axis=0),
        "aw": jnp.concatenate([jnp.zeros((lw, hw), F32), a_w2[0]], axis=0),
        "ones": jnp.asarray(head_of_lane[:, None] == head_of_lane[None, :], BF16),
        "seg": jnp.asarray(np.tile(head_of_lane[:, None] == head_of_lane[None, :], (2, 1)), BF16),
    }
    woa = w_out[0, :ha_heads * dv].astype(BF16)
    wob = w_out[0, ha_heads * dv:].astype(BF16)
    g2 = row(norm2[0])
    wq = peer_wq[0].astype(BF16)
    keys = peer_keys[0].astype(BF16)
    u = _pack_table(peer_u[0].astype(F32), False, scale=np.sqrt(0.5))
    vt = _pack_table(peer_v[0].astype(F32), True)
    gf = row(norm_f)

    za_m, zb_m = _in_proj(meta_tokens.astype(F32), g1, w_a, w_b, n_meta)
    _, hs_m = _hgrn(za_m[None], 1, n_meta, lbt, gn, jnp.zeros((1, ha_heads, dk, dv), F32), ha_heads, dk, dv)
    _, rs_m, sh_m = _rwkv(zb_m.reshape(1, n_meta, -1), jnp.zeros((1, zb_m.shape[1]), F32),
                          jnp.zeros((1, hb_heads, hd, hd), F32), rw, 1, n_meta, hd)

    def group(x, nseq, length, hs0, rs0, sh0, chunk_rows, hgrn_nb, tc):
        xf = x.reshape(nseq * length, d)
        n = xf.shape[0]
        tb = _block(n, PEER_TOKENS)
        tb_wide = _block(n, WIDE_TOKENS)
        assert tb % LANES == 0
        za, zb = _in_proj(xf, g1, w_a, w_b, tb_wide)
        oa, hs = _hgrn(za.reshape(nseq, length, -1), hgrn_nb, chunk_rows, lbt, gn, hs0, ha_heads, dk, dv)
        ob, rs, sh = _rwkv(zb.reshape(nseq, length, -1), sh0, rs0, rw, 8, tc, hd)
        x1, h2t, st = _out_proj(oa.reshape(n, -1), ob.reshape(n, hw), xf, woa, wob, g2, wq, keys, tb_wide)
        t1, e1, e2 = _route(st, p_heads, nk, tb_wide)
        y = _peer(h2t, st, t1, e1, e2, u, vt, x1, gf, p_heads, nk, tb)
        return y.reshape(nseq, length, d), hs, rs, sh

    y_p, hs_p, rs_p, sh_p = group(
        x_prompt, bp, sp, hs_m, jnp.broadcast_to(rs_m, (bp,) + rs_m.shape[1:]),
        jnp.broadcast_to(sh_m, (bp, sh_m.shape[1])), HGRN_CHUNK, 8, 64)
    y_s, hs_s, rs_s, sh_s = group(
        x_sample, bs, ss, state_hgrn[0].astype(F32), state_rwkv[0].astype(F32),
        state_shift[0].astype(F32), ss, 8, ss)

    return (y_p, y_s, hs_p[None], rs_p[None], sh_p[None], hs_s[None], rs_s[None], sh_s[None])
```

```python
import functools

import numpy as np
import jax
import jax.numpy as jnp
from jax import lax
from jax.experimental import pallas as pl
from jax.experimental.pallas import tpu as pltpu

F32 = jnp.float32
BF16 = jnp.bfloat16

EPS = 1e-6
GN_EPS = 64e-5
HGRN_CHUNK = 64
HGRN_SHORT_CHUNK = 16
PEER_TOPK = 16
PEER_TOKENS = 256
WIDE_TOKENS = 512
RWKV_SLOT = 32
LANES = 128
SUBLANES = 8
VMEM_LIMIT_BYTES = 52 * 1024 * 1024

_NT = (((1,), (1,)), ((), ()))
_TN = (((0,), (0,)), ((), ()))


def _dot(a, b):
    return jnp.dot(a, b, preferred_element_type=F32)


def _split3(a):
    a1 = a.astype(BF16)
    r1 = a - a1.astype(F32)
    a2 = r1.astype(BF16)
    a3 = (r1 - a2.astype(F32)).astype(BF16)
    return a1, a2, a3


def _head_sums(x, ones2):
    out = []
    for c0 in range(0, x.shape[1], LANES):
        xs = x[:, c0:c0 + LANES]
        hi = xs.astype(BF16)
        lo = (xs - hi.astype(F32)).astype(BF16)
        out.append(_dot(jnp.concatenate([hi, lo], axis=1), ones2))
    return jnp.concatenate(out, axis=1)


def _dot_exact_lhs(a_exact3, b):
    return _dot(a_exact3, jnp.concatenate(_split3(b), axis=0))


def _dot_hi(a, b):
    ah = a.astype(BF16)
    al = (a - ah.astype(F32)).astype(BF16)
    bh = b.astype(BF16)
    bl = (b - bh.astype(F32)).astype(BF16)
    return _dot(ah, bh) + _dot(ah, bl) + _dot(al, bh)


def _sigmoid(x):
    return 1.0 / (1.0 + jnp.exp(-x))


def _rms(x, g):
    return x * lax.rsqrt(jnp.mean(x * x, axis=-1, keepdims=True) + EPS) * g


def _params(*sem):
    return pltpu.CompilerParams(dimension_semantics=sem, vmem_limit_bytes=VMEM_LIMIT_BYTES)


def _const_spec(shape):
    nd = len(shape)
    return pl.BlockSpec(shape, lambda *_: (0,) * nd)


def _in_proj_body(x_ref, g_ref, wa_ref, wb_ref, za_ref, zb_ref):
    hb = _rms(x_ref[...], g_ref[...]).astype(BF16)
    za_ref[...] = _dot(hb, wa_ref[...])
    zb_ref[...] = _dot(hb, wb_ref[...])


def _in_proj(x, g, wa, wb, tb):
    n, d = x.shape
    na, nb = wa.shape[1], wb.shape[1]
    return pl.pallas_call(
        _in_proj_body,
        grid=(n // tb,),
        in_specs=[pl.BlockSpec((tb, d), lambda i: (i, 0)), _const_spec((1, d)),
                  _const_spec(wa.shape), _const_spec(wb.shape)],
        out_specs=[pl.BlockSpec((tb, na), lambda i: (i, 0)), pl.BlockSpec((tb, nb), lambda i: (i, 0))],
        out_shape=[jax.ShapeDtypeStruct((n, na), F32), jax.ShapeDtypeStruct((n, nb), F32)],
        compiler_params=_params("parallel"),
        name="in_proj",
    )(x, g, wa, wb)


def _hgrn_tables(c):
    nl = int(np.log2(c)) + 1
    mats = np.zeros((nl, c, c), np.float32)
    masks = np.zeros((nl, c, c), np.float32)
    idx = np.arange(c)
    mats[0] = (idx[None, :] <= idx[:, None])
    masks[0] = np.eye(c)
    for l in range(1, nl):
        blk = 1 << l
        half = blk >> 1
        for t in range(c):
            start = (t // blk) * blk
            ref = start + half - 1
            if t - start >= half:
                mats[l, t, ref + 1:t + 1] = 1.0
                masks[l, t, start:start + half] = 1.0
            else:
                mats[l, t, t + 1:ref + 1] = 1.0
    return mats.reshape(nl * c, c), masks


def _hgrn_body(za_ref, lbt_ref, gn_ref, s0_ref, mats_ref, masks_ref, oa_ref, sout_ref, st_scr,
               *, nb, rows, chunk, heads, dk, dv):
    c_id = pl.program_id(1)
    qk = heads * dk
    wd = heads * dv
    nl = masks_ref.shape[0]

    @pl.when(c_id == 0)
    def _():
        for s in range(nb):
            for h in range(heads):
                st_scr[s, h] = s0_ref[min(s, s0_ref.shape[0] - 1), h].T

    th = lbt_ref[...]
    ex = jnp.exp(th - jnp.max(th, axis=0, keepdims=True))
    lb = ex[0:1] / jnp.sum(ex, axis=0, keepdims=True)
    gn = gn_ref[...]

    for s in range(nb):
        za = za_ref[s]
        zq = za[:, :qk]
        zf = za[:, qk:2 * qk]
        zi = za[:, 2 * qk:2 * qk + wd]
        zo = za[:, 2 * qk + wd:]
        logf = jnp.log(lb + (1.0 - lb) * _sigmoid(zf))
        kin = (1.0 - lb) * _sigmoid(-zf)
        qa = zq * _sigmoid(zq)
        if rows < chunk:
            pad = jnp.zeros((chunk - rows, qk), F32)
            logf = jnp.concatenate([logf, pad], axis=0)
            kin = jnp.concatenate([kin, pad], axis=0)
            qa = jnp.concatenate([qa, pad], axis=0)
            zi = jnp.concatenate([zi, jnp.zeros((chunk - rows, wd), F32)], axis=0)

        dec = _dot_exact_lhs(mats_ref[...], logf)
        for h in range(heads):
            ks = slice(h * dk, (h + 1) * dk)
            vs = slice(h * dv, (h + 1) * dv)
            q = qa[:, ks]
            k = kin[:, ks]
            v = zi[:, vs].astype(BF16)
            cum = dec[0:chunk, ks]
            sc = lax.dot_general(q.astype(BF16), k.astype(BF16), _NT, preferred_element_type=F32)
            scores = jnp.where(masks_ref[0] != 0.0, sc, 0.0)
            for l in range(1, nl):
                e = jnp.exp(dec[l * chunk:(l + 1) * chunk, ks])
                sc = lax.dot_general((q * e).astype(BF16), (k * e).astype(BF16), _NT, preferred_element_type=F32)
                scores = scores + jnp.where(masks_ref[l] != 0.0, sc, 0.0)
            st = st_scr[s, h]
            o = _dot(scores.astype(BF16), v) + lax.dot_general(
                (q * jnp.exp(cum)).astype(BF16), st.astype(BF16), _NT, preferred_element_type=F32)
            last = cum[chunk - 1:chunk, :]
            kd = (k * jnp.exp(last - cum)).astype(BF16)
            st_scr[s, h] = jnp.exp(last) * st + lax.dot_general(v, kd, _TN, preferred_element_type=F32)

            o = o[:rows]
            o = o * lax.rsqrt(jnp.mean(o * o, axis=-1, keepdims=True) + EPS) * gn[:, vs]
            zoh = zo[:, vs]
            oa_ref[s, :, vs] = o * (zoh * _sigmoid(zoh))

    @pl.when(c_id == pl.num_programs(1) - 1)
    def _():
        for s in range(nb):
            for h in range(heads):
                sout_ref[s, h] = st_scr[s, h].T


def _hgrn(za, nb, rows, lb_theta, gn, s0, heads, dk, dv):
    nseq, length, cols = za.shape
    chunk = HGRN_CHUNK if rows > HGRN_SHORT_CHUNK else HGRN_SHORT_CHUNK
    mats, masks = _hgrn_tables(chunk)
    mats = jnp.asarray(np.tile(mats, (1, 3)), BF16)
    masks = jnp.asarray(masks, F32)
    wd = heads * dv
    shared = s0.shape[0] == 1
    s0_spec = (pl.BlockSpec((1, heads, dk, dv), lambda s, c: (0, 0, 0, 0)) if shared
               else pl.BlockSpec((nb, heads, dk, dv), lambda s, c: (s, 0, 0, 0)))
    body = functools.partial(_hgrn_body, nb=nb, rows=rows, chunk=chunk, heads=heads, dk=dk, dv=dv)
    return pl.pallas_call(
        body,
        grid=(nseq // nb, length // rows),
        in_specs=[pl.BlockSpec((nb, rows, cols), lambda s, c: (s, c, 0)),
                  _const_spec(lb_theta.shape), _const_spec(gn.shape), s0_spec,
                  _const_spec(mats.shape), _const_spec(masks.shape)],
        out_specs=[pl.BlockSpec((nb, rows, wd), lambda s, c: (s, c, 0)),
                   pl.BlockSpec((nb, heads, dk, dv), lambda s, c: (s, 0, 0, 0))],
        out_shape=[jax.ShapeDtypeStruct((nseq, length, wd), F32),
                   jax.ShapeDtypeStruct((nseq, heads, dk, dv), F32)],
        scratch_shapes=[pltpu.VMEM((nb, heads, dv, dk), F32)],
        compiler_params=_params("parallel", "arbitrary"),
        name="hgrn",
    )(za, lb_theta, gn, s0, mats, masks)


def _rwkv_body(zb_ref, sh0_ref, s0_ref, mu_ref, w0_ref, ww_ref, a0_ref, aw_ref, gw_ref, kk_ref, ka_ref,
               rk_ref, lnw_ref, lnb_ref, ones_ref, seg_ref,
               ob_ref, sout_ref, shout_ref,
               s_scr, prev_scr, al_scr, w_scr, be_scr, k_scr, r_scr, v_scr, g_scr, y_scr, vt_scr,
               *, nb, tc, hw, hd, lora):
    t_id = pl.program_id(1)
    pairs = hw // LANES
    rin = zb_ref.shape[-1]

    @pl.when(t_id == 0)
    def _():
        for b in range(nb):
            for p in range(pairs):
                s_scr[b, p] = jnp.concatenate([s0_ref[b, 2 * p], s0_ref[b, 2 * p + 1]], axis=1)
        prev_scr[...] = sh0_ref[...]

    row = lax.broadcasted_iota(jnp.int32, (tc, rin), 0)
    mu = mu_ref[...]
    zms = []
    for b in range(nb):
        zb = zb_ref[b]
        prev = jnp.where(row == 0, prev_scr[b:b + 1, :], pltpu.roll(zb, 1, axis=0))
        prev_scr[b:b + 1, :] = zb[tc - 1:tc, :]
        zms.append(zb + (prev - zb) * mu)
    zm = jnp.concatenate(zms, axis=0)
    r = zm[:, :hw]
    k = zm[:, hw:2 * hw]
    v = zm[:, 2 * hw:3 * hw]
    dwa = zm[:, 3 * hw:3 * hw + lora]
    dg = zm[:, 3 * hw + lora:]
    w_raw = w0_ref[...] + _dot_hi(jnp.tanh(dwa), ww_ref[...])
    decay = jnp.exp(-np.float32(np.exp(-0.5)) * _sigmoid(w_raw))
    a = _sigmoid(a0_ref[...] + _dot_hi(dwa, aw_ref[...]))
    g = _dot_hi(_sigmoid(dg), gw_ref[...])
    kk = k * kk_ref[...]
    ss = _head_sums(kk * kk, seg_ref[...])
    kk = kk / jnp.maximum(jnp.sqrt(ss), 1e-12)
    k2 = k * (1.0 + (a - 1.0) * ka_ref[...])
    shp = (nb, tc, hw)
    al_scr[...] = (-kk).reshape(shp)
    w_scr[...] = decay.reshape(shp)
    be_scr[...] = (kk * a).reshape(shp)
    k_scr[...] = k2.reshape(shp)
    r_scr[...] = r.reshape(shp)
    v_scr[...] = v.reshape(shp)
    g_scr[...] = g.reshape(shp)

    slot = RWKV_SLOT
    li2 = lax.broadcasted_iota(jnp.int32, (slot, LANES), 1)
    vi = lax.broadcasted_iota(jnp.int32, (hd, LANES), 0)
    sel = jnp.where(vi == lax.broadcasted_iota(jnp.int32, (hd, LANES), 1) % hd, 1.0, 0.0).astype(BF16)
    for b in range(nb):
        for p in range(pairs):
            for g0 in range(0, tc, slot):
                nt = min(slot, tc - g0)
                vb = v[b * tc + g0:b * tc + g0 + nt, p * LANES:(p + 1) * LANES]
                if nt < slot:
                    vb = jnp.concatenate([vb, jnp.zeros((slot - nt, LANES), F32)], axis=0)
                vh = vb.astype(BF16)
                vl = (vb - vh.astype(F32)).astype(BF16)
                zero = jnp.zeros_like(vh)
                stack = jnp.concatenate([jnp.where(li2 < hd, vh, zero), jnp.where(li2 >= hd, vh, zero),
                                         jnp.where(li2 < hd, vl, zero), jnp.where(li2 >= hd, vl, zero)], axis=0)
                vt_scr[g0 // slot, b, p] = lax.dot_general(
                    sel, stack, _NT, preferred_element_type=F32).astype(BF16)

    rows = pairs * hd
    li = lax.broadcasted_iota(jnp.int32, (rows, LANES), 1)
    key_lane = li % hd
    ones1 = ones_ref[...]
    ones_f = ones1.astype(F32)
    qi = lax.broadcasted_iota(jnp.int32, (4 * slot, LANES), 0)
    ql = lax.broadcasted_iota(jnp.int32, (4 * slot, LANES), 1)
    tok_of_row = jnp.where((qi // slot) % 2 == ql // hd, qi % slot, -1)

    def bcast(ref, b, t):
        rowv = ref[b, pl.ds(t, 1), :]
        return jnp.concatenate(
            [jnp.broadcast_to(rowv[:, p * LANES:(p + 1) * LANES], (hd, LANES)) for p in range(pairs)], axis=0)

    def step(t, carry):
        pick = jnp.where(tok_of_row == t % slot, 1.0, 0.0).astype(BF16)
        seqs = range(nb)
        vcols = [_dot(vt_scr[t // slot, b].reshape(rows, 4 * slot), pick) for b in seqs]
        for b in seqs:
            old = s_scr[b].reshape(rows, LANES)
            sa = _dot(old * bcast(al_scr, b, t), ones_f)
            new = old * bcast(w_scr, b, t) + sa * bcast(be_scr, b, t) + vcols[b] * bcast(k_scr, b, t)
            s_scr[b] = new.reshape(pairs, hd, LANES)
            yrep = _dot(new * bcast(r_scr, b, t), ones_f)
            y_scr[b] = jnp.where(key_lane == t, yrep, y_scr[b])
        return carry

    y_scr[...] = jnp.zeros_like(y_scr)
    lax.fori_loop(0, tc, step, 0, unroll=16)

    def token_rows(b, p):
        yt = y_scr[b, p * hd:(p + 1) * hd, :].T
        return jnp.concatenate([yt[0:tc], yt[hd:hd + tc]], axis=1)

    y = jnp.concatenate(
        [jnp.concatenate([token_rows(b, p) for p in range(pairs)], axis=1) for b in range(nb)], axis=0)
    seg = seg_ref[...]
    inv = np.float32(1.0 / hd)
    mean = _head_sums(y, seg) * inv
    yc = y - mean
    var = _head_sums(yc * yc, seg) * inv
    yn = yc * lax.rsqrt(var + GN_EPS) * lnw_ref[...] + lnb_ref[...]
    r = r_scr[...].reshape(nb * tc, hw)
    k2 = k_scr[...].reshape(nb * tc, hw)
    v = v_scr[...].reshape(nb * tc, hw)
    bonus = _head_sums(r * k2 * rk_ref[...], seg) * v
    ob_ref[...] = ((yn + bonus) * g_scr[...].reshape(nb * tc, hw)).reshape(shp)

    @pl.when(t_id == pl.num_programs(1) - 1)
    def _():
        for b in range(nb):
            for p in range(pairs):
                sout_ref[b, 2 * p] = s_scr[b, p, :, 0:hd]
                sout_ref[b, 2 * p + 1] = s_scr[b, p, :, hd:2 * hd]
        shout_ref[...] = prev_scr[...]


def _rwkv(zb, sh0, s0, wts, nb, tc, hd):
    nseq, length, rin = zb.shape
    hw = wts["w0"].shape[1]
    pairs = hw // LANES
    lora = wts["ww"].shape[0]
    body = functools.partial(_rwkv_body, nb=nb, tc=tc, hw=hw, hd=hd, lora=lora)
    names = ("mu", "w0", "ww", "a0", "aw", "gw", "kk", "ka", "rk", "lnw", "lnb", "ones", "seg")
    consts = [wts[n] for n in names]
    tok = pltpu.VMEM((nb, tc, hw), F32)
    return pl.pallas_call(
        body,
        grid=(nseq // nb, length // tc),
        in_specs=[pl.BlockSpec((nb, tc, rin), lambda s, t: (s, t, 0)),
                  pl.BlockSpec((nb, rin), lambda s, t: (s, 0)),
                  pl.BlockSpec((nb, 2 * pairs, hd, hd), lambda s, t: (s, 0, 0, 0))]
                 + [_const_spec(c.shape) for c in consts],
        out_specs=[pl.BlockSpec((nb, tc, hw), lambda s, t: (s, t, 0)),
                   pl.BlockSpec((nb, 2 * pairs, hd, hd), lambda s, t: (s, 0, 0, 0)),
                   pl.BlockSpec((nb, rin), lambda s, t: (s, 0))],
        out_shape=[jax.ShapeDtypeStruct((nseq, length, hw), F32),
                   jax.ShapeDtypeStruct((nseq, 2 * pairs, hd, hd), F32),
                   jax.ShapeDtypeStruct((nseq, rin), F32)],
        scratch_shapes=[pltpu.VMEM((nb, pairs, hd, LANES), F32), pltpu.VMEM((nb, rin), F32)]
                       + [tok] * 7 + [pltpu.VMEM((nb, pairs * hd, LANES), F32),
                                      pltpu.VMEM((-(-tc // RWKV_SLOT), nb, pairs, hd, 4 * RWKV_SLOT), BF16)],
        compiler_params=_params("parallel", "arbitrary"),
        name="rwkv",
    )(zb, sh0, s0, *consts)


def _out_proj_body(oa_ref, ob_ref, x_ref, woa_ref, wob_ref, g2_ref, wq_ref, keys_ref,
                   x1_ref, h2t_ref, st_ref, *, nk, dh):
    mix = _dot(oa_ref[...].astype(BF16), woa_ref[...]) + _dot(ob_ref[...].astype(BF16), wob_ref[...])
    x1 = x_ref[...] + mix
    x1_ref[...] = x1
    h2f = _rms(x1, g2_ref[...])
    h2t_ref[...] = h2f.T.astype(BF16)
    q = _dot(h2f.astype(BF16), wq_ref[...]).astype(BF16)
    for hp in range(q.shape[1] // dh):
        st_ref[hp * nk:(hp + 1) * nk, :] = lax.dot_general(
            keys_ref[hp % 2], q[:, hp * dh:(hp + 1) * dh], _NT, preferred_element_type=F32)


def _out_proj(oa, ob, x, woa, wob, g2, wq, keys, tb):
    n, d = x.shape
    nk, dh = keys.shape[1], keys.shape[2]
    nrow = (wq.shape[1] // dh) * nk
    body = functools.partial(_out_proj_body, nk=nk, dh=dh)
    return pl.pallas_call(
        body,
        grid=(n // tb,),
        in_specs=[pl.BlockSpec((tb, oa.shape[1]), lambda i: (i, 0)),
                  pl.BlockSpec((tb, ob.shape[1]), lambda i: (i, 0)),
                  pl.BlockSpec((tb, d), lambda i: (i, 0)),
                  _const_spec(woa.shape), _const_spec(wob.shape), _const_spec(g2.shape),
                  _const_spec(wq.shape), _const_spec(keys.shape)],
        out_specs=[pl.BlockSpec((tb, d), lambda i: (i, 0)), pl.BlockSpec((d, tb), lambda i: (0, i)),
                   pl.BlockSpec((nrow, tb), lambda i: (0, i))],
        out_shape=[jax.ShapeDtypeStruct((n, d), F32), jax.ShapeDtypeStruct((d, n), BF16),
                   jax.ShapeDtypeStruct((nrow, n), F32)],
        compiler_params=_params("parallel"),
        name="out_proj",
    )(oa, ob, x, woa, wob, g2, wq, keys)


def _sort16_network():
    n, pairs, p = 16, [], 1
    while p < n:
        k = p
        while k >= 1:
            for j in range(k % p, n - k, 2 * k):
                for i in range(min(k, n - j - k)):
                    if (i + j) // (2 * p) == (i + j + k) // (2 * p):
                        pairs.append((i + j, i + j + k))
            k //= 2
        p *= 2
    return pairs


def _top16_desc(tiles):
    t = list(tiles)

    def exchange(i, j):
        t[i], t[j] = jnp.maximum(t[i], t[j]), jnp.minimum(t[i], t[j])

    for i, j in _sort16_network():
        exchange(i, j)
    for shift in (4, 2, 1):
        t = [jnp.maximum(t[i], pltpu.roll(t[15 - i], shift, axis=0)) for i in range(16)]
        for d in (8, 4, 2, 1):
            for i in range(16):
                if not i & d:
                    exchange(i, i + d)
    return t


def _route_body(st_ref, t1_ref, e1_ref, e2_ref, top_scr, cand_scr, *, nk, topk):
    neg = -jnp.inf
    nx = topk + 1
    half = topk // 2

    def top_values(h, s):
        tiles = [s[i * SUBLANES:(i + 1) * SUBLANES] for i in range(nk // SUBLANES)]
        srt = _top16_desc(tiles)
        for i in range(topk):
            top_scr[h, i:i + 1, :] = srt[i][0:1]
        top_scr[h, topk:nx, :] = jnp.max(jnp.where(s < srt[topk - 1][0:1], s, neg), axis=0, keepdims=True)

    s1 = st_ref[0:nk, :]
    s2 = st_ref[nk:2 * nk, :]
    top_values(0, s1)
    top_values(1, s2)
    cand_scr[...] = jnp.full(cand_scr.shape, neg, F32)
    sv2 = top_scr[1, 0:nx, :]
    cand_scr[0:nx, :] = top_scr[0, 0:1, :] + sv2
    base = 3 * SUBLANES
    for a in range(1, half):
        cand_scr[base + (a - 1) * half:base + a * half, :] = top_scr[0, a:a + 1, :] + sv2[0:half]
    base += (half - 1) * half
    cand_scr[base:base + nx - half, :] = top_scr[0, half:nx, :] + sv2[0:1]

    cand = cand_scr[...]
    best = _top16_desc([cand[i * SUBLANES:(i + 1) * SUBLANES] for i in range(nk // SUBLANES)])
    z = jnp.zeros_like(best[0][0:1])
    for i in range(topk):
        z = z + jnp.exp(best[i][0:1] - best[0][0:1])
    hi = best[topk - 1][0:1]
    lo = jnp.max(jnp.where(cand < hi, cand, neg), axis=0, keepdims=True)
    thr = jnp.where(lo == neg, hi, 0.5 * (hi + lo))
    t1_ref[0] = thr - s1
    e1_ref[0] = jnp.exp(s1 - top_scr[0, 0:1, :]) * (np.float32(np.sqrt(0.5)) / z)
    e2_ref[0] = jnp.exp(s2 - sv2[0:1])


def _route(st, heads, nk, tb):
    n = st.shape[1]
    topk = PEER_TOPK
    half = topk // 2
    ncand = nk
    assert topk == 16 and nk == 16 * SUBLANES and half == SUBLANES
    assert 3 * SUBLANES + (half - 1) * half + topk + 1 - half <= ncand
    body = functools.partial(_route_body, nk=nk, topk=topk)
    blk = pl.BlockSpec((1, nk, tb), lambda i, h: (h, 0, i))
    return pl.pallas_call(
        body,
        grid=(n // tb, heads),
        in_specs=[pl.BlockSpec((2 * nk, tb), lambda i, h: (h, i))],
        out_specs=[blk, blk, blk],
        out_shape=[jax.ShapeDtypeStruct((heads, nk, n), F32)] * 3,
        scratch_shapes=[pltpu.VMEM((2, 3 * SUBLANES, tb), F32), pltpu.VMEM((ncand, tb), F32)],
        compiler_params=_params("parallel", "parallel"),
        name="route",
    )(st)


PEER_SUB = 2 * LANES
PEER_ROW_SPLIT = 2


def _peer_body(h2t_ref, s2_ref, t1_ref, e1_ref, e2_ref, u_ref, vt_ref, x1_ref, gf_ref, y_ref,
               acc_scr, coef_scr, *, heads, nk, tb, n_tb):
    i = pl.program_id(0)
    e = pl.program_id(1)
    eblk = 2 * u_ref.shape[0]
    i1_per_block = eblk // nk
    i1_per_sub = PEER_SUB // nk
    cur = e % 2

    @pl.when((i == 0) & (e == 0))
    def _():
        acc_scr[...] = jnp.zeros_like(acc_scr)
        coef_scr[1] = jnp.zeros(coef_scr.shape[1:], BF16)

    @pl.when((i < n_tb) | (e == 0))
    def _():
        i1_0 = pl.multiple_of(e * i1_per_block, SUBLANES)
        n_sub = eblk // PEER_SUB
        h2t = h2t_ref[...]

        acc_scr[...] += _dot(pltpu.bitcast(vt_ref[...], BF16), coef_scr[1 - cur])

        def activations(k):
            u_rows = pltpu.bitcast(u_ref[k * PEER_SUB // 2:(k + 1) * PEER_SUB // 2, :], BF16)
            return _dot(u_rows, h2t)

        act = activations(0)
        for k in range(n_sub):
            if k + 1 < n_sub:
                act_next = activations(k + 1)
            for lg in range(tb // LANES):
                ls = slice(lg * LANES, (lg + 1) * LANES)
                t1b = [t1_ref[h, pl.ds(i1_0, i1_per_block), ls] for h in range(heads)]
                e1b = [e1_ref[h, pl.ds(i1_0, i1_per_block), ls] for h in range(heads)]
                for rs in range(PEER_ROW_SPLIT):
                    rsl = slice(rs * nk // PEER_ROW_SPLIT, (rs + 1) * nk // PEER_ROW_SPLIT)
                    gates = [None] * i1_per_sub
                    for h in range(heads):
                        s2 = s2_ref[h, rsl, ls]
                        e2 = e2_ref[h, rsl, ls]
                        for jj in range(i1_per_sub):
                            j = k * i1_per_sub + jj
                            term = jnp.where(s2 >= t1b[h][j:j + 1], e1b[h][j:j + 1] * e2, 0.0)
                            gates[jj] = term if gates[jj] is None else gates[jj] + term
                    for jj in range(i1_per_sub):
                        a = act[jj * nk + rsl.start:jj * nk + rsl.stop, ls]
                        r0 = (k * i1_per_sub + jj) * nk
                        coef_scr[cur, r0 + rsl.start:r0 + rsl.stop, ls] = (gates[jj] * (a + a * lax.erf(a))).astype(BF16)
            if k + 1 < n_sub:
                act = act_next

    @pl.when(e == 0)
    def _():
        @pl.when(i > 0)
        def _():
            y_ref[...] = _rms(x1_ref[...] + acc_scr[...].T, gf_ref[...])

        acc_scr[...] = jnp.zeros_like(acc_scr)


def _peer(h2t, st, t1, e1, e2, u, vt, x1, gf, heads, nk, tb):
    n, d = x1.shape
    ne = 2 * u.shape[0]
    eblk = SUBLANES * nk
    n_eb = ne // eblk
    n_tb = n // tb
    assert n_eb % 2 == 0
    body = functools.partial(_peer_body, heads=heads, nk=nk, tb=tb, n_tb=n_tb)
    this = lambda i: jnp.minimum(i, n_tb - 1)
    done = lambda i: jnp.maximum(i - 1, 0)
    tok = pl.BlockSpec((heads, nk, tb), lambda i, e: (0, 0, this(i)))
    return pl.pallas_call(
        body,
        grid=(n_tb + 1, n_eb),
        in_specs=[pl.BlockSpec((d, tb), lambda i, e: (0, this(i))),
                  pl.BlockSpec((heads, None, nk, tb), lambda i, e: (0, 1, 0, this(i))),
                  tok, tok, tok,
                  pl.BlockSpec((eblk // 2, d), lambda i, e: (jnp.where(i < n_tb, e, 0), 0)),
                  pl.BlockSpec((d // 2, eblk), lambda i, e: (0, jnp.where(i < n_tb, (e + n_eb - 1) % n_eb, n_eb - 1))),
                  pl.BlockSpec((tb, d), lambda i, e: (done(i), 0)),
                  _const_spec(gf.shape)],
        out_specs=pl.BlockSpec((tb, d), lambda i, e: (done(i), 0)),
        out_shape=jax.ShapeDtypeStruct((n, d), F32),
        scratch_shapes=[pltpu.VMEM((d, tb), F32), pltpu.VMEM((2, eblk, tb), BF16)],
        compiler_params=_params("arbitrary", "arbitrary"),
        name="peer",
    )(h2t, st.reshape(heads, 2, nk, n), t1, e1, e2, u, vt, x1, gf)


def _pack_body(x_ref, o_ref, *, transpose, scale):
    x = x_ref[...] * np.float32(scale)
    if transpose:
        x = x.T
    o_ref[...] = pltpu.bitcast(x.astype(BF16), jnp.uint32)


def _pack_table(x, transpose, scale=1.0, rows=512):
    n, c = x.shape
    if transpose:
        out_shape, out_spec = (c // 2, n), pl.BlockSpec((c // 2, rows), lambda i: (0, i))
    else:
        out_shape, out_spec = (n // 2, c), pl.BlockSpec((rows // 2, c), lambda i: (i, 0))
    return pl.pallas_call(
        functools.partial(_pack_body, transpose=transpose, scale=scale),
        grid=(n // rows,),
        in_specs=[pl.BlockSpec((rows, c), lambda i: (i, 0))],
        out_specs=out_spec,
        out_shape=jax.ShapeDtypeStruct(out_shape, jnp.uint32),
        compiler_params=_params("parallel"),
        name="pack_table",
    )(x)


def _block(n, pref):
    b = pref
    while n % b:
        b //= 2
    return b


def kernel(x_prompt, x_sample, state_hgrn, state_rwkv, state_shift, meta_tokens, norm1, w_in, lb_theta, hgrn_norm, mu_shift, w0, w_w2, a0, a_w2, g_w2, k_k, k_a, r_k, ln_w, ln_b, w_out, norm2, peer_wq, peer_keys, peer_u, peer_v, norm_f):
    assert w_in.shape[0] == 1, "single-layer trunk"
    bp, sp, d = x_prompt.shape
    bs, ss, _ = x_sample.shape
    n_meta = meta_tokens.shape[0]
    _, _, ha_heads, dk, dv = state_hgrn.shape
    _, _, hb_heads, hd, _ = state_rwkv.shape
    ha_in = 2 * ha_heads * dk + 2 * ha_heads * dv
    hw = hb_heads * hd
    lw, la = w_w2.shape[1], a_w2.shape[1]
    nk, dh = peer_keys.shape[2], peer_keys.shape[3]
    p_heads = peer_wq.shape[2] // (2 * dh)
    assert sp % HGRN_CHUNK == 0 and ss <= HGRN_CHUNK and n_meta <= HGRN_CHUNK
    assert hb_heads % 2 == 0 and 2 * hd == LANES and lw + la == LANES and nk == LANES

    row = lambda t: t.reshape(1, -1).astype(F32)
    w_a = w_in[0, :, :ha_in].astype(BF16)
    w_b = w_in[0, :, ha_in:].astype(BF16)
    g1 = row(norm1[0])
    lbt = lb_theta.astype(F32)
    gn = row(hgrn_norm[0])

    head_of_lane = np.arange(LANES) // hd
    rw = {
        "mu": row(mu_shift[0]), "w0": row(w0[0]), "a0": row(a0[0]), "kk": row(k_k[0]), "ka": row(k_a[0]),
        "rk": row(r_k[0]), "lnw": row(ln_w[0]), "lnb": row(ln_b[0]), "gw": g_w2[0].astype(F32),
        "ww": jnp.concatenate([w_w2[0], jnp.zeros((la, hw), F32)], axis=0),
        "aw": jnp.concatenate([jnp.zeros((lw, hw), F32), a_w2[0]], axis=0),
        "ones": jnp.asarray(head_of_lane[:, None] == head_of_lane[None, :], BF16),
        "seg": jnp.asarray(np.tile(head_of_lane[:, None] == head_of_lane[None, :], (2, 1)), BF16),
    }
    woa = w_out[0, :ha_heads * dv].astype(BF16)
    wob = w_out[0, ha_heads * dv:].astype(BF16)
    g2 = row(norm2[0])
    wq = peer_wq[0].astype(BF16)
    keys = peer_keys[0].astype(BF16)
    u = _pack_table(peer_u[0].astype(F32), False, scale=np.sqrt(0.5))
    vt = _pack_table(peer_v[0].astype(F32), True)
    gf = row(norm_f)

    za_m, zb_m = _in_proj(meta_tokens.astype(F32), g1, w_a, w_b, n_meta)
    _, hs_m = _hgrn(za_m[None], 1, n_meta, lbt, gn, jnp.zeros((1, ha_heads, dk, dv), F32), ha_heads, dk, dv)
    _, rs_m, sh_m = _rwkv(zb_m.reshape(1, n_meta, -1), jnp.zeros((1, zb_m.shape[1]), F32),
                          jnp.zeros((1, hb_heads, hd, hd), F32), rw, 1, n_meta, hd)

    def group(x, nseq, length, hs0, rs0, sh0, chunk_rows, hgrn_nb, tc):
        xf = x.reshape(nseq * length, d)
        n = xf.shape[0]
        tb = _block(n, PEER_TOKENS)
        tb_wide = _block(n, WIDE_TOKENS)
        assert tb % LANES == 0
        za, zb = _in_proj(xf, g1, w_a, w_b, tb_wide)
        oa, hs = _hgrn(za.reshape(nseq, length, -1), hgrn_nb, chunk_rows, lbt, gn, hs0, ha_heads, dk, dv)
        ob, rs, sh = _rwkv(zb.reshape(nseq, length, -1), sh0, rs0, rw, 8, tc, hd)
        x1, h2t, st = _out_proj(oa.reshape(n, -1), ob.reshape(n, hw), xf, woa, wob, g2, wq, keys, tb_wide)
        t1, e1, e2 = _route(st, p_heads, nk, tb_wide)
        y = _peer(h2t, st, t1, e1, e2, u, vt, x1, gf, p_heads, nk, tb)
        return y.reshape(nseq, length, d), hs, rs, sh

    y_p, hs_p, rs_p, sh_p = group(
        x_prompt, bp, sp, hs_m, jnp.broadcast_to(rs_m, (bp,) + rs_m.shape[1:]),
        jnp.broadcast_to(sh_m, (bp, sh_m.shape[1])), HGRN_CHUNK, 8, 64)
    y_s, hs_s, rs_s, sh_s = group(
        x_sample, bs, ss, state_hgrn[0].astype(F32), state_rwkv[0].astype(F32),
        state_shift[0].astype(F32), ss, 8, ss)

    return (y_p, y_s, hs_p[None], rs_p[None], sh_p[None], hs_s[None], rs_s[None], sh_s[None])
```
